```python
import math
import jax, jax.numpy as jnp
from jax import lax
import numpy as np

D_MODEL = 1024
BATCH = 32
SEQ = 2048
DEPTH = 4
DEC_BATCH = 1
DEC_SEQ = 16384
PAST_LEN = 128

N_MIXERS = 4
GRID_W = 64
EPS = 1e-6
HEAD_DIM = 64
N_HEADS = D_MODEL // HEAD_DIM
N_KV_HEADS = N_HEADS // 4
Q_PER_KV = N_HEADS // N_KV_HEADS
ROPE_THETA = 10000.0
ROPE_AXIS_DIM = HEAD_DIM // 2
ROPE_FREQS = ROPE_AXIS_DIM // 2
Q_BLOCK = 128
HY_EMB_DIM = 33
HY_BANDS = (HY_EMB_DIM - 1) // 2
HY_FILTER_WIDTH = 64
HY_FAST_DECAY = 0.3
HY_SLOW_DECAY = 1.5
HY_TARGET = 1e-2
POOL_WINDOWS = (2, 4, 8, 16)
POOL_GROUPS = 4
POOL_GROUP_DIM = D_MODEL // POOL_GROUPS
MOE_GROUPS = 4
MOE_EXPERTS_PER_GROUP = 8
MOE_N_EXPERTS = MOE_GROUPS * MOE_EXPERTS_PER_GROUP
MOE_TOP_K = 2
MOE_D_FF = D_MODEL // 4
TOKEN_BLOCK_MAX = 2048
N_ATTN = (DEPTH + 3) // 4
N_HYENA = (DEPTH + 2) // 4
N_POOL = (DEPTH + 1) // 4
N_SCONV = DEPTH // 4

kernel_name = "hybrid_bidir_encoder_hier_moe"


def rms_norm(x, g):
    xf = x.astype(jnp.float32)
    y = xf * lax.rsqrt(jnp.mean(xf * xf, axis=-1, keepdims=True) + EPS)
    return (y * g.astype(jnp.float32)).astype(x.dtype)


def modulate(x, g, shift, scale):
    return rms_norm(x, g) * (1 + scale[:, None, :]) + shift[:, None, :]


def conv3_centred(x, w, b):
    xp = jnp.pad(x, ((0, 0), (1, 1), (0, 0)))
    return xp[:, :-2] * w[0] + xp[:, 1:-1] * w[1] + xp[:, 2:] * w[2] + b


def axial_rope_tables(L):
    rows = L // GRID_W
    r = jnp.broadcast_to(jnp.arange(rows)[:, None], (rows, GRID_W)).reshape(-1)
    c = jnp.broadcast_to(jnp.arange(GRID_W)[None, :], (rows, GRID_W)).reshape(-1)
    inv_freq = ROPE_THETA ** (-jnp.arange(ROPE_FREQS, dtype=jnp.float32) / ROPE_FREQS)
    pos = jnp.stack([r, c], axis=-1).astype(jnp.float32)
    ang = pos[:, :, None] * inv_freq[None, None, :]
    return jnp.cos(ang), jnp.sin(ang)


def apply_axial_rope(x, cos, sin):
    B, L, H, _ = x.shape
    xf = x.astype(jnp.float32).reshape(B, L, H, 2, 2, ROPE_FREQS)
    x1, x2 = xf[..., 0, :], xf[..., 1, :]
    c = cos[None, :, None]
    s = sin[None, :, None]
    out = jnp.stack([x1 * c - x2 * s, x2 * c + x1 * s], axis=-2)
    return out.reshape(B, L, H, HEAD_DIM).astype(x.dtype)


def attention_mixer(h, wqkv, q_norm, k_norm, wo):
    B, L, _ = h.shape
    qkv = h @ wqkv
    q, k, v = jnp.split(qkv, [N_HEADS * HEAD_DIM, (N_HEADS + N_KV_HEADS) * HEAD_DIM], axis=-1)
    q = rms_norm(q.reshape(B, L, N_HEADS, HEAD_DIM), q_norm)
    k = rms_norm(k.reshape(B, L, N_KV_HEADS, HEAD_DIM), k_norm)
    v = v.reshape(B, L, N_KV_HEADS, HEAD_DIM)
    cos, sin = axial_rope_tables(L)
    q = apply_axial_rope(q, cos, sin)
    k = apply_axial_rope(k, cos, sin)
    n_blk = L // Q_BLOCK
    qb = q.reshape(B, n_blk, Q_BLOCK, N_KV_HEADS, Q_PER_KV, HEAD_DIM).transpose(1, 0, 2, 3, 4, 5)
    kf = k.astype(jnp.float32)
    vf = v.astype(jnp.float32)
    scale = HEAD_DIM ** -0.5

    def attend(q_blk):
        s = jnp.einsum('bqkgd,bskd->bkgqs', q_blk.astype(jnp.float32), kf) * scale
        p = jax.nn.softmax(s, axis=-1)
        return jnp.einsum('bkgqs,bskd->bqkgd', p, vf).astype(h.dtype)

    o = lax.map(attend, qb)
    o = o.transpose(1, 0, 2, 3, 4, 5).reshape(B, L, N_HEADS * HEAD_DIM)
    return o @ wo


def hyena_filter(L, w1, b1, w2, b2, w3, freq):
    f32 = jnp.float32
    t = jnp.linspace(0.0, 1.0, L, dtype=f32)[:, None]
    w = 2.0 * math.pi * jnp.arange(L, dtype=f32)[:, None] / L
    f = jnp.linspace(1e-4, HY_BANDS - 1, HY_BANDS, dtype=f32)[None, :]
    z = jnp.concatenate([t, jnp.cos(f * w), -jnp.sin(f * w)], axis=-1)
    fr = freq.astype(f32)
    a = jnp.sin(fr * (z @ w1.astype(f32) + b1.astype(f32)))
    a = jnp.sin(fr * (a @ w2.astype(f32) + b2.astype(f32)))
    hf = a @ w3.astype(f32)
    max_decay = math.log(HY_TARGET) / HY_FAST_DECAY
    min_decay = math.log(HY_TARGET) / HY_SLOW_DECAY
    deltas = jnp.linspace(min_decay, max_decay, D_MODEL, dtype=f32)
    decay = jnp.exp(-t * jnp.abs(deltas)[None, :])
    h_fwd = hf[:, :D_MODEL] * decay
    h_bwd = hf[:, D_MODEL:] * decay
    return jnp.concatenate([h_fwd, jnp.zeros((1, D_MODEL), f32), h_bwd[:0:-1]], axis=0)


def hyena_mixer(h, w_in, conv_w, conv_b, w1, b1, w2, b2, w3, freq, skip, w_out):
    B, L, _ = h.shape
    u = conv3_centred(h @ w_in, conv_w, conv_b)
    x0, x1, v = jnp.split(u, 3, axis=-1)
    z = (v * x1).astype(jnp.float32)
    k = hyena_filter(L, w1, b1, w2, b2, w3, freq)
    zf = jnp.fft.rfft(z, n=2 * L, axis=1)
    kf = jnp.fft.rfft(k, axis=0)
    y = jnp.fft.irfft(zf * kf[None], n=2 * L, axis=1)[:, :L] + z * skip.astype(jnp.float32)
    y = y.astype(h.dtype) * x0
    return y @ w_out


def pool_mixer(h, w_group, scale):
    B, L, _ = h.shape
    hf = h.astype(jnp.float32)
    cs = jnp.concatenate([jnp.zeros((B, 1, D_MODEL), jnp.float32), jnp.cumsum(hf, axis=1)], axis=1)
    t = jnp.arange(L)
    outs = []
    for gi, win in enumerate(POOL_WINDOWS):
        sl = slice(gi * POOL_GROUP_DIM, (gi + 1) * POOL_GROUP_DIM)
        lo = jnp.clip(t - win // 2, 0, L)
        hi = jnp.clip(t + win // 2, 0, L)
        csg = cs[..., sl]
        cnt = (hi - lo).astype(jnp.float32)[None, :, None]
        mean = (jnp.take(csg, hi, axis=1) - jnp.take(csg, lo, axis=1)) / cnt
        outs.append(mean - hf[..., sl])
    pooled = jnp.stack(outs, axis=2).astype(h.dtype)
    y = jnp.einsum('blgc,gcd->blgd', pooled, w_group).reshape(B, L, D_MODEL)
    return y * scale


def short_conv_mixer(h, w_in, conv_w, conv_b, w_out):
    bg, cg, hp = jnp.split(h @ w_in, 3, axis=-1)
    y = bg * conv3_centred(cg * hp, conv_w, conv_b)
    return y @ w_out


def hier_moe(h, w_group, b_group, w_router, b_router, w_gate, w_up, w_down):
    B, L, D = h.shape
    T = B * L
    x = h.reshape(T, D)
    g_logits = (x @ w_group + b_group).astype(jnp.float32)
    g_prob = jax.nn.softmax(g_logits, axis=-1)
    g_idx = jnp.argmax(g_logits, axis=-1)
    g_w = jnp.max(g_prob, axis=-1)
    e_logits = (x @ w_router + b_router).astype(jnp.float32).reshape(T, MOE_GROUPS, MOE_EXPERTS_PER_GROUP)
    e_in = jnp.einsum('tge,tg->te', e_logits, jax.nn.one_hot(g_idx, MOE_GROUPS, dtype=jnp.float32))
    top_v, top_i = lax.top_k(e_in, MOE_TOP_K)
    top_w = jax.nn.softmax(top_v, axis=-1) * g_w[:, None]
    expert_id = g_idx[:, None] * MOE_EXPERTS_PER_GROUP + top_i
    combine = jnp.sum(jax.nn.one_hot(expert_id, MOE_N_EXPERTS, dtype=jnp.float32) * top_w[..., None], axis=1)
    blk = math.gcd(T, TOKEN_BLOCK_MAX)
    xb = x.reshape(T // blk, blk, D)
    cb = combine.astype(h.dtype).reshape(T // blk, blk, MOE_N_EXPERTS)

    def block(args):
        xt, ct = args
        a = jnp.einsum('nd,edf->nef', xt, w_gate)
        u = jnp.einsum('nd,edf->nef', xt, w_up)
        hh = jax.nn.silu(a) * u * ct[..., None]
        return jnp.einsum('nef,efd->nd', hh, w_down)

    y = lax.map(block, (xb, cb))
    return y.reshape(B, L, D)


def setup_inputs(seed: int = 0) -> dict:
    key = jax.random.key(seed)
    ks = iter(jax.random.split(key, 64))
    f32 = jnp.float32

    def nrm(shape, s):
        return s * jax.random.normal(next(ks), shape, f32)

    def gain(shape):
        return 1.0 + nrm(shape, 0.05)

    D = D_MODEL
    qkv_w = (N_HEADS + 2 * N_KV_HEADS) * HEAD_DIM
    return {
        "x_prompt": nrm((BATCH, SEQ, D), 1.0),
        "x_sample": nrm((DEC_BATCH, DEC_SEQ, D), 1.0),
        "c_prompt": nrm((BATCH, D), 1.0),
        "c_sample": nrm((DEC_BATCH, D), 1.0),
        "norm1_g": gain((DEPTH, D)),
        "norm2_g": gain((DEPTH, D)),
        "ada_w": nrm((DEPTH, D, 6 * D), 0.5 * D ** -0.5),
        "ada_b": nrm((DEPTH, 6 * D), 0.02),
        "attn_wqkv": nrm((N_ATTN, D, qkv_w), D ** -0.5),
        "attn_q_norm": gain((N_ATTN, HEAD_DIM)),
        "attn_k_norm": gain((N_ATTN, HEAD_DIM)),
        "attn_wo": nrm((N_ATTN, N_HEADS * HEAD_DIM, D), (N_HEADS * HEAD_DIM) ** -0.5),
        "hy_w_in": nrm((N_HYENA, D, 3 * D), D ** -0.5),
        "hy_conv_w": nrm((N_HYENA, 3, 3 * D), 3 ** -0.5),
        "hy_conv_b": nrm((N_HYENA, 3 * D), 0.02),
        "hy_ffn_w1": nrm((N_HYENA, HY_EMB_DIM, HY_FILTER_WIDTH), HY_EMB_DIM ** -0.5),
        "hy_ffn_b1": nrm((N_HYENA, HY_FILTER_WIDTH), 0.1),
        "hy_ffn_w2": nrm((N_HYENA, HY_FILTER_WIDTH, HY_FILTER_WIDTH), HY_FILTER_WIDTH ** -0.5),
        "hy_ffn_b2": nrm((N_HYENA, HY_FILTER_WIDTH), 0.1),
        "hy_ffn_w3": nrm((N_HYENA, HY_FILTER_WIDTH, 2 * D), 0.05 * HY_FILTER_WIDTH ** -0.5),
        "hy_freq": gain((N_HYENA, HY_FILTER_WIDTH)),
        "hy_skip": nrm((N_HYENA, D), 1.0),
        "hy_w_out": nrm((N_HYENA, D, D), D ** -0.5),
        "pool_w": nrm((N_POOL, POOL_GROUPS, POOL_GROUP_DIM, POOL_GROUP_DIM), POOL_GROUP_DIM ** -0.5),
        "pool_scale": gain((N_POOL, D)),
        "sc_w_in": nrm((N_SCONV, D, 3 * D), D ** -0.5),
        "sc_conv_w": nrm((N_SCONV, 3, D), 3 ** -0.5),
        "sc_conv_b": nrm((N_SCONV, D), 0.02),
        "sc_w_out": nrm((N_SCONV, D, D), D ** -0.5),
        "moe_w_group": nrm((DEPTH, D, MOE_GROUPS), D ** -0.5),
        "moe_b_group": nrm((DEPTH, MOE_GROUPS), 0.01),
        "moe_w_router": nrm((DEPTH, D, MOE_N_EXPERTS), D ** -0.5),
        "moe_b_router": nrm((DEPTH, MOE_N_EXPERTS), 0.01),
        "moe_w_gate": nrm((DEPTH, MOE_N_EXPERTS, D, MOE_D_FF), D ** -0.5),
        "moe_w_up": nrm((DEPTH, MOE_N_EXPERTS, D, MOE_D_FF), D ** -0.5),
        "moe_w_down": nrm((DEPTH, MOE_N_EXPERTS, MOE_D_FF, D), MOE_D_FF ** -0.5),
    }


def reference(x_prompt, x_sample, c_prompt, c_sample,
              norm1_g, norm2_g, ada_w, ada_b,
              attn_wqkv, attn_q_norm, attn_k_norm, attn_wo,
              hy_w_in, hy_conv_w, hy_conv_b, hy_ffn_w1, hy_ffn_b1, hy_ffn_w2, hy_ffn_b2,
              hy_ffn_w3, hy_freq, hy_skip, hy_w_out,
              pool_w, pool_scale,
              sc_w_in, sc_conv_w, sc_conv_b, sc_w_out,
              moe_w_group, moe_b_group, moe_w_router, moe_b_router,
              moe_w_gate, moe_w_up, moe_w_down):

    def trunk(x, c):
        for i in range(DEPTH):
            mod = jax.nn.silu(c) @ ada_w[i] + ada_b[i]
            sh1, sc1, g1, sh2, sc2, g2 = jnp.split(mod, 6, axis=-1)
            h = modulate(x, norm1_g[i], sh1, sc1)
            m, j = i % N_MIXERS, i // N_MIXERS
            if m == 0:
                y = attention_mixer(h, attn_wqkv[j], attn_q_norm[j], attn_k_norm[j], attn_wo[j])
            elif m == 1:
                y = hyena_mixer(h, hy_w_in[j], hy_conv_w[j], hy_conv_b[j], hy_ffn_w1[j], hy_ffn_b1[j],
                                hy_ffn_w2[j], hy_ffn_b2[j], hy_ffn_w3[j], hy_freq[j], hy_skip[j], hy_w_out[j])
            elif m == 2:
                y = pool_mixer(h, pool_w[j], pool_scale[j])
            else:
                y = short_conv_mixer(h, sc_w_in[j], sc_conv_w[j], sc_conv_b[j], sc_w_out[j])
            x = x + g1[:, None, :] * y
            h = modulate(x, norm2_g[i], sh2, sc2)
            x = x + g2[:, None, :] * hier_moe(h, moe_w_group[i], moe_b_group[i], moe_w_router[i],
                                               moe_b_router[i], moe_w_gate[i], moe_w_up[i], moe_w_down[i])
        return x

    y_prompt = trunk(x_prompt, c_prompt)
    y_sample = trunk(x_sample, c_sample)
    return (y_prompt, y_sample)
```

```python
import functools
import math

import jax
import jax.numpy as jnp
import numpy as np
from jax import lax
from jax.experimental import pallas as pl
from jax.experimental.pallas import tpu as pltpu

F32 = jnp.float32
BF16 = jnp.bfloat16

D_MODEL = 1024
DEPTH = 4
EPS = 1e-6
GRID_W = 64
HEAD_DIM = 64
N_HEADS = 16
N_KV_HEADS = 4
Q_PER_KV = 4
ROPE_THETA = 10000.0
ROPE_FREQS = 16
HY_EMB_DIM = 33
HY_BANDS = 16
HY_FILTER_WIDTH = 64
HY_FAST_DECAY = 0.3
HY_SLOW_DECAY = 1.5
HY_TARGET = 1e-2
POOL_WINDOWS = (2, 4, 8, 16)
POOL_GROUP_DIM = 256
MOE_GROUPS = 4
MOE_EXPERTS_PER_GROUP = 8
MOE_N_EXPERTS = 32
MOE_D_FF = 256

LANES = 128
HALO = 8
DFT_N2 = 256
VMEM_LIMIT = 56 * 1024 * 1024

ROW_TILE = 512


def _params(sem):
    return pltpu.CompilerParams(dimension_semantics=sem, vmem_limit_bytes=VMEM_LIMIT)


def _dot(a, b):
    return jnp.dot(a, b, preferred_element_type=F32)


def _split(a):
    hi = a.astype(BF16)
    lo = (a - hi.astype(F32)).astype(BF16)
    return hi, lo


def _dot3(a, b):
    ah, al = _split(a)
    bh, bl = _split(b)
    return _dot(ah, bh) + (_dot(ah, bl) + _dot(al, bh))


def _modulate(x, g, shift, scale):
    ms = jnp.mean(x * x, axis=-1, keepdims=True)
    return x * lax.rsqrt(ms + EPS) * g * (1.0 + scale) + shift


def _silu(x):
    return x * (1.0 / (1.0 + jnp.exp(-x)))


def _ada_kernel(c_ref, w_ref, b_ref, o_ref):
    c = c_ref[...]
    o_ref[0] = _dot3(_silu(c), w_ref[0]) + b_ref[0]


def ada_mod(c_all, ada_w, ada_b):
    rows = c_all.shape[0]
    n = ada_w.shape[2]
    tn = 1536
    return pl.pallas_call(
        _ada_kernel,
        grid=(DEPTH, n // tn),
        in_specs=[
            pl.BlockSpec((rows, D_MODEL), lambda l, j: (0, 0)),
            pl.BlockSpec((1, D_MODEL, tn), lambda l, j: (l, 0, j)),
            pl.BlockSpec((1, 1, tn), lambda l, j: (l, 0, j)),
        ],
        out_specs=pl.BlockSpec((1, rows, tn), lambda l, j: (l, 0, j)),
        out_shape=jax.ShapeDtypeStruct((DEPTH, rows, n), F32),
        compiler_params=_params(("parallel", "parallel")),
        name="ada_mod",
    )(c_all, ada_w, ada_b.reshape(DEPTH, 1, n))


def _row_spec(tm, width=D_MODEL):
    return pl.BlockSpec((tm, width), lambda i: (i, 0))


def _mod_spec(tm, seq_len):
    return pl.BlockSpec((1, 6, D_MODEL), lambda i: ((i * tm) // seq_len, 0, 0))


def _const_spec(shape):
    nd = len(shape)
    return pl.BlockSpec(shape, lambda i: (0,) * nd)


def _halo_specs(tm, n_rows):
    per = tm // HALO
    last = n_rows // HALO - 1
    prev = pl.BlockSpec((HALO, D_MODEL), lambda i: (jnp.maximum(i * per - 1, 0), 0))
    nxt = pl.BlockSpec((HALO, D_MODEL), lambda i: (jnp.minimum((i + 1) * per, last), 0))
    return prev, nxt


def _edge_flags(tm, seq_len):
    i = pl.program_id(0)
    per_seq = seq_len // tm
    pos = i % per_seq
    return pos == 0, pos == per_seq - 1


def _ext_rows(prev_ref, x_ref, next_ref):
    return jnp.concatenate([prev_ref[...], x_ref[...], next_ref[...]], axis=0)


def _shift_rows(u, tm):
    n = u.shape[0]
    up = pltpu.roll(u, 1, axis=0)[HALO:HALO + tm]
    dn = pltpu.roll(u, n - 1, axis=0)[HALO:HALO + tm]
    return up, u[HALO:HALO + tm], dn


def _conv3(u, w_ref, b_ref, cols, tm, first, last):
    up, mid, dn = _shift_rows(u, tm)
    row = lax.broadcasted_iota(jnp.int32, (tm, 1), 0)
    up = jnp.where(jnp.logical_and(first, row == 0), 0.0, up)
    dn = jnp.where(jnp.logical_and(last, row == tm - 1), 0.0, dn)
    w = w_ref[:, cols]
    return up * w[0:1] + mid * w[1:2] + dn * w[2:3] + b_ref[:, cols]


def _norm_rope(t, gain, headmean, cos, sin_signed):
    width = t.shape[1]
    ms = _dot((t * t).astype(BF16), headmean[:width, :width])
    y = t * lax.rsqrt(ms + EPS) * gain
    lane = lax.broadcasted_iota(jnp.int32, y.shape, 1)
    first = (lane % 32) < ROPE_FREQS
    partner = jnp.where(first, pltpu.roll(y, width - ROPE_FREQS, axis=1),
                        pltpu.roll(y, ROPE_FREQS, axis=1))
    reps = width // LANES
    return y * jnp.tile(cos, (1, reps)) + partner * jnp.tile(sin_signed, (1, reps))


def _qkv_kernel(x_ref, mod_ref, g_ref, w_ref, qg_ref, kg_ref, hm_ref, cos_ref, sin_ref,
                q_ref, k_ref, v_ref):
    m = mod_ref[0]
    h = _modulate(x_ref[...], g_ref[...], m[0:1], m[1:2]).astype(BF16)
    qkv = _dot(h, w_ref[...])
    nq = N_HEADS * HEAD_DIM
    nk = N_KV_HEADS * HEAD_DIM
    cos = cos_ref[...]
    sin = sin_ref[...]
    hm = hm_ref[...]
    q = _norm_rope(qkv[:, :nq], qg_ref[...], hm, cos, sin)
    k = _norm_rope(qkv[:, nq:nq + nk], kg_ref[...], hm, cos, sin)
    v = qkv[:, nq + nk:]
    q_ref[...] = q.astype(BF16)
    ones = jnp.ones((v.shape[0], HEAD_DIM), F32)
    for g in range(N_KV_HEADS):
        sl = slice(g * HEAD_DIM, (g + 1) * HEAD_DIM)
        k_ref[g] = k[:, sl].astype(BF16)
        v_ref[g] = jnp.concatenate([v[:, sl], ones], axis=1).astype(BF16)


def _rope_tables(seq_len):
    rows = seq_len // GRID_W
    r = jnp.broadcast_to(jnp.arange(rows)[:, None], (rows, GRID_W)).reshape(-1)
    c = jnp.broadcast_to(jnp.arange(GRID_W)[None, :], (rows, GRID_W)).reshape(-1)
    inv_freq = ROPE_THETA ** (-jnp.arange(ROPE_FREQS, dtype=F32) / ROPE_FREQS)
    pos = jnp.stack([r, c], axis=-1).astype(F32)
    ang = pos[:, :, None] * inv_freq[None, None, :]
    cos = jnp.cos(ang)
    sin = jnp.sin(ang)
    cos64 = jnp.concatenate([cos, cos], axis=-1).reshape(seq_len, HEAD_DIM)
    sin64 = jnp.concatenate([-sin, sin], axis=-1).reshape(seq_len, HEAD_DIM)
    return jnp.tile(cos64, (1, 2)), jnp.tile(sin64, (1, 2))


def attn_qkv(x, mod, norm_g, wqkv, q_norm, k_norm, seq_len):
    n_rows = x.shape[0]
    tm = ROW_TILE
    nq = N_HEADS * HEAD_DIM
    nk = N_KV_HEADS * HEAD_DIM
    cos, sin = _rope_tables(seq_len)
    qg = jnp.tile(q_norm, N_HEADS)[None, :] * (HEAD_DIM ** -0.5 * math.log2(math.e))
    kg = jnp.tile(k_norm, N_KV_HEADS)[None, :]
    head = np.arange(nq) // HEAD_DIM
    headmean = jnp.asarray((head[:, None] == head[None, :]).astype(np.float32) / HEAD_DIM, BF16)
    per_seq = seq_len // tm
    tab_spec = pl.BlockSpec((tm, LANES), lambda i: (i % per_seq, 0))
    return pl.pallas_call(
        _qkv_kernel,
        grid=(n_rows // tm,),
        in_specs=[
            _row_spec(tm), _mod_spec(tm, seq_len), _const_spec((1, D_MODEL)),
            _const_spec((D_MODEL, nq + 2 * nk)), _const_spec((1, nq)), _const_spec((1, nk)),
            _const_spec((nq, nq)), tab_spec, tab_spec,
        ],
        out_specs=[
            _row_spec(tm, nq),
            pl.BlockSpec((N_KV_HEADS, tm, HEAD_DIM), lambda i: (0, i, 0)),
            pl.BlockSpec((N_KV_HEADS, tm, 2 * HEAD_DIM), lambda i: (0, i, 0)),
        ],
        out_shape=[
            jax.ShapeDtypeStruct((n_rows, nq), BF16),
            jax.ShapeDtypeStruct((N_KV_HEADS, n_rows, HEAD_DIM), BF16),
            jax.ShapeDtypeStruct((N_KV_HEADS, n_rows, 2 * HEAD_DIM), BF16),
        ],
        compiler_params=_params(("parallel",)),
        name="attn_qkv",
    )(x, mod, norm_g[None, :], wqkv.astype(BF16), qg, kg, headmean, cos, sin)


def _flash_kernel(q_ref, k_ref, v_ref, o_ref, *, tq, tk, n_chunks):
    q = q_ref[...]
    qs = jnp.concatenate([q[:, j * HEAD_DIM:(j + 1) * HEAD_DIM] for j in range(Q_PER_KV)], axis=0)
    rows = Q_PER_KV * tq

    def body(c, carry):
        m, acc = carry
        start = pl.multiple_of(c * tk, tk)
        kc = k_ref[0, pl.ds(start, tk), :]
        vc = v_ref[0, pl.ds(start, tk), :]
        s = lax.dot_general(qs, kc, (((1,), (1,)), ((), ())), preferred_element_type=F32)
        m_new = jnp.maximum(m, jnp.max(s, axis=-1, keepdims=True))
        alpha = jnp.exp2(m - m_new)
        p = jnp.exp2(s - m_new)
        acc = acc * alpha + _dot(p.astype(BF16), vc)
        return m_new, acc

    m0 = jnp.full((rows, 1), -jnp.inf, F32)
    acc0 = jnp.zeros((rows, 2 * HEAD_DIM), F32)
    _, acc = lax.fori_loop(0, n_chunks, body, (m0, acc0))
    o = acc[:, :HEAD_DIM] / acc[:, HEAD_DIM:HEAD_DIM + 1]
    o_ref[...] = jnp.concatenate([o[j * tq:(j + 1) * tq] for j in range(Q_PER_KV)],
                                 axis=1).astype(BF16)


def attn_flash(q, k, v, batch, seq_len):
    n_rows = q.shape[0]
    tq = 128
    tk = 512
    per_seq = seq_len // tq
    width = Q_PER_KV * HEAD_DIM
    kern = functools.partial(_flash_kernel, tq=tq, tk=tk, n_chunks=seq_len // tk)
    return pl.pallas_call(
        kern,
        grid=(batch, N_KV_HEADS, per_seq),
        in_specs=[
            pl.BlockSpec((tq, width), lambda b, g, i: (b * per_seq + i, g)),
            pl.BlockSpec((1, seq_len, HEAD_DIM), lambda b, g, i: (g, b, 0)),
            pl.BlockSpec((1, seq_len, 2 * HEAD_DIM), lambda b, g, i: (g, b, 0)),
        ],
        out_specs=pl.BlockSpec((tq, width), lambda b, g, i: (b * per_seq + i, g)),
        out_shape=jax.ShapeDtypeStruct((n_rows, N_HEADS * HEAD_DIM), BF16),
        compiler_params=_params(("parallel", "parallel", "parallel")),
        name="attn_flash",
    )(q, k, v)


def _proj_res_kernel(y_ref, w_ref, x_ref, mod_ref, o_ref, *, gate_row):
    gate = mod_ref[0][gate_row:gate_row + 1]
    o_ref[...] = x_ref[...] + gate * _dot(y_ref[...], w_ref[...])


def proj_residual(y, w, x, mod, gate_row, seq_len):
    n_rows = x.shape[0]
    tm = ROW_TILE
    return pl.pallas_call(
        functools.partial(_proj_res_kernel, gate_row=gate_row),
        grid=(n_rows // tm,),
        in_specs=[_row_spec(tm, y.shape[1]), _const_spec(w.shape), _row_spec(tm),
                  _mod_spec(tm, seq_len)],
        out_specs=_row_spec(tm),
        out_shape=jax.ShapeDtypeStruct((n_rows, D_MODEL), F32),
        compiler_params=_params(("parallel",)),
        name="proj_residual",
    )(y, w.astype(BF16), x, mod)


def _hy_in_kernel(prev_ref, x_ref, next_ref, mod_ref, g_ref, w_ref, cw_ref, cb_ref,
                  z_ref, x0_ref, *, tm, seq_len):
    first, last = _edge_flags(tm, seq_len)
    m = mod_ref[0]
    h = _modulate(_ext_rows(prev_ref, x_ref, next_ref), g_ref[...], m[0:1], m[1:2]).astype(BF16)
    tn = 256
    for j in range(D_MODEL // tn):
        part = []
        for s in range(3):
            cols = slice(s * D_MODEL + j * tn, s * D_MODEL + (j + 1) * tn)
            part.append(_conv3(_dot(h, w_ref[:, cols]), cw_ref, cb_ref, cols, tm, first, last))
        out_cols = slice(j * tn, (j + 1) * tn)
        x0_ref[:, out_cols] = part[0].astype(BF16)
        z_ref[:, out_cols] = part[2] * part[1]


def hyena_in(x, mod, norm_g, w_in, conv_w, conv_b, seq_len):
    n_rows = x.shape[0]
    tm = ROW_TILE
    prev, nxt = _halo_specs(tm, n_rows)
    return pl.pallas_call(
        functools.partial(_hy_in_kernel, tm=tm, seq_len=seq_len),
        grid=(n_rows // tm,),
        in_specs=[prev, _row_spec(tm), nxt, _mod_spec(tm, seq_len), _const_spec((1, D_MODEL)),
                  _const_spec((D_MODEL, 3 * D_MODEL)), _const_spec((3, 3 * D_MODEL)),
                  _const_spec((1, 3 * D_MODEL))],
        out_specs=[_row_spec(tm), _row_spec(tm)],
        out_shape=[jax.ShapeDtypeStruct((n_rows, D_MODEL), F32),
                   jax.ShapeDtypeStruct((n_rows, D_MODEL), BF16)],
        compiler_params=_params(("parallel",)),
        name="hyena_in",
    )(x, x, x, mod, norm_g[None, :], w_in.astype(BF16), conv_w, conv_b[None, :])


def _hy_filter_kernel(feat_ref, w1_ref, b1_ref, w2_ref, b2_ref, w3_ref, fr_ref, dl_ref, o_ref):
    feat = feat_ref[...]
    fr = fr_ref[...]
    a = jnp.sin(fr * (_dot3(feat, w1_ref[...]) + b1_ref[...]))
    a = jnp.sin(fr * (_dot3(a, w2_ref[...]) + b2_ref[...]))
    hf = _dot3(a, w3_ref[...])
    decay = jnp.exp(-feat[:, 0:1] * dl_ref[...])
    o_ref[0] = hf[:, :D_MODEL] * decay
    o_ref[1] = hf[:, D_MODEL:] * decay


def _pad_to(a, rows, cols):
    return jnp.pad(a.astype(F32), ((0, rows - a.shape[0]), (0, cols - a.shape[1])))


def hyena_filter(seq_len, w1, b1, w2, b2, w3, freq):
    t = jnp.linspace(0.0, 1.0, seq_len, dtype=F32)[:, None]
    w = 2.0 * math.pi * jnp.arange(seq_len, dtype=F32)[:, None] / seq_len
    f = jnp.linspace(1e-4, HY_BANDS - 1, HY_BANDS, dtype=F32)[None, :]
    feat = _pad_to(jnp.concatenate([t, jnp.cos(f * w), -jnp.sin(f * w)], axis=-1), seq_len, LANES)
    max_decay = math.log(HY_TARGET) / HY_FAST_DECAY
    min_decay = math.log(HY_TARGET) / HY_SLOW_DECAY
    absdelta = jnp.abs(jnp.linspace(min_decay, max_decay, D_MODEL, dtype=F32))[None, :]
    tl = 512
    return pl.pallas_call(
        _hy_filter_kernel,
        grid=(seq_len // tl,),
        in_specs=[_row_spec(tl, LANES), _const_spec((LANES, LANES)), _const_spec((1, LANES)),
                  _const_spec((LANES, LANES)), _const_spec((1, LANES)),
                  _const_spec((LANES, 2 * D_MODEL)), _const_spec((1, LANES)),
                  _const_spec((1, D_MODEL))],
        out_specs=pl.BlockSpec((2, tl, D_MODEL), lambda i: (0, i, 0)),
        out_shape=jax.ShapeDtypeStruct((2, seq_len, D_MODEL), F32),
        compiler_params=_params(("parallel",)),
        name="hyena_filter",
    )(feat, _pad_to(w1, LANES, LANES), _pad_to(b1[None, :], 1, LANES), _pad_to(w2, LANES, LANES),
      _pad_to(b2[None, :], 1, LANES), _pad_to(w3, LANES, 2 * D_MODEL),
      _pad_to(freq[None, :], 1, LANES), absdelta)


class _FFTPlan:
    def __init__(self, seq_len):
        self.n = 2 * seq_len
        self.n1 = self.n // DFT_N2
        self.r = self.n1 // 2
        self.k1n = self.n1 // 2 + 1
        self.kron = max(HALO, LANES // self.r)
        ang = 2.0 * np.pi * np.outer(np.arange(self.k1n), np.arange(self.r)) / self.n1
        eye = np.eye(self.kron)
        self.fwd_cos = jnp.asarray(np.kron(np.cos(ang), eye), BF16)
        self.fwd_sin = jnp.asarray(np.kron(-np.sin(ang), eye), BF16)
        wgt = np.full((self.k1n,), 2.0)
        wgt[0] = wgt[-1] = 1.0
        scale = (wgt / self.n)[None, :]
        self.inv_cos = jnp.asarray(np.kron(np.cos(ang).T * scale, eye), BF16)
        self.inv_sin = jnp.asarray(np.kron(-np.sin(ang).T * scale, eye), BF16)
        a2 = 2.0 * np.pi * np.outer(np.arange(DFT_N2), np.arange(DFT_N2)) / DFT_N2
        self.f_cos = jnp.asarray(np.cos(a2), F32)
        self.f_sin = jnp.asarray(-np.sin(a2), F32)
        tw = 2.0 * np.pi * np.outer(np.arange(self.k1n), np.arange(DFT_N2)) / self.n
        self.tw_cos = jnp.asarray(np.cos(tw), F32)
        self.tw_sin = jnp.asarray(-np.sin(tw), F32)


def _fft_a_kernel(z_ref, wc_ref, ws_ref, ar_ref, ai_ref, *, rq, kq):
    z = z_ref[0, :, 0].reshape(rq, D_MODEL).astype(BF16)
    shape = ar_ref.shape[1:2] + ar_ref.shape[3:]
    ar_ref[0, :, 0] = _dot(wc_ref[...], z).reshape(shape)
    ai_ref[0, :, 0] = _dot(ws_ref[...], z).reshape(shape)


def fft_stage_a(z, plan, batch):
    q = plan.kron
    nhi = DFT_N2 // q
    zv = z.reshape(batch, plan.r, nhi, q, D_MODEL)
    out = jax.ShapeDtypeStruct((batch, plan.k1n, nhi, q, D_MODEL), F32)
    ospec = pl.BlockSpec((1, plan.k1n, 1, q, D_MODEL), lambda b, h: (b, 0, h, 0, 0))
    wspec = pl.BlockSpec(plan.fwd_cos.shape, lambda b, h: (0, 0))
    ar, ai = pl.pallas_call(
        functools.partial(_fft_a_kernel, rq=plan.r * q, kq=plan.k1n * q),
        grid=(batch, nhi),
        in_specs=[pl.BlockSpec((1, plan.r, 1, q, D_MODEL), lambda b, h: (b, 0, h, 0, 0)),
                  wspec, wspec],
        out_specs=[ospec, ospec],
        out_shape=[out, out],
        compiler_params=_params(("parallel", "parallel")),
        name="fft_stage_a",
    )(zv, plan.fwd_cos, plan.fwd_sin)
    shape = (batch, plan.k1n, DFT_N2, D_MODEL)
    return ar.reshape(shape), ai.reshape(shape)


def _twiddled_dft(fr, fi, tr, ti):
    return (fr * tr - fi * ti).astype(BF16), (fr * ti + fi * tr).astype(BF16)


def _fft_b_fwd_kernel(ar_ref, ai_ref, fr_ref, fi_ref, twr_ref, twi_ref, br_ref, bi_ref,
                      gr_ref, gi_ref):
    @pl.when(pl.program_id(1) == 0)
    def _():
        gr, gi = _twiddled_dft(fr_ref[...], fi_ref[...], twr_ref[0], twi_ref[0])
        gr_ref[...] = gr
        gi_ref[...] = gi

    gr = gr_ref[...]
    gi = gi_ref[...]
    ar = ar_ref[0, 0].astype(BF16)
    ai = ai_ref[0, 0].astype(BF16)
    br_ref[0, 0] = _dot(gr, ar) - _dot(gi, ai)
    bi_ref[0, 0] = _dot(gr, ai) + _dot(gi, ar)


def fft_stage_b_fwd(ar, ai, plan):
    batch = ar.shape[0]
    blk = pl.BlockSpec((1, 1, DFT_N2, D_MODEL), lambda k, b: (b, k, 0, 0))
    cst = pl.BlockSpec((DFT_N2, DFT_N2), lambda k, b: (0, 0))
    tws = pl.BlockSpec((1, 1, DFT_N2), lambda k, b: (k, 0, 0))
    out = jax.ShapeDtypeStruct(ar.shape, F32)
    return pl.pallas_call(
        _fft_b_fwd_kernel,
        grid=(plan.k1n, batch),
        in_specs=[blk, blk, cst, cst, tws, tws],
        out_specs=[blk, blk],
        out_shape=[out, out],
        scratch_shapes=[pltpu.VMEM((DFT_N2, DFT_N2), BF16)] * 2,
        compiler_params=_params(("parallel", "arbitrary")),
        name="fft_stage_b_fwd",
    )(ar, ai, plan.f_cos, plan.f_sin, plan.tw_cos[:, None, :], plan.tw_sin[:, None, :])


def _fft_b_conv_kernel(ar_ref, ai_ref, hr_ref, hi_ref, hb0_ref, fr_ref, fi_ref,
                       twr_ref, twi_ref, tcr_ref, tci_ref, cr_ref, ci_ref,
                       gr_ref, gi_ref, gtr_ref, gti_ref, kr_ref, ki_ref):
    @pl.when(pl.program_id(1) == 0)
    def _():
        fr = fr_ref[...]
        fi = fi_ref[...]
        gr, gi = _twiddled_dft(fr, fi, twr_ref[0], twi_ref[0])
        gr_ref[...] = gr
        gi_ref[...] = gi
        gtr, gti = _twiddled_dft(fr, fi, tcr_ref[0], tci_ref[0])
        gtr_ref[...] = gtr
        gti_ref[...] = gti
        kr_ref[...] = hr_ref[0, 0] + hr_ref[1, 0] - hb0_ref[...]
        ki_ref[...] = hi_ref[0, 0] - hi_ref[1, 0]

    gr = gr_ref[...]
    gi = gi_ref[...]
    ar = ar_ref[0, 0].astype(BF16)
    ai = ai_ref[0, 0].astype(BF16)
    br = _dot(gr, ar) - _dot(gi, ai)
    bi = _dot(gr, ai) + _dot(gi, ar)
    kr = kr_ref[...]
    ki = ki_ref[...]
    pr = (br * kr - bi * ki).astype(BF16)
    pi = (br * ki + bi * kr).astype(BF16)
    gtr = gtr_ref[...]
    gti = gti_ref[...]
    cr_ref[0, 0] = _dot(gtr, pr) + _dot(gti, pi)
    ci_ref[0, 0] = _dot(gtr, pi) - _dot(gti, pr)


def fft_stage_b_conv(ar, ai, hr, hi, hb0, plan):
    batch = ar.shape[0]
    blk = pl.BlockSpec((1, 1, DFT_N2, D_MODEL), lambda k, b: (b, k, 0, 0))
    hblk = pl.BlockSpec((2, 1, DFT_N2, D_MODEL), lambda k, b: (0, k, 0, 0))
    cst = pl.BlockSpec((DFT_N2, DFT_N2), lambda k, b: (0, 0))
    tws = pl.BlockSpec((1, 1, DFT_N2), lambda k, b: (k, 0, 0))
    twc = pl.BlockSpec((1, DFT_N2, 1), lambda k, b: (k, 0, 0))
    out = jax.ShapeDtypeStruct(ar.shape, F32)
    return pl.pallas_call(
        _fft_b_conv_kernel,
        grid=(plan.k1n, batch),
        in_specs=[blk, blk, hblk, hblk, pl.BlockSpec((1, D_MODEL), lambda k, b: (0, 0)),
                  cst, cst, tws, tws, twc, twc],
        out_specs=[blk, blk],
        out_shape=[out, out],
        scratch_shapes=[pltpu.VMEM((DFT_N2, DFT_N2), BF16)] * 4
        + [pltpu.VMEM((DFT_N2, D_MODEL), F32)] * 2,
        compiler_params=_params(("parallel", "arbitrary")),
        name="fft_stage_b_conv",
    )(ar, ai, hr, hi, hb0, plan.f_cos, plan.f_sin,
      plan.tw_cos[:, None, :], plan.tw_sin[:, None, :],
      plan.tw_cos[:, :, None], plan.tw_sin[:, :, None])


def _fft_a_inv_kernel(cr_ref, ci_ref, vc_ref, vs_ref, z_ref, x0_ref, skip_ref, y_ref, *, kq):
    cr = cr_ref[0, :, 0].reshape(kq, D_MODEL).astype(BF16)
    ci = ci_ref[0, :, 0].reshape(kq, D_MODEL).astype(BF16)
    conv = _dot(vc_ref[...], cr) + _dot(vs_ref[...], ci)
    shape = z_ref.shape[1:2] + z_ref.shape[3:]
    z = z_ref[0, :, 0]
    y = conv.reshape(shape) + z * skip_ref[...]
    y_ref[0, :, 0] = (y * x0_ref[0, :, 0].astype(F32)).astype(BF16)


def fft_stage_a_inv(cr, ci, z, x0, skip, plan, batch):
    q = plan.kron
    nhi = DFT_N2 // q
    cshape = (batch, plan.k1n, nhi, q, D_MODEL)
    tshape = (batch, plan.r, nhi, q, D_MODEL)
    cspec = pl.BlockSpec((1, plan.k1n, 1, q, D_MODEL), lambda b, h: (b, 0, h, 0, 0))
    tspec = pl.BlockSpec((1, plan.r, 1, q, D_MODEL), lambda b, h: (b, 0, h, 0, 0))
    wspec = pl.BlockSpec(plan.inv_cos.shape, lambda b, h: (0, 0))
    y = pl.pallas_call(
        functools.partial(_fft_a_inv_kernel, kq=plan.k1n * q),
        grid=(batch, nhi),
        in_specs=[cspec, cspec, wspec, wspec, tspec, tspec,
                  pl.BlockSpec((1, D_MODEL), lambda b, h: (0, 0))],
        out_specs=tspec,
        out_shape=jax.ShapeDtypeStruct(tshape, BF16),
        compiler_params=_params(("parallel", "parallel")),
        name="fft_stage_a_inv",
    )(cr.reshape(cshape), ci.reshape(cshape), plan.inv_cos, plan.inv_sin,
      z.reshape(tshape), x0.reshape(tshape), skip[None, :])
    return y.reshape(z.shape)


def hyena_mixer(x, mod, norm_g, p, batch, seq_len):
    z, x0 = hyena_in(x, mod, norm_g, p["w_in"], p["conv_w"], p["conv_b"], seq_len)
    plan = _FFTPlan(seq_len)
    filt = hyena_filter(seq_len, p["w1"], p["b1"], p["w2"], p["b2"], p["w3"], p["freq"])
    fr, fi = fft_stage_a(filt.reshape(2 * seq_len, D_MODEL), plan, 2)
    hr, hi = fft_stage_b_fwd(fr, fi, plan)
    ar, ai = fft_stage_a(z, plan, batch)
    cr, ci = fft_stage_b_conv(ar, ai, hr, hi, filt[1, 0:1, :], plan)
    y = fft_stage_a_inv(cr, ci, z, x0, p["skip"], plan, batch)
    return proj_residual(y, p["w_out"], x, mod, 2, seq_len)


def _pool_kernel(prev_ref, x_ref, next_ref, mod_ref, g_ref, w_ref, s_ref, o_ref, *, tm, seq_len):
    first, last = _edge_flags(tm, seq_len)
    m = mod_ref[0]
    x = x_ref[...]
    h = _modulate(_ext_rows(prev_ref, x_ref, next_ref), g_ref[...], m[0:1], m[1:2])
    n = tm + 2 * HALO
    row = lax.broadcasted_iota(jnp.int32, (n, 1), 0)
    outside = jnp.logical_or(jnp.logical_and(first, row < HALO),
                             jnp.logical_and(last, row >= HALO + tm))
    h = jnp.where(outside, 0.0, h)
    pos = (pl.program_id(0) * tm) % seq_len + lax.broadcasted_iota(jnp.int32, (tm, 1), 0)
    ys = []
    for gi, win in enumerate(POOL_WINDOWS):
        cols = slice(gi * POOL_GROUP_DIM, (gi + 1) * POOL_GROUP_DIM)
        hg = h[:, cols]
        acc = hg
        span = 1
        while span < win:
            acc = acc + pltpu.roll(acc, span, axis=0)
            span *= 2
        lead = win // 2 - 1
        if lead:
            acc = pltpu.roll(acc, n - lead, axis=0)
        half = win // 2
        cnt = jnp.minimum(pos + half, seq_len) - jnp.maximum(pos - half, 0)
        pooled = acc[HALO:HALO + tm] / cnt.astype(F32) - hg[HALO:HALO + tm]
        ys.append(_dot(pooled.astype(BF16), w_ref[gi]))
    y = jnp.concatenate(ys, axis=1) * s_ref[...]
    o_ref[...] = x + m[2:3] * y


def pool_mixer(x, mod, norm_g, w_group, scale, seq_len):
    n_rows = x.shape[0]
    tm = ROW_TILE
    prev, nxt = _halo_specs(tm, n_rows)
    return pl.pallas_call(
        functools.partial(_pool_kernel, tm=tm, seq_len=seq_len),
        grid=(n_rows // tm,),
        in_specs=[prev, _row_spec(tm), nxt, _mod_spec(tm, seq_len), _const_spec((1, D_MODEL)),
                  _const_spec(w_group.shape), _const_spec((1, D_MODEL))],
        out_specs=_row_spec(tm),
        out_shape=jax.ShapeDtypeStruct((n_rows, D_MODEL), F32),
        compiler_params=_params(("parallel",)),
        name="pool_mixer",
    )(x, x, x, mod, norm_g[None, :], w_group.astype(BF16), scale[None, :])


def _sconv_kernel(prev_ref, x_ref, next_ref, mod_ref, g_ref, w_ref, cw_ref, cb_ref, wo_ref,
                  o_ref, y_ref, *, tm, seq_len):
    first, last = _edge_flags(tm, seq_len)
    m = mod_ref[0]
    h = _modulate(_ext_rows(prev_ref, x_ref, next_ref), g_ref[...], m[0:1], m[1:2]).astype(BF16)
    tn = 256
    for j in range(D_MODEL // tn):
        cols = slice(j * tn, (j + 1) * tn)
        bg = _dot(h, w_ref[:, cols])[HALO:HALO + tm]
        cg = _dot(h, w_ref[:, D_MODEL + j * tn:D_MODEL + (j + 1) * tn])
        hp = _dot(h, w_ref[:, 2 * D_MODEL + j * tn:2 * D_MODEL + (j + 1) * tn])
        y_ref[:, cols] = (bg * _conv3(cg * hp, cw_ref, cb_ref, cols, tm, first, last)).astype(BF16)
    o_ref[...] = x_ref[...] + m[2:3] * _dot(y_ref[...], wo_ref[...])


def sconv_mixer(x, mod, norm_g, w_in, conv_w, conv_b, w_out, seq_len):
    n_rows = x.shape[0]
    tm = ROW_TILE
    prev, nxt = _halo_specs(tm, n_rows)
    return pl.pallas_call(
        functools.partial(_sconv_kernel, tm=tm, seq_len=seq_len),
        grid=(n_rows // tm,),
        in_specs=[prev, _row_spec(tm), nxt, _mod_spec(tm, seq_len), _const_spec((1, D_MODEL)),
                  _const_spec((D_MODEL, 3 * D_MODEL)), _const_spec((3, D_MODEL)),
                  _const_spec((1, D_MODEL)), _const_spec((D_MODEL, D_MODEL))],
        out_specs=_row_spec(tm),
        out_shape=jax.ShapeDtypeStruct((n_rows, D_MODEL), F32),
        scratch_shapes=[pltpu.VMEM((tm, D_MODEL), BF16)],
        compiler_params=_params(("parallel",)),
        name="sconv_mixer",
    )(x, x, x, mod, norm_g[None, :], w_in.astype(BF16), conv_w, conv_b[None, :],
      w_out.astype(BF16))


def _router_kernel(x_ref, mod_ref, g_ref, w_ref, b_ref, h_ref, c_ref):
    m = mod_ref[0]
    h = _modulate(x_ref[...], g_ref[...], m[3:4], m[4:5])
    h_ref[...] = h.astype(BF16)
    lg = _dot3(h, w_ref[...]) + b_ref[...]
    lane = lax.broadcasted_iota(jnp.int32, lg.shape, 1)
    lane_f = lane.astype(F32)
    neg = -jnp.inf
    far = float(LANES)

    def first_argmax(vals):
        top = jnp.max(vals, axis=-1, keepdims=True)
        idx = jnp.min(jnp.where(vals == top, lane_f, far), axis=-1, keepdims=True)
        return top, idx

    is_group = jnp.logical_and(lane >= MOE_N_EXPERTS, lane < MOE_N_EXPERTS + MOE_GROUPS)
    gl = jnp.where(is_group, lg, neg)
    gmax, gidx = first_argmax(gl)
    gsum = jnp.sum(jnp.where(is_group, jnp.exp(gl - gmax), 0.0), axis=-1, keepdims=True)
    g_w = 1.0 / gsum
    grp = gidx - float(MOE_N_EXPERTS)
    lo = grp * MOE_EXPERTS_PER_GROUP
    in_grp = jnp.logical_and(lane_f >= lo, lane_f < lo + MOE_EXPERTS_PER_GROUP)
    el = jnp.where(in_grp, lg, neg)
    v1, i1 = first_argmax(el)
    el2 = jnp.where(lane_f == i1, neg, el)
    v2, i2 = first_argmax(el2)
    e2 = jnp.exp(v2 - v1)
    w1 = 1.0 / (1.0 + e2)
    w2 = e2 * w1
    c_ref[...] = (jnp.where(lane_f == i1, w1 * g_w, 0.0)
                  + jnp.where(lane_f == i2, w2 * g_w, 0.0))


def moe_router(x, mod, norm_g, w_group, b_group, w_router, b_router, seq_len):
    n_rows = x.shape[0]
    tm = ROW_TILE
    w = _pad_to(jnp.concatenate([w_router, w_group], axis=1), D_MODEL, LANES)
    b = _pad_to(jnp.concatenate([b_router, b_group])[None, :], 1, LANES)
    return pl.pallas_call(
        _router_kernel,
        grid=(n_rows // tm,),
        in_specs=[_row_spec(tm), _mod_spec(tm, seq_len), _const_spec((1, D_MODEL)),
                  _const_spec((D_MODEL, LANES)), _const_spec((1, LANES))],
        out_specs=[_row_spec(tm), _row_spec(tm, LANES)],
        out_shape=[jax.ShapeDtypeStruct((n_rows, D_MODEL), BF16),
                   jax.ShapeDtypeStruct((n_rows, LANES), F32)],
        compiler_params=_params(("parallel",)),
        name="moe_router",
    )(x, mod, norm_g[None, :], w, b)


def _moe_dense_kernel(h_ref, c_ref, wg_ref, wu_ref, wd_ref, x_ref, mod_ref, o_ref, acc_ref):
    e = pl.program_id(1)

    @pl.when(e == 0)
    def _():
        acc_ref[...] = jnp.zeros_like(acc_ref)

    h = h_ref[...]
    comb = c_ref[...]
    lane = lax.broadcasted_iota(jnp.int32, comb.shape, 1)
    ce = jnp.sum(jnp.where(lane == e, comb, 0.0), axis=-1, keepdims=True)
    a = _dot(h, wg_ref[0])
    u = _dot(h, wu_ref[0])
    hh = (_silu(a) * u * ce).astype(BF16)
    acc_ref[...] += _dot(hh, wd_ref[0])

    @pl.when(e == MOE_N_EXPERTS - 1)
    def _():
        o_ref[...] = x_ref[...] + mod_ref[0][5:6] * acc_ref[...]


def moe_dense(h, comb, w_gate, w_up, w_down, x, mod, seq_len):
    n_rows = x.shape[0]
    tm = 1024
    row = lambda width: pl.BlockSpec((tm, width), lambda i, e: (i, 0))
    return pl.pallas_call(
        _moe_dense_kernel,
        grid=(n_rows // tm, MOE_N_EXPERTS),
        in_specs=[row(D_MODEL), row(LANES),
                  pl.BlockSpec((1, D_MODEL, MOE_D_FF), lambda i, e: (e, 0, 0)),
                  pl.BlockSpec((1, D_MODEL, MOE_D_FF), lambda i, e: (e, 0, 0)),
                  pl.BlockSpec((1, MOE_D_FF, D_MODEL), lambda i, e: (e, 0, 0)),
                  row(D_MODEL),
                  pl.BlockSpec((1, 6, D_MODEL), lambda i, e: ((i * tm) // seq_len, 0, 0))],
        out_specs=row(D_MODEL),
        out_shape=jax.ShapeDtypeStruct((n_rows, D_MODEL), F32),
        scratch_shapes=[pltpu.VMEM((tm, D_MODEL), F32)],
        compiler_params=_params(("parallel", "arbitrary")),
        name="moe_dense",
    )(h, comb, w_gate, w_up, w_down, x, mod)


def _trunk(x3, mods, p):
    batch, seq_len, _ = x3.shape
    x = x3.reshape(batch * seq_len, D_MODEL)
    for i in range(DEPTH):
        mod = mods[i]
        g1 = p["norm1_g"][i]
        kind = i % 4
        if kind == 0:
            q, k, v = attn_qkv(x, mod, g1, p["attn_wqkv"][0], p["attn_q_norm"][0],
                               p["attn_k_norm"][0], seq_len)
            o = attn_flash(q, k, v, batch, seq_len)
            x = proj_residual(o, p["attn_wo"][0], x, mod, 2, seq_len)
        elif kind == 1:
            hp = {"w_in": p["hy_w_in"][0], "conv_w": p["hy_conv_w"][0], "conv_b": p["hy_conv_b"][0],
                  "w1": p["hy_ffn_w1"][0], "b1": p["hy_ffn_b1"][0], "w2": p["hy_ffn_w2"][0],
                  "b2": p["hy_ffn_b2"][0], "w3": p["hy_ffn_w3"][0], "freq": p["hy_freq"][0],
                  "skip": p["hy_skip"][0], "w_out": p["hy_w_out"][0]}
            x = hyena_mixer(x, mod, g1, hp, batch, seq_len)
        elif kind == 2:
            x = pool_mixer(x, mod, g1, p["pool_w"][0], p["pool_scale"][0], seq_len)
        else:
            x = sconv_mixer(x, mod, g1, p["sc_w_in"][0], p["sc_conv_w"][0], p["sc_conv_b"][0],
                            p["sc_w_out"][0], seq_len)
        h, comb = moe_router(x, mod, p["norm2_g"][i], p["moe_w_group"][i], p["moe_b_group"][i],
                             p["moe_w_router"][i], p["moe_b_router"][i], seq_len)
        x = moe_dense(h, comb, p["moe_w_gate_bf16"][i], p["moe_w_up_bf16"][i],
                      p["moe_w_down_bf16"][i], x, mod, seq_len)
    return x.reshape(batch, seq_len, D_MODEL)


def kernel(x_prompt, x_sample, c_prompt, c_sample, norm1_g, norm2_g, ada_w, ada_b, attn_wqkv, attn_q_norm, attn_k_norm, attn_wo, hy_w_in, hy_conv_w, hy_conv_b, hy_ffn_w1, hy_ffn_b1, hy_ffn_w2, hy_ffn_b2, hy_ffn_w3, hy_freq, hy_skip, hy_w_out, pool_w, pool_scale, sc_w_in, sc_conv_w, sc_conv_b, sc_w_out, moe_w_group, moe_b_group, moe_w_router, moe_b_router, moe_w_gate, moe_w_up, moe_w_down):
    p = dict(norm1_g=norm1_g, norm2_g=norm2_g, attn_wqkv=attn_wqkv, attn_q_norm=attn_q_norm,
             attn_k_norm=attn_k_norm, attn_wo=attn_wo, hy_w_in=hy_w_in, hy_conv_w=hy_conv_w,
             hy_conv_b=hy_conv_b, hy_ffn_w1=hy_ffn_w1, hy_ffn_b1=hy_ffn_b1, hy_ffn_w2=hy_ffn_w2,
             hy_ffn_b2=hy_ffn_b2, hy_ffn_w3=hy_ffn_w3, hy_freq=hy_freq, hy_skip=hy_skip,
             hy_w_out=hy_w_out, pool_w=pool_w, pool_scale=pool_scale, sc_w_in=sc_w_in,
             sc_conv_w=sc_conv_w, sc_conv_b=sc_conv_b, sc_w_out=sc_w_out, moe_w_group=moe_w_group,
             moe_b_group=moe_b_group, moe_w_router=moe_w_router, moe_b_router=moe_b_router,
             moe_w_gate_bf16=moe_w_gate.astype(BF16), moe_w_up_bf16=moe_w_up.astype(BF16),
             moe_w_down_bf16=moe_w_down.astype(BF16))
    nb = c_prompt.shape[0]
    ns = c_sample.shape[0]
    rows = -(-(nb + ns) // HALO) * HALO
    c_all = jnp.pad(jnp.concatenate([c_prompt, c_sample], axis=0), ((0, rows - nb - ns), (0, 0)))
    mod = ada_mod(c_all, ada_w, ada_b).reshape(DEPTH, rows, 6, D_MODEL)
    mods_prompt = [mod[i, :nb] for i in range(DEPTH)]
    mods_sample = [mod[i, nb:nb + ns] for i in range(DEPTH)]
    return _trunk(x_prompt, mods_prompt, p), _trunk(x_sample, mods_sample, p)
```

```python
import functools
import math

import jax
import jax.numpy as jnp
import numpy as np
from jax import lax
from jax.experimental import pallas as pl
from jax.experimental.pallas import tpu as pltpu
from jax.experimental.pallas import tpu_sc as plsc

F32 = jnp.float32
BF16 = jnp.bfloat16

D_MODEL = 1024
DEPTH = 4
EPS = 1e-6
GRID_W = 64
HEAD_DIM = 64
N_HEADS = 16
N_KV_HEADS = 4
Q_PER_KV = 4
ROPE_THETA = 10000.0
ROPE_FREQS = 16
HY_EMB_DIM = 33
HY_BANDS = 16
HY_FILTER_WIDTH = 64
HY_FAST_DECAY = 0.3
HY_SLOW_DECAY = 1.5
HY_TARGET = 1e-2
POOL_WINDOWS = (2, 4, 8, 16)
POOL_GROUP_DIM = 256
MOE_GROUPS = 4
MOE_EXPERTS_PER_GROUP = 8
MOE_N_EXPERTS = 32
MOE_D_FF = 256

LANES = 128
HALO = 8
DFT_N2 = 256
VMEM_LIMIT = 56 * 1024 * 1024

ROW_TILE = 512


def _params(sem):
    return pltpu.CompilerParams(dimension_semantics=sem, vmem_limit_bytes=VMEM_LIMIT)


def _dot(a, b):
    return jnp.dot(a, b, preferred_element_type=F32)


def _split(a):
    hi = a.astype(BF16)
    lo = (a - hi.astype(F32)).astype(BF16)
    return hi, lo


def _dot3(a, b):
    ah, al = _split(a)
    bh, bl = _split(b)
    return _dot(ah, bh) + (_dot(ah, bl) + _dot(al, bh))


def _modulate(x, g, shift, scale):
    ms = jnp.mean(x * x, axis=-1, keepdims=True)
    return x * lax.rsqrt(ms + EPS) * g * (1.0 + scale) + shift


def _silu(x):
    return x * (1.0 / (1.0 + jnp.exp(-x)))


def _ada_kernel(c_ref, w_ref, b_ref, o_ref):
    c = c_ref[...]
    o_ref[0] = _dot3(_silu(c), w_ref[0]) + b_ref[0]


def ada_mod(c_all, ada_w, ada_b):
    rows = c_all.shape[0]
    n = ada_w.shape[2]
    tn = 1536
    return pl.pallas_call(
        _ada_kernel,
        grid=(DEPTH, n // tn),
        in_specs=[
            pl.BlockSpec((rows, D_MODEL), lambda l, j: (0, 0)),
            pl.BlockSpec((1, D_MODEL, tn), lambda l, j: (l, 0, j)),
            pl.BlockSpec((1, 1, tn), lambda l, j: (l, 0, j)),
        ],
        out_specs=pl.BlockSpec((1, rows, tn), lambda l, j: (l, 0, j)),
        out_shape=jax.ShapeDtypeStruct((DEPTH, rows, n), F32),
        compiler_params=_params(("parallel", "parallel")),
        name="ada_mod",
    )(c_all, ada_w, ada_b.reshape(DEPTH, 1, n))


def _row_spec(tm, width=D_MODEL):
    return pl.BlockSpec((tm, width), lambda i: (i, 0))


def _mod_spec(tm, seq_len):
    return pl.BlockSpec((1, 6, D_MODEL), lambda i: ((i * tm) // seq_len, 0, 0))


def _const_spec(shape):
    nd = len(shape)
    return pl.BlockSpec(shape, lambda i: (0,) * nd)


def _halo_specs(tm, n_rows):
    per = tm // HALO
    last = n_rows // HALO - 1
    prev = pl.BlockSpec((HALO, D_MODEL), lambda i: (jnp.maximum(i * per - 1, 0), 0))
    nxt = pl.BlockSpec((HALO, D_MODEL), lambda i: (jnp.minimum((i + 1) * per, last), 0))
    return prev, nxt


def _edge_flags(tm, seq_len):
    i = pl.program_id(0)
    per_seq = seq_len // tm
    pos = i % per_seq
    return pos == 0, pos == per_seq - 1


def _ext_rows(prev_ref, x_ref, next_ref):
    return jnp.concatenate([prev_ref[...], x_ref[...], next_ref[...]], axis=0)


def _shift_rows(u, tm):
    n = u.shape[0]
    up = pltpu.roll(u, 1, axis=0)[HALO:HALO + tm]
    dn = pltpu.roll(u, n - 1, axis=0)[HALO:HALO + tm]
    return up, u[HALO:HALO + tm], dn


def _conv3(u, w_ref, b_ref, cols, tm, first, last):
    up, mid, dn = _shift_rows(u, tm)
    row = lax.broadcasted_iota(jnp.int32, (tm, 1), 0)
    up = jnp.where(jnp.logical_and(first, row == 0), 0.0, up)
    dn = jnp.where(jnp.logical_and(last, row == tm - 1), 0.0, dn)
    w = w_ref[:, cols]
    return up * w[0:1] + mid * w[1:2] + dn * w[2:3] + b_ref[:, cols]


def _norm_rope(t, gain, headmean, cos, sin_signed):
    width = t.shape[1]
    ms = _dot((t * t).astype(BF16), headmean[:width, :width])
    y = t * lax.rsqrt(ms + EPS) * gain
    lane = lax.broadcasted_iota(jnp.int32, y.shape, 1)
    first = (lane % 32) < ROPE_FREQS
    partner = jnp.where(first, pltpu.roll(y, width - ROPE_FREQS, axis=1),
                        pltpu.roll(y, ROPE_FREQS, axis=1))
    reps = width // LANES
    return y * jnp.tile(cos, (1, reps)) + partner * jnp.tile(sin_signed, (1, reps))


def _qkv_kernel(x_ref, mod_ref, g_ref, w_ref, qg_ref, kg_ref, hm_ref, cos_ref, sin_ref,
                q_ref, k_ref, v_ref):
    m = mod_ref[0]
    h = _modulate(x_ref[...], g_ref[...], m[0:1], m[1:2]).astype(BF16)
    qkv = _dot(h, w_ref[...])
    nq = N_HEADS * HEAD_DIM
    nk = N_KV_HEADS * HEAD_DIM
    cos = cos_ref[...]
    sin = sin_ref[...]
    hm = hm_ref[...]
    q = _norm_rope(qkv[:, :nq], qg_ref[...], hm, cos, sin)
    k = _norm_rope(qkv[:, nq:nq + nk], kg_ref[...], hm, cos, sin)
    v = qkv[:, nq + nk:]
    q_ref[...] = q.astype(BF16)
    ones = jnp.ones((v.shape[0], HEAD_DIM), F32)
    for g in range(N_KV_HEADS):
        sl = slice(g * HEAD_DIM, (g + 1) * HEAD_DIM)
        k_ref[g] = k[:, sl].astype(BF16)
        v_ref[g] = jnp.concatenate([v[:, sl], ones], axis=1).astype(BF16)


def _rope_tables(seq_len):
    rows = seq_len // GRID_W
    r = jnp.broadcast_to(jnp.arange(rows)[:, None], (rows, GRID_W)).reshape(-1)
    c = jnp.broadcast_to(jnp.arange(GRID_W)[None, :], (rows, GRID_W)).reshape(-1)
    inv_freq = ROPE_THETA ** (-jnp.arange(ROPE_FREQS, dtype=F32) / ROPE_FREQS)
    pos = jnp.stack([r, c], axis=-1).astype(F32)
    ang = pos[:, :, None] * inv_freq[None, None, :]
    cos = jnp.cos(ang)
    sin = jnp.sin(ang)
    cos64 = jnp.concatenate([cos, cos], axis=-1).reshape(seq_len, HEAD_DIM)
    sin64 = jnp.concatenate([-sin, sin], axis=-1).reshape(seq_len, HEAD_DIM)
    return jnp.tile(cos64, (1, 2)), jnp.tile(sin64, (1, 2))


def attn_qkv(x, mod, norm_g, wqkv, q_norm, k_norm, seq_len):
    n_rows = x.shape[0]
    tm = ROW_TILE
    nq = N_HEADS * HEAD_DIM
    nk = N_KV_HEADS * HEAD_DIM
    cos, sin = _rope_tables(seq_len)
    qg = jnp.tile(q_norm, N_HEADS)[None, :] * (HEAD_DIM ** -0.5 * math.log2(math.e))
    kg = jnp.tile(k_norm, N_KV_HEADS)[None, :]
    head = np.arange(nq) // HEAD_DIM
    headmean = jnp.asarray((head[:, None] == head[None, :]).astype(np.float32) / HEAD_DIM, BF16)
    per_seq = seq_len // tm
    tab_spec = pl.BlockSpec((tm, LANES), lambda i: (i % per_seq, 0))
    return pl.pallas_call(
        _qkv_kernel,
        grid=(n_rows // tm,),
        in_specs=[
            _row_spec(tm), _mod_spec(tm, seq_len), _const_spec((1, D_MODEL)),
            _const_spec((D_MODEL, nq + 2 * nk)), _const_spec((1, nq)), _const_spec((1, nk)),
            _const_spec((nq, nq)), tab_spec, tab_spec,
        ],
        out_specs=[
            _row_spec(tm, nq),
            pl.BlockSpec((N_KV_HEADS, tm, HEAD_DIM), lambda i: (0, i, 0)),
            pl.BlockSpec((N_KV_HEADS, tm, 2 * HEAD_DIM), lambda i: (0, i, 0)),
        ],
        out_shape=[
            jax.ShapeDtypeStruct((n_rows, nq), BF16),
            jax.ShapeDtypeStruct((N_KV_HEADS, n_rows, HEAD_DIM), BF16),
            jax.ShapeDtypeStruct((N_KV_HEADS, n_rows, 2 * HEAD_DIM), BF16),
        ],
        compiler_params=_params(("parallel",)),
        name="attn_qkv",
    )(x, mod, norm_g[None, :], wqkv.astype(BF16), qg, kg, headmean, cos, sin)


def _flash_kernel(q_ref, k_ref, v_ref, o_ref, *, tq, tk, n_chunks):
    q = q_ref[...]
    qs = jnp.concatenate([q[:, j * HEAD_DIM:(j + 1) * HEAD_DIM] for j in range(Q_PER_KV)], axis=0)
    rows = Q_PER_KV * tq

    def body(c, carry):
        m, acc = carry
        start = pl.multiple_of(c * tk, tk)
        kc = k_ref[0, pl.ds(start, tk), :]
        vc = v_ref[0, pl.ds(start, tk), :]
        s = lax.dot_general(qs, kc, (((1,), (1,)), ((), ())), preferred_element_type=F32)
        m_new = jnp.maximum(m, jnp.max(s, axis=-1, keepdims=True))
        alpha = jnp.exp2(m - m_new)
        p = jnp.exp2(s - m_new)
        acc = acc * alpha + _dot(p.astype(BF16), vc)
        return m_new, acc

    m0 = jnp.full((rows, 1), -jnp.inf, F32)
    acc0 = jnp.zeros((rows, 2 * HEAD_DIM), F32)
    _, acc = lax.fori_loop(0, n_chunks, body, (m0, acc0))
    o = acc[:, :HEAD_DIM] / acc[:, HEAD_DIM:HEAD_DIM + 1]
    o_ref[...] = jnp.concatenate([o[j * tq:(j + 1) * tq] for j in range(Q_PER_KV)],
                                 axis=1).astype(BF16)


def attn_flash(q, k, v, batch, seq_len):
    n_rows = q.shape[0]
    tq = 128
    tk = 512
    per_seq = seq_len // tq
    width = Q_PER_KV * HEAD_DIM
    kern = functools.partial(_flash_kernel, tq=tq, tk=tk, n_chunks=seq_len // tk)
    return pl.pallas_call(
        kern,
        grid=(batch, N_KV_HEADS, per_seq),
        in_specs=[
            pl.BlockSpec((tq, width), lambda b, g, i: (b * per_seq + i, g)),
            pl.BlockSpec((1, seq_len, HEAD_DIM), lambda b, g, i: (g, b, 0)),
            pl.BlockSpec((1, seq_len, 2 * HEAD_DIM), lambda b, g, i: (g, b, 0)),
        ],
        out_specs=pl.BlockSpec((tq, width), lambda b, g, i: (b * per_seq + i, g)),
        out_shape=jax.ShapeDtypeStruct((n_rows, N_HEADS * HEAD_DIM), BF16),
        compiler_params=_params(("parallel", "parallel", "parallel")),
        name="attn_flash",
    )(q, k, v)


def _proj_res_kernel(y_ref, w_ref, x_ref, mod_ref, o_ref, *, gate_row):
    gate = mod_ref[0][gate_row:gate_row + 1]
    o_ref[...] = x_ref[...] + gate * _dot(y_ref[...], w_ref[...])


def proj_residual(y, w, x, mod, gate_row, seq_len):
    n_rows = x.shape[0]
    tm = ROW_TILE
    return pl.pallas_call(
        functools.partial(_proj_res_kernel, gate_row=gate_row),
        grid=(n_rows // tm,),
        in_specs=[_row_spec(tm, y.shape[1]), _const_spec(w.shape), _row_spec(tm),
                  _mod_spec(tm, seq_len)],
        out_specs=_row_spec(tm),
        out_shape=jax.ShapeDtypeStruct((n_rows, D_MODEL), F32),
        compiler_params=_params(("parallel",)),
        name="proj_residual",
    )(y, w.astype(BF16), x, mod)


def _hy_in_kernel(prev_ref, x_ref, next_ref, mod_ref, g_ref, w_ref, cw_ref, cb_ref,
                  z_ref, x0_ref, *, tm, seq_len):
    first, last = _edge_flags(tm, seq_len)
    m = mod_ref[0]
    h = _modulate(_ext_rows(prev_ref, x_ref, next_ref), g_ref[...], m[0:1], m[1:2]).astype(BF16)
    tn = 256
    for j in range(D_MODEL // tn):
        part = []
        for s in range(3):
            cols = slice(s * D_MODEL + j * tn, s * D_MODEL + (j + 1) * tn)
            part.append(_conv3(_dot(h, w_ref[:, cols]), cw_ref, cb_ref, cols, tm, first, last))
        out_cols = slice(j * tn, (j + 1) * tn)
        x0_ref[:, out_cols] = part[0].astype(BF16)
        z_ref[:, out_cols] = part[2] * part[1]


def hyena_in(x, mod, norm_g, w_in, conv_w, conv_b, seq_len):
    n_rows = x.shape[0]
    tm = ROW_TILE
    prev, nxt = _halo_specs(tm, n_rows)
    return pl.pallas_call(
        functools.partial(_hy_in_kernel, tm=tm, seq_len=seq_len),
        grid=(n_rows // tm,),
        in_specs=[prev, _row_spec(tm), nxt, _mod_spec(tm, seq_len), _const_spec((1, D_MODEL)),
                  _const_spec((D_MODEL, 3 * D_MODEL)), _const_spec((3, 3 * D_MODEL)),
                  _const_spec((1, 3 * D_MODEL))],
        out_specs=[_row_spec(tm), _row_spec(tm)],
        out_shape=[jax.ShapeDtypeStruct((n_rows, D_MODEL), F32),
                   jax.ShapeDtypeStruct((n_rows, D_MODEL), BF16)],
        compiler_params=_params(("parallel",)),
        name="hyena_in",
    )(x, x, x, mod, norm_g[None, :], w_in.astype(BF16), conv_w, conv_b[None, :])


def _hy_filter_kernel(feat_ref, w1_ref, b1_ref, w2_ref, b2_ref, w3_ref, fr_ref, dl_ref, o_ref):
    feat = feat_ref[...]
    fr = fr_ref[...]
    a = jnp.sin(fr * (_dot3(feat, w1_ref[...]) + b1_ref[...]))
    a = jnp.sin(fr * (_dot3(a, w2_ref[...]) + b2_ref[...]))
    hf = _dot3(a, w3_ref[...])
    decay = jnp.exp(-feat[:, 0:1] * dl_ref[...])
    o_ref[0] = hf[:, :D_MODEL] * decay
    o_ref[1] = hf[:, D_MODEL:] * decay


def _pad_to(a, rows, cols):
    return jnp.pad(a.astype(F32), ((0, rows - a.shape[0]), (0, cols - a.shape[1])))


def hyena_filter(seq_len, w1, b1, w2, b2, w3, freq):
    t = jnp.linspace(0.0, 1.0, seq_len, dtype=F32)[:, None]
    w = 2.0 * math.pi * jnp.arange(seq_len, dtype=F32)[:, None] / seq_len
    f = jnp.linspace(1e-4, HY_BANDS - 1, HY_BANDS, dtype=F32)[None, :]
    feat = _pad_to(jnp.concatenate([t, jnp.cos(f * w), -jnp.sin(f * w)], axis=-1), seq_len, LANES)
    max_decay = math.log(HY_TARGET) / HY_FAST_DECAY
    min_decay = math.log(HY_TARGET) / HY_SLOW_DECAY
    absdelta = jnp.abs(jnp.linspace(min_decay, max_decay, D_MODEL, dtype=F32))[None, :]
    tl = 512
    return pl.pallas_call(
        _hy_filter_kernel,
        grid=(seq_len // tl,),
        in_specs=[_row_spec(tl, LANES), _const_spec((LANES, LANES)), _const_spec((1, LANES)),
                  _const_spec((LANES, LANES)), _const_spec((1, LANES)),
                  _const_spec((LANES, 2 * D_MODEL)), _const_spec((1, LANES)),
                  _const_spec((1, D_MODEL))],
        out_specs=pl.BlockSpec((2, tl, D_MODEL), lambda i: (0, i, 0)),
        out_shape=jax.ShapeDtypeStruct((2, seq_len, D_MODEL), F32),
        compiler_params=_params(("parallel",)),
        name="hyena_filter",
    )(feat, _pad_to(w1, LANES, LANES), _pad_to(b1[None, :], 1, LANES), _pad_to(w2, LANES, LANES),
      _pad_to(b2[None, :], 1, LANES), _pad_to(w3, LANES, 2 * D_MODEL),
      _pad_to(freq[None, :], 1, LANES), absdelta)


class _FFTPlan:
    def __init__(self, seq_len):
        self.n = 2 * seq_len
        self.n1 = self.n // DFT_N2
        self.r = self.n1 // 2
        self.k1n = self.n1 // 2 + 1
        self.kron = max(HALO, LANES // self.r)
        ang = 2.0 * np.pi * np.outer(np.arange(self.k1n), np.arange(self.r)) / self.n1
        eye = np.eye(self.kron)
        self.fwd_cos = jnp.asarray(np.kron(np.cos(ang), eye), BF16)
        self.fwd_sin = jnp.asarray(np.kron(-np.sin(ang), eye), BF16)
        wgt = np.full((self.k1n,), 2.0)
        wgt[0] = wgt[-1] = 1.0
        scale = (wgt / self.n)[None, :]
        self.inv_cos = jnp.asarray(np.kron(np.cos(ang).T * scale, eye), BF16)
        self.inv_sin = jnp.asarray(np.kron(-np.sin(ang).T * scale, eye), BF16)
        a2 = 2.0 * np.pi * np.outer(np.arange(DFT_N2), np.arange(DFT_N2)) / DFT_N2
        self.f_cos = jnp.asarray(np.cos(a2), F32)
        self.f_sin = jnp.asarray(-np.sin(a2), F32)
        tw = 2.0 * np.pi * np.outer(np.arange(self.k1n), np.arange(DFT_N2)) / self.n
        self.tw_cos = jnp.asarray(np.cos(tw), F32)
        self.tw_sin = jnp.asarray(-np.sin(tw), F32)


def _fft_a_kernel(z_ref, wc_ref, ws_ref, ar_ref, ai_ref, *, rq, kq):
    z = z_ref[0, :, 0].reshape(rq, D_MODEL).astype(BF16)
    shape = ar_ref.shape[1:2] + ar_ref.shape[3:]
    ar_ref[0, :, 0] = _dot(wc_ref[...], z).reshape(shape)
    ai_ref[0, :, 0] = _dot(ws_ref[...], z).reshape(shape)


def fft_stage_a(z, plan, batch):
    q = plan.kron
    nhi = DFT_N2 // q
    zv = z.reshape(batch, plan.r, nhi, q, D_MODEL)
    out = jax.ShapeDtypeStruct((batch, plan.k1n, nhi, q, D_MODEL), F32)
    ospec = pl.BlockSpec((1, plan.k1n, 1, q, D_MODEL), lambda b, h: (b, 0, h, 0, 0))
    wspec = pl.BlockSpec(plan.fwd_cos.shape, lambda b, h: (0, 0))
    ar, ai = pl.pallas_call(
        functools.partial(_fft_a_kernel, rq=plan.r * q, kq=plan.k1n * q),
        grid=(batch, nhi),
        in_specs=[pl.BlockSpec((1, plan.r, 1, q, D_MODEL), lambda b, h: (b, 0, h, 0, 0)),
                  wspec, wspec],
        out_specs=[ospec, ospec],
        out_shape=[out, out],
        compiler_params=_params(("parallel", "parallel")),
        name="fft_stage_a",
    )(zv, plan.fwd_cos, plan.fwd_sin)
    shape = (batch, plan.k1n, DFT_N2, D_MODEL)
    return ar.reshape(shape), ai.reshape(shape)


def _twiddled_dft(fr, fi, tr, ti):
    return (fr * tr - fi * ti).astype(BF16), (fr * ti + fi * tr).astype(BF16)


def _fft_b_fwd_kernel(ar_ref, ai_ref, fr_ref, fi_ref, twr_ref, twi_ref, br_ref, bi_ref,
                      gr_ref, gi_ref):
    @pl.when(pl.program_id(1) == 0)
    def _():
        gr, gi = _twiddled_dft(fr_ref[...], fi_ref[...], twr_ref[0], twi_ref[0])
        gr_ref[...] = gr
        gi_ref[...] = gi

    gr = gr_ref[...]
    gi = gi_ref[...]
    ar = ar_ref[0, 0].astype(BF16)
    ai = ai_ref[0, 0].astype(BF16)
    br_ref[0, 0] = _dot(gr, ar) - _dot(gi, ai)
    bi_ref[0, 0] = _dot(gr, ai) + _dot(gi, ar)


def fft_stage_b_fwd(ar, ai, plan):
    batch = ar.shape[0]
    blk = pl.BlockSpec((1, 1, DFT_N2, D_MODEL), lambda k, b: (b, k, 0, 0))
    cst = pl.BlockSpec((DFT_N2, DFT_N2), lambda k, b: (0, 0))
    tws = pl.BlockSpec((1, 1, DFT_N2), lambda k, b: (k, 0, 0))
    out = jax.ShapeDtypeStruct(ar.shape, F32)
    return pl.pallas_call(
        _fft_b_fwd_kernel,
        grid=(plan.k1n, batch),
        in_specs=[blk, blk, cst, cst, tws, tws],
        out_specs=[blk, blk],
        out_shape=[out, out],
        scratch_shapes=[pltpu.VMEM((DFT_N2, DFT_N2), BF16)] * 2,
        compiler_params=_params(("parallel", "arbitrary")),
        name="fft_stage_b_fwd",
    )(ar, ai, plan.f_cos, plan.f_sin, plan.tw_cos[:, None, :], plan.tw_sin[:, None, :])


def _fft_b_conv_kernel(ar_ref, ai_ref, hr_ref, hi_ref, hb0_ref, fr_ref, fi_ref,
                       twr_ref, twi_ref, tcr_ref, tci_ref, cr_ref, ci_ref,
                       gr_ref, gi_ref, gtr_ref, gti_ref, kr_ref, ki_ref):
    @pl.when(pl.program_id(1) == 0)
    def _():
        fr = fr_ref[...]
        fi = fi_ref[...]
        gr, gi = _twiddled_dft(fr, fi, twr_ref[0], twi_ref[0])
        gr_ref[...] = gr
        gi_ref[...] = gi
        gtr, gti = _twiddled_dft(fr, fi, tcr_ref[0], tci_ref[0])
        gtr_ref[...] = gtr
        gti_ref[...] = gti
        kr_ref[...] = hr_ref[0, 0] + hr_ref[1, 0] - hb0_ref[...]
        ki_ref[...] = hi_ref[0, 0] - hi_ref[1, 0]

    gr = gr_ref[...]
    gi = gi_ref[...]
    ar = ar_ref[0, 0].astype(BF16)
    ai = ai_ref[0, 0].astype(BF16)
    br = _dot(gr, ar) - _dot(gi, ai)
    bi = _dot(gr, ai) + _dot(gi, ar)
    kr = kr_ref[...]
    ki = ki_ref[...]
    pr = (br * kr - bi * ki).astype(BF16)
    pi = (br * ki + bi * kr).astype(BF16)
    gtr = gtr_ref[...]
    gti = gti_ref[...]
    cr_ref[0, 0] = _dot(gtr, pr) + _dot(gti, pi)
    ci_ref[0, 0] = _dot(gtr, pi) - _dot(gti, pr)


def fft_stage_b_conv(ar, ai, hr, hi, hb0, plan):
    batch = ar.shape[0]
    blk = pl.BlockSpec((1, 1, DFT_N2, D_MODEL), lambda k, b: (b, k, 0, 0))
    hblk = pl.BlockSpec((2, 1, DFT_N2, D_MODEL), lambda k, b: (0, k, 0, 0))
    cst = pl.BlockSpec((DFT_N2, DFT_N2), lambda k, b: (0, 0))
    tws = pl.BlockSpec((1, 1, DFT_N2), lambda k, b: (k, 0, 0))
    twc = pl.BlockSpec((1, DFT_N2, 1), lambda k, b: (k, 0, 0))
    out = jax.ShapeDtypeStruct(ar.shape, F32)
    return pl.pallas_call(
        _fft_b_conv_kernel,
        grid=(plan.k1n, batch),
        in_specs=[blk, blk, hblk, hblk, pl.BlockSpec((1, D_MODEL), lambda k, b: (0, 0)),
                  cst, cst, tws, tws, twc, twc],
        out_specs=[blk, blk],
        out_shape=[out, out],
        scratch_shapes=[pltpu.VMEM((DFT_N2, DFT_N2), BF16)] * 4
        + [pltpu.VMEM((DFT_N2, D_MODEL), F32)] * 2,
        compiler_params=_params(("parallel", "arbitrary")),
        name="fft_stage_b_conv",
    )(ar, ai, hr, hi, hb0, plan.f_cos, plan.f_sin,
      plan.tw_cos[:, None, :], plan.tw_sin[:, None, :],
      plan.tw_cos[:, :, None], plan.tw_sin[:, :, None])


def _fft_a_inv_kernel(cr_ref, ci_ref, vc_ref, vs_ref, z_ref, x0_ref, skip_ref, y_ref, *, kq):
    cr = cr_ref[0, :, 0].reshape(kq, D_MODEL).astype(BF16)
    ci = ci_ref[0, :, 0].reshape(kq, D_MODEL).astype(BF16)
    conv = _dot(vc_ref[...], cr) + _dot(vs_ref[...], ci)
    shape = z_ref.shape[1:2] + z_ref.shape[3:]
    z = z_ref[0, :, 0]
    y = conv.reshape(shape) + z * skip_ref[...]
    y_ref[0, :, 0] = (y * x0_ref[0, :, 0].astype(F32)).astype(BF16)


def fft_stage_a_inv(cr, ci, z, x0, skip, plan, batch):
    q = plan.kron
    nhi = DFT_N2 // q
    cshape = (batch, plan.k1n, nhi, q, D_MODEL)
    tshape = (batch, plan.r, nhi, q, D_MODEL)
    cspec = pl.BlockSpec((1, plan.k1n, 1, q, D_MODEL), lambda b, h: (b, 0, h, 0, 0))
    tspec = pl.BlockSpec((1, plan.r, 1, q, D_MODEL), lambda b, h: (b, 0, h, 0, 0))
    wspec = pl.BlockSpec(plan.inv_cos.shape, lambda b, h: (0, 0))
    y = pl.pallas_call(
        functools.partial(_fft_a_inv_kernel, kq=plan.k1n * q),
        grid=(batch, nhi),
        in_specs=[cspec, cspec, wspec, wspec, tspec, tspec,
                  pl.BlockSpec((1, D_MODEL), lambda b, h: (0, 0))],
        out_specs=tspec,
        out_shape=jax.ShapeDtypeStruct(tshape, BF16),
        compiler_params=_params(("parallel", "parallel")),
        name="fft_stage_a_inv",
    )(cr.reshape(cshape), ci.reshape(cshape), plan.inv_cos, plan.inv_sin,
      z.reshape(tshape), x0.reshape(tshape), skip[None, :])
    return y.reshape(z.shape)


def hyena_mixer(x, mod, norm_g, p, batch, seq_len):
    z, x0 = hyena_in(x, mod, norm_g, p["w_in"], p["conv_w"], p["conv_b"], seq_len)
    plan = _FFTPlan(seq_len)
    filt = hyena_filter(seq_len, p["w1"], p["b1"], p["w2"], p["b2"], p["w3"], p["freq"])
    fr, fi = fft_stage_a(filt.reshape(2 * seq_len, D_MODEL), plan, 2)
    hr, hi = fft_stage_b_fwd(fr, fi, plan)
    ar, ai = fft_stage_a(z, plan, batch)
    cr, ci = fft_stage_b_conv(ar, ai, hr, hi, filt[1, 0:1, :], plan)
    y = fft_stage_a_inv(cr, ci, z, x0, p["skip"], plan, batch)
    return proj_residual(y, p["w_out"], x, mod, 2, seq_len)


def _pool_kernel(prev_ref, x_ref, next_ref, mod_ref, g_ref, w_ref, s_ref, o_ref, *, tm, seq_len):
    first, last = _edge_flags(tm, seq_len)
    m = mod_ref[0]
    x = x_ref[...]
    h = _modulate(_ext_rows(prev_ref, x_ref, next_ref), g_ref[...], m[0:1], m[1:2])
    n = tm + 2 * HALO
    row = lax.broadcasted_iota(jnp.int32, (n, 1), 0)
    outside = jnp.logical_or(jnp.logical_and(first, row < HALO),
                             jnp.logical_and(last, row >= HALO + tm))
    h = jnp.where(outside, 0.0, h)
    pos = (pl.program_id(0) * tm) % seq_len + lax.broadcasted_iota(jnp.int32, (tm, 1), 0)
    ys = []
    for gi, win in enumerate(POOL_WINDOWS):
        cols = slice(gi * POOL_GROUP_DIM, (gi + 1) * POOL_GROUP_DIM)
        hg = h[:, cols]
        acc = hg
        span = 1
        while span < win:
            acc = acc + pltpu.roll(acc, span, axis=0)
            span *= 2
        lead = win // 2 - 1
        if lead:
            acc = pltpu.roll(acc, n - lead, axis=0)
        half = win // 2
        cnt = jnp.minimum(pos + half, seq_len) - jnp.maximum(pos - half, 0)
        pooled = acc[HALO:HALO + tm] / cnt.astype(F32) - hg[HALO:HALO + tm]
        ys.append(_dot(pooled.astype(BF16), w_ref[gi]))
    y = jnp.concatenate(ys, axis=1) * s_ref[...]
    o_ref[...] = x + m[2:3] * y


def pool_mixer(x, mod, norm_g, w_group, scale, seq_len):
    n_rows = x.shape[0]
    tm = ROW_TILE
    prev, nxt = _halo_specs(tm, n_rows)
    return pl.pallas_call(
        functools.partial(_pool_kernel, tm=tm, seq_len=seq_len),
        grid=(n_rows // tm,),
        in_specs=[prev, _row_spec(tm), nxt, _mod_spec(tm, seq_len), _const_spec((1, D_MODEL)),
                  _const_spec(w_group.shape), _const_spec((1, D_MODEL))],
        out_specs=_row_spec(tm),
        out_shape=jax.ShapeDtypeStruct((n_rows, D_MODEL), F32),
        compiler_params=_params(("parallel",)),
        name="pool_mixer",
    )(x, x, x, mod, norm_g[None, :], w_group.astype(BF16), scale[None, :])


def _sconv_kernel(prev_ref, x_ref, next_ref, mod_ref, g_ref, w_ref, cw_ref, cb_ref, wo_ref,
                  o_ref, y_ref, *, tm, seq_len):
    first, last = _edge_flags(tm, seq_len)
    m = mod_ref[0]
    h = _modulate(_ext_rows(prev_ref, x_ref, next_ref), g_ref[...], m[0:1], m[1:2]).astype(BF16)
    tn = 256
    for j in range(D_MODEL // tn):
        cols = slice(j * tn, (j + 1) * tn)
        bg = _dot(h, w_ref[:, cols])[HALO:HALO + tm]
        cg = _dot(h, w_ref[:, D_MODEL + j * tn:D_MODEL + (j + 1) * tn])
        hp = _dot(h, w_ref[:, 2 * D_MODEL + j * tn:2 * D_MODEL + (j + 1) * tn])
        y_ref[:, cols] = (bg * _conv3(cg * hp, cw_ref, cb_ref, cols, tm, first, last)).astype(BF16)
    o_ref[...] = x_ref[...] + m[2:3] * _dot(y_ref[...], wo_ref[...])


def sconv_mixer(x, mod, norm_g, w_in, conv_w, conv_b, w_out, seq_len):
    n_rows = x.shape[0]
    tm = ROW_TILE
    prev, nxt = _halo_specs(tm, n_rows)
    return pl.pallas_call(
        functools.partial(_sconv_kernel, tm=tm, seq_len=seq_len),
        grid=(n_rows // tm,),
        in_specs=[prev, _row_spec(tm), nxt, _mod_spec(tm, seq_len), _const_spec((1, D_MODEL)),
                  _const_spec((D_MODEL, 3 * D_MODEL)), _const_spec((3, D_MODEL)),
                  _const_spec((1, D_MODEL)), _const_spec((D_MODEL, D_MODEL))],
        out_specs=_row_spec(tm),
        out_shape=jax.ShapeDtypeStruct((n_rows, D_MODEL), F32),
        scratch_shapes=[pltpu.VMEM((tm, D_MODEL), BF16)],
        compiler_params=_params(("parallel",)),
        name="sconv_mixer",
    )(x, x, x, mod, norm_g[None, :], w_in.astype(BF16), conv_w, conv_b[None, :],
      w_out.astype(BF16))


def _pack_bf16(x):
    w = x.shape[1] // 2
    bits = pltpu.bitcast(x.astype(BF16).astype(F32), jnp.uint32)
    return (bits[:, :w] >> 16) | bits[:, w:]


def _unpack_bf16(p):
    lo = pltpu.bitcast(p << 16, F32)
    hi = pltpu.bitcast(p & jnp.uint32(0xFFFF0000), F32)
    return jnp.concatenate([lo, hi], axis=1)


META_E1, META_E2, META_W1, META_W2, META_R1, META_R2 = range(6)


def _lane_pick(vals, lane_f, idx):
    return jnp.sum(jnp.where(lane_f == idx, vals, 0.0), axis=-1, keepdims=True)


def _router_kernel(x_ref, mod_ref, g_ref, w_ref, b_ref, tri_ref, h_ref, meta_ref, cnt_ref):
    @pl.when(pl.program_id(0) == 0)
    def _():
        cnt_ref[...] = jnp.zeros_like(cnt_ref)

    m = mod_ref[0]
    h = _modulate(x_ref[...], g_ref[...], m[3:4], m[4:5])
    h_ref[...] = _pack_bf16(h)
    lg = _dot3(h, w_ref[...]) + b_ref[...]
    lane = lax.broadcasted_iota(jnp.int32, lg.shape, 1)
    lane_f = lane.astype(F32)
    neg = -jnp.inf
    far = float(LANES)

    def first_argmax(vals):
        top = jnp.max(vals, axis=-1, keepdims=True)
        idx = jnp.min(jnp.where(vals == top, lane_f, far), axis=-1, keepdims=True)
        return top, idx

    is_group = jnp.logical_and(lane >= MOE_N_EXPERTS, lane < MOE_N_EXPERTS + MOE_GROUPS)
    gl = jnp.where(is_group, lg, neg)
    gmax, gidx = first_argmax(gl)
    gsum = jnp.sum(jnp.where(is_group, jnp.exp(gl - gmax), 0.0), axis=-1, keepdims=True)
    g_w = 1.0 / gsum
    grp = gidx - float(MOE_N_EXPERTS)
    lo = grp * MOE_EXPERTS_PER_GROUP
    in_grp = jnp.logical_and(lane_f >= lo, lane_f < lo + MOE_EXPERTS_PER_GROUP)
    el = jnp.where(in_grp, lg, neg)
    v1, i1 = first_argmax(el)
    el2 = jnp.where(lane_f == i1, neg, el)
    v2, i2 = first_argmax(el2)
    e2 = jnp.exp(v2 - v1)
    w1 = 1.0 / (1.0 + e2)
    w2 = e2 * w1
    onehot = jnp.where(jnp.logical_or(lane_f == i1, lane_f == i2), 1.0, 0.0)
    before = _dot(tri_ref[...], onehot.astype(BF16)) + cnt_ref[...]
    cnt_ref[...] += jnp.sum(onehot, axis=0, keepdims=True)
    cols = ((META_E1, i1), (META_E2, i2), (META_W1, w1 * g_w), (META_W2, w2 * g_w),
            (META_R1, _lane_pick(before, lane_f, i1)), (META_R2, _lane_pick(before, lane_f, i2)))
    meta = jnp.zeros_like(lg)
    for col, val in cols:
        meta = jnp.where(lane == col, val, meta)
    meta_ref[...] = meta


def moe_router(x, mod, norm_g, w_group, b_group, w_router, b_router, seq_len):
    n_rows = x.shape[0]
    tm = ROW_TILE
    w = _pad_to(jnp.concatenate([w_router, w_group], axis=1), D_MODEL, LANES)
    b = _pad_to(jnp.concatenate([b_router, b_group])[None, :], 1, LANES)
    tri = jnp.asarray(np.tril(np.ones((tm, tm), np.float32), -1), BF16)
    return pl.pallas_call(
        _router_kernel,
        grid=(n_rows // tm,),
        in_specs=[_row_spec(tm), _mod_spec(tm, seq_len), _const_spec((1, D_MODEL)),
                  _const_spec((D_MODEL, LANES)), _const_spec((1, LANES)), _const_spec((tm, tm))],
        out_specs=[_row_spec(tm, D_MODEL // 2), _row_spec(tm, LANES), _const_spec((1, LANES))],
        out_shape=[jax.ShapeDtypeStruct((n_rows, D_MODEL // 2), jnp.uint32),
                   jax.ShapeDtypeStruct((n_rows, LANES), F32),
                   jax.ShapeDtypeStruct((1, LANES), F32)],
        compiler_params=_params(("arbitrary",)),
        name="moe_router",
    )(x, mod, norm_g[None, :], w, b, tri)


def _slot_kernel(meta_ref, offs_ref, pos_ref):
    meta = meta_ref[...]
    lane = lax.broadcasted_iota(jnp.int32, meta.shape, 1)
    lane_f = lane.astype(F32)
    offs = offs_ref[...]
    p1 = _lane_pick(offs, lane_f, meta[:, META_E1:META_E1 + 1]) + meta[:, META_R1:META_R1 + 1]
    p2 = _lane_pick(offs, lane_f, meta[:, META_E2:META_E2 + 1]) + meta[:, META_R2:META_R2 + 1]
    pos_ref[...] = jnp.where(lane == 0, p1, jnp.where(lane == 1, p2, 0.0)).astype(jnp.int32)


def moe_slots(meta, offsets):
    n_rows = meta.shape[0]
    tm = ROW_TILE
    return pl.pallas_call(
        _slot_kernel,
        grid=(n_rows // tm,),
        in_specs=[_row_spec(tm, LANES), _const_spec((1, LANES))],
        out_specs=_row_spec(tm, LANES),
        out_shape=jax.ShapeDtypeStruct((n_rows, LANES), jnp.int32),
        compiler_params=_params(("parallel",)),
        name="moe_slots",
    )(meta, offsets)


def _expert_kernel(te_ref, nu_ref, nv_ref, xs_ref, wg_ref, wu_ref, wd_ref, o_ref):
    j = pl.program_id(0)

    @pl.when(j < nu_ref[0])
    def _():
        x = _unpack_bf16(xs_ref[...])
        row = lax.broadcasted_iota(jnp.int32, (x.shape[0], 1), 0)
        x = jnp.where(row < nv_ref[j], x, 0.0).astype(BF16)
        a = _dot(x, wg_ref[0])
        u = _dot(x, wu_ref[0])
        hh = (_silu(a) * u).astype(BF16)
        o_ref[...] = _pack_bf16(_dot(hh, wd_ref[0]))


def moe_experts(xs, tile_expert, n_used, n_valid, w_gate, w_up, w_down, tm):
    n_slots, half = xs.shape
    wspec = lambda shape: pl.BlockSpec((1,) + shape, lambda j, te, nu, nv: (te[j], 0, 0))
    row = pl.BlockSpec((tm, half), lambda j, te, nu, nv: (j, 0))
    grid_spec = pltpu.PrefetchScalarGridSpec(
        num_scalar_prefetch=3,
        grid=(n_slots // tm,),
        in_specs=[row, wspec((D_MODEL, MOE_D_FF)), wspec((D_MODEL, MOE_D_FF)),
                  wspec((MOE_D_FF, D_MODEL))],
        out_specs=row,
    )
    return pl.pallas_call(
        _expert_kernel,
        grid_spec=grid_spec,
        out_shape=jax.ShapeDtypeStruct((n_slots, half), jnp.uint32),
        compiler_params=_params(("arbitrary",)),
        name="moe_experts",
    )(tile_expert, n_used, n_valid, xs, w_gate, w_up, w_down)


def _combine_kernel(x_ref, ya_ref, yb_ref, meta_ref, mod_ref, o_ref):
    meta = meta_ref[...]
    y = (meta[:, META_W1:META_W1 + 1] * _unpack_bf16(ya_ref[...])
         + meta[:, META_W2:META_W2 + 1] * _unpack_bf16(yb_ref[...]))
    o_ref[...] = x_ref[...] + mod_ref[0][5:6] * y


def moe_combine(x, ya, yb, meta, mod, seq_len):
    n_rows = x.shape[0]
    tm = ROW_TILE
    half = D_MODEL // 2
    return pl.pallas_call(
        _combine_kernel,
        grid=(n_rows // tm,),
        in_specs=[_row_spec(tm), _row_spec(tm, half), _row_spec(tm, half), _row_spec(tm, LANES),
                  _mod_spec(tm, seq_len)],
        out_specs=_row_spec(tm),
        out_shape=jax.ShapeDtypeStruct((n_rows, D_MODEL), F32),
        compiler_params=_params(("parallel",)),
        name="moe_combine",
    )(x, ya, yb, meta, mod)


MOE_TILE = 512


def hier_moe(x, mod, norm_g, w_group, b_group, w_router, b_router, w_gate, w_up, w_down, seq_len):
    n_rows = x.shape[0]
    tm = MOE_TILE
    hp, meta, counts = moe_router(x, mod, norm_g, w_group, b_group, w_router, b_router, seq_len)
    cnt = counts[0, :MOE_N_EXPERTS].astype(jnp.int32)
    padded = (cnt + tm - 1) // tm * tm
    ends = jnp.cumsum(padded)
    starts = ends - padded
    n_slots = 2 * n_rows + MOE_N_EXPERTS * tm
    tile_start = jnp.arange(n_slots // tm, dtype=jnp.int32) * tm
    tile_expert = jnp.minimum(jnp.searchsorted(ends, tile_start, side="right"),
                              MOE_N_EXPERTS - 1).astype(jnp.int32)
    n_used = (ends[-1:] // tm).astype(jnp.int32)
    n_valid = jnp.clip(starts[tile_expert] + cnt[tile_expert] - tile_start, 0, tm).astype(jnp.int32)
    offsets = _pad_to(starts[None, :].astype(F32), 1, LANES)
    pos = moe_slots(meta, offsets)
    idx0 = pos[:, 0].reshape(n_rows // SC_BLOCK, SC_BLOCK)
    idx1 = pos[:, 1].reshape(n_rows // SC_BLOCK, SC_BLOCK)
    xs = sc_scatter_rows(hp, idx0, idx1, n_slots)
    ys = moe_experts(xs, tile_expert, n_used, n_valid, w_gate, w_up, w_down, tm)
    ya, yb = sc_gather_rows(ys, idx0, idx1)
    return moe_combine(x, ya, yb, meta, mod, seq_len)


SC_CORES = 2
SC_SUBCORES = 16
SC_WORKERS = SC_CORES * SC_SUBCORES
SC_BLOCK = 128


def _sc_mesh():
    return plsc.VectorSubcoreMesh(core_axis_name="c", subcore_axis_name="s")


def _sc_worker():
    return lax.axis_index("s") * SC_CORES + lax.axis_index("c")


def sc_scatter_rows(rows, idx0, idx1, n_slots):
    n_rows, width = rows.shape
    per_worker = n_rows // SC_BLOCK // SC_WORKERS

    @functools.partial(
        pl.kernel, mesh=_sc_mesh(),
        out_type=jax.ShapeDtypeStruct((n_slots, width), rows.dtype),
        scratch_types=[pltpu.VMEM((SC_BLOCK,), jnp.int32), pltpu.VMEM((SC_BLOCK,), jnp.int32),
                       pltpu.VMEM((SC_BLOCK, width), rows.dtype)],
        name="sc_scatter_rows",
    )
    def scatter(rows_hbm, i0_hbm, i1_hbm, out_hbm, i0_v, i1_v, rows_v):
        first = _sc_worker() * per_worker

        @pl.loop(0, per_worker)
        def _(j):
            blk = first + j
            pltpu.sync_copy(i0_hbm.at[blk], i0_v)
            pltpu.sync_copy(i1_hbm.at[blk], i1_v)
            pltpu.sync_copy(rows_hbm.at[pl.ds(blk * SC_BLOCK, SC_BLOCK)], rows_v)
            pltpu.sync_copy(rows_v, out_hbm.at[i0_v])
            pltpu.sync_copy(rows_v, out_hbm.at[i1_v])

    return scatter(rows, idx0, idx1)


def sc_gather_rows(src, idx0, idx1):
    width = src.shape[1]
    n_rows = idx0.shape[0] * SC_BLOCK
    per_worker = n_rows // SC_BLOCK // SC_WORKERS
    out = jax.ShapeDtypeStruct((n_rows, width), src.dtype)

    @functools.partial(
        pl.kernel, mesh=_sc_mesh(), out_type=(out, out),
        scratch_types=[pltpu.VMEM((SC_BLOCK,), jnp.int32), pltpu.VMEM((SC_BLOCK, width), src.dtype)],
        name="sc_gather_rows",
    )
    def gather(src_hbm, i0_hbm, i1_hbm, a_hbm, b_hbm, idx_v, rows_v):
        first = _sc_worker() * per_worker

        @pl.loop(0, per_worker)
        def _(j):
            blk = first + j
            dst = pl.ds(blk * SC_BLOCK, SC_BLOCK)
            pltpu.sync_copy(i0_hbm.at[blk], idx_v)
            pltpu.sync_copy(src_hbm.at[idx_v], rows_v)
            pltpu.sync_copy(rows_v, a_hbm.at[dst])
            pltpu.sync_copy(i1_hbm.at[blk], idx_v)
            pltpu.sync_copy(src_hbm.at[idx_v], rows_v)
            pltpu.sync_copy(rows_v, b_hbm.at[dst])

    return gather(src, idx0, idx1)


def _trunk(x3, mods, p):
    batch, seq_len, _ = x3.shape
    x = x3.reshape(batch * seq_len, D_MODEL)
    for i in range(DEPTH):
        mod = mods[i]
        g1 = p["norm1_g"][i]
        kind = i % 4
        if kind == 0:
            q, k, v = attn_qkv(x, mod, g1, p["attn_wqkv"][0], p["attn_q_norm"][0],
                               p["attn_k_norm"][0], seq_len)
            o = attn_flash(q, k, v, batch, seq_len)
            x = proj_residual(o, p["attn_wo"][0], x, mod, 2, seq_len)
        elif kind == 1:
            hp = {"w_in": p["hy_w_in"][0], "conv_w": p["hy_conv_w"][0], "conv_b": p["hy_conv_b"][0],
                  "w1": p["hy_ffn_w1"][0], "b1": p["hy_ffn_b1"][0], "w2": p["hy_ffn_w2"][0],
                  "b2": p["hy_ffn_b2"][0], "w3": p["hy_ffn_w3"][0], "freq": p["hy_freq"][0],
                  "skip": p["hy_skip"][0], "w_out": p["hy_w_out"][0]}
            x = hyena_mixer(x, mod, g1, hp, batch, seq_len)
        elif kind == 2:
            x = pool_mixer(x, mod, g1, p["pool_w"][0], p["pool_scale"][0], seq_len)
        else:
            x = sconv_mixer(x, mod, g1, p["sc_w_in"][0], p["sc_conv_w"][0], p["sc_conv_b"][0],
                            p["sc_w_out"][0], seq_len)
        x = hier_moe(x, mod, p["norm2_g"][i], p["moe_w_group"][i], p["moe_b_group"][i],
                     p["moe_w_router"][i], p["moe_b_router"][i], p["moe_w_gate_bf16"][i],
                     p["moe_w_up_bf16"][i], p["moe_w_down_bf16"][i], seq_len)
    return x.reshape(batch, seq_len, D_MODEL)


def kernel(x_prompt, x_sample, c_prompt, c_sample, norm1_g, norm2_g, ada_w, ada_b, attn_wqkv, attn_q_norm, attn_k_norm, attn_wo, hy_w_in, hy_conv_w, hy_conv_b, hy_ffn_w1, hy_ffn_b1, hy_ffn_w2, hy_ffn_b2, hy_ffn_w3, hy_freq, hy_skip, hy_w_out, pool_w, pool_scale, sc_w_in, sc_conv_w, sc_conv_b, sc_w_out, moe_w_group, moe_b_group, moe_w_router, moe_b_router, moe_w_gate, moe_w_up, moe_w_down):
    p = dict(norm1_g=norm1_g, norm2_g=norm2_g, attn_wqkv=attn_wqkv, attn_q_norm=attn_q_norm,
             attn_k_norm=attn_k_norm, attn_wo=attn_wo, hy_w_in=hy_w_in, hy_conv_w=hy_conv_w,
             hy_conv_b=hy_conv_b, hy_ffn_w1=hy_ffn_w1, hy_ffn_b1=hy_ffn_b1, hy_ffn_w2=hy_ffn_w2,
             hy_ffn_b2=hy_ffn_b2, hy_ffn_w3=hy_ffn_w3, hy_freq=hy_freq, hy_skip=hy_skip,
             hy_w_out=hy_w_out, pool_w=pool_w, pool_scale=pool_scale, sc_w_in=sc_w_in,
             sc_conv_w=sc_conv_w, sc_conv_b=sc_conv_b, sc_w_out=sc_w_out, moe_w_group=moe_w_group,
             moe_b_group=moe_b_group, moe_w_router=moe_w_router, moe_b_router=moe_b_router,
             moe_w_gate_bf16=moe_w_gate.astype(BF16), moe_w_up_bf16=moe_w_up.astype(BF16),
             moe_w_down_bf16=moe_w_down.astype(BF16))
    nb = c_prompt.shape[0]
    ns = c_sample.shape[0]
    rows = -(-(nb + ns) // HALO) * HALO
    c_all = jnp.pad(jnp.concatenate([c_prompt, c_sample], axis=0), ((0, rows - nb - ns), (0, 0)))
    mod = ada_mod(c_all, ada_w, ada_b).reshape(DEPTH, rows, 6, D_MODEL)
    mods_prompt = [mod[i, :nb] for i in range(DEPTH)]
    mods_sample = [mod[i, nb:nb + ns] for i in range(DEPTH)]
    return _trunk(x_prompt, mods_prompt, p), _trunk(x_sample, mods_sample, p)
```

```python
import functools
import math

import jax
import jax.numpy as jnp
import numpy as np
from jax import lax
from jax.experimental import pallas as pl
from jax.experimental.pallas import tpu as pltpu
from jax.experimental.pallas import tpu_sc as plsc

F32 = jnp.float32
BF16 = jnp.bfloat16

D_MODEL = 1024
DEPTH = 4
EPS = 1e-6
GRID_W = 64
HEAD_DIM = 64
N_HEADS = 16
N_KV_HEADS = 4
Q_PER_KV = 4
ROPE_THETA = 10000.0
ROPE_FREQS = 16
HY_EMB_DIM = 33
HY_BANDS = 16
HY_FILTER_WIDTH = 64
HY_FAST_DECAY = 0.3
HY_SLOW_DECAY = 1.5
HY_TARGET = 1e-2
POOL_WINDOWS = (2, 4, 8, 16)
POOL_GROUP_DIM = 256
MOE_GROUPS = 4
MOE_EXPERTS_PER_GROUP = 8
MOE_N_EXPERTS = 32
MOE_D_FF = 256

LANES = 128
HALO = 8
DFT_N2 = 256
VMEM_LIMIT = 56 * 1024 * 1024

ROW_TILE = 512


def _params(sem):
    return pltpu.CompilerParams(dimension_semantics=sem, vmem_limit_bytes=VMEM_LIMIT)


def _dot(a, b):
    return jnp.dot(a, b, preferred_element_type=F32)


def _split(a):
    hi = a.astype(BF16)
    lo = (a - hi.astype(F32)).astype(BF16)
    return hi, lo


def _dot3(a, b):
    ah, al = _split(a)
    bh, bl = _split(b)
    return _dot(ah, bh) + (_dot(ah, bl) + _dot(al, bh))


def _modulate(x, g, shift, scale):
    ms = jnp.mean(x * x, axis=-1, keepdims=True)
    return x * lax.rsqrt(ms + EPS) * g * (1.0 + scale) + shift


def _silu(x):
    return x * (1.0 / (1.0 + jnp.exp(-x)))


def _ada_kernel(c_ref, w_ref, b_ref, o_ref):
    c = c_ref[...]
    o_ref[0] = _dot3(_silu(c), w_ref[0]) + b_ref[0]


def ada_mod(c_all, ada_w, ada_b):
    rows = c_all.shape[0]
    n = ada_w.shape[2]
    tn = 1536
    return pl.pallas_call(
        _ada_kernel,
        grid=(DEPTH, n // tn),
        in_specs=[
            pl.BlockSpec((rows, D_MODEL), lambda l, j: (0, 0)),
            pl.BlockSpec((1, D_MODEL, tn), lambda l, j: (l, 0, j)),
            pl.BlockSpec((1, 1, tn), lambda l, j: (l, 0, j)),
        ],
        out_specs=pl.BlockSpec((1, rows, tn), lambda l, j: (l, 0, j)),
        out_shape=jax.ShapeDtypeStruct((DEPTH, rows, n), F32),
        compiler_params=_params(("parallel", "parallel")),
        name="ada_mod",
    )(c_all, ada_w, ada_b.reshape(DEPTH, 1, n))


def _row_spec(tm, width=D_MODEL):
    return pl.BlockSpec((tm, width), lambda i: (i, 0))


def _mod_spec(tm, seq_len):
    return pl.BlockSpec((1, 6, D_MODEL), lambda i: ((i * tm) // seq_len, 0, 0))


def _const_spec(shape):
    nd = len(shape)
    return pl.BlockSpec(shape, lambda i: (0,) * nd)


def _halo_specs(tm, n_rows):
    per = tm // HALO
    last = n_rows // HALO - 1
    prev = pl.BlockSpec((HALO, D_MODEL), lambda i: (jnp.maximum(i * per - 1, 0), 0))
    nxt = pl.BlockSpec((HALO, D_MODEL), lambda i: (jnp.minimum((i + 1) * per, last), 0))
    return prev, nxt


def _edge_flags(tm, seq_len):
    i = pl.program_id(0)
    per_seq = seq_len // tm
    pos = i % per_seq
    return pos == 0, pos == per_seq - 1


def _ext_rows(prev_ref, x_ref, next_ref):
    return jnp.concatenate([prev_ref[...], x_ref[...], next_ref[...]], axis=0)


def _shift_rows(u, tm):
    n = u.shape[0]
    up = pltpu.roll(u, 1, axis=0)[HALO:HALO + tm]
    dn = pltpu.roll(u, n - 1, axis=0)[HALO:HALO + tm]
    return up, u[HALO:HALO + tm], dn


def _conv3(u, w_ref, b_ref, cols, tm, first, last):
    up, mid, dn = _shift_rows(u, tm)
    row = lax.broadcasted_iota(jnp.int32, (tm, 1), 0)
    up = jnp.where(jnp.logical_and(first, row == 0), 0.0, up)
    dn = jnp.where(jnp.logical_and(last, row == tm - 1), 0.0, dn)
    w = w_ref[:, cols]
    return up * w[0:1] + mid * w[1:2] + dn * w[2:3] + b_ref[:, cols]


def _norm_rope(t, gain, headmean, cos, sin_signed):
    width = t.shape[1]
    ms = _dot((t * t).astype(BF16), headmean[:width, :width])
    y = t * lax.rsqrt(ms + EPS) * gain
    lane = lax.broadcasted_iota(jnp.int32, y.shape, 1)
    first = (lane % 32) < ROPE_FREQS
    partner = jnp.where(first, pltpu.roll(y, width - ROPE_FREQS, axis=1),
                        pltpu.roll(y, ROPE_FREQS, axis=1))
    reps = width // LANES
    return y * jnp.tile(cos, (1, reps)) + partner * jnp.tile(sin_signed, (1, reps))


def _qkv_kernel(x_ref, mod_ref, g_ref, w_ref, qg_ref, kg_ref, hm_ref, cos_ref, sin_ref,
                q_ref, k_ref, v_ref):
    m = mod_ref[0]
    h = _modulate(x_ref[...], g_ref[...], m[0:1], m[1:2]).astype(BF16)
    qkv = _dot(h, w_ref[...])
    nq = N_HEADS * HEAD_DIM
    nk = N_KV_HEADS * HEAD_DIM
    cos = cos_ref[...]
    sin = sin_ref[...]
    hm = hm_ref[...]
    q = _norm_rope(qkv[:, :nq], qg_ref[...], hm, cos, sin)
    k = _norm_rope(qkv[:, nq:nq + nk], kg_ref[...], hm, cos, sin)
    v = qkv[:, nq + nk:]
    q_ref[...] = q.astype(BF16)
    ones = jnp.ones((v.shape[0], HEAD_DIM), F32)
    for g in range(N_KV_HEADS):
        sl = slice(g * HEAD_DIM, (g + 1) * HEAD_DIM)
        k_ref[g] = k[:, sl].astype(BF16)
        v_ref[g] = jnp.concatenate([v[:, sl], ones], axis=1).astype(BF16)


def _rope_tables(seq_len):
    rows = seq_len // GRID_W
    r = jnp.broadcast_to(jnp.arange(rows)[:, None], (rows, GRID_W)).reshape(-1)
    c = jnp.broadcast_to(jnp.arange(GRID_W)[None, :], (rows, GRID_W)).reshape(-1)
    inv_freq = ROPE_THETA ** (-jnp.arange(ROPE_FREQS, dtype=F32) / ROPE_FREQS)
    pos = jnp.stack([r, c], axis=-1).astype(F32)
    ang = pos[:, :, None] * inv_freq[None, None, :]
    cos = jnp.cos(ang)
    sin = jnp.sin(ang)
    cos64 = jnp.concatenate([cos, cos], axis=-1).reshape(seq_len, HEAD_DIM)
    sin64 = jnp.concatenate([-sin, sin], axis=-1).reshape(seq_len, HEAD_DIM)
    return jnp.tile(cos64, (1, 2)), jnp.tile(sin64, (1, 2))


def attn_qkv(x, mod, norm_g, wqkv, q_norm, k_norm, seq_len):
    n_rows = x.shape[0]
    tm = ROW_TILE
    nq = N_HEADS * HEAD_DIM
    nk = N_KV_HEADS * HEAD_DIM
    cos, sin = _rope_tables(seq_len)
    qg = jnp.tile(q_norm, N_HEADS)[None, :] * (HEAD_DIM ** -0.5 * math.log2(math.e))
    kg = jnp.tile(k_norm, N_KV_HEADS)[None, :]
    head = np.arange(nq) // HEAD_DIM
    headmean = jnp.asarray((head[:, None] == head[None, :]).astype(np.float32) / HEAD_DIM, BF16)
    per_seq = seq_len // tm
    tab_spec = pl.BlockSpec((tm, LANES), lambda i: (i % per_seq, 0))
    return pl.pallas_call(
        _qkv_kernel,
        grid=(n_rows // tm,),
        in_specs=[
            _row_spec(tm), _mod_spec(tm, seq_len), _const_spec((1, D_MODEL)),
            _const_spec((D_MODEL, nq + 2 * nk)), _const_spec((1, nq)), _const_spec((1, nk)),
            _const_spec((nq, nq)), tab_spec, tab_spec,
        ],
        out_specs=[
            _row_spec(tm, nq),
            pl.BlockSpec((N_KV_HEADS, tm, HEAD_DIM), lambda i: (0, i, 0)),
            pl.BlockSpec((N_KV_HEADS, tm, 2 * HEAD_DIM), lambda i: (0, i, 0)),
        ],
        out_shape=[
            jax.ShapeDtypeStruct((n_rows, nq), BF16),
            jax.ShapeDtypeStruct((N_KV_HEADS, n_rows, HEAD_DIM), BF16),
            jax.ShapeDtypeStruct((N_KV_HEADS, n_rows, 2 * HEAD_DIM), BF16),
        ],
        compiler_params=_params(("parallel",)),
        name="attn_qkv",
    )(x, mod, norm_g[None, :], wqkv.astype(BF16), qg, kg, headmean, cos, sin)


def _flash_kernel(q_ref, k_ref, v_ref, o_ref, *, tq, tk, n_chunks):
    q = q_ref[...]
    qs = jnp.concatenate([q[:, j * HEAD_DIM:(j + 1) * HEAD_DIM] for j in range(Q_PER_KV)], axis=0)
    rows = Q_PER_KV * tq

    def body(c, carry):
        m, acc = carry
        start = pl.multiple_of(c * tk, tk)
        kc = k_ref[0, pl.ds(start, tk), :]
        vc = v_ref[0, pl.ds(start, tk), :]
        s = lax.dot_general(qs, kc, (((1,), (1,)), ((), ())), preferred_element_type=F32)
        m_new = jnp.maximum(m, jnp.max(s, axis=-1, keepdims=True))
        alpha = jnp.exp2(m - m_new)
        p = jnp.exp2(s - m_new)
        acc = acc * alpha + _dot(p.astype(BF16), vc)
        return m_new, acc

    m0 = jnp.full((rows, 1), -jnp.inf, F32)
    acc0 = jnp.zeros((rows, 2 * HEAD_DIM), F32)
    _, acc = lax.fori_loop(0, n_chunks, body, (m0, acc0))
    o = acc[:, :HEAD_DIM] / acc[:, HEAD_DIM:HEAD_DIM + 1]
    o_ref[...] = jnp.concatenate([o[j * tq:(j + 1) * tq] for j in range(Q_PER_KV)],
                                 axis=1).astype(BF16)


def attn_flash(q, k, v, batch, seq_len):
    n_rows = q.shape[0]
    tq = 256
    tk = min(seq_len, 2048)
    per_seq = seq_len // tq
    width = Q_PER_KV * HEAD_DIM
    kern = functools.partial(_flash_kernel, tq=tq, tk=tk, n_chunks=seq_len // tk)
    return pl.pallas_call(
        kern,
        grid=(batch, N_KV_HEADS, per_seq),
        in_specs=[
            pl.BlockSpec((tq, width), lambda b, g, i: (b * per_seq + i, g)),
            pl.BlockSpec((1, seq_len, HEAD_DIM), lambda b, g, i: (g, b, 0)),
            pl.BlockSpec((1, seq_len, 2 * HEAD_DIM), lambda b, g, i: (g, b, 0)),
        ],
        out_specs=pl.BlockSpec((tq, width), lambda b, g, i: (b * per_seq + i, g)),
        out_shape=jax.ShapeDtypeStruct((n_rows, N_HEADS * HEAD_DIM), BF16),
        compiler_params=_params(("parallel", "parallel", "parallel")),
        name="attn_flash",
    )(q, k, v)


def _proj_res_kernel(y_ref, w_ref, x_ref, mod_ref, o_ref, *, gate_row):
    gate = mod_ref[0][gate_row:gate_row + 1]
    o_ref[...] = x_ref[...] + gate * _dot(y_ref[...], w_ref[...])


def proj_residual(y, w, x, mod, gate_row, seq_len):
    n_rows = x.shape[0]
    tm = ROW_TILE
    return pl.pallas_call(
        functools.partial(_proj_res_kernel, gate_row=gate_row),
        grid=(n_rows // tm,),
        in_specs=[_row_spec(tm, y.shape[1]), _const_spec(w.shape), _row_spec(tm),
                  _mod_spec(tm, seq_len)],
        out_specs=_row_spec(tm),
        out_shape=jax.ShapeDtypeStruct((n_rows, D_MODEL), F32),
        compiler_params=_params(("parallel",)),
        name="proj_residual",
    )(y, w.astype(BF16), x, mod)


def _hy_in_kernel(prev_ref, x_ref, next_ref, mod_ref, g_ref, w_ref, cw_ref, cb_ref,
                  z_ref, x0_ref, *, tm, seq_len):
    first, last = _edge_flags(tm, seq_len)
    m = mod_ref[0]
    h = _modulate(_ext_rows(prev_ref, x_ref, next_ref), g_ref[...], m[0:1], m[1:2]).astype(BF16)
    tn = 256
    for j in range(D_MODEL // tn):
        part = []
        for s in range(3):
            cols = slice(s * D_MODEL + j * tn, s * D_MODEL + (j + 1) * tn)
            part.append(_conv3(_dot(h, w_ref[:, cols]), cw_ref, cb_ref, cols, tm, first, last))
        out_cols = slice(j * tn, (j + 1) * tn)
        x0_ref[:, out_cols] = part[0].astype(BF16)
        z_ref[:, out_cols] = part[2] * part[1]


def hyena_in(x, mod, norm_g, w_in, conv_w, conv_b, seq_len):
    n_rows = x.shape[0]
    tm = ROW_TILE
    prev, nxt = _halo_specs(tm, n_rows)
    return pl.pallas_call(
        functools.partial(_hy_in_kernel, tm=tm, seq_len=seq_len),
        grid=(n_rows // tm,),
        in_specs=[prev, _row_spec(tm), nxt, _mod_spec(tm, seq_len), _const_spec((1, D_MODEL)),
                  _const_spec((D_MODEL, 3 * D_MODEL)), _const_spec((3, 3 * D_MODEL)),
                  _const_spec((1, 3 * D_MODEL))],
        out_specs=[_row_spec(tm), _row_spec(tm)],
        out_shape=[jax.ShapeDtypeStruct((n_rows, D_MODEL), F32),
                   jax.ShapeDtypeStruct((n_rows, D_MODEL), BF16)],
        compiler_params=_params(("parallel",)),
        name="hyena_in",
    )(x, x, x, mod, norm_g[None, :], w_in.astype(BF16), conv_w, conv_b[None, :])


def _hy_filter_kernel(feat_ref, w1_ref, b1_ref, w2_ref, b2_ref, w3_ref, fr_ref, dl_ref, o_ref):
    feat = feat_ref[...]
    fr = fr_ref[...]
    a = jnp.sin(fr * (_dot3(feat, w1_ref[...]) + b1_ref[...]))
    a = jnp.sin(fr * (_dot3(a, w2_ref[...]) + b2_ref[...]))
    hf = _dot3(a, w3_ref[...])
    decay = jnp.exp(-feat[:, 0:1] * dl_ref[...])
    o_ref[0] = hf[:, :D_MODEL] * decay
    o_ref[1] = hf[:, D_MODEL:] * decay


def _pad_to(a, rows, cols):
    return jnp.pad(a.astype(F32), ((0, rows - a.shape[0]), (0, cols - a.shape[1])))


def hyena_filter(seq_len, w1, b1, w2, b2, w3, freq):
    t = jnp.linspace(0.0, 1.0, seq_len, dtype=F32)[:, None]
    w = 2.0 * math.pi * jnp.arange(seq_len, dtype=F32)[:, None] / seq_len
    f = jnp.linspace(1e-4, HY_BANDS - 1, HY_BANDS, dtype=F32)[None, :]
    feat = _pad_to(jnp.concatenate([t, jnp.cos(f * w), -jnp.sin(f * w)], axis=-1), seq_len, LANES)
    max_decay = math.log(HY_TARGET) / HY_FAST_DECAY
    min_decay = math.log(HY_TARGET) / HY_SLOW_DECAY
    absdelta = jnp.abs(jnp.linspace(min_decay, max_decay, D_MODEL, dtype=F32))[None, :]
    tl = 512
    return pl.pallas_call(
        _hy_filter_kernel,
        grid=(seq_len // tl,),
        in_specs=[_row_spec(tl, LANES), _const_spec((LANES, LANES)), _const_spec((1, LANES)),
                  _const_spec((LANES, LANES)), _const_spec((1, LANES)),
                  _const_spec((LANES, 2 * D_MODEL)), _const_spec((1, LANES)),
                  _const_spec((1, D_MODEL))],
        out_specs=pl.BlockSpec((2, tl, D_MODEL), lambda i: (0, i, 0)),
        out_shape=jax.ShapeDtypeStruct((2, seq_len, D_MODEL), F32),
        compiler_params=_params(("parallel",)),
        name="hyena_filter",
    )(feat, _pad_to(w1, LANES, LANES), _pad_to(b1[None, :], 1, LANES), _pad_to(w2, LANES, LANES),
      _pad_to(b2[None, :], 1, LANES), _pad_to(w3, LANES, 2 * D_MODEL),
      _pad_to(freq[None, :], 1, LANES), absdelta)


class _FFTPlan:
    def __init__(self, seq_len):
        self.n = 2 * seq_len
        self.n1 = self.n // DFT_N2
        self.r = self.n1 // 2
        self.k1n = self.n1 // 2 + 1
        self.kron = max(HALO, LANES // self.r)
        ang = 2.0 * np.pi * np.outer(np.arange(self.k1n), np.arange(self.r)) / self.n1
        eye = np.eye(self.kron)
        self.fwd_cos = jnp.asarray(np.kron(np.cos(ang), eye), BF16)
        self.fwd_sin = jnp.asarray(np.kron(-np.sin(ang), eye), BF16)
        wgt = np.full((self.k1n,), 2.0)
        wgt[0] = wgt[-1] = 1.0
        scale = (wgt / self.n)[None, :]
        self.inv_cos = jnp.asarray(np.kron(np.cos(ang).T * scale, eye), BF16)
        self.inv_sin = jnp.asarray(np.kron(-np.sin(ang).T * scale, eye), BF16)
        a2 = 2.0 * np.pi * np.outer(np.arange(DFT_N2), np.arange(DFT_N2)) / DFT_N2
        self.f_cos = jnp.asarray(np.cos(a2), F32)
        self.f_sin = jnp.asarray(-np.sin(a2), F32)
        tw = 2.0 * np.pi * np.outer(np.arange(self.k1n), np.arange(DFT_N2)) / self.n
        self.tw_cos = jnp.asarray(np.cos(tw), F32)
        self.tw_sin = jnp.asarray(-np.sin(tw), F32)


def _fft_a_kernel(z_ref, wc_ref, ws_ref, ar_ref, ai_ref, *, rq, kq):
    z = z_ref[0, :, 0].reshape(rq, D_MODEL).astype(BF16)
    shape = ar_ref.shape[1:2] + ar_ref.shape[3:]
    ar_ref[0, :, 0] = _dot(wc_ref[...], z).reshape(shape)
    ai_ref[0, :, 0] = _dot(ws_ref[...], z).reshape(shape)


def fft_stage_a(z, plan, batch):
    q = plan.kron
    nhi = DFT_N2 // q
    zv = z.reshape(batch, plan.r, nhi, q, D_MODEL)
    out = jax.ShapeDtypeStruct((batch, plan.k1n, nhi, q, D_MODEL), F32)
    ospec = pl.BlockSpec((1, plan.k1n, 1, q, D_MODEL), lambda b, h: (b, 0, h, 0, 0))
    wspec = pl.BlockSpec(plan.fwd_cos.shape, lambda b, h: (0, 0))
    ar, ai = pl.pallas_call(
        functools.partial(_fft_a_kernel, rq=plan.r * q, kq=plan.k1n * q),
        grid=(batch, nhi),
        in_specs=[pl.BlockSpec((1, plan.r, 1, q, D_MODEL), lambda b, h: (b, 0, h, 0, 0)),
                  wspec, wspec],
        out_specs=[ospec, ospec],
        out_shape=[out, out],
        compiler_params=_params(("parallel", "parallel")),
        name="fft_stage_a",
    )(zv, plan.fwd_cos, plan.fwd_sin)
    shape = (batch, plan.k1n, DFT_N2, D_MODEL)
    return ar.reshape(shape), ai.reshape(shape)


def _twiddled_dft(fr, fi, tr, ti):
    return (fr * tr - fi * ti).astype(BF16), (fr * ti + fi * tr).astype(BF16)


def _fft_b_fwd_kernel(ar_ref, ai_ref, fr_ref, fi_ref, twr_ref, twi_ref, br_ref, bi_ref,
                      gr_ref, gi_ref):
    @pl.when(pl.program_id(1) == 0)
    def _():
        gr, gi = _twiddled_dft(fr_ref[...], fi_ref[...], twr_ref[0], twi_ref[0])
        gr_ref[...] = gr
        gi_ref[...] = gi

    gr = gr_ref[...]
    gi = gi_ref[...]
    ar = ar_ref[0, 0].astype(BF16)
    ai = ai_ref[0, 0].astype(BF16)
    br_ref[0, 0] = _dot(gr, ar) - _dot(gi, ai)
    bi_ref[0, 0] = _dot(gr, ai) + _dot(gi, ar)


def fft_stage_b_fwd(ar, ai, plan):
    batch = ar.shape[0]
    blk = pl.BlockSpec((1, 1, DFT_N2, D_MODEL), lambda k, b: (b, k, 0, 0))
    cst = pl.BlockSpec((DFT_N2, DFT_N2), lambda k, b: (0, 0))
    tws = pl.BlockSpec((1, 1, DFT_N2), lambda k, b: (k, 0, 0))
    out = jax.ShapeDtypeStruct(ar.shape, F32)
    return pl.pallas_call(
        _fft_b_fwd_kernel,
        grid=(plan.k1n, batch),
        in_specs=[blk, blk, cst, cst, tws, tws],
        out_specs=[blk, blk],
        out_shape=[out, out],
        scratch_shapes=[pltpu.VMEM((DFT_N2, DFT_N2), BF16)] * 2,
        compiler_params=_params(("parallel", "arbitrary")),
        name="fft_stage_b_fwd",
    )(ar, ai, plan.f_cos, plan.f_sin, plan.tw_cos[:, None, :], plan.tw_sin[:, None, :])


def _fft_b_conv_kernel(ar_ref, ai_ref, hr_ref, hi_ref, hb0_ref, fr_ref, fi_ref,
                       twr_ref, twi_ref, tcr_ref, tci_ref, cr_ref, ci_ref,
                       gr_ref, gi_ref, gtr_ref, gti_ref, kr_ref, ki_ref):
    @pl.when(pl.program_id(1) == 0)
    def _():
        fr = fr_ref[...]
        fi = fi_ref[...]
        gr, gi = _twiddled_dft(fr, fi, twr_ref[0], twi_ref[0])
        gr_ref[...] = gr
        gi_ref[...] = gi
        gtr, gti = _twiddled_dft(fr, fi, tcr_ref[0], tci_ref[0])
        gtr_ref[...] = gtr
        gti_ref[...] = gti
        kr_ref[...] = hr_ref[0, 0] + hr_ref[1, 0] - hb0_ref[...]
        ki_ref[...] = hi_ref[0, 0] - hi_ref[1, 0]

    gr = gr_ref[...]
    gi = gi_ref[...]
    ar = ar_ref[0, 0].astype(BF16)
    ai = ai_ref[0, 0].astype(BF16)
    br = _dot(gr, ar) - _dot(gi, ai)
    bi = _dot(gr, ai) + _dot(gi, ar)
    kr = kr_ref[...]
    ki = ki_ref[...]
    pr = (br * kr - bi * ki).astype(BF16)
    pi = (br * ki + bi * kr).astype(BF16)
    gtr = gtr_ref[...]
    gti = gti_ref[...]
    cr_ref[0, 0] = _dot(gtr, pr) + _dot(gti, pi)
    ci_ref[0, 0] = _dot(gtr, pi) - _dot(gti, pr)


def fft_stage_b_conv(ar, ai, hr, hi, hb0, plan):
    batch = ar.shape[0]
    blk = pl.BlockSpec((1, 1, DFT_N2, D_MODEL), lambda k, b: (b, k, 0, 0))
    hblk = pl.BlockSpec((2, 1, DFT_N2, D_MODEL), lambda k, b: (0, k, 0, 0))
    cst = pl.BlockSpec((DFT_N2, DFT_N2), lambda k, b: (0, 0))
    tws = pl.BlockSpec((1, 1, DFT_N2), lambda k, b: (k, 0, 0))
    twc = pl.BlockSpec((1, DFT_N2, 1), lambda k, b: (k, 0, 0))
    out = jax.ShapeDtypeStruct(ar.shape, F32)
    return pl.pallas_call(
        _fft_b_conv_kernel,
        grid=(plan.k1n, batch),
        in_specs=[blk, blk, hblk, hblk, pl.BlockSpec((1, D_MODEL), lambda k, b: (0, 0)),
                  cst, cst, tws, tws, twc, twc],
        out_specs=[blk, blk],
        out_shape=[out, out],
        scratch_shapes=[pltpu.VMEM((DFT_N2, DFT_N2), BF16)] * 4
        + [pltpu.VMEM((DFT_N2, D_MODEL), F32)] * 2,
        compiler_params=_params(("parallel", "arbitrary")),
        name="fft_stage_b_conv",
    )(ar, ai, hr, hi, hb0, plan.f_cos, plan.f_sin,
      plan.tw_cos[:, None, :], plan.tw_sin[:, None, :],
      plan.tw_cos[:, :, None], plan.tw_sin[:, :, None])


def _fft_a_inv_kernel(cr_ref, ci_ref, vc_ref, vs_ref, z_ref, x0_ref, skip_ref, y_ref, *, kq):
    cr = cr_ref[0, :, 0].reshape(kq, D_MODEL).astype(BF16)
    ci = ci_ref[0, :, 0].reshape(kq, D_MODEL).astype(BF16)
    conv = _dot(vc_ref[...], cr) + _dot(vs_ref[...], ci)
    shape = z_ref.shape[1:2] + z_ref.shape[3:]
    z = z_ref[0, :, 0]
    y = conv.reshape(shape) + z * skip_ref[...]
    y_ref[0, :, 0] = (y * x0_ref[0, :, 0].astype(F32)).astype(BF16)


def fft_stage_a_inv(cr, ci, z, x0, skip, plan, batch):
    q = plan.kron
    nhi = DFT_N2 // q
    cshape = (batch, plan.k1n, nhi, q, D_MODEL)
    tshape = (batch, plan.r, nhi, q, D_MODEL)
    cspec = pl.BlockSpec((1, plan.k1n, 1, q, D_MODEL), lambda b, h: (b, 0, h, 0, 0))
    tspec = pl.BlockSpec((1, plan.r, 1, q, D_MODEL), lambda b, h: (b, 0, h, 0, 0))
    wspec = pl.BlockSpec(plan.inv_cos.shape, lambda b, h: (0, 0))
    y = pl.pallas_call(
        functools.partial(_fft_a_inv_kernel, kq=plan.k1n * q),
        grid=(batch, nhi),
        in_specs=[cspec, cspec, wspec, wspec, tspec, tspec,
                  pl.BlockSpec((1, D_MODEL), lambda b, h: (0, 0))],
        out_specs=tspec,
        out_shape=jax.ShapeDtypeStruct(tshape, BF16),
        compiler_params=_params(("parallel", "parallel")),
        name="fft_stage_a_inv",
    )(cr.reshape(cshape), ci.reshape(cshape), plan.inv_cos, plan.inv_sin,
      z.reshape(tshape), x0.reshape(tshape), skip[None, :])
    return y.reshape(z.shape)


def hyena_mixer(x, mod, norm_g, p, batch, seq_len):
    z, x0 = hyena_in(x, mod, norm_g, p["w_in"], p["conv_w"], p["conv_b"], seq_len)
    plan = _FFTPlan(seq_len)
    filt = hyena_filter(seq_len, p["w1"], p["b1"], p["w2"], p["b2"], p["w3"], p["freq"])
    fr, fi = fft_stage_a(filt.reshape(2 * seq_len, D_MODEL), plan, 2)
    hr, hi = fft_stage_b_fwd(fr, fi, plan)
    ar, ai = fft_stage_a(z, plan, batch)
    cr, ci = fft_stage_b_conv(ar, ai, hr, hi, filt[1, 0:1, :], plan)
    y = fft_stage_a_inv(cr, ci, z, x0, p["skip"], plan, batch)
    return proj_residual(y, p["w_out"], x, mod, 2, seq_len)


def _pool_kernel(prev_ref, x_ref, next_ref, mod_ref, g_ref, w_ref, s_ref, o_ref, *, tm, seq_len):
    first, last = _edge_flags(tm, seq_len)
    m = mod_ref[0]
    x = x_ref[...]
    h = _modulate(_ext_rows(prev_ref, x_ref, next_ref), g_ref[...], m[0:1], m[1:2])
    n = tm + 2 * HALO
    row = lax.broadcasted_iota(jnp.int32, (n, 1), 0)
    outside = jnp.logical_or(jnp.logical_and(first, row < HALO),
                             jnp.logical_and(last, row >= HALO + tm))
    h = jnp.where(outside, 0.0, h)
    pos = (pl.program_id(0) * tm) % seq_len + lax.broadcasted_iota(jnp.int32, (tm, 1), 0)
    ys = []
    for gi, win in enumerate(POOL_WINDOWS):
        cols = slice(gi * POOL_GROUP_DIM, (gi + 1) * POOL_GROUP_DIM)
        hg = h[:, cols]
        acc = hg
        span = 1
        while span < win:
            acc = acc + pltpu.roll(acc, span, axis=0)
            span *= 2
        lead = win // 2 - 1
        if lead:
            acc = pltpu.roll(acc, n - lead, axis=0)
        half = win // 2
        cnt = jnp.minimum(pos + half, seq_len) - jnp.maximum(pos - half, 0)
        pooled = acc[HALO:HALO + tm] / cnt.astype(F32) - hg[HALO:HALO + tm]
        ys.append(_dot(pooled.astype(BF16), w_ref[gi]))
    y = jnp.concatenate(ys, axis=1) * s_ref[...]
    o_ref[...] = x + m[2:3] * y


def pool_mixer(x, mod, norm_g, w_group, scale, seq_len):
    n_rows = x.shape[0]
    tm = ROW_TILE
    prev, nxt = _halo_specs(tm, n_rows)
    return pl.pallas_call(
        functools.partial(_pool_kernel, tm=tm, seq_len=seq_len),
        grid=(n_rows // tm,),
        in_specs=[prev, _row_spec(tm), nxt, _mod_spec(tm, seq_len), _const_spec((1, D_MODEL)),
                  _const_spec(w_group.shape), _const_spec((1, D_MODEL))],
        out_specs=_row_spec(tm),
        out_shape=jax.ShapeDtypeStruct((n_rows, D_MODEL), F32),
        compiler_params=_params(("parallel",)),
        name="pool_mixer",
    )(x, x, x, mod, norm_g[None, :], w_group.astype(BF16), scale[None, :])


def _sconv_kernel(prev_ref, x_ref, next_ref, mod_ref, g_ref, w_ref, cw_ref, cb_ref, wo_ref,
                  o_ref, y_ref, *, tm, seq_len):
    first, last = _edge_flags(tm, seq_len)
    m = mod_ref[0]
    h = _modulate(_ext_rows(prev_ref, x_ref, next_ref), g_ref[...], m[0:1], m[1:2]).astype(BF16)
    tn = 256
    for j in range(D_MODEL // tn):
        cols = slice(j * tn, (j + 1) * tn)
        bg = _dot(h, w_ref[:, cols])[HALO:HALO + tm]
        cg = _dot(h, w_ref[:, D_MODEL + j * tn:D_MODEL + (j + 1) * tn])
        hp = _dot(h, w_ref[:, 2 * D_MODEL + j * tn:2 * D_MODEL + (j + 1) * tn])
        y_ref[:, cols] = (bg * _conv3(cg * hp, cw_ref, cb_ref, cols, tm, first, last)).astype(BF16)
    o_ref[...] = x_ref[...] + m[2:3] * _dot(y_ref[...], wo_ref[...])


def sconv_mixer(x, mod, norm_g, w_in, conv_w, conv_b, w_out, seq_len):
    n_rows = x.shape[0]
    tm = ROW_TILE
    prev, nxt = _halo_specs(tm, n_rows)
    return pl.pallas_call(
        functools.partial(_sconv_kernel, tm=tm, seq_len=seq_len),
        grid=(n_rows // tm,),
        in_specs=[prev, _row_spec(tm), nxt, _mod_spec(tm, seq_len), _const_spec((1, D_MODEL)),
                  _const_spec((D_MODEL, 3 * D_MODEL)), _const_spec((3, D_MODEL)),
                  _const_spec((1, D_MODEL)), _const_spec((D_MODEL, D_MODEL))],
        out_specs=_row_spec(tm),
        out_shape=jax.ShapeDtypeStruct((n_rows, D_MODEL), F32),
        scratch_shapes=[pltpu.VMEM((tm, D_MODEL), BF16)],
        compiler_params=_params(("parallel",)),
        name="sconv_mixer",
    )(x, x, x, mod, norm_g[None, :], w_in.astype(BF16), conv_w, conv_b[None, :],
      w_out.astype(BF16))


def _pack_bf16(x):
    w = x.shape[1] // 2
    bits = pltpu.bitcast(x.astype(BF16).astype(F32), jnp.uint32)
    return (bits[:, :w] >> 16) | bits[:, w:]


def _unpack_bf16(p):
    lo = pltpu.bitcast(p << 16, F32)
    hi = pltpu.bitcast(p & jnp.uint32(0xFFFF0000), F32)
    return jnp.concatenate([lo, hi], axis=1)


META_E1, META_E2, META_W1, META_W2, META_R1, META_R2 = range(6)


def _lane_pick(vals, lane_f, idx):
    return jnp.sum(jnp.where(lane_f == idx, vals, 0.0), axis=-1, keepdims=True)


def _router_kernel(x_ref, mod_ref, g_ref, w_ref, b_ref, tri_ref, h_ref, meta_ref, cnt_ref):
    @pl.when(pl.program_id(0) == 0)
    def _():
        cnt_ref[...] = jnp.zeros_like(cnt_ref)

    m = mod_ref[0]
    h = _modulate(x_ref[...], g_ref[...], m[3:4], m[4:5])
    h_ref[...] = _pack_bf16(h)
    lg = _dot3(h, w_ref[...]) + b_ref[...]
    lane = lax.broadcasted_iota(jnp.int32, lg.shape, 1)
    lane_f = lane.astype(F32)
    neg = -jnp.inf
    far = float(LANES)

    def first_argmax(vals):
        top = jnp.max(vals, axis=-1, keepdims=True)
        idx = jnp.min(jnp.where(vals == top, lane_f, far), axis=-1, keepdims=True)
        return top, idx

    is_group = jnp.logical_and(lane >= MOE_N_EXPERTS, lane < MOE_N_EXPERTS + MOE_GROUPS)
    gl = jnp.where(is_group, lg, neg)
    gmax, gidx = first_argmax(gl)
    gsum = jnp.sum(jnp.where(is_group, jnp.exp(gl - gmax), 0.0), axis=-1, keepdims=True)
    g_w = 1.0 / gsum
    grp = gidx - float(MOE_N_EXPERTS)
    lo = grp * MOE_EXPERTS_PER_GROUP
    in_grp = jnp.logical_and(lane_f >= lo, lane_f < lo + MOE_EXPERTS_PER_GROUP)
    el = jnp.where(in_grp, lg, neg)
    v1, i1 = first_argmax(el)
    el2 = jnp.where(lane_f == i1, neg, el)
    v2, i2 = first_argmax(el2)
    e2 = jnp.exp(v2 - v1)
    w1 = 1.0 / (1.0 + e2)
    w2 = e2 * w1
    onehot = jnp.where(jnp.logical_or(lane_f == i1, lane_f == i2), 1.0, 0.0)
    before = _dot(tri_ref[...], onehot.astype(BF16)) + cnt_ref[...]
    cnt_ref[...] += jnp.sum(onehot, axis=0, keepdims=True)
    cols = ((META_E1, i1), (META_E2, i2), (META_W1, w1 * g_w), (META_W2, w2 * g_w),
            (META_R1, _lane_pick(before, lane_f, i1)), (META_R2, _lane_pick(before, lane_f, i2)))
    meta = jnp.zeros_like(lg)
    for col, val in cols:
        meta = jnp.where(lane == col, val, meta)
    meta_ref[...] = meta


def moe_router(x, mod, norm_g, w_group, b_group, w_router, b_router, seq_len):
    n_rows = x.shape[0]
    tm = ROW_TILE
    w = _pad_to(jnp.concatenate([w_router, w_group], axis=1), D_MODEL, LANES)
    b = _pad_to(jnp.concatenate([b_router, b_group])[None, :], 1, LANES)
    tri = jnp.asarray(np.tril(np.ones((tm, tm), np.float32), -1), BF16)
    return pl.pallas_call(
        _router_kernel,
        grid=(n_rows // tm,),
        in_specs=[_row_spec(tm), _mod_spec(tm, seq_len), _const_spec((1, D_MODEL)),
                  _const_spec((D_MODEL, LANES)), _const_spec((1, LANES)), _const_spec((tm, tm))],
        out_specs=[_row_spec(tm, D_MODEL // 2), _row_spec(tm, LANES), _const_spec((1, LANES))],
        out_shape=[jax.ShapeDtypeStruct((n_rows, D_MODEL // 2), jnp.uint32),
                   jax.ShapeDtypeStruct((n_rows, LANES), F32),
                   jax.ShapeDtypeStruct((1, LANES), F32)],
        compiler_params=_params(("arbitrary",)),
        name="moe_router",
    )(x, mod, norm_g[None, :], w, b, tri)


def _slot_kernel(meta_ref, offs_ref, pos_ref):
    meta = meta_ref[...]
    lane = lax.broadcasted_iota(jnp.int32, meta.shape, 1)
    lane_f = lane.astype(F32)
    offs = offs_ref[...]
    p1 = _lane_pick(offs, lane_f, meta[:, META_E1:META_E1 + 1]) + meta[:, META_R1:META_R1 + 1]
    p2 = _lane_pick(offs, lane_f, meta[:, META_E2:META_E2 + 1]) + meta[:, META_R2:META_R2 + 1]
    pos_ref[...] = jnp.where(lane == 0, p1, jnp.where(lane == 1, p2, 0.0)).astype(jnp.int32)


def moe_slots(meta, offsets):
    n_rows = meta.shape[0]
    tm = ROW_TILE
    return pl.pallas_call(
        _slot_kernel,
        grid=(n_rows // tm,),
        in_specs=[_row_spec(tm, LANES), _const_spec((1, LANES))],
        out_specs=_row_spec(tm, LANES),
        out_shape=jax.ShapeDtypeStruct((n_rows, LANES), jnp.int32),
        compiler_params=_params(("parallel",)),
        name="moe_slots",
    )(meta, offsets)


def _expert_kernel(te_ref, nu_ref, nv_ref, xs_ref, wg_ref, wu_ref, wd_ref, o_ref):
    j = pl.program_id(0)

    @pl.when(j < nu_ref[0])
    def _():
        x = _unpack_bf16(xs_ref[...])
        row = lax.broadcasted_iota(jnp.int32, (x.shape[0], 1), 0)
        x = jnp.where(row < nv_ref[j], x, 0.0).astype(BF16)
        a = _dot(x, wg_ref[0])
        u = _dot(x, wu_ref[0])
        hh = (_silu(a) * u).astype(BF16)
        o_ref[...] = _pack_bf16(_dot(hh, wd_ref[0]))


def moe_experts(xs, tile_expert, n_used, n_valid, w_gate, w_up, w_down, tm):
    n_slots, half = xs.shape
    wspec = lambda shape: pl.BlockSpec((1,) + shape, lambda j, te, nu, nv: (te[j], 0, 0))
    row = pl.BlockSpec((tm, half), lambda j, te, nu, nv: (j, 0))
    grid_spec = pltpu.PrefetchScalarGridSpec(
        num_scalar_prefetch=3,
        grid=(n_slots // tm,),
        in_specs=[row, wspec((D_MODEL, MOE_D_FF)), wspec((D_MODEL, MOE_D_FF)),
                  wspec((MOE_D_FF, D_MODEL))],
        out_specs=row,
    )
    return pl.pallas_call(
        _expert_kernel,
        grid_spec=grid_spec,
        out_shape=jax.ShapeDtypeStruct((n_slots, half), jnp.uint32),
        compiler_params=_params(("arbitrary",)),
        name="moe_experts",
    )(tile_expert, n_used, n_valid, xs, w_gate, w_up, w_down)


def _combine_kernel(x_ref, ya_ref, yb_ref, meta_ref, mod_ref, o_ref):
    meta = meta_ref[...]
    y = (meta[:, META_W1:META_W1 + 1] * _unpack_bf16(ya_ref[...])
         + meta[:, META_W2:META_W2 + 1] * _unpack_bf16(yb_ref[...]))
    o_ref[...] = x_ref[...] + mod_ref[0][5:6] * y


def moe_combine(x, ya, yb, meta, mod, seq_len):
    n_rows = x.shape[0]
    tm = ROW_TILE
    half = D_MODEL // 2
    return pl.pallas_call(
        _combine_kernel,
        grid=(n_rows // tm,),
        in_specs=[_row_spec(tm), _row_spec(tm, half), _row_spec(tm, half), _row_spec(tm, LANES),
                  _mod_spec(tm, seq_len)],
        out_specs=_row_spec(tm),
        out_shape=jax.ShapeDtypeStruct((n_rows, D_MODEL), F32),
        compiler_params=_params(("parallel",)),
        name="moe_combine",
    )(x, ya, yb, meta, mod)


MOE_TILE = 512


def hier_moe(x, mod, norm_g, w_group, b_group, w_router, b_router, w_gate, w_up, w_down, seq_len):
    n_rows = x.shape[0]
    tm = MOE_TILE
    hp, meta, counts = moe_router(x, mod, norm_g, w_group, b_group, w_router, b_router, seq_len)
    cnt = counts[0, :MOE_N_EXPERTS].astype(jnp.int32)
    padded = (cnt + tm - 1) // tm * tm
    ends = jnp.cumsum(padded)
    starts = ends - padded
    n_slots = 2 * n_rows + MOE_N_EXPERTS * tm
    tile_start = jnp.arange(n_slots // tm, dtype=jnp.int32) * tm
    tile_expert = jnp.minimum(jnp.sum(tile_start[:, None] >= ends[None, :], axis=1),
                              MOE_N_EXPERTS - 1).astype(jnp.int32)
    n_used = (ends[-1:] // tm).astype(jnp.int32)
    n_valid = jnp.clip(starts[tile_expert] + cnt[tile_expert] - tile_start, 0, tm).astype(jnp.int32)
    offsets = _pad_to(starts[None, :].astype(F32), 1, LANES)
    pos = moe_slots(meta, offsets)
    idx0 = pos[:, 0].reshape(n_rows // SC_BLOCK, SC_BLOCK)
    idx1 = pos[:, 1].reshape(n_rows // SC_BLOCK, SC_BLOCK)
    xs = sc_scatter_rows(hp, idx0, idx1, n_slots)
    ys = moe_experts(xs, tile_expert, n_used, n_valid, w_gate, w_up, w_down, tm)
    ya, yb = sc_gather_rows(ys, idx0, idx1)
    return moe_combine(x, ya, yb, meta, mod, seq_len)


SC_CORES = 2
SC_SUBCORES = 16
SC_WORKERS = SC_CORES * SC_SUBCORES
SC_BLOCK = 128


def _sc_mesh():
    return plsc.VectorSubcoreMesh(core_axis_name="c", subcore_axis_name="s")


def _sc_worker():
    return lax.axis_index("s") * SC_CORES + lax.axis_index("c")


def sc_scatter_rows(rows, idx0, idx1, n_slots):
    n_rows, width = rows.shape
    per_worker = n_rows // SC_BLOCK // SC_WORKERS

    @functools.partial(
        pl.kernel, mesh=_sc_mesh(),
        out_type=jax.ShapeDtypeStruct((n_slots, width), rows.dtype),
        scratch_types=[pltpu.VMEM((SC_BLOCK,), jnp.int32), pltpu.VMEM((SC_BLOCK,), jnp.int32),
                       pltpu.VMEM((SC_BLOCK, width), rows.dtype)],
        name="sc_scatter_rows",
    )
    def scatter(rows_hbm, i0_hbm, i1_hbm, out_hbm, i0_v, i1_v, rows_v):
        first = _sc_worker() * per_worker

        @pl.loop(0, per_worker)
        def _(j):
            blk = first + j
            pltpu.sync_copy(i0_hbm.at[blk], i0_v)
            pltpu.sync_copy(i1_hbm.at[blk], i1_v)
            pltpu.sync_copy(rows_hbm.at[pl.ds(blk * SC_BLOCK, SC_BLOCK)], rows_v)
            pltpu.sync_copy(rows_v, out_hbm.at[i0_v])
            pltpu.sync_copy(rows_v, out_hbm.at[i1_v])

    return scatter(rows, idx0, idx1)


def sc_gather_rows(src, idx0, idx1):
    width = src.shape[1]
    n_rows = idx0.shape[0] * SC_BLOCK
    per_worker = n_rows // SC_BLOCK // SC_WORKERS
    out = jax.ShapeDtypeStruct((n_rows, width), src.dtype)

    @functools.partial(
        pl.kernel, mesh=_sc_mesh(), out_type=(out, out),
        scratch_types=[pltpu.VMEM((SC_BLOCK,), jnp.int32), pltpu.VMEM((SC_BLOCK, width), src.dtype)],
        name="sc_gather_rows",
    )
    def gather(src_hbm, i0_hbm, i1_hbm, a_hbm, b_hbm, idx_v, rows_v):
        first = _sc_worker() * per_worker

        @pl.loop(0, per_worker)
        def _(j):
            blk = first + j
            dst = pl.ds(blk * SC_BLOCK, SC_BLOCK)
            pltpu.sync_copy(i0_hbm.at[blk], idx_v)
            pltpu.sync_copy(src_hbm.at[idx_v], rows_v)
            pltpu.sync_copy(rows_v, a_hbm.at[dst])
            pltpu.sync_copy(i1_hbm.at[blk], idx_v)
            pltpu.sync_copy(src_hbm.at[idx_v], rows_v)
            pltpu.sync_copy(rows_v, b_hbm.at[dst])

    return gather(src, idx0, idx1)


def _trunk(x3, mods, p):
    batch, seq_len, _ = x3.shape
    x = x3.reshape(batch * seq_len, D_MODEL)
    for i in range(DEPTH):
        mod = mods[i]
        g1 = p["norm1_g"][i]
        kind = i % 4
        if kind == 0:
            q, k, v = attn_qkv(x, mod, g1, p["attn_wqkv"][0], p["attn_q_norm"][0],
                               p["attn_k_norm"][0], seq_len)
            o = attn_flash(q, k, v, batch, seq_len)
            x = proj_residual(o, p["attn_wo"][0], x, mod, 2, seq_len)
        elif kind == 1:
            hp = {"w_in": p["hy_w_in"][0], "conv_w": p["hy_conv_w"][0], "conv_b": p["hy_conv_b"][0],
                  "w1": p["hy_ffn_w1"][0], "b1": p["hy_ffn_b1"][0], "w2": p["hy_ffn_w2"][0],
                  "b2": p["hy_ffn_b2"][0], "w3": p["hy_ffn_w3"][0], "freq": p["hy_freq"][0],
                  "skip": p["hy_skip"][0], "w_out": p["hy_w_out"][0]}
            x = hyena_mixer(x, mod, g1, hp, batch, seq_len)
        elif kind == 2:
            x = pool_mixer(x, mod, g1, p["pool_w"][0], p["pool_scale"][0], seq_len)
        else:
            x = sconv_mixer(x, mod, g1, p["sc_w_in"][0], p["sc_conv_w"][0], p["sc_conv_b"][0],
                            p["sc_w_out"][0], seq_len)
        x = hier_moe(x, mod, p["norm2_g"][i], p["moe_w_group"][i], p["moe_b_group"][i],
                     p["moe_w_router"][i], p["moe_b_router"][i], p["moe_w_gate_bf16"][i],
                     p["moe_w_up_bf16"][i], p["moe_w_down_bf16"][i], seq_len)
    return x.reshape(batch, seq_len, D_MODEL)


def kernel(x_prompt, x_sample, c_prompt, c_sample, norm1_g, norm2_g, ada_w, ada_b, attn_wqkv, attn_q_norm, attn_k_norm, attn_wo, hy_w_in, hy_conv_w, hy_conv_b, hy_ffn_w1, hy_ffn_b1, hy_ffn_w2, hy_ffn_b2, hy_ffn_w3, hy_freq, hy_skip, hy_w_out, pool_w, pool_scale, sc_w_in, sc_conv_w, sc_conv_b, sc_w_out, moe_w_group, moe_b_group, moe_w_router, moe_b_router, moe_w_gate, moe_w_up, moe_w_down):
    p = dict(norm1_g=norm1_g, norm2_g=norm2_g, attn_wqkv=attn_wqkv, attn_q_norm=attn_q_norm,
             attn_k_norm=attn_k_norm, attn_wo=attn_wo, hy_w_in=hy_w_in, hy_conv_w=hy_conv_w,
             hy_conv_b=hy_conv_b, hy_ffn_w1=hy_ffn_w1, hy_ffn_b1=hy_ffn_b1, hy_ffn_w2=hy_ffn_w2,
             hy_ffn_b2=hy_ffn_b2, hy_ffn_w3=hy_ffn_w3, hy_freq=hy_freq, hy_skip=hy_skip,
             hy_w_out=hy_w_out, pool_w=pool_w, pool_scale=pool_scale, sc_w_in=sc_w_in,
             sc_conv_w=sc_conv_w, sc_conv_b=sc_conv_b, sc_w_out=sc_w_out, moe_w_group=moe_w_group,
             moe_b_group=moe_b_group, moe_w_router=moe_w_router, moe_b_router=moe_b_router,
             moe_w_gate_bf16=moe_w_gate.astype(BF16), moe_w_up_bf16=moe_w_up.astype(BF16),
             moe_w_down_bf16=moe_w_down.astype(BF16))
    nb = c_prompt.shape[0]
    ns = c_sample.shape[0]
    rows = -(-(nb + ns) // HALO) * HALO
    c_all = jnp.pad(jnp.concatenate([c_prompt, c_sample], axis=0), ((0, rows - nb - ns), (0, 0)))
    mod = ada_mod(c_all, ada_w, ada_b).reshape(DEPTH, rows, 6, D_MODEL)
    mods_prompt = [mod[i, :nb] for i in range(DEPTH)]
    mods_sample = [mod[i, nb:nb + ns] for i in range(DEPTH)]
    return _trunk(x_prompt, mods_prompt, p), _trunk(x_sample, mods_sample, p)
```

```python
import functools
import math

import jax
import jax.numpy as jnp
import numpy as np
from jax import lax
from jax.experimental import pallas as pl
from jax.experimental.pallas import tpu as pltpu
from jax.experimental.pallas import tpu_sc as plsc

F32 = jnp.float32
BF16 = jnp.bfloat16

D_MODEL = 1024
DEPTH = 4
EPS = 1e-6
GRID_W = 64
HEAD_DIM = 64
N_HEADS = 16
N_KV_HEADS = 4
Q_PER_KV = 4
ROPE_THETA = 10000.0
ROPE_FREQS = 16
HY_EMB_DIM = 33
HY_BANDS = 16
HY_FILTER_WIDTH = 64
HY_FAST_DECAY = 0.3
HY_SLOW_DECAY = 1.5
HY_TARGET = 1e-2
POOL_WINDOWS = (2, 4, 8, 16)
POOL_GROUP_DIM = 256
MOE_GROUPS = 4
MOE_EXPERTS_PER_GROUP = 8
MOE_N_EXPERTS = 32
MOE_D_FF = 256

LANES = 128
HALO = 8
DFT_N2 = 256
VMEM_LIMIT = 56 * 1024 * 1024

ROW_TILE = 512


def _params(sem):
    return pltpu.CompilerParams(dimension_semantics=sem, vmem_limit_bytes=VMEM_LIMIT)


def _dot(a, b):
    return jnp.dot(a, b, preferred_element_type=F32)


def _split(a):
    hi = a.astype(BF16)
    lo = (a - hi.astype(F32)).astype(BF16)
    return hi, lo


def _dot3(a, b):
    ah, al = _split(a)
    bh, bl = _split(b)
    return _dot(ah, bh) + (_dot(ah, bl) + _dot(al, bh))


def _modulate(x, g, shift, scale):
    ms = jnp.mean(x * x, axis=-1, keepdims=True)
    return x * lax.rsqrt(ms + EPS) * g * (1.0 + scale) + shift


def _silu(x):
    return x * (1.0 / (1.0 + jnp.exp(-x)))


def _ada_kernel(c_ref, w_ref, b_ref, o_ref):
    c = c_ref[...]
    o_ref[0] = _dot3(_silu(c), w_ref[0]) + b_ref[0]


def ada_mod(c_all, ada_w, ada_b):
    rows = c_all.shape[0]
    n = ada_w.shape[2]
    tn = 1536
    return pl.pallas_call(
        _ada_kernel,
        grid=(DEPTH, n // tn),
        in_specs=[
            pl.BlockSpec((rows, D_MODEL), lambda l, j: (0, 0)),
            pl.BlockSpec((1, D_MODEL, tn), lambda l, j: (l, 0, j)),
            pl.BlockSpec((1, 1, tn), lambda l, j: (l, 0, j)),
        ],
        out_specs=pl.BlockSpec((1, rows, tn), lambda l, j: (l, 0, j)),
        out_shape=jax.ShapeDtypeStruct((DEPTH, rows, n), F32),
        compiler_params=_params(("parallel", "parallel")),
        name="ada_mod",
    )(c_all, ada_w, ada_b.reshape(DEPTH, 1, n))


def _row_spec(tm, width=D_MODEL):
    return pl.BlockSpec((tm, width), lambda i: (i, 0))


def _mod_spec(tm, seq_len):
    return pl.BlockSpec((1, 6, D_MODEL), lambda i: ((i * tm) // seq_len, 0, 0))


def _const_spec(shape):
    nd = len(shape)
    return pl.BlockSpec(shape, lambda i: (0,) * nd)


def _halo_specs(tm, n_rows):
    per = tm // HALO
    last = n_rows // HALO - 1
    prev = pl.BlockSpec((HALO, D_MODEL), lambda i: (jnp.maximum(i * per - 1, 0), 0))
    nxt = pl.BlockSpec((HALO, D_MODEL), lambda i: (jnp.minimum((i + 1) * per, last), 0))
    return prev, nxt


def _edge_flags(tm, seq_len):
    i = pl.program_id(0)
    per_seq = seq_len // tm
    pos = i % per_seq
    return pos == 0, pos == per_seq - 1


def _ext_rows(prev_ref, x_ref, next_ref):
    return jnp.concatenate([prev_ref[...], x_ref[...], next_ref[...]], axis=0)


def _shift_rows(u, tm):
    n = u.shape[0]
    up = pltpu.roll(u, 1, axis=0)[HALO:HALO + tm]
    dn = pltpu.roll(u, n - 1, axis=0)[HALO:HALO + tm]
    return up, u[HALO:HALO + tm], dn


def _conv3(u, w_ref, b_ref, cols, tm, first, last):
    up, mid, dn = _shift_rows(u, tm)
    row = lax.broadcasted_iota(jnp.int32, (tm, 1), 0)
    up = jnp.where(jnp.logical_and(first, row == 0), 0.0, up)
    dn = jnp.where(jnp.logical_and(last, row == tm - 1), 0.0, dn)
    w = w_ref[:, cols]
    return up * w[0:1] + mid * w[1:2] + dn * w[2:3] + b_ref[:, cols]


def _norm_rope(t, gain, headmean, cos, sin_signed):
    width = t.shape[1]
    ms = _dot((t * t).astype(BF16), headmean[:width, :width])
    y = t * lax.rsqrt(ms + EPS) * gain
    lane = lax.broadcasted_iota(jnp.int32, y.shape, 1)
    first = (lane % 32) < ROPE_FREQS
    partner = jnp.where(first, pltpu.roll(y, width - ROPE_FREQS, axis=1),
                        pltpu.roll(y, ROPE_FREQS, axis=1))
    reps = width // LANES
    return y * jnp.tile(cos, (1, reps)) + partner * jnp.tile(sin_signed, (1, reps))


def _qkv_kernel(x_ref, mod_ref, g_ref, w_ref, qg_ref, kg_ref, hm_ref, cos_ref, sin_ref,
                q_ref, k_ref, v_ref):
    m = mod_ref[0]
    h = _modulate(x_ref[...], g_ref[...], m[0:1], m[1:2]).astype(BF16)
    qkv = _dot(h, w_ref[...])
    nq = N_HEADS * HEAD_DIM
    nk = N_KV_HEADS * HEAD_DIM
    cos = cos_ref[...]
    sin = sin_ref[...]
    hm = hm_ref[...]
    q = _norm_rope(qkv[:, :nq], qg_ref[...], hm, cos, sin)
    k = _norm_rope(qkv[:, nq:nq + nk], kg_ref[...], hm, cos, sin)
    v = qkv[:, nq + nk:]
    q_ref[...] = q.astype(BF16)
    ones = jnp.ones((v.shape[0], HEAD_DIM), F32)
    for g in range(N_KV_HEADS):
        sl = slice(g * HEAD_DIM, (g + 1) * HEAD_DIM)
        k_ref[g] = k[:, sl].astype(BF16)
        v_ref[g] = jnp.concatenate([v[:, sl], ones], axis=1).astype(BF16)


def _rope_tables(seq_len):
    rows = seq_len // GRID_W
    r = jnp.broadcast_to(jnp.arange(rows)[:, None], (rows, GRID_W)).reshape(-1)
    c = jnp.broadcast_to(jnp.arange(GRID_W)[None, :], (rows, GRID_W)).reshape(-1)
    inv_freq = ROPE_THETA ** (-jnp.arange(ROPE_FREQS, dtype=F32) / ROPE_FREQS)
    pos = jnp.stack([r, c], axis=-1).astype(F32)
    ang = pos[:, :, None] * inv_freq[None, None, :]
    cos = jnp.cos(ang)
    sin = jnp.sin(ang)
    cos64 = jnp.concatenate([cos, cos], axis=-1).reshape(seq_len, HEAD_DIM)
    sin64 = jnp.concatenate([-sin, sin], axis=-1).reshape(seq_len, HEAD_DIM)
    return jnp.tile(cos64, (1, 2)), jnp.tile(sin64, (1, 2))


def attn_qkv(x, mod, norm_g, wqkv, q_norm, k_norm, seq_len):
    n_rows = x.shape[0]
    tm = ROW_TILE
    nq = N_HEADS * HEAD_DIM
    nk = N_KV_HEADS * HEAD_DIM
    cos, sin = _rope_tables(seq_len)
    qg = jnp.tile(q_norm, N_HEADS)[None, :] * (HEAD_DIM ** -0.5 * math.log2(math.e))
    kg = jnp.tile(k_norm, N_KV_HEADS)[None, :]
    head = np.arange(nq) // HEAD_DIM
    headmean = jnp.asarray((head[:, None] == head[None, :]).astype(np.float32) / HEAD_DIM, BF16)
    per_seq = seq_len // tm
    tab_spec = pl.BlockSpec((tm, LANES), lambda i: (i % per_seq, 0))
    return pl.pallas_call(
        _qkv_kernel,
        grid=(n_rows // tm,),
        in_specs=[
            _row_spec(tm), _mod_spec(tm, seq_len), _const_spec((1, D_MODEL)),
            _const_spec((D_MODEL, nq + 2 * nk)), _const_spec((1, nq)), _const_spec((1, nk)),
            _const_spec((nq, nq)), tab_spec, tab_spec,
        ],
        out_specs=[
            _row_spec(tm, nq),
            pl.BlockSpec((N_KV_HEADS, tm, HEAD_DIM), lambda i: (0, i, 0)),
            pl.BlockSpec((N_KV_HEADS, tm, 2 * HEAD_DIM), lambda i: (0, i, 0)),
        ],
        out_shape=[
            jax.ShapeDtypeStruct((n_rows, nq), BF16),
            jax.ShapeDtypeStruct((N_KV_HEADS, n_rows, HEAD_DIM), BF16),
            jax.ShapeDtypeStruct((N_KV_HEADS, n_rows, 2 * HEAD_DIM), BF16),
        ],
        compiler_params=_params(("parallel",)),
        name="attn_qkv",
    )(x, mod, norm_g[None, :], wqkv.astype(BF16), qg, kg, headmean, cos, sin)


def _flash_kernel(q_ref, k_ref, v_ref, o_ref, *, tq, tk, n_chunks):
    q = q_ref[...]
    qs = jnp.concatenate([q[:, j * HEAD_DIM:(j + 1) * HEAD_DIM] for j in range(Q_PER_KV)], axis=0)
    rows = Q_PER_KV * tq

    def body(c, carry):
        m, acc = carry
        start = pl.multiple_of(c * tk, tk)
        kc = k_ref[0, pl.ds(start, tk), :]
        vc = v_ref[0, pl.ds(start, tk), :]
        s = lax.dot_general(qs, kc, (((1,), (1,)), ((), ())), preferred_element_type=F32)
        m_new = jnp.maximum(m, jnp.max(s, axis=-1, keepdims=True))
        alpha = jnp.exp2(m - m_new)
        p = jnp.exp2(s - m_new)
        acc = acc * alpha + _dot(p.astype(BF16), vc)
        return m_new, acc

    m0 = jnp.full((rows, 1), -jnp.inf, F32)
    acc0 = jnp.zeros((rows, 2 * HEAD_DIM), F32)
    _, acc = lax.fori_loop(0, n_chunks, body, (m0, acc0))
    o = acc[:, :HEAD_DIM] / acc[:, HEAD_DIM:HEAD_DIM + 1]
    o_ref[...] = jnp.concatenate([o[j * tq:(j + 1) * tq] for j in range(Q_PER_KV)],
                                 axis=1).astype(BF16)


def attn_flash(q, k, v, batch, seq_len):
    n_rows = q.shape[0]
    tq = 256
    tk = min(seq_len, 2048)
    per_seq = seq_len // tq
    width = Q_PER_KV * HEAD_DIM
    kern = functools.partial(_flash_kernel, tq=tq, tk=tk, n_chunks=seq_len // tk)
    return pl.pallas_call(
        kern,
        grid=(batch, N_KV_HEADS, per_seq),
        in_specs=[
            pl.BlockSpec((tq, width), lambda b, g, i: (b * per_seq + i, g)),
            pl.BlockSpec((1, seq_len, HEAD_DIM), lambda b, g, i: (g, b, 0)),
            pl.BlockSpec((1, seq_len, 2 * HEAD_DIM), lambda b, g, i: (g, b, 0)),
        ],
        out_specs=pl.BlockSpec((tq, width), lambda b, g, i: (b * per_seq + i, g)),
        out_shape=jax.ShapeDtypeStruct((n_rows, N_HEADS * HEAD_DIM), BF16),
        compiler_params=_params(("parallel", "parallel", "parallel")),
        name="attn_flash",
    )(q, k, v)


def _proj_res_kernel(y_ref, w_ref, x_ref, mod_ref, o_ref, *, gate_row):
    gate = mod_ref[0][gate_row:gate_row + 1]
    o_ref[...] = x_ref[...] + gate * _dot(y_ref[...], w_ref[...])


def proj_residual(y, w, x, mod, gate_row, seq_len):
    n_rows = x.shape[0]
    tm = ROW_TILE
    return pl.pallas_call(
        functools.partial(_proj_res_kernel, gate_row=gate_row),
        grid=(n_rows // tm,),
        in_specs=[_row_spec(tm, y.shape[1]), _const_spec(w.shape), _row_spec(tm),
                  _mod_spec(tm, seq_len)],
        out_specs=_row_spec(tm),
        out_shape=jax.ShapeDtypeStruct((n_rows, D_MODEL), F32),
        compiler_params=_params(("parallel",)),
        name="proj_residual",
    )(y, w.astype(BF16), x, mod)


def _hy_in_kernel(prev_ref, x_ref, next_ref, mod_ref, g_ref, w_ref, cw_ref, cb_ref,
                  z_ref, x0_ref, *, tm, seq_len):
    first, last = _edge_flags(tm, seq_len)
    m = mod_ref[0]
    h = _modulate(_ext_rows(prev_ref, x_ref, next_ref), g_ref[...], m[0:1], m[1:2]).astype(BF16)
    tn = 256
    for j in range(D_MODEL // tn):
        part = []
        for s in range(3):
            cols = slice(s * D_MODEL + j * tn, s * D_MODEL + (j + 1) * tn)
            part.append(_conv3(_dot(h, w_ref[:, cols]), cw_ref, cb_ref, cols, tm, first, last))
        out_cols = slice(j * tn, (j + 1) * tn)
        x0_ref[:, out_cols] = part[0].astype(BF16)
        z_ref[:, out_cols] = part[2] * part[1]


def hyena_in(x, mod, norm_g, w_in, conv_w, conv_b, seq_len):
    n_rows = x.shape[0]
    tm = ROW_TILE
    prev, nxt = _halo_specs(tm, n_rows)
    return pl.pallas_call(
        functools.partial(_hy_in_kernel, tm=tm, seq_len=seq_len),
        grid=(n_rows // tm,),
        in_specs=[prev, _row_spec(tm), nxt, _mod_spec(tm, seq_len), _const_spec((1, D_MODEL)),
                  _const_spec((D_MODEL, 3 * D_MODEL)), _const_spec((3, 3 * D_MODEL)),
                  _const_spec((1, 3 * D_MODEL))],
        out_specs=[_row_spec(tm), _row_spec(tm)],
        out_shape=[jax.ShapeDtypeStruct((n_rows, D_MODEL), F32),
                   jax.ShapeDtypeStruct((n_rows, D_MODEL), BF16)],
        compiler_params=_params(("parallel",)),
        name="hyena_in",
    )(x, x, x, mod, norm_g[None, :], w_in.astype(BF16), conv_w, conv_b[None, :])


def _hy_filter_kernel(feat_ref, w1_ref, b1_ref, w2_ref, b2_ref, w3_ref, fr_ref, dl_ref, o_ref):
    feat = feat_ref[...]
    fr = fr_ref[...]
    a = jnp.sin(fr * (_dot3(feat, w1_ref[...]) + b1_ref[...]))
    a = jnp.sin(fr * (_dot3(a, w2_ref[...]) + b2_ref[...]))
    hf = _dot3(a, w3_ref[...])
    decay = jnp.exp(-feat[:, 0:1] * dl_ref[...])
    o_ref[0] = hf[:, :D_MODEL] * decay
    o_ref[1] = hf[:, D_MODEL:] * decay


def _pad_to(a, rows, cols):
    return jnp.pad(a.astype(F32), ((0, rows - a.shape[0]), (0, cols - a.shape[1])))


def hyena_filter(seq_len, w1, b1, w2, b2, w3, freq):
    t = jnp.linspace(0.0, 1.0, seq_len, dtype=F32)[:, None]
    w = 2.0 * math.pi * jnp.arange(seq_len, dtype=F32)[:, None] / seq_len
    f = jnp.linspace(1e-4, HY_BANDS - 1, HY_BANDS, dtype=F32)[None, :]
    feat = _pad_to(jnp.concatenate([t, jnp.cos(f * w), -jnp.sin(f * w)], axis=-1), seq_len, LANES)
    max_decay = math.log(HY_TARGET) / HY_FAST_DECAY
    min_decay = math.log(HY_TARGET) / HY_SLOW_DECAY
    absdelta = jnp.abs(jnp.linspace(min_decay, max_decay, D_MODEL, dtype=F32))[None, :]
    tl = 512
    return pl.pallas_call(
        _hy_filter_kernel,
        grid=(seq_len // tl,),
        in_specs=[_row_spec(tl, LANES), _const_spec((LANES, LANES)), _const_spec((1, LANES)),
                  _const_spec((LANES, LANES)), _const_spec((1, LANES)),
                  _const_spec((LANES, 2 * D_MODEL)), _const_spec((1, LANES)),
                  _const_spec((1, D_MODEL))],
        out_specs=pl.BlockSpec((2, tl, D_MODEL), lambda i: (0, i, 0)),
        out_shape=jax.ShapeDtypeStruct((2, seq_len, D_MODEL), F32),
        compiler_params=_params(("parallel",)),
        name="hyena_filter",
    )(feat, _pad_to(w1, LANES, LANES), _pad_to(b1[None, :], 1, LANES), _pad_to(w2, LANES, LANES),
      _pad_to(b2[None, :], 1, LANES), _pad_to(w3, LANES, 2 * D_MODEL),
      _pad_to(freq[None, :], 1, LANES), absdelta)


class _FFTPlan:
    def __init__(self, seq_len):
        self.n = 2 * seq_len
        self.n1 = self.n // DFT_N2
        self.r = self.n1 // 2
        self.k1n = self.n1 // 2 + 1
        self.kron = max(HALO, LANES // self.r)
        ang = 2.0 * np.pi * np.outer(np.arange(self.k1n), np.arange(self.r)) / self.n1
        eye = np.eye(self.kron)
        self.fwd_cos = jnp.asarray(np.kron(np.cos(ang), eye), BF16)
        self.fwd_sin = jnp.asarray(np.kron(-np.sin(ang), eye), BF16)
        wgt = np.full((self.k1n,), 2.0)
        wgt[0] = wgt[-1] = 1.0
        scale = (wgt / self.n)[None, :]
        self.inv_cos = jnp.asarray(np.kron(np.cos(ang).T * scale, eye), BF16)
        self.inv_sin = jnp.asarray(np.kron(-np.sin(ang).T * scale, eye), BF16)
        a2 = 2.0 * np.pi * np.outer(np.arange(DFT_N2), np.arange(DFT_N2)) / DFT_N2
        self.f_cos = jnp.asarray(np.cos(a2), F32)
        self.f_sin = jnp.asarray(-np.sin(a2), F32)
        tw = 2.0 * np.pi * np.outer(np.arange(self.k1n), np.arange(DFT_N2)) / self.n
        self.tw_cos = jnp.asarray(np.cos(tw), F32)
        self.tw_sin = jnp.asarray(-np.sin(tw), F32)


def _fft_a_kernel(z_ref, wc_ref, ws_ref, ar_ref, ai_ref, *, rq, kq):
    z = z_ref[0, :, 0].reshape(rq, D_MODEL).astype(BF16)
    shape = ar_ref.shape[1:2] + ar_ref.shape[3:]
    ar_ref[0, :, 0] = _dot(wc_ref[...], z).reshape(shape).astype(BF16)
    ai_ref[0, :, 0] = _dot(ws_ref[...], z).reshape(shape).astype(BF16)


def fft_stage_a(z, plan, batch):
    q = plan.kron
    nhi = DFT_N2 // q
    zv = z.reshape(batch, plan.r, nhi, q, D_MODEL)
    out = jax.ShapeDtypeStruct((batch, plan.k1n, nhi, q, D_MODEL), BF16)
    ospec = pl.BlockSpec((1, plan.k1n, 1, q, D_MODEL), lambda b, h: (b, 0, h, 0, 0))
    wspec = pl.BlockSpec(plan.fwd_cos.shape, lambda b, h: (0, 0))
    ar, ai = pl.pallas_call(
        functools.partial(_fft_a_kernel, rq=plan.r * q, kq=plan.k1n * q),
        grid=(batch, nhi),
        in_specs=[pl.BlockSpec((1, plan.r, 1, q, D_MODEL), lambda b, h: (b, 0, h, 0, 0)),
                  wspec, wspec],
        out_specs=[ospec, ospec],
        out_shape=[out, out],
        compiler_params=_params(("parallel", "parallel")),
        name="fft_stage_a",
    )(zv, plan.fwd_cos, plan.fwd_sin)
    shape = (batch, plan.k1n, DFT_N2, D_MODEL)
    return ar.reshape(shape), ai.reshape(shape)


def _twiddled_dft(fr, fi, tr, ti):
    return (fr * tr - fi * ti).astype(BF16), (fr * ti + fi * tr).astype(BF16)


def _fft_b_fwd_kernel(ar_ref, ai_ref, fr_ref, fi_ref, twr_ref, twi_ref, br_ref, bi_ref,
                      gr_ref, gi_ref):
    @pl.when(pl.program_id(1) == 0)
    def _():
        gr, gi = _twiddled_dft(fr_ref[...], fi_ref[...], twr_ref[0], twi_ref[0])
        gr_ref[...] = gr
        gi_ref[...] = gi

    gr = gr_ref[...]
    gi = gi_ref[...]
    ar = ar_ref[0, 0].astype(BF16)
    ai = ai_ref[0, 0].astype(BF16)
    br_ref[0, 0] = _dot(gr, ar) - _dot(gi, ai)
    bi_ref[0, 0] = _dot(gr, ai) + _dot(gi, ar)


def fft_stage_b_fwd(ar, ai, plan):
    batch = ar.shape[0]
    blk = pl.BlockSpec((1, 1, DFT_N2, D_MODEL), lambda k, b: (b, k, 0, 0))
    cst = pl.BlockSpec((DFT_N2, DFT_N2), lambda k, b: (0, 0))
    tws = pl.BlockSpec((1, 1, DFT_N2), lambda k, b: (k, 0, 0))
    out = jax.ShapeDtypeStruct(ar.shape, F32)
    return pl.pallas_call(
        _fft_b_fwd_kernel,
        grid=(plan.k1n, batch),
        in_specs=[blk, blk, cst, cst, tws, tws],
        out_specs=[blk, blk],
        out_shape=[out, out],
        scratch_shapes=[pltpu.VMEM((DFT_N2, DFT_N2), BF16)] * 2,
        compiler_params=_params(("parallel", "arbitrary")),
        name="fft_stage_b_fwd",
    )(ar, ai, plan.f_cos, plan.f_sin, plan.tw_cos[:, None, :], plan.tw_sin[:, None, :])


def _fft_b_conv_kernel(ar_ref, ai_ref, hr_ref, hi_ref, hb0_ref, fr_ref, fi_ref,
                       twr_ref, twi_ref, tcr_ref, tci_ref, cr_ref, ci_ref,
                       gr_ref, gi_ref, gtr_ref, gti_ref, kr_ref, ki_ref):
    @pl.when(pl.program_id(1) == 0)
    def _():
        fr = fr_ref[...]
        fi = fi_ref[...]
        gr, gi = _twiddled_dft(fr, fi, twr_ref[0], twi_ref[0])
        gr_ref[...] = gr
        gi_ref[...] = gi
        gtr, gti = _twiddled_dft(fr, fi, tcr_ref[0], tci_ref[0])
        gtr_ref[...] = gtr
        gti_ref[...] = gti
        kr_ref[...] = hr_ref[0, 0] + hr_ref[1, 0] - hb0_ref[...]
        ki_ref[...] = hi_ref[0, 0] - hi_ref[1, 0]

    gr = gr_ref[...]
    gi = gi_ref[...]
    ar = ar_ref[0, 0].astype(BF16)
    ai = ai_ref[0, 0].astype(BF16)
    br = _dot(gr, ar) - _dot(gi, ai)
    bi = _dot(gr, ai) + _dot(gi, ar)
    kr = kr_ref[...]
    ki = ki_ref[...]
    pr = (br * kr - bi * ki).astype(BF16)
    pi = (br * ki + bi * kr).astype(BF16)
    gtr = gtr_ref[...]
    gti = gti_ref[...]
    cr_ref[0, 0] = (_dot(gtr, pr) + _dot(gti, pi)).astype(BF16)
    ci_ref[0, 0] = (_dot(gtr, pi) - _dot(gti, pr)).astype(BF16)


def fft_stage_b_conv(ar, ai, hr, hi, hb0, plan):
    batch = ar.shape[0]
    blk = pl.BlockSpec((1, 1, DFT_N2, D_MODEL), lambda k, b: (b, k, 0, 0))
    hblk = pl.BlockSpec((2, 1, DFT_N2, D_MODEL), lambda k, b: (0, k, 0, 0))
    cst = pl.BlockSpec((DFT_N2, DFT_N2), lambda k, b: (0, 0))
    tws = pl.BlockSpec((1, 1, DFT_N2), lambda k, b: (k, 0, 0))
    twc = pl.BlockSpec((1, DFT_N2, 1), lambda k, b: (k, 0, 0))
    out = jax.ShapeDtypeStruct(ar.shape, BF16)
    return pl.pallas_call(
        _fft_b_conv_kernel,
        grid=(plan.k1n, batch),
        in_specs=[blk, blk, hblk, hblk, pl.BlockSpec((1, D_MODEL), lambda k, b: (0, 0)),
                  cst, cst, tws, tws, twc, twc],
        out_specs=[blk, blk],
        out_shape=[out, out],
        scratch_shapes=[pltpu.VMEM((DFT_N2, DFT_N2), BF16)] * 4
        + [pltpu.VMEM((DFT_N2, D_MODEL), F32)] * 2,
        compiler_params=_params(("parallel", "arbitrary")),
        name="fft_stage_b_conv",
    )(ar, ai, hr, hi, hb0, plan.f_cos, plan.f_sin,
      plan.tw_cos[:, None, :], plan.tw_sin[:, None, :],
      plan.tw_cos[:, :, None], plan.tw_sin[:, :, None])


def _fft_a_inv_kernel(cr_ref, ci_ref, vc_ref, vs_ref, z_ref, x0_ref, skip_ref, y_ref, *, kq):
    cr = cr_ref[0, :, 0].reshape(kq, D_MODEL).astype(BF16)
    ci = ci_ref[0, :, 0].reshape(kq, D_MODEL).astype(BF16)
    conv = _dot(vc_ref[...], cr) + _dot(vs_ref[...], ci)
    shape = z_ref.shape[1:2] + z_ref.shape[3:]
    z = z_ref[0, :, 0]
    y = conv.reshape(shape) + z * skip_ref[...]
    y_ref[0, :, 0] = (y * x0_ref[0, :, 0].astype(F32)).astype(BF16)


def fft_stage_a_inv(cr, ci, z, x0, skip, plan, batch):
    q = plan.kron
    nhi = DFT_N2 // q
    cshape = (batch, plan.k1n, nhi, q, D_MODEL)
    tshape = (batch, plan.r, nhi, q, D_MODEL)
    cspec = pl.BlockSpec((1, plan.k1n, 1, q, D_MODEL), lambda b, h: (b, 0, h, 0, 0))
    tspec = pl.BlockSpec((1, plan.r, 1, q, D_MODEL), lambda b, h: (b, 0, h, 0, 0))
    wspec = pl.BlockSpec(plan.inv_cos.shape, lambda b, h: (0, 0))
    y = pl.pallas_call(
        functools.partial(_fft_a_inv_kernel, kq=plan.k1n * q),
        grid=(batch, nhi),
        in_specs=[cspec, cspec, wspec, wspec, tspec, tspec,
                  pl.BlockSpec((1, D_MODEL), lambda b, h: (0, 0))],
        out_specs=tspec,
        out_shape=jax.ShapeDtypeStruct(tshape, BF16),
        compiler_params=_params(("parallel", "parallel")),
        name="fft_stage_a_inv",
    )(cr.reshape(cshape), ci.reshape(cshape), plan.inv_cos, plan.inv_sin,
      z.reshape(tshape), x0.reshape(tshape), skip[None, :])
    return y.reshape(z.shape)


def hyena_mixer(x, mod, norm_g, p, batch, seq_len):
    z, x0 = hyena_in(x, mod, norm_g, p["w_in"], p["conv_w"], p["conv_b"], seq_len)
    plan = _FFTPlan(seq_len)
    filt = hyena_filter(seq_len, p["w1"], p["b1"], p["w2"], p["b2"], p["w3"], p["freq"])
    fr, fi = fft_stage_a(filt.reshape(2 * seq_len, D_MODEL), plan, 2)
    hr, hi = fft_stage_b_fwd(fr, fi, plan)
    ar, ai = fft_stage_a(z, plan, batch)
    cr, ci = fft_stage_b_conv(ar, ai, hr, hi, filt[1, 0:1, :], plan)
    y = fft_stage_a_inv(cr, ci, z, x0, p["skip"], plan, batch)
    return proj_residual(y, p["w_out"], x, mod, 2, seq_len)


def _pool_kernel(prev_ref, x_ref, next_ref, mod_ref, g_ref, w_ref, s_ref, o_ref, *, tm, seq_len):
    first, last = _edge_flags(tm, seq_len)
    m = mod_ref[0]
    x = x_ref[...]
    h = _modulate(_ext_rows(prev_ref, x_ref, next_ref), g_ref[...], m[0:1], m[1:2])
    n = tm + 2 * HALO
    row = lax.broadcasted_iota(jnp.int32, (n, 1), 0)
    outside = jnp.logical_or(jnp.logical_and(first, row < HALO),
                             jnp.logical_and(last, row >= HALO + tm))
    h = jnp.where(outside, 0.0, h)
    pos = (pl.program_id(0) * tm) % seq_len + lax.broadcasted_iota(jnp.int32, (tm, 1), 0)
    ys = []
    for gi, win in enumerate(POOL_WINDOWS):
        cols = slice(gi * POOL_GROUP_DIM, (gi + 1) * POOL_GROUP_DIM)
        hg = h[:, cols]
        acc = hg
        span = 1
        while span < win:
            acc = acc + pltpu.roll(acc, span, axis=0)
            span *= 2
        lead = win // 2 - 1
        if lead:
            acc = pltpu.roll(acc, n - lead, axis=0)
        half = win // 2
        cnt = jnp.minimum(pos + half, seq_len) - jnp.maximum(pos - half, 0)
        pooled = acc[HALO:HALO + tm] / cnt.astype(F32) - hg[HALO:HALO + tm]
        ys.append(_dot(pooled.astype(BF16), w_ref[gi]))
    y = jnp.concatenate(ys, axis=1) * s_ref[...]
    o_ref[...] = x + m[2:3] * y


def pool_mixer(x, mod, norm_g, w_group, scale, seq_len):
    n_rows = x.shape[0]
    tm = ROW_TILE
    prev, nxt = _halo_specs(tm, n_rows)
    return pl.pallas_call(
        functools.partial(_pool_kernel, tm=tm, seq_len=seq_len),
        grid=(n_rows // tm,),
        in_specs=[prev, _row_spec(tm), nxt, _mod_spec(tm, seq_len), _const_spec((1, D_MODEL)),
                  _const_spec(w_group.shape), _const_spec((1, D_MODEL))],
        out_specs=_row_spec(tm),
        out_shape=jax.ShapeDtypeStruct((n_rows, D_MODEL), F32),
        compiler_params=_params(("parallel",)),
        name="pool_mixer",
    )(x, x, x, mod, norm_g[None, :], w_group.astype(BF16), scale[None, :])


def _sconv_kernel(prev_ref, x_ref, next_ref, mod_ref, g_ref, w_ref, cw_ref, cb_ref, wo_ref,
                  o_ref, y_ref, *, tm, seq_len):
    first, last = _edge_flags(tm, seq_len)
    m = mod_ref[0]
    h = _modulate(_ext_rows(prev_ref, x_ref, next_ref), g_ref[...], m[0:1], m[1:2]).astype(BF16)
    tn = 256
    for j in range(D_MODEL // tn):
        cols = slice(j * tn, (j + 1) * tn)
        bg = _dot(h, w_ref[:, cols])[HALO:HALO + tm]
        cg = _dot(h, w_ref[:, D_MODEL + j * tn:D_MODEL + (j + 1) * tn])
        hp = _dot(h, w_ref[:, 2 * D_MODEL + j * tn:2 * D_MODEL + (j + 1) * tn])
        y_ref[:, cols] = (bg * _conv3(cg * hp, cw_ref, cb_ref, cols, tm, first, last)).astype(BF16)
    o_ref[...] = x_ref[...] + m[2:3] * _dot(y_ref[...], wo_ref[...])


def sconv_mixer(x, mod, norm_g, w_in, conv_w, conv_b, w_out, seq_len):
    n_rows = x.shape[0]
    tm = ROW_TILE
    prev, nxt = _halo_specs(tm, n_rows)
    return pl.pallas_call(
        functools.partial(_sconv_kernel, tm=tm, seq_len=seq_len),
        grid=(n_rows // tm,),
        in_specs=[prev, _row_spec(tm), nxt, _mod_spec(tm, seq_len), _const_spec((1, D_MODEL)),
                  _const_spec((D_MODEL, 3 * D_MODEL)), _const_spec((3, D_MODEL)),
                  _const_spec((1, D_MODEL)), _const_spec((D_MODEL, D_MODEL))],
        out_specs=_row_spec(tm),
        out_shape=jax.ShapeDtypeStruct((n_rows, D_MODEL), F32),
        scratch_shapes=[pltpu.VMEM((tm, D_MODEL), BF16)],
        compiler_params=_params(("parallel",)),
        name="sconv_mixer",
    )(x, x, x, mod, norm_g[None, :], w_in.astype(BF16), conv_w, conv_b[None, :],
      w_out.astype(BF16))


def _pack_bf16(x):
    w = x.shape[1] // 2
    bits = pltpu.bitcast(x.astype(BF16).astype(F32), jnp.uint32)
    return (bits[:, :w] >> 16) | bits[:, w:]


def _unpack_bf16(p):
    lo = pltpu.bitcast(p << 16, F32)
    hi = pltpu.bitcast(p & jnp.uint32(0xFFFF0000), F32)
    return jnp.concatenate([lo, hi], axis=1)


META_E1, META_E2, META_W1, META_W2, META_R1, META_R2 = range(6)
META_ROWS = 8


def _router_kernel(x_ref, mod_ref, g_ref, wh_ref, b_ref, tri_ref,
                   h_ref, meta_ref, meta_t_ref, cnt_ref):
    @pl.when(pl.program_id(0) == 0)
    def _():
        cnt_ref[...] = jnp.zeros_like(cnt_ref)

    m = mod_ref[0]
    x = x_ref[...]
    ms = jnp.mean(x * x, axis=-1, keepdims=True)
    h = x * lax.rsqrt(ms + EPS) * (g_ref[...] * (1.0 + m[4:5])) + m[3:4]
    hi = h.astype(BF16)
    hi32 = hi.astype(F32)
    lo = (h - hi32).astype(BF16)
    half = D_MODEL // 2
    bits = pltpu.bitcast(hi32, jnp.uint32)
    h_ref[...] = (bits[:, :half] >> 16) | bits[:, half:]
    part = _dot(hi, wh_ref[...])
    lg = part[:, :LANES] + (part[:, LANES:] + _dot(lo, wh_ref[:, :LANES])) + b_ref[...]
    lgt = lg.T
    tm = lgt.shape[1]
    neg = -jnp.inf
    sub = lax.broadcasted_iota(jnp.int32, (HALO, tm), 0).astype(F32)

    def first_argmax(vals):
        top = jnp.max(vals, axis=0, keepdims=True)
        idx = jnp.min(jnp.where(vals == top, sub, float(HALO)), axis=0, keepdims=True)
        return top, idx

    gl = jnp.where(sub < MOE_GROUPS, lgt[MOE_N_EXPERTS:MOE_N_EXPERTS + HALO], neg)
    gmax, grp = first_argmax(gl)
    g_w = 1.0 / jnp.sum(jnp.exp(gl - gmax), axis=0, keepdims=True)
    el = lgt[:MOE_EXPERTS_PER_GROUP]
    for g in range(1, MOE_GROUPS):
        el = jnp.where(grp == float(g),
                       lgt[g * MOE_EXPERTS_PER_GROUP:(g + 1) * MOE_EXPERTS_PER_GROUP], el)
    v1, i1 = first_argmax(el)
    v2, i2 = first_argmax(jnp.where(sub == i1, neg, el))
    ex = jnp.exp(v2 - v1)
    w1 = 1.0 / (1.0 + ex)
    w2 = ex * w1
    e1 = grp * MOE_EXPERTS_PER_GROUP + i1
    e2 = grp * MOE_EXPERTS_PER_GROUP + i2
    expert = lax.broadcasted_iota(jnp.int32, (MOE_N_EXPERTS, tm), 0).astype(F32)
    onehot = jnp.where(jnp.logical_or(expert == e1, expert == e2), 1.0, 0.0)
    before = _dot(onehot.astype(BF16), tri_ref[...]) + cnt_ref[...]
    cnt_ref[...] += jnp.sum(onehot, axis=1, keepdims=True)
    r1 = jnp.sum(jnp.where(expert == e1, before, 0.0), axis=0, keepdims=True)
    r2 = jnp.sum(jnp.where(expert == e2, before, 0.0), axis=0, keepdims=True)
    fields = ((META_E1, e1), (META_E2, e2), (META_W1, w1 * g_w), (META_W2, w2 * g_w),
              (META_R1, r1), (META_R2, r2))
    field = lax.broadcasted_iota(jnp.int32, (LANES, tm), 0)
    meta_t = jnp.zeros((LANES, tm), F32)
    for row, val in fields:
        meta_t = jnp.where(field == row, val, meta_t)
    meta_t_ref[...] = meta_t[:META_ROWS]
    meta_ref[...] = meta_t.T


def moe_router(x, mod, norm_g, w_group, b_group, w_router, b_router, seq_len):
    n_rows = x.shape[0]
    tm = ROW_TILE
    w = _pad_to(jnp.concatenate([w_router, w_group], axis=1), D_MODEL, LANES)
    wh = w.astype(BF16)
    wl = (w - wh.astype(F32)).astype(BF16)
    whl = jnp.concatenate([wh, wl], axis=1)
    b = _pad_to(jnp.concatenate([b_router, b_group])[None, :], 1, LANES)
    tri = jnp.asarray(np.triu(np.ones((tm, tm), np.float32), 1), BF16)
    return pl.pallas_call(
        _router_kernel,
        grid=(n_rows // tm,),
        in_specs=[_row_spec(tm), _mod_spec(tm, seq_len), _const_spec((1, D_MODEL)),
                  _const_spec((D_MODEL, 2 * LANES)), _const_spec((1, LANES)),
                  _const_spec((tm, tm))],
        out_specs=[_row_spec(tm, D_MODEL // 2), _row_spec(tm, LANES),
                   pl.BlockSpec((META_ROWS, tm), lambda i: (0, i)),
                   _const_spec((MOE_N_EXPERTS, 1))],
        out_shape=[jax.ShapeDtypeStruct((n_rows, D_MODEL // 2), jnp.uint32),
                   jax.ShapeDtypeStruct((n_rows, LANES), F32),
                   jax.ShapeDtypeStruct((META_ROWS, n_rows), F32),
                   jax.ShapeDtypeStruct((MOE_N_EXPERTS, 1), F32)],
        compiler_params=_params(("arbitrary",)),
        name="moe_router",
    )(x, mod, norm_g[None, :], whl, b, tri)


def _slot_kernel(offs_ref, meta_t_ref, pos_ref):
    meta = meta_t_ref[...]
    start = jnp.zeros_like(meta)
    for e in range(MOE_N_EXPERTS):
        start = jnp.where(meta == float(e), offs_ref[e], start)
    shift = META_ROWS - (META_R1 - META_E1)
    pos_ref[...] = (start + pltpu.roll(meta, shift, axis=0)).astype(jnp.int32)


def moe_slots(meta_t, offsets):
    n_rows = meta_t.shape[1]
    tn = min(n_rows, 8192)
    blk = pl.BlockSpec((META_ROWS, tn), lambda i, offs: (0, i))
    return pl.pallas_call(
        _slot_kernel,
        grid_spec=pltpu.PrefetchScalarGridSpec(
            num_scalar_prefetch=1, grid=(n_rows // tn,), in_specs=[blk], out_specs=blk),
        out_shape=jax.ShapeDtypeStruct((META_ROWS, n_rows), jnp.int32),
        compiler_params=_params(("parallel",)),
        name="moe_slots",
    )(offsets, meta_t)


def _expert_kernel(te_ref, nu_ref, nv_ref, xs_ref, wg_ref, wu_ref, wd_ref, o_ref):
    j = pl.program_id(0)

    @pl.when(j < nu_ref[0])
    def _():
        x = _unpack_bf16(xs_ref[...])
        row = lax.broadcasted_iota(jnp.int32, (x.shape[0], 1), 0)
        x = jnp.where(row < nv_ref[j], x, 0.0).astype(BF16)
        a = _dot(x, wg_ref[0])
        u = _dot(x, wu_ref[0])
        hh = (_silu(a) * u).astype(BF16)
        o_ref[...] = _pack_bf16(_dot(hh, wd_ref[0]))


def moe_experts(xs, tile_expert, n_used, n_valid, w_gate, w_up, w_down, tm):
    n_slots, half = xs.shape
    wspec = lambda shape: pl.BlockSpec((1,) + shape, lambda j, te, nu, nv: (te[j], 0, 0))
    row = pl.BlockSpec((tm, half), lambda j, te, nu, nv: (j, 0))
    grid_spec = pltpu.PrefetchScalarGridSpec(
        num_scalar_prefetch=3,
        grid=(n_slots // tm,),
        in_specs=[row, wspec((D_MODEL, MOE_D_FF)), wspec((D_MODEL, MOE_D_FF)),
                  wspec((MOE_D_FF, D_MODEL))],
        out_specs=row,
    )
    return pl.pallas_call(
        _expert_kernel,
        grid_spec=grid_spec,
        out_shape=jax.ShapeDtypeStruct((n_slots, half), jnp.uint32),
        compiler_params=_params(("arbitrary",)),
        name="moe_experts",
    )(tile_expert, n_used, n_valid, xs, w_gate, w_up, w_down)


def _combine_kernel(x_ref, ya_ref, yb_ref, meta_ref, mod_ref, o_ref):
    meta = meta_ref[...]
    y = (meta[:, META_W1:META_W1 + 1] * _unpack_bf16(ya_ref[...])
         + meta[:, META_W2:META_W2 + 1] * _unpack_bf16(yb_ref[...]))
    o_ref[...] = x_ref[...] + mod_ref[0][5:6] * y


def moe_combine(x, ya, yb, meta, mod, seq_len):
    n_rows = x.shape[0]
    tm = ROW_TILE
    half = D_MODEL // 2
    return pl.pallas_call(
        _combine_kernel,
        grid=(n_rows // tm,),
        in_specs=[_row_spec(tm), _row_spec(tm, half), _row_spec(tm, half), _row_spec(tm, LANES),
                  _mod_spec(tm, seq_len)],
        out_specs=_row_spec(tm),
        out_shape=jax.ShapeDtypeStruct((n_rows, D_MODEL), F32),
        compiler_params=_params(("parallel",)),
        name="moe_combine",
    )(x, ya, yb, meta, mod)


MOE_TILE = 512


def hier_moe(x, mod, norm_g, w_group, b_group, w_router, b_router, w_gate, w_up, w_down, seq_len):
    n_rows = x.shape[0]
    tm = MOE_TILE
    hp, meta, meta_t, counts = moe_router(x, mod, norm_g, w_group, b_group, w_router, b_router,
                                          seq_len)
    cnt = counts[:, 0].astype(jnp.int32)
    padded = (cnt + tm - 1) // tm * tm
    ends = jnp.cumsum(padded)
    starts = ends - padded
    n_slots = 2 * n_rows + MOE_N_EXPERTS * tm
    tile_start = jnp.arange(n_slots // tm, dtype=jnp.int32) * tm
    tile_expert = jnp.minimum(jnp.sum(tile_start[:, None] >= ends[None, :], axis=1),
                              MOE_N_EXPERTS - 1).astype(jnp.int32)
    n_used = (ends[-1:] // tm).astype(jnp.int32)
    n_valid = jnp.clip(starts[tile_expert] + cnt[tile_expert] - tile_start, 0, tm).astype(jnp.int32)
    pos = moe_slots(meta_t, starts.astype(F32))
    idx0 = pos[META_E1].reshape(n_rows // SC_BLOCK, SC_BLOCK)
    idx1 = pos[META_E2].reshape(n_rows // SC_BLOCK, SC_BLOCK)
    xs = sc_scatter_rows(hp, idx0, idx1, n_slots)
    ys = moe_experts(xs, tile_expert, n_used, n_valid, w_gate, w_up, w_down, tm)
    ya, yb = sc_gather_rows(ys, idx0, idx1)
    return moe_combine(x, ya, yb, meta, mod, seq_len)


SC_CORES = 2
SC_SUBCORES = 16
SC_WORKERS = SC_CORES * SC_SUBCORES
SC_BLOCK = 128


def _sc_mesh():
    return plsc.VectorSubcoreMesh(core_axis_name="c", subcore_axis_name="s")


def _sc_worker():
    return lax.axis_index("s") * SC_CORES + lax.axis_index("c")


def sc_scatter_rows(rows, idx0, idx1, n_slots):
    n_rows, width = rows.shape
    per_worker = n_rows // SC_BLOCK // SC_WORKERS

    @functools.partial(
        pl.kernel, mesh=_sc_mesh(),
        out_type=jax.ShapeDtypeStruct((n_slots, width), rows.dtype),
        scratch_types=[pltpu.VMEM((SC_BLOCK,), jnp.int32), pltpu.VMEM((SC_BLOCK,), jnp.int32),
                       pltpu.VMEM((SC_BLOCK, width), rows.dtype)],
        name="sc_scatter_rows",
    )
    def scatter(rows_hbm, i0_hbm, i1_hbm, out_hbm, i0_v, i1_v, rows_v):
        first = _sc_worker() * per_worker

        @pl.loop(0, per_worker)
        def _(j):
            blk = first + j
            pltpu.sync_copy(i0_hbm.at[blk], i0_v)
            pltpu.sync_copy(i1_hbm.at[blk], i1_v)
            pltpu.sync_copy(rows_hbm.at[pl.ds(blk * SC_BLOCK, SC_BLOCK)], rows_v)
            pltpu.sync_copy(rows_v, out_hbm.at[i0_v])
            pltpu.sync_copy(rows_v, out_hbm.at[i1_v])

    return scatter(rows, idx0, idx1)


def sc_gather_rows(src, idx0, idx1):
    width = src.shape[1]
    n_rows = idx0.shape[0] * SC_BLOCK
    per_worker = n_rows // SC_BLOCK // SC_WORKERS
    out = jax.ShapeDtypeStruct((n_rows, width), src.dtype)

    @functools.partial(
        pl.kernel, mesh=_sc_mesh(), out_type=(out, out),
        scratch_types=[pltpu.VMEM((SC_BLOCK,), jnp.int32), pltpu.VMEM((SC_BLOCK, width), src.dtype)],
        name="sc_gather_rows",
    )
    def gather(src_hbm, i0_hbm, i1_hbm, a_hbm, b_hbm, idx_v, rows_v):
        first = _sc_worker() * per_worker

        @pl.loop(0, per_worker)
        def _(j):
            blk = first + j
            dst = pl.ds(blk * SC_BLOCK, SC_BLOCK)
            pltpu.sync_copy(i0_hbm.at[blk], idx_v)
            pltpu.sync_copy(src_hbm.at[idx_v], rows_v)
            pltpu.sync_copy(rows_v, a_hbm.at[dst])
            pltpu.sync_copy(i1_hbm.at[blk], idx_v)
            pltpu.sync_copy(src_hbm.at[idx_v], rows_v)
            pltpu.sync_copy(rows_v, b_hbm.at[dst])

    return gather(src, idx0, idx1)


def _trunk(x3, mods, p):
    batch, seq_len, _ = x3.shape
    x = x3.reshape(batch * seq_len, D_MODEL)
    for i in range(DEPTH):
        mod = mods[i]
        g1 = p["norm1_g"][i]
        kind = i % 4
        if kind == 0:
            q, k, v = attn_qkv(x, mod, g1, p["attn_wqkv"][0], p["attn_q_norm"][0],
                               p["attn_k_norm"][0], seq_len)
            o = attn_flash(q, k, v, batch, seq_len)
            x = proj_residual(o, p["attn_wo"][0], x, mod, 2, seq_len)
        elif kind == 1:
            hp = {"w_in": p["hy_w_in"][0], "conv_w": p["hy_conv_w"][0], "conv_b": p["hy_conv_b"][0],
                  "w1": p["hy_ffn_w1"][0], "b1": p["hy_ffn_b1"][0], "w2": p["hy_ffn_w2"][0],
                  "b2": p["hy_ffn_b2"][0], "w3": p["hy_ffn_w3"][0], "freq": p["hy_freq"][0],
                  "skip": p["hy_skip"][0], "w_out": p["hy_w_out"][0]}
            x = hyena_mixer(x, mod, g1, hp, batch, seq_len)
        elif kind == 2:
            x = pool_mixer(x, mod, g1, p["pool_w"][0], p["pool_scale"][0], seq_len)
        else:
            x = sconv_mixer(x, mod, g1, p["sc_w_in"][0], p["sc_conv_w"][0], p["sc_conv_b"][0],
                            p["sc_w_out"][0], seq_len)
        x = hier_moe(x, mod, p["norm2_g"][i], p["moe_w_group"][i], p["moe_b_group"][i],
                     p["moe_w_router"][i], p["moe_b_router"][i], p["moe_w_gate_bf16"][i],
                     p["moe_w_up_bf16"][i], p["moe_w_down_bf16"][i], seq_len)
    return x.reshape(batch, seq_len, D_MODEL)


def kernel(x_prompt, x_sample, c_prompt, c_sample, norm1_g, norm2_g, ada_w, ada_b, attn_wqkv, attn_q_norm, attn_k_norm, attn_wo, hy_w_in, hy_conv_w, hy_conv_b, hy_ffn_w1, hy_ffn_b1, hy_ffn_w2, hy_ffn_b2, hy_ffn_w3, hy_freq, hy_skip, hy_w_out, pool_w, pool_scale, sc_w_in, sc_conv_w, sc_conv_b, sc_w_out, moe_w_group, moe_b_group, moe_w_router, moe_b_router, moe_w_gate, moe_w_up, moe_w_down):
    p = dict(norm1_g=norm1_g, norm2_g=norm2_g, attn_wqkv=attn_wqkv, attn_q_norm=attn_q_norm,
             attn_k_norm=attn_k_norm, attn_wo=attn_wo, hy_w_in=hy_w_in, hy_conv_w=hy_conv_w,
             hy_conv_b=hy_conv_b, hy_ffn_w1=hy_ffn_w1, hy_ffn_b1=hy_ffn_b1, hy_ffn_w2=hy_ffn_w2,
             hy_ffn_b2=hy_ffn_b2, hy_ffn_w3=hy_ffn_w3, hy_freq=hy_freq, hy_skip=hy_skip,
             hy_w_out=hy_w_out, pool_w=pool_w, pool_scale=pool_scale, sc_w_in=sc_w_in,
             sc_conv_w=sc_conv_w, sc_conv_b=sc_conv_b, sc_w_out=sc_w_out, moe_w_group=moe_w_group,
             moe_b_group=moe_b_group, moe_w_router=moe_w_router, moe_b_router=moe_b_router,
             moe_w_gate_bf16=moe_w_gate.astype(BF16), moe_w_up_bf16=moe_w_up.astype(BF16),
             moe_w_down_bf16=moe_w_down.astype(BF16))
    nb = c_prompt.shape[0]
    ns = c_sample.shape[0]
    rows = -(-(nb + ns) // HALO) * HALO
    c_all = jnp.pad(jnp.concatenate([c_prompt, c_sample], axis=0), ((0, rows - nb - ns), (0, 0)))
    mod = ada_mod(c_all, ada_w, ada_b).reshape(DEPTH, rows, 6, D_MODEL)
    mods_prompt = [mod[i, :nb] for i in range(DEPTH)]
    mods_sample = [mod[i, nb:nb + ns] for i in range(DEPTH)]
    return _trunk(x_prompt, mods_prompt, p), _trunk(x_sample, mods_sample, p)
```

```python
import functools
import math

import jax
import jax.numpy as jnp
import numpy as np
from jax import lax
from jax.experimental import pallas as pl
from jax.experimental.pallas import tpu as pltpu
from jax.experimental.pallas import tpu_sc as plsc

F32 = jnp.float32
BF16 = jnp.bfloat16

D_MODEL = 1024
DEPTH = 4
EPS = 1e-6
GRID_W = 64
HEAD_DIM = 64
N_HEADS = 16
N_KV_HEADS = 4
Q_PER_KV = 4
ROPE_THETA = 10000.0
ROPE_FREQS = 16
HY_EMB_DIM = 33
HY_BANDS = 16
HY_FILTER_WIDTH = 64
HY_FAST_DECAY = 0.3
HY_SLOW_DECAY = 1.5
HY_TARGET = 1e-2
POOL_WINDOWS = (2, 4, 8, 16)
POOL_GROUP_DIM = 256
MOE_GROUPS = 4
MOE_EXPERTS_PER_GROUP = 8
MOE_N_EXPERTS = 32
MOE_D_FF = 256

LANES = 128
HALO = 8
DFT_N2 = 256
VMEM_LIMIT = 56 * 1024 * 1024

ROW_TILE = 512
STREAM_TILE = 1024


def _params(sem):
    return pltpu.CompilerParams(dimension_semantics=sem, vmem_limit_bytes=VMEM_LIMIT)


def _dot(a, b):
    return jnp.dot(a, b, preferred_element_type=F32)


def _split(a):
    hi = a.astype(BF16)
    lo = (a - hi.astype(F32)).astype(BF16)
    return hi, lo


def _dot3(a, b):
    ah, al = _split(a)
    bh, bl = _split(b)
    return _dot(ah, bh) + (_dot(ah, bl) + _dot(al, bh))


def _modulate(x, g, shift, scale):
    ms = jnp.mean(x * x, axis=-1, keepdims=True)
    return x * lax.rsqrt(ms + EPS) * g * (1.0 + scale) + shift


def _silu(x):
    return x * (1.0 / (1.0 + jnp.exp(-x)))


def _ada_kernel(c_ref, w_ref, b_ref, o_ref):
    c = c_ref[...]
    o_ref[0] = _dot3(_silu(c), w_ref[0]) + b_ref[0]


def ada_mod(c_all, ada_w, ada_b):
    rows = c_all.shape[0]
    n = ada_w.shape[2]
    tn = 1536
    return pl.pallas_call(
        _ada_kernel,
        grid=(DEPTH, n // tn),
        in_specs=[
            pl.BlockSpec((rows, D_MODEL), lambda l, j: (0, 0)),
            pl.BlockSpec((1, D_MODEL, tn), lambda l, j: (l, 0, j)),
            pl.BlockSpec((1, 1, tn), lambda l, j: (l, 0, j)),
        ],
        out_specs=pl.BlockSpec((1, rows, tn), lambda l, j: (l, 0, j)),
        out_shape=jax.ShapeDtypeStruct((DEPTH, rows, n), F32),
        compiler_params=_params(("parallel", "parallel")),
        name="ada_mod",
    )(c_all, ada_w, ada_b.reshape(DEPTH, 1, n))


def _row_spec(tm, width=D_MODEL):
    return pl.BlockSpec((tm, width), lambda i: (i, 0))


def _mod_spec(tm, seq_len):
    return pl.BlockSpec((1, 6, D_MODEL), lambda i: ((i * tm) // seq_len, 0, 0))


def _const_spec(shape):
    nd = len(shape)
    return pl.BlockSpec(shape, lambda i: (0,) * nd)


def _halo_specs(tm, n_rows):
    per = tm // HALO
    last = n_rows // HALO - 1
    prev = pl.BlockSpec((HALO, D_MODEL), lambda i: (jnp.maximum(i * per - 1, 0), 0))
    nxt = pl.BlockSpec((HALO, D_MODEL), lambda i: (jnp.minimum((i + 1) * per, last), 0))
    return prev, nxt


def _edge_flags(tm, seq_len):
    i = pl.program_id(0)
    per_seq = seq_len // tm
    pos = i % per_seq
    return pos == 0, pos == per_seq - 1


def _ext_rows(prev_ref, x_ref, next_ref):
    return jnp.concatenate([prev_ref[...], x_ref[...], next_ref[...]], axis=0)


def _shift_rows(u, tm):
    n = u.shape[0]
    up = pltpu.roll(u, 1, axis=0)[HALO:HALO + tm]
    dn = pltpu.roll(u, n - 1, axis=0)[HALO:HALO + tm]
    return up, u[HALO:HALO + tm], dn


def _conv3(u, w_ref, b_ref, cols, tm, first, last):
    up, mid, dn = _shift_rows(u, tm)
    row = lax.broadcasted_iota(jnp.int32, (tm, 1), 0)
    up = jnp.where(jnp.logical_and(first, row == 0), 0.0, up)
    dn = jnp.where(jnp.logical_and(last, row == tm - 1), 0.0, dn)
    w = w_ref[:, cols]
    return up * w[0:1] + mid * w[1:2] + dn * w[2:3] + b_ref[:, cols]


def _norm_rope(t, gain, headmean, cos, sin_signed):
    width = t.shape[1]
    ms = _dot((t * t).astype(BF16), headmean[:width, :width])
    y = t * lax.rsqrt(ms + EPS) * gain
    lane = lax.broadcasted_iota(jnp.int32, y.shape, 1)
    first = (lane % 32) < ROPE_FREQS
    partner = jnp.where(first, pltpu.roll(y, width - ROPE_FREQS, axis=1),
                        pltpu.roll(y, ROPE_FREQS, axis=1))
    reps = width // LANES
    return y * jnp.tile(cos, (1, reps)) + partner * jnp.tile(sin_signed, (1, reps))


def _qkv_kernel(x_ref, mod_ref, g_ref, w_ref, qg_ref, kg_ref, hm_ref, cos_ref, sin_ref,
                q_ref, k_ref, v_ref):
    m = mod_ref[0]
    h = _modulate(x_ref[...], g_ref[...], m[0:1], m[1:2]).astype(BF16)
    qkv = _dot(h, w_ref[...])
    nq = N_HEADS * HEAD_DIM
    nk = N_KV_HEADS * HEAD_DIM
    cos = cos_ref[...]
    sin = sin_ref[...]
    hm = hm_ref[...]
    q = _norm_rope(qkv[:, :nq], qg_ref[...], hm, cos, sin)
    k = _norm_rope(qkv[:, nq:nq + nk], kg_ref[...], hm, cos, sin)
    v = qkv[:, nq + nk:]
    q_ref[...] = q.astype(BF16)
    ones = jnp.ones((v.shape[0], HEAD_DIM), F32)
    for g in range(N_KV_HEADS):
        sl = slice(g * HEAD_DIM, (g + 1) * HEAD_DIM)
        k_ref[g] = k[:, sl].astype(BF16)
        v_ref[g] = jnp.concatenate([v[:, sl], ones], axis=1).astype(BF16)


def _rope_tables(seq_len):
    rows = seq_len // GRID_W
    r = jnp.broadcast_to(jnp.arange(rows)[:, None], (rows, GRID_W)).reshape(-1)
    c = jnp.broadcast_to(jnp.arange(GRID_W)[None, :], (rows, GRID_W)).reshape(-1)
    inv_freq = ROPE_THETA ** (-jnp.arange(ROPE_FREQS, dtype=F32) / ROPE_FREQS)
    pos = jnp.stack([r, c], axis=-1).astype(F32)
    ang = pos[:, :, None] * inv_freq[None, None, :]
    cos = jnp.cos(ang)
    sin = jnp.sin(ang)
    cos64 = jnp.concatenate([cos, cos], axis=-1).reshape(seq_len, HEAD_DIM)
    sin64 = jnp.concatenate([-sin, sin], axis=-1).reshape(seq_len, HEAD_DIM)
    return jnp.tile(cos64, (1, 2)), jnp.tile(sin64, (1, 2))


def attn_qkv(x, mod, norm_g, wqkv, q_norm, k_norm, seq_len):
    n_rows = x.shape[0]
    tm = ROW_TILE
    nq = N_HEADS * HEAD_DIM
    nk = N_KV_HEADS * HEAD_DIM
    cos, sin = _rope_tables(seq_len)
    qg = jnp.tile(q_norm, N_HEADS)[None, :] * (HEAD_DIM ** -0.5 * math.log2(math.e))
    kg = jnp.tile(k_norm, N_KV_HEADS)[None, :]
    head = np.arange(nq) // HEAD_DIM
    headmean = jnp.asarray((head[:, None] == head[None, :]).astype(np.float32) / HEAD_DIM, BF16)
    per_seq = seq_len // tm
    tab_spec = pl.BlockSpec((tm, LANES), lambda i: (i % per_seq, 0))
    return pl.pallas_call(
        _qkv_kernel,
        grid=(n_rows // tm,),
        in_specs=[
            _row_spec(tm), _mod_spec(tm, seq_len), _const_spec((1, D_MODEL)),
            _const_spec((D_MODEL, nq + 2 * nk)), _const_spec((1, nq)), _const_spec((1, nk)),
            _const_spec((nq, nq)), tab_spec, tab_spec,
        ],
        out_specs=[
            _row_spec(tm, nq),
            pl.BlockSpec((N_KV_HEADS, tm, HEAD_DIM), lambda i: (0, i, 0)),
            pl.BlockSpec((N_KV_HEADS, tm, 2 * HEAD_DIM), lambda i: (0, i, 0)),
        ],
        out_shape=[
            jax.ShapeDtypeStruct((n_rows, nq), BF16),
            jax.ShapeDtypeStruct((N_KV_HEADS, n_rows, HEAD_DIM), BF16),
            jax.ShapeDtypeStruct((N_KV_HEADS, n_rows, 2 * HEAD_DIM), BF16),
        ],
        compiler_params=_params(("parallel",)),
        name="attn_qkv",
    )(x, mod, norm_g[None, :], wqkv.astype(BF16), qg, kg, headmean, cos, sin)


def _flash_kernel(q_ref, k_ref, v_ref, o_ref, *, tq, tk, n_chunks):
    q = q_ref[...]
    qs = jnp.concatenate([q[:, j * HEAD_DIM:(j + 1) * HEAD_DIM] for j in range(Q_PER_KV)], axis=0)
    rows = Q_PER_KV * tq

    def body(c, carry):
        m, acc = carry
        start = pl.multiple_of(c * tk, tk)
        kc = k_ref[0, pl.ds(start, tk), :]
        vc = v_ref[0, pl.ds(start, tk), :]
        s = lax.dot_general(qs, kc, (((1,), (1,)), ((), ())), preferred_element_type=F32)
        m_new = jnp.maximum(m, jnp.max(s, axis=-1, keepdims=True))
        alpha = jnp.exp2(m - m_new)
        p = jnp.exp2(s - m_new)
        acc = acc * alpha + _dot(p.astype(BF16), vc)
        return m_new, acc

    m0 = jnp.full((rows, 1), -jnp.inf, F32)
    acc0 = jnp.zeros((rows, 2 * HEAD_DIM), F32)
    _, acc = lax.fori_loop(0, n_chunks, body, (m0, acc0))
    o = acc[:, :HEAD_DIM] / acc[:, HEAD_DIM:HEAD_DIM + 1]
    o_ref[...] = jnp.concatenate([o[j * tq:(j + 1) * tq] for j in range(Q_PER_KV)],
                                 axis=1).astype(BF16)


def attn_flash(q, k, v, batch, seq_len):
    n_rows = q.shape[0]
    tq = 256
    tk = min(seq_len, 2048)
    per_seq = seq_len // tq
    width = Q_PER_KV * HEAD_DIM
    kern = functools.partial(_flash_kernel, tq=tq, tk=tk, n_chunks=seq_len // tk)
    return pl.pallas_call(
        kern,
        grid=(batch, N_KV_HEADS, per_seq),
        in_specs=[
            pl.BlockSpec((tq, width), lambda b, g, i: (b * per_seq + i, g)),
            pl.BlockSpec((1, seq_len, HEAD_DIM), lambda b, g, i: (g, b, 0)),
            pl.BlockSpec((1, seq_len, 2 * HEAD_DIM), lambda b, g, i: (g, b, 0)),
        ],
        out_specs=pl.BlockSpec((tq, width), lambda b, g, i: (b * per_seq + i, g)),
        out_shape=jax.ShapeDtypeStruct((n_rows, N_HEADS * HEAD_DIM), BF16),
        compiler_params=_params(("parallel", "parallel", "parallel")),
        name="attn_flash",
    )(q, k, v)


def _proj_res_kernel(y_ref, w_ref, x_ref, mod_ref, o_ref, *, gate_row):
    gate = mod_ref[0][gate_row:gate_row + 1]
    o_ref[...] = x_ref[...] + gate * _dot(y_ref[...], w_ref[...])


def proj_residual(y, w, x, mod, gate_row, seq_len):
    n_rows = x.shape[0]
    tm = STREAM_TILE
    return pl.pallas_call(
        functools.partial(_proj_res_kernel, gate_row=gate_row),
        grid=(n_rows // tm,),
        in_specs=[_row_spec(tm, y.shape[1]), _const_spec(w.shape), _row_spec(tm),
                  _mod_spec(tm, seq_len)],
        out_specs=_row_spec(tm),
        out_shape=jax.ShapeDtypeStruct((n_rows, D_MODEL), F32),
        compiler_params=_params(("parallel",)),
        name="proj_residual",
    )(y, w.astype(BF16), x, mod)


def _hy_in_kernel(prev_ref, x_ref, next_ref, mod_ref, g_ref, w_ref, cw_ref, cb_ref,
                  z_ref, x0_ref, *, tm, seq_len):
    first, last = _edge_flags(tm, seq_len)
    m = mod_ref[0]
    h = _modulate(_ext_rows(prev_ref, x_ref, next_ref), g_ref[...], m[0:1], m[1:2]).astype(BF16)
    tn = 256
    for j in range(D_MODEL // tn):
        part = []
        for s in range(3):
            cols = slice(s * D_MODEL + j * tn, s * D_MODEL + (j + 1) * tn)
            part.append(_conv3(_dot(h, w_ref[:, cols]), cw_ref, cb_ref, cols, tm, first, last))
        out_cols = slice(j * tn, (j + 1) * tn)
        x0_ref[:, out_cols] = part[0].astype(BF16)
        z_ref[:, out_cols] = part[2] * part[1]


def hyena_in(x, mod, norm_g, w_in, conv_w, conv_b, seq_len):
    n_rows = x.shape[0]
    tm = ROW_TILE
    prev, nxt = _halo_specs(tm, n_rows)
    return pl.pallas_call(
        functools.partial(_hy_in_kernel, tm=tm, seq_len=seq_len),
        grid=(n_rows // tm,),
        in_specs=[prev, _row_spec(tm), nxt, _mod_spec(tm, seq_len), _const_spec((1, D_MODEL)),
                  _const_spec((D_MODEL, 3 * D_MODEL)), _const_spec((3, 3 * D_MODEL)),
                  _const_spec((1, 3 * D_MODEL))],
        out_specs=[_row_spec(tm), _row_spec(tm)],
        out_shape=[jax.ShapeDtypeStruct((n_rows, D_MODEL), F32),
                   jax.ShapeDtypeStruct((n_rows, D_MODEL), BF16)],
        compiler_params=_params(("parallel",)),
        name="hyena_in",
    )(x, x, x, mod, norm_g[None, :], w_in.astype(BF16), conv_w, conv_b[None, :])


def _hy_filter_kernel(feat_ref, w1_ref, b1_ref, w2_ref, b2_ref, w3_ref, fr_ref, dl_ref, o_ref):
    feat = feat_ref[...]
    fr = fr_ref[...]
    a = jnp.sin(fr * (_dot3(feat, w1_ref[...]) + b1_ref[...]))
    a = jnp.sin(fr * (_dot3(a, w2_ref[...]) + b2_ref[...]))
    hf = _dot3(a, w3_ref[...])
    decay = jnp.exp(-feat[:, 0:1] * dl_ref[...])
    o_ref[0] = hf[:, :D_MODEL] * decay
    o_ref[1] = hf[:, D_MODEL:] * decay


def _pad_to(a, rows, cols):
    return jnp.pad(a.astype(F32), ((0, rows - a.shape[0]), (0, cols - a.shape[1])))


def hyena_filter(seq_len, w1, b1, w2, b2, w3, freq):
    t = jnp.linspace(0.0, 1.0, seq_len, dtype=F32)[:, None]
    w = 2.0 * math.pi * jnp.arange(seq_len, dtype=F32)[:, None] / seq_len
    f = jnp.linspace(1e-4, HY_BANDS - 1, HY_BANDS, dtype=F32)[None, :]
    feat = _pad_to(jnp.concatenate([t, jnp.cos(f * w), -jnp.sin(f * w)], axis=-1), seq_len, LANES)
    max_decay = math.log(HY_TARGET) / HY_FAST_DECAY
    min_decay = math.log(HY_TARGET) / HY_SLOW_DECAY
    absdelta = jnp.abs(jnp.linspace(min_decay, max_decay, D_MODEL, dtype=F32))[None, :]
    tl = 512
    return pl.pallas_call(
        _hy_filter_kernel,
        grid=(seq_len // tl,),
        in_specs=[_row_spec(tl, LANES), _const_spec((LANES, LANES)), _const_spec((1, LANES)),
                  _const_spec((LANES, LANES)), _const_spec((1, LANES)),
                  _const_spec((LANES, 2 * D_MODEL)), _const_spec((1, LANES)),
                  _const_spec((1, D_MODEL))],
        out_specs=pl.BlockSpec((2, tl, D_MODEL), lambda i: (0, i, 0)),
        out_shape=jax.ShapeDtypeStruct((2, seq_len, D_MODEL), F32),
        compiler_params=_params(("parallel",)),
        name="hyena_filter",
    )(feat, _pad_to(w1, LANES, LANES), _pad_to(b1[None, :], 1, LANES), _pad_to(w2, LANES, LANES),
      _pad_to(b2[None, :], 1, LANES), _pad_to(w3, LANES, 2 * D_MODEL),
      _pad_to(freq[None, :], 1, LANES), absdelta)


class _FFTPlan:
    def __init__(self, seq_len):
        self.n = 2 * seq_len
        self.n1 = self.n // DFT_N2
        self.r = self.n1 // 2
        self.k1n = self.n1 // 2 + 1
        self.kron = max(HALO, LANES // self.r)
        ang = 2.0 * np.pi * np.outer(np.arange(self.k1n), np.arange(self.r)) / self.n1
        eye = np.eye(self.kron)
        self.fwd_cos = jnp.asarray(np.kron(np.cos(ang), eye), BF16)
        self.fwd_sin = jnp.asarray(np.kron(-np.sin(ang), eye), BF16)
        wgt = np.full((self.k1n,), 2.0)
        wgt[0] = wgt[-1] = 1.0
        scale = (wgt / self.n)[None, :]
        self.inv_cos = jnp.asarray(np.kron(np.cos(ang).T * scale, eye), BF16)
        self.inv_sin = jnp.asarray(np.kron(-np.sin(ang).T * scale, eye), BF16)
        a2 = 2.0 * np.pi * np.outer(np.arange(DFT_N2), np.arange(DFT_N2)) / DFT_N2
        self.f_cos = jnp.asarray(np.cos(a2), F32)
        self.f_sin = jnp.asarray(-np.sin(a2), F32)
        tw = 2.0 * np.pi * np.outer(np.arange(self.k1n), np.arange(DFT_N2)) / self.n
        self.tw_cos = jnp.asarray(np.cos(tw), F32)
        self.tw_sin = jnp.asarray(-np.sin(tw), F32)


def _fft_a_kernel(z_ref, wc_ref, ws_ref, ar_ref, ai_ref, *, rq):
    shape = ar_ref.shape[1:2] + ar_ref.shape[3:]
    for t in range(z_ref.shape[2]):
        z = z_ref[0, :, t].reshape(rq, D_MODEL).astype(BF16)
        ar_ref[0, :, t] = _dot(wc_ref[...], z).reshape(shape).astype(BF16)
        ai_ref[0, :, t] = _dot(ws_ref[...], z).reshape(shape).astype(BF16)


def _fft_group(plan):
    nhi = DFT_N2 // plan.kron
    per_group = plan.r * plan.kron * D_MODEL * 4
    return max(1, min(nhi, (2 << 20) // per_group))


def fft_stage_a(z, plan, batch):
    q = plan.kron
    nhi = DFT_N2 // q
    hb = _fft_group(plan)
    zv = z.reshape(batch, plan.r, nhi, q, D_MODEL)
    out = jax.ShapeDtypeStruct((batch, plan.k1n, nhi, q, D_MODEL), BF16)
    ospec = pl.BlockSpec((1, plan.k1n, hb, q, D_MODEL), lambda b, h: (b, 0, h, 0, 0))
    wspec = pl.BlockSpec(plan.fwd_cos.shape, lambda b, h: (0, 0))
    ar, ai = pl.pallas_call(
        functools.partial(_fft_a_kernel, rq=plan.r * q),
        grid=(batch, nhi // hb),
        in_specs=[pl.BlockSpec((1, plan.r, hb, q, D_MODEL), lambda b, h: (b, 0, h, 0, 0)),
                  wspec, wspec],
        out_specs=[ospec, ospec],
        out_shape=[out, out],
        compiler_params=_params(("parallel", "parallel")),
        name="fft_stage_a",
    )(zv, plan.fwd_cos, plan.fwd_sin)
    shape = (batch, plan.k1n, DFT_N2, D_MODEL)
    return ar.reshape(shape), ai.reshape(shape)


def _twiddled_dft(fr, fi, tr, ti):
    return (fr * tr - fi * ti).astype(BF16), (fr * ti + fi * tr).astype(BF16)


def _fft_b_fwd_kernel(ar_ref, ai_ref, fr_ref, fi_ref, twr_ref, twi_ref, br_ref, bi_ref,
                      gr_ref, gi_ref):
    @pl.when(pl.program_id(1) == 0)
    def _():
        gr, gi = _twiddled_dft(fr_ref[...], fi_ref[...], twr_ref[0], twi_ref[0])
        gr_ref[...] = gr
        gi_ref[...] = gi

    gr = gr_ref[...]
    gi = gi_ref[...]
    ar = ar_ref[0, 0].astype(BF16)
    ai = ai_ref[0, 0].astype(BF16)
    br_ref[0, 0] = _dot(gr, ar) - _dot(gi, ai)
    bi_ref[0, 0] = _dot(gr, ai) + _dot(gi, ar)


def fft_stage_b_fwd(ar, ai, plan):
    batch = ar.shape[0]
    blk = pl.BlockSpec((1, 1, DFT_N2, D_MODEL), lambda k, b: (b, k, 0, 0))
    cst = pl.BlockSpec((DFT_N2, DFT_N2), lambda k, b: (0, 0))
    tws = pl.BlockSpec((1, 1, DFT_N2), lambda k, b: (k, 0, 0))
    out = jax.ShapeDtypeStruct(ar.shape, F32)
    return pl.pallas_call(
        _fft_b_fwd_kernel,
        grid=(plan.k1n, batch),
        in_specs=[blk, blk, cst, cst, tws, tws],
        out_specs=[blk, blk],
        out_shape=[out, out],
        scratch_shapes=[pltpu.VMEM((DFT_N2, DFT_N2), BF16)] * 2,
        compiler_params=_params(("parallel", "arbitrary")),
        name="fft_stage_b_fwd",
    )(ar, ai, plan.f_cos, plan.f_sin, plan.tw_cos[:, None, :], plan.tw_sin[:, None, :])


def _fft_b_conv_kernel(ar_ref, ai_ref, hr_ref, hi_ref, hb0_ref, fr_ref, fi_ref,
                       twr_ref, twi_ref, tcr_ref, tci_ref, cr_ref, ci_ref,
                       gr_ref, gi_ref, gtr_ref, gti_ref, kr_ref, ki_ref):
    @pl.when(pl.program_id(1) == 0)
    def _():
        fr = fr_ref[...]
        fi = fi_ref[...]
        gr, gi = _twiddled_dft(fr, fi, twr_ref[0], twi_ref[0])
        gr_ref[...] = gr
        gi_ref[...] = gi
        gtr, gti = _twiddled_dft(fr, fi, tcr_ref[0], tci_ref[0])
        gtr_ref[...] = gtr
        gti_ref[...] = gti
        kr_ref[...] = hr_ref[0, 0] + hr_ref[1, 0] - hb0_ref[...]
        ki_ref[...] = hi_ref[0, 0] - hi_ref[1, 0]

    gr = gr_ref[...]
    gi = gi_ref[...]
    ar = ar_ref[0, 0].astype(BF16)
    ai = ai_ref[0, 0].astype(BF16)
    br = _dot(gr, ar) - _dot(gi, ai)
    bi = _dot(gr, ai) + _dot(gi, ar)
    kr = kr_ref[...]
    ki = ki_ref[...]
    pr = (br * kr - bi * ki).astype(BF16)
    pi = (br * ki + bi * kr).astype(BF16)
    gtr = gtr_ref[...]
    gti = gti_ref[...]
    cr_ref[0, 0] = (_dot(gtr, pr) + _dot(gti, pi)).astype(BF16)
    ci_ref[0, 0] = (_dot(gtr, pi) - _dot(gti, pr)).astype(BF16)


def fft_stage_b_conv(ar, ai, hr, hi, hb0, plan):
    batch = ar.shape[0]
    blk = pl.BlockSpec((1, 1, DFT_N2, D_MODEL), lambda k, b: (b, k, 0, 0))
    hblk = pl.BlockSpec((2, 1, DFT_N2, D_MODEL), lambda k, b: (0, k, 0, 0))
    cst = pl.BlockSpec((DFT_N2, DFT_N2), lambda k, b: (0, 0))
    tws = pl.BlockSpec((1, 1, DFT_N2), lambda k, b: (k, 0, 0))
    twc = pl.BlockSpec((1, DFT_N2, 1), lambda k, b: (k, 0, 0))
    out = jax.ShapeDtypeStruct(ar.shape, BF16)
    return pl.pallas_call(
        _fft_b_conv_kernel,
        grid=(plan.k1n, batch),
        in_specs=[blk, blk, hblk, hblk, pl.BlockSpec((1, D_MODEL), lambda k, b: (0, 0)),
                  cst, cst, tws, tws, twc, twc],
        out_specs=[blk, blk],
        out_shape=[out, out],
        scratch_shapes=[pltpu.VMEM((DFT_N2, DFT_N2), BF16)] * 4
        + [pltpu.VMEM((DFT_N2, D_MODEL), F32)] * 2,
        compiler_params=_params(("parallel", "arbitrary")),
        name="fft_stage_b_conv",
    )(ar, ai, hr, hi, hb0, plan.f_cos, plan.f_sin,
      plan.tw_cos[:, None, :], plan.tw_sin[:, None, :],
      plan.tw_cos[:, :, None], plan.tw_sin[:, :, None])


def _fft_a_inv_kernel(cr_ref, ci_ref, vc_ref, vs_ref, z_ref, x0_ref, skip_ref, y_ref, *, kq):
    shape = z_ref.shape[1:2] + z_ref.shape[3:]
    for t in range(z_ref.shape[2]):
        cr = cr_ref[0, :, t].reshape(kq, D_MODEL).astype(BF16)
        ci = ci_ref[0, :, t].reshape(kq, D_MODEL).astype(BF16)
        conv = _dot(vc_ref[...], cr) + _dot(vs_ref[...], ci)
        y = conv.reshape(shape) + z_ref[0, :, t] * skip_ref[...]
        y_ref[0, :, t] = (y * x0_ref[0, :, t].astype(F32)).astype(BF16)


def fft_stage_a_inv(cr, ci, z, x0, skip, plan, batch):
    q = plan.kron
    nhi = DFT_N2 // q
    hb = _fft_group(plan)
    cshape = (batch, plan.k1n, nhi, q, D_MODEL)
    tshape = (batch, plan.r, nhi, q, D_MODEL)
    cspec = pl.BlockSpec((1, plan.k1n, hb, q, D_MODEL), lambda b, h: (b, 0, h, 0, 0))
    tspec = pl.BlockSpec((1, plan.r, hb, q, D_MODEL), lambda b, h: (b, 0, h, 0, 0))
    wspec = pl.BlockSpec(plan.inv_cos.shape, lambda b, h: (0, 0))
    y = pl.pallas_call(
        functools.partial(_fft_a_inv_kernel, kq=plan.k1n * q),
        grid=(batch, nhi // hb),
        in_specs=[cspec, cspec, wspec, wspec, tspec, tspec,
                  pl.BlockSpec((1, D_MODEL), lambda b, h: (0, 0))],
        out_specs=tspec,
        out_shape=jax.ShapeDtypeStruct(tshape, BF16),
        compiler_params=_params(("parallel", "parallel")),
        name="fft_stage_a_inv",
    )(cr.reshape(cshape), ci.reshape(cshape), plan.inv_cos, plan.inv_sin,
      z.reshape(tshape), x0.reshape(tshape), skip[None, :])
    return y.reshape(z.shape)


def hyena_mixer(x, mod, norm_g, p, batch, seq_len):
    z, x0 = hyena_in(x, mod, norm_g, p["w_in"], p["conv_w"], p["conv_b"], seq_len)
    plan = _FFTPlan(seq_len)
    filt = hyena_filter(seq_len, p["w1"], p["b1"], p["w2"], p["b2"], p["w3"], p["freq"])
    fr, fi = fft_stage_a(filt.reshape(2 * seq_len, D_MODEL), plan, 2)
    hr, hi = fft_stage_b_fwd(fr, fi, plan)
    ar, ai = fft_stage_a(z, plan, batch)
    cr, ci = fft_stage_b_conv(ar, ai, hr, hi, filt[1, 0:1, :], plan)
    y = fft_stage_a_inv(cr, ci, z, x0, p["skip"], plan, batch)
    return proj_residual(y, p["w_out"], x, mod, 2, seq_len)


def _pool_kernel(prev_ref, x_ref, next_ref, mod_ref, g_ref, w_ref, s_ref, o_ref, *, tm, seq_len):
    first, last = _edge_flags(tm, seq_len)
    m = mod_ref[0]
    x = x_ref[...]
    h = _modulate(_ext_rows(prev_ref, x_ref, next_ref), g_ref[...], m[0:1], m[1:2])
    n = tm + 2 * HALO
    row = lax.broadcasted_iota(jnp.int32, (n, 1), 0)
    outside = jnp.logical_or(jnp.logical_and(first, row < HALO),
                             jnp.logical_and(last, row >= HALO + tm))
    h = jnp.where(outside, 0.0, h)
    pos = (pl.program_id(0) * tm) % seq_len + lax.broadcasted_iota(jnp.int32, (tm, 1), 0)
    ys = []
    for gi, win in enumerate(POOL_WINDOWS):
        cols = slice(gi * POOL_GROUP_DIM, (gi + 1) * POOL_GROUP_DIM)
        hg = h[:, cols]
        acc = hg
        span = 1
        while span < win:
            acc = acc + pltpu.roll(acc, span, axis=0)
            span *= 2
        lead = win // 2 - 1
        if lead:
            acc = pltpu.roll(acc, n - lead, axis=0)
        half = win // 2
        cnt = jnp.minimum(pos + half, seq_len) - jnp.maximum(pos - half, 0)
        pooled = acc[HALO:HALO + tm] / cnt.astype(F32) - hg[HALO:HALO + tm]
        ys.append(_dot(pooled.astype(BF16), w_ref[gi]))
    y = jnp.concatenate(ys, axis=1) * s_ref[...]
    o_ref[...] = x + m[2:3] * y


def pool_mixer(x, mod, norm_g, w_group, scale, seq_len):
    n_rows = x.shape[0]
    tm = ROW_TILE
    prev, nxt = _halo_specs(tm, n_rows)
    return pl.pallas_call(
        functools.partial(_pool_kernel, tm=tm, seq_len=seq_len),
        grid=(n_rows // tm,),
        in_specs=[prev, _row_spec(tm), nxt, _mod_spec(tm, seq_len), _const_spec((1, D_MODEL)),
                  _const_spec(w_group.shape), _const_spec((1, D_MODEL))],
        out_specs=_row_spec(tm),
        out_shape=jax.ShapeDtypeStruct((n_rows, D_MODEL), F32),
        compiler_params=_params(("parallel",)),
        name="pool_mixer",
    )(x, x, x, mod, norm_g[None, :], w_group.astype(BF16), scale[None, :])


def _sconv_kernel(prev_ref, x_ref, next_ref, mod_ref, g_ref, w_ref, cw_ref, cb_ref, wo_ref,
                  o_ref, y_ref, *, tm, seq_len):
    first, last = _edge_flags(tm, seq_len)
    m = mod_ref[0]
    h = _modulate(_ext_rows(prev_ref, x_ref, next_ref), g_ref[...], m[0:1], m[1:2]).astype(BF16)
    tn = 256
    for j in range(D_MODEL // tn):
        cols = slice(j * tn, (j + 1) * tn)
        bg = _dot(h, w_ref[:, cols])[HALO:HALO + tm]
        cg = _dot(h, w_ref[:, D_MODEL + j * tn:D_MODEL + (j + 1) * tn])
        hp = _dot(h, w_ref[:, 2 * D_MODEL + j * tn:2 * D_MODEL + (j + 1) * tn])
        y_ref[:, cols] = (bg * _conv3(cg * hp, cw_ref, cb_ref, cols, tm, first, last)).astype(BF16)
    o_ref[...] = x_ref[...] + m[2:3] * _dot(y_ref[...], wo_ref[...])


def sconv_mixer(x, mod, norm_g, w_in, conv_w, conv_b, w_out, seq_len):
    n_rows = x.shape[0]
    tm = ROW_TILE
    prev, nxt = _halo_specs(tm, n_rows)
    return pl.pallas_call(
        functools.partial(_sconv_kernel, tm=tm, seq_len=seq_len),
        grid=(n_rows // tm,),
        in_specs=[prev, _row_spec(tm), nxt, _mod_spec(tm, seq_len), _const_spec((1, D_MODEL)),
                  _const_spec((D_MODEL, 3 * D_MODEL)), _const_spec((3, D_MODEL)),
                  _const_spec((1, D_MODEL)), _const_spec((D_MODEL, D_MODEL))],
        out_specs=_row_spec(tm),
        out_shape=jax.ShapeDtypeStruct((n_rows, D_MODEL), F32),
        scratch_shapes=[pltpu.VMEM((tm, D_MODEL), BF16)],
        compiler_params=_params(("parallel",)),
        name="sconv_mixer",
    )(x, x, x, mod, norm_g[None, :], w_in.astype(BF16), conv_w, conv_b[None, :],
      w_out.astype(BF16))


def _pack_bf16(x):
    w = x.shape[1] // 2
    bits = pltpu.bitcast(x.astype(BF16).astype(F32), jnp.uint32)
    return (bits[:, :w] >> 16) | bits[:, w:]


def _unpack_bf16(p):
    lo = pltpu.bitcast(p << 16, F32)
    hi = pltpu.bitcast(p & jnp.uint32(0xFFFF0000), F32)
    return jnp.concatenate([lo, hi], axis=1)


META_E1, META_E2, META_W1, META_W2, META_R1, META_R2 = range(6)
META_ROWS = 8


def _router_kernel(x_ref, mod_ref, g_ref, wh_ref, b_ref, tri_ref,
                   h_ref, meta_ref, meta_t_ref, cnt_ref):
    @pl.when(pl.program_id(0) == 0)
    def _():
        cnt_ref[...] = jnp.zeros_like(cnt_ref)

    m = mod_ref[0]
    x = x_ref[...]
    ms = jnp.mean(x * x, axis=-1, keepdims=True)
    h = x * lax.rsqrt(ms + EPS) * (g_ref[...] * (1.0 + m[4:5])) + m[3:4]
    hi = h.astype(BF16)
    hi32 = hi.astype(F32)
    lo = (h - hi32).astype(BF16)
    half = D_MODEL // 2
    bits = pltpu.bitcast(hi32, jnp.uint32)
    h_ref[...] = (bits[:, :half] >> 16) | bits[:, half:]
    part = _dot(hi, wh_ref[...])
    lg = part[:, :LANES] + (part[:, LANES:] + _dot(lo, wh_ref[:, :LANES])) + b_ref[...]
    lgt = lg.T
    tm = lgt.shape[1]
    neg = -jnp.inf
    sub = lax.broadcasted_iota(jnp.int32, (HALO, tm), 0).astype(F32)

    def first_argmax(vals):
        top = jnp.max(vals, axis=0, keepdims=True)
        idx = jnp.min(jnp.where(vals == top, sub, float(HALO)), axis=0, keepdims=True)
        return top, idx

    gl = jnp.where(sub < MOE_GROUPS, lgt[MOE_N_EXPERTS:MOE_N_EXPERTS + HALO], neg)
    gmax, grp = first_argmax(gl)
    g_w = 1.0 / jnp.sum(jnp.exp(gl - gmax), axis=0, keepdims=True)
    el = lgt[:MOE_EXPERTS_PER_GROUP]
    for g in range(1, MOE_GROUPS):
        el = jnp.where(grp == float(g),
                       lgt[g * MOE_EXPERTS_PER_GROUP:(g + 1) * MOE_EXPERTS_PER_GROUP], el)
    v1, i1 = first_argmax(el)
    v2, i2 = first_argmax(jnp.where(sub == i1, neg, el))
    ex = jnp.exp(v2 - v1)
    w1 = 1.0 / (1.0 + ex)
    w2 = ex * w1
    e1 = grp * MOE_EXPERTS_PER_GROUP + i1
    e2 = grp * MOE_EXPERTS_PER_GROUP + i2
    expert = lax.broadcasted_iota(jnp.int32, (MOE_N_EXPERTS, tm), 0).astype(F32)
    onehot = jnp.where(jnp.logical_or(expert == e1, expert == e2), 1.0, 0.0)
    before = _dot(onehot.astype(BF16), tri_ref[...]) + cnt_ref[...]
    cnt_ref[...] += jnp.sum(onehot, axis=1, keepdims=True)
    r1 = jnp.sum(jnp.where(expert == e1, before, 0.0), axis=0, keepdims=True)
    r2 = jnp.sum(jnp.where(expert == e2, before, 0.0), axis=0, keepdims=True)
    fields = ((META_E1, e1), (META_E2, e2), (META_W1, w1 * g_w), (META_W2, w2 * g_w),
              (META_R1, r1), (META_R2, r2))
    field = lax.broadcasted_iota(jnp.int32, (LANES, tm), 0)
    meta_t = jnp.zeros((LANES, tm), F32)
    for row, val in fields:
        meta_t = jnp.where(field == row, val, meta_t)
    meta_t_ref[...] = meta_t[:META_ROWS]
    meta_ref[...] = meta_t.T


def moe_router(x, mod, norm_g, w_group, b_group, w_router, b_router, seq_len):
    n_rows = x.shape[0]
    tm = ROW_TILE
    w = _pad_to(jnp.concatenate([w_router, w_group], axis=1), D_MODEL, LANES)
    wh = w.astype(BF16)
    wl = (w - wh.astype(F32)).astype(BF16)
    whl = jnp.concatenate([wh, wl], axis=1)
    b = _pad_to(jnp.concatenate([b_router, b_group])[None, :], 1, LANES)
    tri = jnp.asarray(np.triu(np.ones((tm, tm), np.float32), 1), BF16)
    return pl.pallas_call(
        _router_kernel,
        grid=(n_rows // tm,),
        in_specs=[_row_spec(tm), _mod_spec(tm, seq_len), _const_spec((1, D_MODEL)),
                  _const_spec((D_MODEL, 2 * LANES)), _const_spec((1, LANES)),
                  _const_spec((tm, tm))],
        out_specs=[_row_spec(tm, D_MODEL // 2), _row_spec(tm, LANES),
                   pl.BlockSpec((META_ROWS, tm), lambda i: (0, i)),
                   _const_spec((MOE_N_EXPERTS, 1))],
        out_shape=[jax.ShapeDtypeStruct((n_rows, D_MODEL // 2), jnp.uint32),
                   jax.ShapeDtypeStruct((n_rows, LANES), F32),
                   jax.ShapeDtypeStruct((META_ROWS, n_rows), F32),
                   jax.ShapeDtypeStruct((MOE_N_EXPERTS, 1), F32)],
        compiler_params=_params(("arbitrary",)),
        name="moe_router",
    )(x, mod, norm_g[None, :], whl, b, tri)


def _slot_kernel(offs_ref, meta_t_ref, pos_ref):
    meta = meta_t_ref[...]
    start = jnp.zeros_like(meta)
    for e in range(MOE_N_EXPERTS):
        start = jnp.where(meta == float(e), offs_ref[e], start)
    shift = META_ROWS - (META_R1 - META_E1)
    pos_ref[...] = (start + pltpu.roll(meta, shift, axis=0)).astype(jnp.int32)


def moe_slots(meta_t, offsets):
    n_rows = meta_t.shape[1]
    tn = min(n_rows, 8192)
    blk = pl.BlockSpec((META_ROWS, tn), lambda i, offs: (0, i))
    return pl.pallas_call(
        _slot_kernel,
        grid_spec=pltpu.PrefetchScalarGridSpec(
            num_scalar_prefetch=1, grid=(n_rows // tn,), in_specs=[blk], out_specs=blk),
        out_shape=jax.ShapeDtypeStruct((META_ROWS, n_rows), jnp.int32),
        compiler_params=_params(("parallel",)),
        name="moe_slots",
    )(offsets, meta_t)


def _expert_kernel(te_ref, nu_ref, nv_ref, xs_ref, wg_ref, wu_ref, wd_ref, o_ref,
                   wgu_ref, wdb_ref):
    j = pl.program_id(0)

    @pl.when(jnp.logical_or(j == 0, te_ref[j] != te_ref[jnp.maximum(j - 1, 0)]))
    def _():
        wgu_ref[:, :MOE_D_FF] = wg_ref[0, 0].astype(BF16)
        wgu_ref[:, MOE_D_FF:] = wu_ref[0, 0].astype(BF16)
        wdb_ref[...] = wd_ref[0, 0].astype(BF16)

    @pl.when(j < nu_ref[0])
    def _():
        x = _unpack_bf16(xs_ref[...])
        row = lax.broadcasted_iota(jnp.int32, (x.shape[0], 1), 0)
        x = jnp.where(row < nv_ref[j], x, 0.0).astype(BF16)
        au = _dot(x, wgu_ref[...])
        hh = (_silu(au[:, :MOE_D_FF]) * au[:, MOE_D_FF:]).astype(BF16)
        o_ref[...] = _pack_bf16(_dot(hh, wdb_ref[...]))


def moe_experts(xs, tile_expert, n_used, n_valid, w_gate, w_up, w_down, layer, tm):
    n_slots, half = xs.shape
    wspec = lambda shape: pl.BlockSpec((1, 1) + shape, lambda j, te, nu, nv: (layer, te[j], 0, 0))
    row = pl.BlockSpec((tm, half), lambda j, te, nu, nv: (j, 0))
    grid_spec = pltpu.PrefetchScalarGridSpec(
        num_scalar_prefetch=3,
        grid=(n_slots // tm,),
        in_specs=[row, wspec((D_MODEL, MOE_D_FF)), wspec((D_MODEL, MOE_D_FF)),
                  wspec((MOE_D_FF, D_MODEL))],
        out_specs=row,
        scratch_shapes=[pltpu.VMEM((D_MODEL, 2 * MOE_D_FF), BF16),
                        pltpu.VMEM((MOE_D_FF, D_MODEL), BF16)],
    )
    return pl.pallas_call(
        _expert_kernel,
        grid_spec=grid_spec,
        out_shape=jax.ShapeDtypeStruct((n_slots, half), jnp.uint32),
        compiler_params=_params(("arbitrary",)),
        name="moe_experts",
    )(tile_expert, n_used, n_valid, xs, w_gate, w_up, w_down)


def _combine_kernel(x_ref, ya_ref, yb_ref, meta_ref, mod_ref, o_ref):
    meta = meta_ref[...]
    y = (meta[:, META_W1:META_W1 + 1] * _unpack_bf16(ya_ref[...])
         + meta[:, META_W2:META_W2 + 1] * _unpack_bf16(yb_ref[...]))
    o_ref[...] = x_ref[...] + mod_ref[0][5:6] * y


def moe_combine(x, ya, yb, meta, mod, seq_len):
    n_rows = x.shape[0]
    tm = STREAM_TILE
    half = D_MODEL // 2
    return pl.pallas_call(
        _combine_kernel,
        grid=(n_rows // tm,),
        in_specs=[_row_spec(tm), _row_spec(tm, half), _row_spec(tm, half), _row_spec(tm, LANES),
                  _mod_spec(tm, seq_len)],
        out_specs=_row_spec(tm),
        out_shape=jax.ShapeDtypeStruct((n_rows, D_MODEL), F32),
        compiler_params=_params(("parallel",)),
        name="moe_combine",
    )(x, ya, yb, meta, mod)


MOE_TILE = 512
MOE_TILE_LARGE = 1024
MOE_LARGE_ROWS = 32768


def hier_moe(x, mod, norm_g, w_group, b_group, w_router, b_router, w_gate, w_up, w_down, layer,
             seq_len):
    n_rows = x.shape[0]
    tm = MOE_TILE_LARGE if n_rows >= MOE_LARGE_ROWS else MOE_TILE
    hp, meta, meta_t, counts = moe_router(x, mod, norm_g, w_group, b_group, w_router, b_router,
                                          seq_len)
    cnt = counts[:, 0].astype(jnp.int32)
    padded = (cnt + tm - 1) // tm * tm
    ends = jnp.cumsum(padded)
    starts = ends - padded
    n_slots = 2 * n_rows + MOE_N_EXPERTS * tm
    tile_start = jnp.arange(n_slots // tm, dtype=jnp.int32) * tm
    tile_expert = jnp.minimum(jnp.sum(tile_start[:, None] >= ends[None, :], axis=1),
                              MOE_N_EXPERTS - 1).astype(jnp.int32)
    n_used = (ends[-1:] // tm).astype(jnp.int32)
    n_valid = jnp.clip(starts[tile_expert] + cnt[tile_expert] - tile_start, 0, tm).astype(jnp.int32)
    pos = moe_slots(meta_t, starts.astype(F32))
    idx0 = pos[META_E1].reshape(n_rows // SC_BLOCK, SC_BLOCK)
    idx1 = pos[META_E2].reshape(n_rows // SC_BLOCK, SC_BLOCK)
    xs = sc_scatter_rows(hp, idx0, idx1, n_slots)
    ys = moe_experts(xs, tile_expert, n_used, n_valid, w_gate, w_up, w_down, layer, tm)
    ya, yb = sc_gather_rows(ys, idx0, idx1)
    return moe_combine(x, ya, yb, meta, mod, seq_len)


SC_CORES = 2
SC_SUBCORES = 16
SC_WORKERS = SC_CORES * SC_SUBCORES
SC_BLOCK = 128


def _sc_mesh():
    return plsc.VectorSubcoreMesh(core_axis_name="c", subcore_axis_name="s")


def _sc_worker():
    return lax.axis_index("s") * SC_CORES + lax.axis_index("c")


def sc_scatter_rows(rows, idx0, idx1, n_slots):
    n_rows, width = rows.shape
    per_worker = n_rows // SC_BLOCK // SC_WORKERS

    @functools.partial(
        pl.kernel, mesh=_sc_mesh(),
        out_type=jax.ShapeDtypeStruct((n_slots, width), rows.dtype),
        scratch_types=[pltpu.VMEM((SC_BLOCK,), jnp.int32), pltpu.VMEM((SC_BLOCK,), jnp.int32),
                       pltpu.VMEM((SC_BLOCK, width), rows.dtype)],
        name="sc_scatter_rows",
    )
    def scatter(rows_hbm, i0_hbm, i1_hbm, out_hbm, i0_v, i1_v, rows_v):
        first = _sc_worker() * per_worker

        @pl.loop(0, per_worker)
        def _(j):
            blk = first + j
            pltpu.sync_copy(i0_hbm.at[blk], i0_v)
            pltpu.sync_copy(i1_hbm.at[blk], i1_v)
            pltpu.sync_copy(rows_hbm.at[pl.ds(blk * SC_BLOCK, SC_BLOCK)], rows_v)
            pltpu.sync_copy(rows_v, out_hbm.at[i0_v])
            pltpu.sync_copy(rows_v, out_hbm.at[i1_v])

    return scatter(rows, idx0, idx1)


def sc_gather_rows(src, idx0, idx1):
    width = src.shape[1]
    n_rows = idx0.shape[0] * SC_BLOCK
    per_worker = n_rows // SC_BLOCK // SC_WORKERS
    out = jax.ShapeDtypeStruct((n_rows, width), src.dtype)

    @functools.partial(
        pl.kernel, mesh=_sc_mesh(), out_type=(out, out),
        scratch_types=[pltpu.VMEM((SC_BLOCK,), jnp.int32), pltpu.VMEM((SC_BLOCK, width), src.dtype)],
        name="sc_gather_rows",
    )
    def gather(src_hbm, i0_hbm, i1_hbm, a_hbm, b_hbm, idx_v, rows_v):
        first = _sc_worker() * per_worker

        @pl.loop(0, per_worker)
        def _(j):
            blk = first + j
            dst = pl.ds(blk * SC_BLOCK, SC_BLOCK)
            pltpu.sync_copy(i0_hbm.at[blk], idx_v)
            pltpu.sync_copy(src_hbm.at[idx_v], rows_v)
            pltpu.sync_copy(rows_v, a_hbm.at[dst])
            pltpu.sync_copy(i1_hbm.at[blk], idx_v)
            pltpu.sync_copy(src_hbm.at[idx_v], rows_v)
            pltpu.sync_copy(rows_v, b_hbm.at[dst])

    return gather(src, idx0, idx1)


def _trunk(x3, mods, p):
    batch, seq_len, _ = x3.shape
    x = x3.reshape(batch * seq_len, D_MODEL)
    for i in range(DEPTH):
        mod = mods[i]
        g1 = p["norm1_g"][i]
        kind = i % 4
        if kind == 0:
            q, k, v = attn_qkv(x, mod, g1, p["attn_wqkv"][0], p["attn_q_norm"][0],
                               p["attn_k_norm"][0], seq_len)
            o = attn_flash(q, k, v, batch, seq_len)
            x = proj_residual(o, p["attn_wo"][0], x, mod, 2, seq_len)
        elif kind == 1:
            hp = {"w_in": p["hy_w_in"][0], "conv_w": p["hy_conv_w"][0], "conv_b": p["hy_conv_b"][0],
                  "w1": p["hy_ffn_w1"][0], "b1": p["hy_ffn_b1"][0], "w2": p["hy_ffn_w2"][0],
                  "b2": p["hy_ffn_b2"][0], "w3": p["hy_ffn_w3"][0], "freq": p["hy_freq"][0],
                  "skip": p["hy_skip"][0], "w_out": p["hy_w_out"][0]}
            x = hyena_mixer(x, mod, g1, hp, batch, seq_len)
        elif kind == 2:
            x = pool_mixer(x, mod, g1, p["pool_w"][0], p["pool_scale"][0], seq_len)
        else:
            x = sconv_mixer(x, mod, g1, p["sc_w_in"][0], p["sc_conv_w"][0], p["sc_conv_b"][0],
                            p["sc_w_out"][0], seq_len)
        x = hier_moe(x, mod, p["norm2_g"][i], p["moe_w_group"][i], p["moe_b_group"][i],
                     p["moe_w_router"][i], p["moe_b_router"][i], p["moe_w_gate"],
                     p["moe_w_up"], p["moe_w_down"], i, seq_len)
    return x.reshape(batch, seq_len, D_MODEL)


def kernel(x_prompt, x_sample, c_prompt, c_sample, norm1_g, norm2_g, ada_w, ada_b, attn_wqkv, attn_q_norm, attn_k_norm, attn_wo, hy_w_in, hy_conv_w, hy_conv_b, hy_ffn_w1, hy_ffn_b1, hy_ffn_w2, hy_ffn_b2, hy_ffn_w3, hy_freq, hy_skip, hy_w_out, pool_w, pool_scale, sc_w_in, sc_conv_w, sc_conv_b, sc_w_out, moe_w_group, moe_b_group, moe_w_router, moe_b_router, moe_w_gate, moe_w_up, moe_w_down):
    p = dict(norm1_g=norm1_g, norm2_g=norm2_g, attn_wqkv=attn_wqkv, attn_q_norm=attn_q_norm,
             attn_k_norm=attn_k_norm, attn_wo=attn_wo, hy_w_in=hy_w_in, hy_conv_w=hy_conv_w,
             hy_conv_b=hy_conv_b, hy_ffn_w1=hy_ffn_w1, hy_ffn_b1=hy_ffn_b1, hy_ffn_w2=hy_ffn_w2,
             hy_ffn_b2=hy_ffn_b2, hy_ffn_w3=hy_ffn_w3, hy_freq=hy_freq, hy_skip=hy_skip,
             hy_w_out=hy_w_out, pool_w=pool_w, pool_scale=pool_scale, sc_w_in=sc_w_in,
             sc_conv_w=sc_conv_w, sc_conv_b=sc_conv_b, sc_w_out=sc_w_out, moe_w_group=moe_w_group,
             moe_b_group=moe_b_group, moe_w_router=moe_w_router, moe_b_router=moe_b_router,
             moe_w_gate=moe_w_gate, moe_w_up=moe_w_up, moe_w_down=moe_w_down)
    nb = c_prompt.shape[0]
    ns = c_sample.shape[0]
    rows = -(-(nb + ns) // HALO) * HALO
    c_all = jnp.pad(jnp.concatenate([c_prompt, c_sample], axis=0), ((0, rows - nb - ns), (0, 0)))
    mod = ada_mod(c_all, ada_w, ada_b).reshape(DEPTH, rows, 6, D_MODEL)
    mods_prompt = [mod[i, :nb] for i in range(DEPTH)]
    mods_sample = [mod[i, nb:nb + ns] for i in range(DEPTH)]
    return _trunk(x_prompt, mods_prompt, p), _trunk(x_sample, mods_sample, p)
```

```python
import functools
import math

import jax
import jax.numpy as jnp
import numpy as np
from jax import lax
from jax.experimental import pallas as pl
from jax.experimental.pallas import tpu as pltpu
from jax.experimental.pallas import tpu_sc as plsc

F32 = jnp.float32
BF16 = jnp.bfloat16

D_MODEL = 1024
DEPTH = 4
EPS = 1e-6
GRID_W = 64
HEAD_DIM = 64
N_HEADS = 16
N_KV_HEADS = 4
Q_PER_KV = 4
ROPE_THETA = 10000.0
ROPE_FREQS = 16
HY_EMB_DIM = 33
HY_BANDS = 16
HY_FILTER_WIDTH = 64
HY_FAST_DECAY = 0.3
HY_SLOW_DECAY = 1.5
HY_TARGET = 1e-2
POOL_WINDOWS = (2, 4, 8, 16)
POOL_GROUP_DIM = 256
MOE_GROUPS = 4
MOE_EXPERTS_PER_GROUP = 8
MOE_N_EXPERTS = 32
MOE_D_FF = 256

LANES = 128
HALO = 8
DFT_N2 = 256
VMEM_LIMIT = 56 * 1024 * 1024

ROW_TILE = 512
STREAM_TILE = 1024


def _params(sem):
    return pltpu.CompilerParams(dimension_semantics=sem, vmem_limit_bytes=VMEM_LIMIT)


def _dot(a, b):
    return jnp.dot(a, b, preferred_element_type=F32)


def _split(a):
    hi = a.astype(BF16)
    lo = (a - hi.astype(F32)).astype(BF16)
    return hi, lo


def _dot3(a, b):
    ah, al = _split(a)
    bh, bl = _split(b)
    return _dot(ah, bh) + (_dot(ah, bl) + _dot(al, bh))


def _modulate(x, g, shift, scale):
    ms = jnp.mean(x * x, axis=-1, keepdims=True)
    return x * lax.rsqrt(ms + EPS) * g * (1.0 + scale) + shift


def _silu(x):
    return x * (1.0 / (1.0 + jnp.exp(-x)))


def _ada_kernel(c_ref, w_ref, b_ref, o_ref):
    c = c_ref[...]
    o_ref[0] = _dot3(_silu(c), w_ref[0]) + b_ref[0]


def ada_mod(c_all, ada_w, ada_b):
    rows = c_all.shape[0]
    n = ada_w.shape[2]
    tn = 1536
    return pl.pallas_call(
        _ada_kernel,
        grid=(DEPTH, n // tn),
        in_specs=[
            pl.BlockSpec((rows, D_MODEL), lambda l, j: (0, 0)),
            pl.BlockSpec((1, D_MODEL, tn), lambda l, j: (l, 0, j)),
            pl.BlockSpec((1, 1, tn), lambda l, j: (l, 0, j)),
        ],
        out_specs=pl.BlockSpec((1, rows, tn), lambda l, j: (l, 0, j)),
        out_shape=jax.ShapeDtypeStruct((DEPTH, rows, n), F32),
        compiler_params=_params(("parallel", "parallel")),
        name="ada_mod",
    )(c_all, ada_w, ada_b.reshape(DEPTH, 1, n))


def _row_spec(tm, width=D_MODEL):
    return pl.BlockSpec((tm, width), lambda i: (i, 0))


def _mod_spec(tm, seq_len):
    return pl.BlockSpec((1, 6, D_MODEL), lambda i: ((i * tm) // seq_len, 0, 0))


def _const_spec(shape):
    nd = len(shape)
    return pl.BlockSpec(shape, lambda i: (0,) * nd)


def _halo_specs(tm, n_rows):
    per = tm // HALO
    last = n_rows // HALO - 1
    prev = pl.BlockSpec((HALO, D_MODEL), lambda i: (jnp.maximum(i * per - 1, 0), 0))
    nxt = pl.BlockSpec((HALO, D_MODEL), lambda i: (jnp.minimum((i + 1) * per, last), 0))
    return prev, nxt


def _edge_flags(tm, seq_len):
    i = pl.program_id(0)
    per_seq = seq_len // tm
    pos = i % per_seq
    return pos == 0, pos == per_seq - 1


def _ext_rows(prev_ref, x_ref, next_ref):
    return jnp.concatenate([prev_ref[...], x_ref[...], next_ref[...]], axis=0)


def _shift_rows(u, tm):
    n = u.shape[0]
    up = pltpu.roll(u, 1, axis=0)[HALO:HALO + tm]
    dn = pltpu.roll(u, n - 1, axis=0)[HALO:HALO + tm]
    return up, u[HALO:HALO + tm], dn


def _conv3(u, w_ref, b_ref, cols, tm, first, last, stage_ref=None):
    if stage_ref is None:
        up, mid, dn = _shift_rows(u, tm)
    else:
        stage_ref[...] = u
        up = stage_ref[pl.ds(HALO - 1, tm), :]
        mid = stage_ref[pl.ds(HALO, tm), :]
        dn = stage_ref[pl.ds(HALO + 1, tm), :]
    row = lax.broadcasted_iota(jnp.int32, (tm, 1), 0)
    up = jnp.where(jnp.logical_and(first, row == 0), 0.0, up)
    dn = jnp.where(jnp.logical_and(last, row == tm - 1), 0.0, dn)
    w = w_ref[:, cols]
    return up * w[0:1] + mid * w[1:2] + dn * w[2:3] + b_ref[:, cols]


def _norm_rope(t, gain, headmean, cos, sin_signed):
    width = t.shape[1]
    ms = _dot((t * t).astype(BF16), headmean[:width, :width])
    y = t * lax.rsqrt(ms + EPS) * gain
    lane = lax.broadcasted_iota(jnp.int32, y.shape, 1)
    first = (lane % 32) < ROPE_FREQS
    partner = jnp.where(first, pltpu.roll(y, width - ROPE_FREQS, axis=1),
                        pltpu.roll(y, ROPE_FREQS, axis=1))
    reps = width // LANES
    return y * jnp.tile(cos, (1, reps)) + partner * jnp.tile(sin_signed, (1, reps))


def _qkv_kernel(x_ref, mod_ref, g_ref, w_ref, qg_ref, kg_ref, hm_ref, cos_ref, sin_ref,
                q_ref, k_ref, v_ref):
    m = mod_ref[0]
    h = _modulate(x_ref[...], g_ref[...], m[0:1], m[1:2]).astype(BF16)
    qkv = _dot(h, w_ref[...])
    nq = N_HEADS * HEAD_DIM
    nk = N_KV_HEADS * HEAD_DIM
    cos = cos_ref[...]
    sin = sin_ref[...]
    hm = hm_ref[...]
    q = _norm_rope(qkv[:, :nq], qg_ref[...], hm, cos, sin)
    k = _norm_rope(qkv[:, nq:nq + nk], kg_ref[...], hm, cos, sin)
    v = qkv[:, nq + nk:]
    q_ref[...] = q.astype(BF16)
    ones = jnp.ones((v.shape[0], HEAD_DIM), F32)
    for g in range(N_KV_HEADS):
        sl = slice(g * HEAD_DIM, (g + 1) * HEAD_DIM)
        k_ref[g] = k[:, sl].astype(BF16)
        v_ref[g] = jnp.concatenate([v[:, sl], ones], axis=1).astype(BF16)


def _rope_tables(seq_len):
    rows = seq_len // GRID_W
    r = jnp.broadcast_to(jnp.arange(rows)[:, None], (rows, GRID_W)).reshape(-1)
    c = jnp.broadcast_to(jnp.arange(GRID_W)[None, :], (rows, GRID_W)).reshape(-1)
    inv_freq = ROPE_THETA ** (-jnp.arange(ROPE_FREQS, dtype=F32) / ROPE_FREQS)
    pos = jnp.stack([r, c], axis=-1).astype(F32)
    ang = pos[:, :, None] * inv_freq[None, None, :]
    cos = jnp.cos(ang)
    sin = jnp.sin(ang)
    cos64 = jnp.concatenate([cos, cos], axis=-1).reshape(seq_len, HEAD_DIM)
    sin64 = jnp.concatenate([-sin, sin], axis=-1).reshape(seq_len, HEAD_DIM)
    return jnp.tile(cos64, (1, 2)), jnp.tile(sin64, (1, 2))


def attn_qkv(x, mod, norm_g, wqkv, q_norm, k_norm, seq_len):
    n_rows = x.shape[0]
    tm = ROW_TILE
    nq = N_HEADS * HEAD_DIM
    nk = N_KV_HEADS * HEAD_DIM
    cos, sin = _rope_tables(seq_len)
    qg = jnp.tile(q_norm, N_HEADS)[None, :] * (HEAD_DIM ** -0.5 * math.log2(math.e))
    kg = jnp.tile(k_norm, N_KV_HEADS)[None, :]
    head = np.arange(nq) // HEAD_DIM
    headmean = jnp.asarray((head[:, None] == head[None, :]).astype(np.float32) / HEAD_DIM, BF16)
    per_seq = seq_len // tm
    tab_spec = pl.BlockSpec((tm, LANES), lambda i: (i % per_seq, 0))
    return pl.pallas_call(
        _qkv_kernel,
        grid=(n_rows // tm,),
        in_specs=[
            _row_spec(tm), _mod_spec(tm, seq_len), _const_spec((1, D_MODEL)),
            _const_spec((D_MODEL, nq + 2 * nk)), _const_spec((1, nq)), _const_spec((1, nk)),
            _const_spec((nq, nq)), tab_spec, tab_spec,
        ],
        out_specs=[
            _row_spec(tm, nq),
            pl.BlockSpec((N_KV_HEADS, tm, HEAD_DIM), lambda i: (0, i, 0)),
            pl.BlockSpec((N_KV_HEADS, tm, 2 * HEAD_DIM), lambda i: (0, i, 0)),
        ],
        out_shape=[
            jax.ShapeDtypeStruct((n_rows, nq), BF16),
            jax.ShapeDtypeStruct((N_KV_HEADS, n_rows, HEAD_DIM), BF16),
            jax.ShapeDtypeStruct((N_KV_HEADS, n_rows, 2 * HEAD_DIM), BF16),
        ],
        compiler_params=_params(("parallel",)),
        name="attn_qkv",
    )(x, mod, norm_g[None, :], wqkv.astype(BF16), qg, kg, headmean, cos, sin)


def _flash_kernel(q_ref, k_ref, v_ref, o_ref, *, tq, tk, n_chunks):
    q = q_ref[...]
    qs = jnp.concatenate([q[:, j * HEAD_DIM:(j + 1) * HEAD_DIM] for j in range(Q_PER_KV)], axis=0)
    rows = Q_PER_KV * tq

    def body(c, carry):
        m, acc = carry
        start = pl.multiple_of(c * tk, tk)
        kc = k_ref[0, pl.ds(start, tk), :]
        vc = v_ref[0, pl.ds(start, tk), :]
        s = lax.dot_general(qs, kc, (((1,), (1,)), ((), ())), preferred_element_type=F32)
        m_new = jnp.maximum(m, jnp.max(s, axis=-1, keepdims=True))
        alpha = jnp.exp2(m - m_new)
        p = jnp.exp2(s - m_new)
        acc = acc * alpha + _dot(p.astype(BF16), vc)
        return m_new, acc

    m0 = jnp.full((rows, 1), -jnp.inf, F32)
    acc0 = jnp.zeros((rows, 2 * HEAD_DIM), F32)
    _, acc = lax.fori_loop(0, n_chunks, body, (m0, acc0))
    o = acc[:, :HEAD_DIM] / acc[:, HEAD_DIM:HEAD_DIM + 1]
    o_ref[...] = jnp.concatenate([o[j * tq:(j + 1) * tq] for j in range(Q_PER_KV)],
                                 axis=1).astype(BF16)


def attn_flash(q, k, v, batch, seq_len):
    n_rows = q.shape[0]
    tq = 256
    tk = min(seq_len, 2048)
    per_seq = seq_len // tq
    width = Q_PER_KV * HEAD_DIM
    kern = functools.partial(_flash_kernel, tq=tq, tk=tk, n_chunks=seq_len // tk)
    return pl.pallas_call(
        kern,
        grid=(batch, N_KV_HEADS, per_seq),
        in_specs=[
            pl.BlockSpec((tq, width), lambda b, g, i: (b * per_seq + i, g)),
            pl.BlockSpec((1, seq_len, HEAD_DIM), lambda b, g, i: (g, b, 0)),
            pl.BlockSpec((1, seq_len, 2 * HEAD_DIM), lambda b, g, i: (g, b, 0)),
        ],
        out_specs=pl.BlockSpec((tq, width), lambda b, g, i: (b * per_seq + i, g)),
        out_shape=jax.ShapeDtypeStruct((n_rows, N_HEADS * HEAD_DIM), BF16),
        compiler_params=_params(("parallel", "parallel", "parallel")),
        name="attn_flash",
    )(q, k, v)


def _proj_res_kernel(y_ref, w_ref, x_ref, mod_ref, *refs, gate_row):
    route_in, o_ref, route_out = refs[:N_ROUTE_IN], refs[N_ROUTE_IN], refs[N_ROUTE_IN + 1:]
    m = mod_ref[0]
    x1 = x_ref[...] + m[gate_row:gate_row + 1] * _dot(y_ref[...], w_ref[...])
    o_ref[...] = x1
    _route_tile(x1, m, *route_in, *route_out)


def proj_residual(y, w, x, mod, gate_row, seq_len, route):
    n_rows = x.shape[0]
    tm = STREAM_TILE
    r_ops, r_in, r_out, r_shape = _router_io(route, tm, n_rows)
    return pl.pallas_call(
        functools.partial(_proj_res_kernel, gate_row=gate_row),
        grid=(n_rows // tm,),
        in_specs=[_row_spec(tm, y.shape[1]), _const_spec(w.shape), _row_spec(tm),
                  _mod_spec(tm, seq_len)] + r_in,
        out_specs=[_row_spec(tm)] + r_out,
        out_shape=[jax.ShapeDtypeStruct((n_rows, D_MODEL), F32)] + r_shape,
        compiler_params=_params(("arbitrary",)),
        name="proj_residual",
    )(y, w.astype(BF16), x, mod, *r_ops)


CONV_COLS = 256


def _hy_in_kernel(prev_ref, x_ref, next_ref, mod_ref, g_ref, w_ref, cw_ref, cb_ref,
                  z_ref, x0_ref, *stage_refs, tm, seq_len):
    first, last = _edge_flags(tm, seq_len)
    m = mod_ref[0]
    h = _modulate(_ext_rows(prev_ref, x_ref, next_ref), g_ref[...], m[0:1], m[1:2]).astype(BF16)
    tn = CONV_COLS
    for j in range(D_MODEL // tn):
        part = []
        for s in range(3):
            cols = slice(s * D_MODEL + j * tn, s * D_MODEL + (j + 1) * tn)
            part.append(_conv3(_dot(h, w_ref[:, cols]), cw_ref, cb_ref, cols, tm, first, last,
                               stage_refs[s]))
        out_cols = slice(j * tn, (j + 1) * tn)
        x0_ref[:, out_cols] = part[0].astype(BF16)
        z_ref[:, out_cols] = part[2] * part[1]


def hyena_in(x, mod, norm_g, w_in, conv_w, conv_b, seq_len):
    n_rows = x.shape[0]
    tm = ROW_TILE
    prev, nxt = _halo_specs(tm, n_rows)
    return pl.pallas_call(
        functools.partial(_hy_in_kernel, tm=tm, seq_len=seq_len),
        grid=(n_rows // tm,),
        in_specs=[prev, _row_spec(tm), nxt, _mod_spec(tm, seq_len), _const_spec((1, D_MODEL)),
                  _const_spec((D_MODEL, 3 * D_MODEL)), _const_spec((3, 3 * D_MODEL)),
                  _const_spec((1, 3 * D_MODEL))],
        out_specs=[_row_spec(tm), _row_spec(tm)],
        out_shape=[jax.ShapeDtypeStruct((n_rows, D_MODEL), F32),
                   jax.ShapeDtypeStruct((n_rows, D_MODEL), BF16)],
        scratch_shapes=[pltpu.VMEM((tm + 2 * HALO, CONV_COLS), F32)] * 3,
        compiler_params=_params(("parallel",)),
        name="hyena_in",
    )(x, x, x, mod, norm_g[None, :], w_in.astype(BF16), conv_w, conv_b[None, :])


def _hy_filter_kernel(feat_ref, w1_ref, b1_ref, w2_ref, b2_ref, w3_ref, fr_ref, dl_ref, o_ref):
    feat = feat_ref[...]
    fr = fr_ref[...]
    a = jnp.sin(fr * (_dot3(feat, w1_ref[...]) + b1_ref[...]))
    a = jnp.sin(fr * (_dot3(a, w2_ref[...]) + b2_ref[...]))
    hf = _dot3(a, w3_ref[...])
    decay = jnp.exp(-feat[:, 0:1] * dl_ref[...])
    o_ref[0] = hf[:, :D_MODEL] * decay
    o_ref[1] = hf[:, D_MODEL:] * decay


def _pad_to(a, rows, cols):
    return jnp.pad(a.astype(F32), ((0, rows - a.shape[0]), (0, cols - a.shape[1])))


def hyena_filter(seq_len, w1, b1, w2, b2, w3, freq):
    t = jnp.linspace(0.0, 1.0, seq_len, dtype=F32)[:, None]
    w = 2.0 * math.pi * jnp.arange(seq_len, dtype=F32)[:, None] / seq_len
    f = jnp.linspace(1e-4, HY_BANDS - 1, HY_BANDS, dtype=F32)[None, :]
    feat = _pad_to(jnp.concatenate([t, jnp.cos(f * w), -jnp.sin(f * w)], axis=-1), seq_len, LANES)
    max_decay = math.log(HY_TARGET) / HY_FAST_DECAY
    min_decay = math.log(HY_TARGET) / HY_SLOW_DECAY
    absdelta = jnp.abs(jnp.linspace(min_decay, max_decay, D_MODEL, dtype=F32))[None, :]
    tl = 512
    return pl.pallas_call(
        _hy_filter_kernel,
        grid=(seq_len // tl,),
        in_specs=[_row_spec(tl, LANES), _const_spec((LANES, LANES)), _const_spec((1, LANES)),
                  _const_spec((LANES, LANES)), _const_spec((1, LANES)),
                  _const_spec((LANES, 2 * D_MODEL)), _const_spec((1, LANES)),
                  _const_spec((1, D_MODEL))],
        out_specs=pl.BlockSpec((2, tl, D_MODEL), lambda i: (0, i, 0)),
        out_shape=jax.ShapeDtypeStruct((2, seq_len, D_MODEL), F32),
        compiler_params=_params(("parallel",)),
        name="hyena_filter",
    )(feat, _pad_to(w1, LANES, LANES), _pad_to(b1[None, :], 1, LANES), _pad_to(w2, LANES, LANES),
      _pad_to(b2[None, :], 1, LANES), _pad_to(w3, LANES, 2 * D_MODEL),
      _pad_to(freq[None, :], 1, LANES), absdelta)


class _FFTPlan:
    def __init__(self, seq_len):
        self.n = 2 * seq_len
        self.n1 = self.n // DFT_N2
        self.r = self.n1 // 2
        self.k1n = self.n1 // 2 + 1
        self.kron = max(HALO, LANES // self.r)
        ang = 2.0 * np.pi * np.outer(np.arange(self.k1n), np.arange(self.r)) / self.n1
        eye = np.eye(self.kron)
        self.fwd_cos = jnp.asarray(np.kron(np.cos(ang), eye), BF16)
        self.fwd_sin = jnp.asarray(np.kron(-np.sin(ang), eye), BF16)
        wgt = np.full((self.k1n,), 2.0)
        wgt[0] = wgt[-1] = 1.0
        scale = (wgt / self.n)[None, :]
        self.inv_cos = jnp.asarray(np.kron(np.cos(ang).T * scale, eye), BF16)
        self.inv_sin = jnp.asarray(np.kron(-np.sin(ang).T * scale, eye), BF16)
        a2 = 2.0 * np.pi * np.outer(np.arange(DFT_N2), np.arange(DFT_N2)) / DFT_N2
        self.f_cos = jnp.asarray(np.cos(a2), F32)
        self.f_sin = jnp.asarray(-np.sin(a2), F32)
        tw = 2.0 * np.pi * np.outer(np.arange(self.k1n), np.arange(DFT_N2)) / self.n
        self.tw_cos = jnp.asarray(np.cos(tw), F32)
        self.tw_sin = jnp.asarray(-np.sin(tw), F32)


def _fft_a_kernel(z_ref, wc_ref, ws_ref, ar_ref, ai_ref, *, rq):
    shape = ar_ref.shape[1:2] + ar_ref.shape[3:]
    for t in range(z_ref.shape[2]):
        z = z_ref[0, :, t].reshape(rq, D_MODEL).astype(BF16)
        ar_ref[0, :, t] = _dot(wc_ref[...], z).reshape(shape).astype(BF16)
        ai_ref[0, :, t] = _dot(ws_ref[...], z).reshape(shape).astype(BF16)


def _fft_group(plan):
    nhi = DFT_N2 // plan.kron
    per_group = plan.r * plan.kron * D_MODEL * 4
    return max(1, min(nhi, (2 << 20) // per_group))


def fft_stage_a(z, plan, batch):
    q = plan.kron
    nhi = DFT_N2 // q
    hb = _fft_group(plan)
    zv = z.reshape(batch, plan.r, nhi, q, D_MODEL)
    out = jax.ShapeDtypeStruct((batch, plan.k1n, nhi, q, D_MODEL), BF16)
    ospec = pl.BlockSpec((1, plan.k1n, hb, q, D_MODEL), lambda b, h: (b, 0, h, 0, 0))
    wspec = pl.BlockSpec(plan.fwd_cos.shape, lambda b, h: (0, 0))
    ar, ai = pl.pallas_call(
        functools.partial(_fft_a_kernel, rq=plan.r * q),
        grid=(batch, nhi // hb),
        in_specs=[pl.BlockSpec((1, plan.r, hb, q, D_MODEL), lambda b, h: (b, 0, h, 0, 0)),
                  wspec, wspec],
        out_specs=[ospec, ospec],
        out_shape=[out, out],
        compiler_params=_params(("parallel", "parallel")),
        name="fft_stage_a",
    )(zv, plan.fwd_cos, plan.fwd_sin)
    shape = (batch, plan.k1n, DFT_N2, D_MODEL)
    return ar.reshape(shape), ai.reshape(shape)


def _twiddled_dft(fr, fi, tr, ti):
    return (fr * tr - fi * ti).astype(BF16), (fr * ti + fi * tr).astype(BF16)


def _fft_b_fwd_kernel(ar_ref, ai_ref, fr_ref, fi_ref, twr_ref, twi_ref, br_ref, bi_ref,
                      gr_ref, gi_ref):
    @pl.when(pl.program_id(1) == 0)
    def _():
        gr, gi = _twiddled_dft(fr_ref[...], fi_ref[...], twr_ref[0], twi_ref[0])
        gr_ref[...] = gr
        gi_ref[...] = gi

    gr = gr_ref[...]
    gi = gi_ref[...]
    ar = ar_ref[0, 0].astype(BF16)
    ai = ai_ref[0, 0].astype(BF16)
    br_ref[0, 0] = _dot(gr, ar) - _dot(gi, ai)
    bi_ref[0, 0] = _dot(gr, ai) + _dot(gi, ar)


def fft_stage_b_fwd(ar, ai, plan):
    batch = ar.shape[0]
    blk = pl.BlockSpec((1, 1, DFT_N2, D_MODEL), lambda k, b: (b, k, 0, 0))
    cst = pl.BlockSpec((DFT_N2, DFT_N2), lambda k, b: (0, 0))
    tws = pl.BlockSpec((1, 1, DFT_N2), lambda k, b: (k, 0, 0))
    out = jax.ShapeDtypeStruct(ar.shape, F32)
    return pl.pallas_call(
        _fft_b_fwd_kernel,
        grid=(plan.k1n, batch),
        in_specs=[blk, blk, cst, cst, tws, tws],
        out_specs=[blk, blk],
        out_shape=[out, out],
        scratch_shapes=[pltpu.VMEM((DFT_N2, DFT_N2), BF16)] * 2,
        compiler_params=_params(("parallel", "arbitrary")),
        name="fft_stage_b_fwd",
    )(ar, ai, plan.f_cos, plan.f_sin, plan.tw_cos[:, None, :], plan.tw_sin[:, None, :])


def _fft_b_conv_kernel(ar_ref, ai_ref, hr_ref, hi_ref, hb0_ref, fr_ref, fi_ref,
                       twr_ref, twi_ref, tcr_ref, tci_ref, cr_ref, ci_ref,
                       gr_ref, gi_ref, gtr_ref, gti_ref, kr_ref, ki_ref):
    @pl.when(pl.program_id(1) == 0)
    def _():
        fr = fr_ref[...]
        fi = fi_ref[...]
        gr, gi = _twiddled_dft(fr, fi, twr_ref[0], twi_ref[0])
        gr_ref[...] = gr
        gi_ref[...] = gi
        gtr, gti = _twiddled_dft(fr, fi, tcr_ref[0], tci_ref[0])
        gtr_ref[...] = gtr
        gti_ref[...] = gti
        kr_ref[...] = hr_ref[0, 0] + hr_ref[1, 0] - hb0_ref[...]
        ki_ref[...] = hi_ref[0, 0] - hi_ref[1, 0]

    gr = gr_ref[...]
    gi = gi_ref[...]
    ar = ar_ref[0, 0].astype(BF16)
    ai = ai_ref[0, 0].astype(BF16)
    br = _dot(gr, ar) - _dot(gi, ai)
    bi = _dot(gr, ai) + _dot(gi, ar)
    kr = kr_ref[...]
    ki = ki_ref[...]
    pr = (br * kr - bi * ki).astype(BF16)
    pi = (br * ki + bi * kr).astype(BF16)
    gtr = gtr_ref[...]
    gti = gti_ref[...]
    cr_ref[0, 0] = (_dot(gtr, pr) + _dot(gti, pi)).astype(BF16)
    ci_ref[0, 0] = (_dot(gtr, pi) - _dot(gti, pr)).astype(BF16)


def fft_stage_b_conv(ar, ai, hr, hi, hb0, plan):
    batch = ar.shape[0]
    blk = pl.BlockSpec((1, 1, DFT_N2, D_MODEL), lambda k, b: (b, k, 0, 0))
    hblk = pl.BlockSpec((2, 1, DFT_N2, D_MODEL), lambda k, b: (0, k, 0, 0))
    cst = pl.BlockSpec((DFT_N2, DFT_N2), lambda k, b: (0, 0))
    tws = pl.BlockSpec((1, 1, DFT_N2), lambda k, b: (k, 0, 0))
    twc = pl.BlockSpec((1, DFT_N2, 1), lambda k, b: (k, 0, 0))
    out = jax.ShapeDtypeStruct(ar.shape, BF16)
    return pl.pallas_call(
        _fft_b_conv_kernel,
        grid=(plan.k1n, batch),
        in_specs=[blk, blk, hblk, hblk, pl.BlockSpec((1, D_MODEL), lambda k, b: (0, 0)),
                  cst, cst, tws, tws, twc, twc],
        out_specs=[blk, blk],
        out_shape=[out, out],
        scratch_shapes=[pltpu.VMEM((DFT_N2, DFT_N2), BF16)] * 4
        + [pltpu.VMEM((DFT_N2, D_MODEL), F32)] * 2,
        compiler_params=_params(("parallel", "arbitrary")),
        name="fft_stage_b_conv",
    )(ar, ai, hr, hi, hb0, plan.f_cos, plan.f_sin,
      plan.tw_cos[:, None, :], plan.tw_sin[:, None, :],
      plan.tw_cos[:, :, None], plan.tw_sin[:, :, None])


def _fft_a_inv_kernel(cr_ref, ci_ref, vc_ref, vs_ref, z_ref, x0_ref, skip_ref, y_ref, *, kq):
    shape = z_ref.shape[1:2] + z_ref.shape[3:]
    for t in range(z_ref.shape[2]):
        cr = cr_ref[0, :, t].reshape(kq, D_MODEL).astype(BF16)
        ci = ci_ref[0, :, t].reshape(kq, D_MODEL).astype(BF16)
        conv = _dot(vc_ref[...], cr) + _dot(vs_ref[...], ci)
        y = conv.reshape(shape) + z_ref[0, :, t] * skip_ref[...]
        y_ref[0, :, t] = (y * x0_ref[0, :, t].astype(F32)).astype(BF16)


def fft_stage_a_inv(cr, ci, z, x0, skip, plan, batch):
    q = plan.kron
    nhi = DFT_N2 // q
    hb = _fft_group(plan)
    cshape = (batch, plan.k1n, nhi, q, D_MODEL)
    tshape = (batch, plan.r, nhi, q, D_MODEL)
    cspec = pl.BlockSpec((1, plan.k1n, hb, q, D_MODEL), lambda b, h: (b, 0, h, 0, 0))
    tspec = pl.BlockSpec((1, plan.r, hb, q, D_MODEL), lambda b, h: (b, 0, h, 0, 0))
    wspec = pl.BlockSpec(plan.inv_cos.shape, lambda b, h: (0, 0))
    y = pl.pallas_call(
        functools.partial(_fft_a_inv_kernel, kq=plan.k1n * q),
        grid=(batch, nhi // hb),
        in_specs=[cspec, cspec, wspec, wspec, tspec, tspec,
                  pl.BlockSpec((1, D_MODEL), lambda b, h: (0, 0))],
        out_specs=tspec,
        out_shape=jax.ShapeDtypeStruct(tshape, BF16),
        compiler_params=_params(("parallel", "parallel")),
        name="fft_stage_a_inv",
    )(cr.reshape(cshape), ci.reshape(cshape), plan.inv_cos, plan.inv_sin,
      z.reshape(tshape), x0.reshape(tshape), skip[None, :])
    return y.reshape(z.shape)


def hyena_mixer(x, mod, norm_g, p, batch, seq_len, route):
    z, x0 = hyena_in(x, mod, norm_g, p["w_in"], p["conv_w"], p["conv_b"], seq_len)
    plan = _FFTPlan(seq_len)
    filt = hyena_filter(seq_len, p["w1"], p["b1"], p["w2"], p["b2"], p["w3"], p["freq"])
    fr, fi = fft_stage_a(filt.reshape(2 * seq_len, D_MODEL), plan, 2)
    hr, hi = fft_stage_b_fwd(fr, fi, plan)
    ar, ai = fft_stage_a(z, plan, batch)
    cr, ci = fft_stage_b_conv(ar, ai, hr, hi, filt[1, 0:1, :], plan)
    y = fft_stage_a_inv(cr, ci, z, x0, p["skip"], plan, batch)
    return proj_residual(y, p["w_out"], x, mod, 2, seq_len, route)


def _pool_kernel(prev_ref, x_ref, next_ref, mod_ref, g_ref, w_ref, s_ref, *refs, tm, seq_len):
    route_in, o_ref, route_out = refs[:N_ROUTE_IN], refs[N_ROUTE_IN], refs[N_ROUTE_IN + 1:]
    first, last = _edge_flags(tm, seq_len)
    m = mod_ref[0]
    x = x_ref[...]
    h = _modulate(_ext_rows(prev_ref, x_ref, next_ref), g_ref[...], m[0:1], m[1:2])
    n = tm + 2 * HALO
    row = lax.broadcasted_iota(jnp.int32, (n, 1), 0)
    outside = jnp.logical_or(jnp.logical_and(first, row < HALO),
                             jnp.logical_and(last, row >= HALO + tm))
    h = jnp.where(outside, 0.0, h)
    pos = (pl.program_id(0) * tm) % seq_len + lax.broadcasted_iota(jnp.int32, (tm, 1), 0)
    ys = []
    for gi, win in enumerate(POOL_WINDOWS):
        cols = slice(gi * POOL_GROUP_DIM, (gi + 1) * POOL_GROUP_DIM)
        hg = h[:, cols]
        acc = hg
        span = 1
        while span < win:
            acc = acc + pltpu.roll(acc, span, axis=0)
            span *= 2
        lead = win // 2 - 1
        if lead:
            acc = pltpu.roll(acc, n - lead, axis=0)
        half = win // 2
        cnt = jnp.minimum(pos + half, seq_len) - jnp.maximum(pos - half, 0)
        pooled = acc[HALO:HALO + tm] / cnt.astype(F32) - hg[HALO:HALO + tm]
        ys.append(_dot(pooled.astype(BF16), w_ref[gi]))
    y = jnp.concatenate(ys, axis=1) * s_ref[...]
    x1 = x + m[2:3] * y
    o_ref[...] = x1
    _route_tile(x1, m, *route_in, *route_out)


def pool_mixer(x, mod, norm_g, w_group, scale, seq_len, route):
    n_rows = x.shape[0]
    tm = ROW_TILE
    prev, nxt = _halo_specs(tm, n_rows)
    r_ops, r_in, r_out, r_shape = _router_io(route, tm, n_rows)
    return pl.pallas_call(
        functools.partial(_pool_kernel, tm=tm, seq_len=seq_len),
        grid=(n_rows // tm,),
        in_specs=[prev, _row_spec(tm), nxt, _mod_spec(tm, seq_len), _const_spec((1, D_MODEL)),
                  _const_spec(w_group.shape), _const_spec((1, D_MODEL))] + r_in,
        out_specs=[_row_spec(tm)] + r_out,
        out_shape=[jax.ShapeDtypeStruct((n_rows, D_MODEL), F32)] + r_shape,
        compiler_params=_params(("arbitrary",)),
        name="pool_mixer",
    )(x, x, x, mod, norm_g[None, :], w_group.astype(BF16), scale[None, :], *r_ops)


def _sconv_kernel(prev_ref, x_ref, next_ref, mod_ref, g_ref, w_ref, cw_ref, cb_ref, wo_ref,
                  *refs, tm, seq_len):
    route_in, o_ref = refs[:N_ROUTE_IN], refs[N_ROUTE_IN]
    route_out, y_ref = refs[N_ROUTE_IN + 1:N_ROUTE_IN + 1 + N_ROUTE_OUT], refs[-1]
    first, last = _edge_flags(tm, seq_len)
    m = mod_ref[0]
    h = _modulate(_ext_rows(prev_ref, x_ref, next_ref), g_ref[...], m[0:1], m[1:2]).astype(BF16)
    tn = 256
    for j in range(D_MODEL // tn):
        cols = slice(j * tn, (j + 1) * tn)
        bg = _dot(h, w_ref[:, cols])[HALO:HALO + tm]
        cg = _dot(h, w_ref[:, D_MODEL + j * tn:D_MODEL + (j + 1) * tn])
        hp = _dot(h, w_ref[:, 2 * D_MODEL + j * tn:2 * D_MODEL + (j + 1) * tn])
        y_ref[:, cols] = (bg * _conv3(cg * hp, cw_ref, cb_ref, cols, tm, first, last)).astype(BF16)
    x1 = x_ref[...] + m[2:3] * _dot(y_ref[...], wo_ref[...])
    o_ref[...] = x1
    _route_tile(x1, m, *route_in, *route_out)


def sconv_mixer(x, mod, norm_g, w_in, conv_w, conv_b, w_out, seq_len, route):
    n_rows = x.shape[0]
    tm = ROW_TILE
    prev, nxt = _halo_specs(tm, n_rows)
    r_ops, r_in, r_out, r_shape = _router_io(route, tm, n_rows)
    return pl.pallas_call(
        functools.partial(_sconv_kernel, tm=tm, seq_len=seq_len),
        grid=(n_rows // tm,),
        in_specs=[prev, _row_spec(tm), nxt, _mod_spec(tm, seq_len), _const_spec((1, D_MODEL)),
                  _const_spec((D_MODEL, 3 * D_MODEL)), _const_spec((3, D_MODEL)),
                  _const_spec((1, D_MODEL)), _const_spec((D_MODEL, D_MODEL))] + r_in,
        out_specs=[_row_spec(tm)] + r_out,
        out_shape=[jax.ShapeDtypeStruct((n_rows, D_MODEL), F32)] + r_shape,
        scratch_shapes=[pltpu.VMEM((tm, D_MODEL), BF16)],
        compiler_params=_params(("arbitrary",)),
        name="sconv_mixer",
    )(x, x, x, mod, norm_g[None, :], w_in.astype(BF16), conv_w, conv_b[None, :],
      w_out.astype(BF16), *r_ops)


def _pack_bf16(x):
    w = x.shape[1] // 2
    bits = pltpu.bitcast(x.astype(BF16).astype(F32), jnp.uint32)
    return (bits[:, :w] >> 16) | bits[:, w:]


def _unpack_bf16(p):
    lo = pltpu.bitcast(p << 16, F32)
    hi = pltpu.bitcast(p & jnp.uint32(0xFFFF0000), F32)
    return jnp.concatenate([lo, hi], axis=1)


META_E1, META_E2, META_W1, META_W2, META_R1, META_R2 = range(6)
META_ROWS = 8


def _route_tile(x, m, g_ref, wh_ref, b_ref, tri_ref, h_ref, meta_ref, meta_t_ref, cnt_ref):
    @pl.when(pl.program_id(0) == 0)
    def _():
        cnt_ref[...] = jnp.zeros_like(cnt_ref)

    ms = jnp.mean(x * x, axis=-1, keepdims=True)
    h = x * lax.rsqrt(ms + EPS) * (g_ref[...] * (1.0 + m[4:5])) + m[3:4]
    hi = h.astype(BF16)
    hi32 = hi.astype(F32)
    lo = (h - hi32).astype(BF16)
    half = D_MODEL // 2
    bits = pltpu.bitcast(hi32, jnp.uint32)
    h_ref[...] = (bits[:, :half] >> 16) | bits[:, half:]
    part = _dot(hi, wh_ref[...])
    lg = part[:, :LANES] + (part[:, LANES:] + _dot(lo, wh_ref[:, :LANES])) + b_ref[...]
    lgt = lg.T
    tm = lgt.shape[1]
    neg = -jnp.inf
    sub = lax.broadcasted_iota(jnp.int32, (HALO, tm), 0).astype(F32)

    def first_argmax(vals):
        top = jnp.max(vals, axis=0, keepdims=True)
        idx = jnp.min(jnp.where(vals == top, sub, float(HALO)), axis=0, keepdims=True)
        return top, idx

    gl = jnp.where(sub < MOE_GROUPS, lgt[MOE_N_EXPERTS:MOE_N_EXPERTS + HALO], neg)
    gmax, grp = first_argmax(gl)
    g_w = 1.0 / jnp.sum(jnp.exp(gl - gmax), axis=0, keepdims=True)
    el = lgt[:MOE_EXPERTS_PER_GROUP]
    for g in range(1, MOE_GROUPS):
        el = jnp.where(grp == float(g),
                       lgt[g * MOE_EXPERTS_PER_GROUP:(g + 1) * MOE_EXPERTS_PER_GROUP], el)
    v1, i1 = first_argmax(el)
    v2, i2 = first_argmax(jnp.where(sub == i1, neg, el))
    ex = jnp.exp(v2 - v1)
    w1 = 1.0 / (1.0 + ex)
    w2 = ex * w1
    e1 = grp * MOE_EXPERTS_PER_GROUP + i1
    e2 = grp * MOE_EXPERTS_PER_GROUP + i2
    expert = lax.broadcasted_iota(jnp.int32, (MOE_N_EXPERTS, tm), 0).astype(F32)
    onehot = jnp.where(jnp.logical_or(expert == e1, expert == e2), 1.0, 0.0)
    before = _dot(onehot.astype(BF16), tri_ref[...]) + cnt_ref[...]
    cnt_ref[...] += jnp.sum(onehot, axis=1, keepdims=True)
    r1 = jnp.sum(jnp.where(expert == e1, before, 0.0), axis=0, keepdims=True)
    r2 = jnp.sum(jnp.where(expert == e2, before, 0.0), axis=0, keepdims=True)
    fields = ((META_E1, e1), (META_E2, e2), (META_W1, w1 * g_w), (META_W2, w2 * g_w),
              (META_R1, r1), (META_R2, r2))
    field = lax.broadcasted_iota(jnp.int32, (LANES, tm), 0)
    meta_t = jnp.zeros((LANES, tm), F32)
    for row, val in fields:
        meta_t = jnp.where(field == row, val, meta_t)
    meta_t_ref[...] = meta_t[:META_ROWS]
    meta_ref[...] = meta_t.T


N_ROUTE_IN = 4
N_ROUTE_OUT = 4


def _router_io(route, tm, n_rows):
    norm_g, w_group, b_group, w_router, b_router = route
    w = _pad_to(jnp.concatenate([w_router, w_group], axis=1), D_MODEL, LANES)
    wh = w.astype(BF16)
    wl = (w - wh.astype(F32)).astype(BF16)
    whl = jnp.concatenate([wh, wl], axis=1)
    b = _pad_to(jnp.concatenate([b_router, b_group])[None, :], 1, LANES)
    tri = jnp.asarray(np.triu(np.ones((tm, tm), np.float32), 1), BF16)
    operands = (norm_g[None, :], whl, b, tri)
    in_specs = [_const_spec((1, D_MODEL)), _const_spec((D_MODEL, 2 * LANES)),
                _const_spec((1, LANES)), _const_spec((tm, tm))]
    out_specs = [_row_spec(tm, D_MODEL // 2), _row_spec(tm, LANES),
                 pl.BlockSpec((META_ROWS, tm), lambda i: (0, i)),
                 _const_spec((MOE_N_EXPERTS, 1))]
    out_shape = [jax.ShapeDtypeStruct((n_rows, D_MODEL // 2), jnp.uint32),
                 jax.ShapeDtypeStruct((n_rows, LANES), F32),
                 jax.ShapeDtypeStruct((META_ROWS, n_rows), F32),
                 jax.ShapeDtypeStruct((MOE_N_EXPERTS, 1), F32)]
    return operands, in_specs, out_specs, out_shape


def _slot_kernel(offs_ref, meta_t_ref, pos_ref):
    meta = meta_t_ref[...]
    start = jnp.zeros_like(meta)
    for e in range(MOE_N_EXPERTS):
        start = jnp.where(meta == float(e), offs_ref[e], start)
    shift = META_ROWS - (META_R1 - META_E1)
    pos_ref[...] = (start + pltpu.roll(meta, shift, axis=0)).astype(jnp.int32)


def moe_slots(meta_t, offsets):
    n_rows = meta_t.shape[1]
    tn = min(n_rows, 8192)
    blk = pl.BlockSpec((META_ROWS, tn), lambda i, offs: (0, i))
    return pl.pallas_call(
        _slot_kernel,
        grid_spec=pltpu.PrefetchScalarGridSpec(
            num_scalar_prefetch=1, grid=(n_rows // tn,), in_specs=[blk], out_specs=blk),
        out_shape=jax.ShapeDtypeStruct((META_ROWS, n_rows), jnp.int32),
        compiler_params=_params(("parallel",)),
        name="moe_slots",
    )(offsets, meta_t)


def _expert_kernel(te_ref, nu_ref, nv_ref, xs_ref, wg_ref, wu_ref, wd_ref, o_ref,
                   wgu_ref, wdb_ref):
    j = pl.program_id(0)

    @pl.when(jnp.logical_or(j == 0, te_ref[j] != te_ref[jnp.maximum(j - 1, 0)]))
    def _():
        wgu_ref[:, :MOE_D_FF] = wg_ref[0, 0].astype(BF16)
        wgu_ref[:, MOE_D_FF:] = wu_ref[0, 0].astype(BF16)
        wdb_ref[...] = wd_ref[0, 0].astype(BF16)

    @pl.when(j < nu_ref[0])
    def _():
        x = _unpack_bf16(xs_ref[...])
        row = lax.broadcasted_iota(jnp.int32, (x.shape[0], 1), 0)
        x = jnp.where(row < nv_ref[j], x, 0.0).astype(BF16)
        au = _dot(x, wgu_ref[...])
        hh = (_silu(au[:, :MOE_D_FF]) * au[:, MOE_D_FF:]).astype(BF16)
        o_ref[...] = _pack_bf16(_dot(hh, wdb_ref[...]))


def moe_experts(xs, tile_expert, n_used, n_valid, w_gate, w_up, w_down, layer, tm):
    n_slots, half = xs.shape
    wspec = lambda shape: pl.BlockSpec((1, 1) + shape, lambda j, te, nu, nv: (layer, te[j], 0, 0))
    row = pl.BlockSpec((tm, half), lambda j, te, nu, nv: (j, 0))
    grid_spec = pltpu.PrefetchScalarGridSpec(
        num_scalar_prefetch=3,
        grid=(n_slots // tm,),
        in_specs=[row, wspec((D_MODEL, MOE_D_FF)), wspec((D_MODEL, MOE_D_FF)),
                  wspec((MOE_D_FF, D_MODEL))],
        out_specs=row,
        scratch_shapes=[pltpu.VMEM((D_MODEL, 2 * MOE_D_FF), BF16),
                        pltpu.VMEM((MOE_D_FF, D_MODEL), BF16)],
    )
    return pl.pallas_call(
        _expert_kernel,
        grid_spec=grid_spec,
        out_shape=jax.ShapeDtypeStruct((n_slots, half), jnp.uint32),
        compiler_params=_params(("arbitrary",)),
        name="moe_experts",
    )(tile_expert, n_used, n_valid, xs, w_gate, w_up, w_down)


def _combine_kernel(x_ref, ya_ref, yb_ref, meta_ref, mod_ref, o_ref):
    meta = meta_ref[...]
    y = (meta[:, META_W1:META_W1 + 1] * _unpack_bf16(ya_ref[...])
         + meta[:, META_W2:META_W2 + 1] * _unpack_bf16(yb_ref[...]))
    o_ref[...] = x_ref[...] + mod_ref[0][5:6] * y


def moe_combine(x, ya, yb, meta, mod, seq_len):
    n_rows = x.shape[0]
    tm = STREAM_TILE
    half = D_MODEL // 2
    return pl.pallas_call(
        _combine_kernel,
        grid=(n_rows // tm,),
        in_specs=[_row_spec(tm), _row_spec(tm, half), _row_spec(tm, half), _row_spec(tm, LANES),
                  _mod_spec(tm, seq_len)],
        out_specs=_row_spec(tm),
        out_shape=jax.ShapeDtypeStruct((n_rows, D_MODEL), F32),
        compiler_params=_params(("parallel",)),
        name="moe_combine",
    )(x, ya, yb, meta, mod)


MOE_TILE = 512
MOE_TILE_LARGE = 1024
MOE_LARGE_ROWS = 32768


def hier_moe(routed, mod, w_gate, w_up, w_down, layer, seq_len):
    x, hp, meta, meta_t, counts = routed
    n_rows = x.shape[0]
    tm = MOE_TILE_LARGE if n_rows >= MOE_LARGE_ROWS else MOE_TILE
    cnt = counts[:, 0].astype(jnp.int32)
    padded = (cnt + tm - 1) // tm * tm
    ends = jnp.cumsum(padded)
    starts = ends - padded
    n_slots = 2 * n_rows + MOE_N_EXPERTS * tm
    tile_start = jnp.arange(n_slots // tm, dtype=jnp.int32) * tm
    tile_expert = jnp.minimum(jnp.sum(tile_start[:, None] >= ends[None, :], axis=1),
                              MOE_N_EXPERTS - 1).astype(jnp.int32)
    n_used = (ends[-1:] // tm).astype(jnp.int32)
    n_valid = jnp.clip(starts[tile_expert] + cnt[tile_expert] - tile_start, 0, tm).astype(jnp.int32)
    pos = moe_slots(meta_t, starts.astype(F32))
    idx0 = pos[META_E1].reshape(n_rows // SC_BLOCK, SC_BLOCK)
    idx1 = pos[META_E2].reshape(n_rows // SC_BLOCK, SC_BLOCK)
    xs = sc_scatter_rows(hp, idx0, idx1, n_slots)
    ys = moe_experts(xs, tile_expert, n_used, n_valid, w_gate, w_up, w_down, layer, tm)
    ya, yb = sc_gather_rows(ys, idx0, idx1)
    return moe_combine(x, ya, yb, meta, mod, seq_len)


SC_CORES = 2
SC_SUBCORES = 16
SC_WORKERS = SC_CORES * SC_SUBCORES
SC_BLOCK = 128


def _sc_mesh():
    return plsc.VectorSubcoreMesh(core_axis_name="c", subcore_axis_name="s")


def _sc_worker():
    return lax.axis_index("s") * SC_CORES + lax.axis_index("c")


def sc_scatter_rows(rows, idx0, idx1, n_slots):
    n_rows, width = rows.shape
    per_worker = n_rows // SC_BLOCK // SC_WORKERS

    @functools.partial(
        pl.kernel, mesh=_sc_mesh(),
        out_type=jax.ShapeDtypeStruct((n_slots, width), rows.dtype),
        scratch_types=[pltpu.VMEM((SC_BLOCK,), jnp.int32), pltpu.VMEM((SC_BLOCK,), jnp.int32),
                       pltpu.VMEM((SC_BLOCK, width), rows.dtype)],
        name="sc_scatter_rows",
    )
    def scatter(rows_hbm, i0_hbm, i1_hbm, out_hbm, i0_v, i1_v, rows_v):
        first = _sc_worker() * per_worker

        @pl.loop(0, per_worker)
        def _(j):
            blk = first + j
            pltpu.sync_copy(i0_hbm.at[blk], i0_v)
            pltpu.sync_copy(i1_hbm.at[blk], i1_v)
            pltpu.sync_copy(rows_hbm.at[pl.ds(blk * SC_BLOCK, SC_BLOCK)], rows_v)
            pltpu.sync_copy(rows_v, out_hbm.at[i0_v])
            pltpu.sync_copy(rows_v, out_hbm.at[i1_v])

    return scatter(rows, idx0, idx1)


def sc_gather_rows(src, idx0, idx1):
    width = src.shape[1]
    n_rows = idx0.shape[0] * SC_BLOCK
    per_worker = n_rows // SC_BLOCK // SC_WORKERS
    out = jax.ShapeDtypeStruct((n_rows, width), src.dtype)

    @functools.partial(
        pl.kernel, mesh=_sc_mesh(), out_type=(out, out),
        scratch_types=[pltpu.VMEM((SC_BLOCK,), jnp.int32), pltpu.VMEM((SC_BLOCK, width), src.dtype)],
        name="sc_gather_rows",
    )
    def gather(src_hbm, i0_hbm, i1_hbm, a_hbm, b_hbm, idx_v, rows_v):
        first = _sc_worker() * per_worker

        @pl.loop(0, per_worker)
        def _(j):
            blk = first + j
            dst = pl.ds(blk * SC_BLOCK, SC_BLOCK)
            pltpu.sync_copy(i0_hbm.at[blk], idx_v)
            pltpu.sync_copy(src_hbm.at[idx_v], rows_v)
            pltpu.sync_copy(rows_v, a_hbm.at[dst])
            pltpu.sync_copy(i1_hbm.at[blk], idx_v)
            pltpu.sync_copy(src_hbm.at[idx_v], rows_v)
            pltpu.sync_copy(rows_v, b_hbm.at[dst])

    return gather(src, idx0, idx1)


def _trunk(x3, mods, p):
    batch, seq_len, _ = x3.shape
    x = x3.reshape(batch * seq_len, D_MODEL)
    for i in range(DEPTH):
        mod = mods[i]
        g1 = p["norm1_g"][i]
        route = (p["norm2_g"][i], p["moe_w_group"][i], p["moe_b_group"][i], p["moe_w_router"][i],
                 p["moe_b_router"][i])
        kind = i % 4
        if kind == 0:
            q, k, v = attn_qkv(x, mod, g1, p["attn_wqkv"][0], p["attn_q_norm"][0],
                               p["attn_k_norm"][0], seq_len)
            o = attn_flash(q, k, v, batch, seq_len)
            routed = proj_residual(o, p["attn_wo"][0], x, mod, 2, seq_len, route)
        elif kind == 1:
            hp = {"w_in": p["hy_w_in"][0], "conv_w": p["hy_conv_w"][0], "conv_b": p["hy_conv_b"][0],
                  "w1": p["hy_ffn_w1"][0], "b1": p["hy_ffn_b1"][0], "w2": p["hy_ffn_w2"][0],
                  "b2": p["hy_ffn_b2"][0], "w3": p["hy_ffn_w3"][0], "freq": p["hy_freq"][0],
                  "skip": p["hy_skip"][0], "w_out": p["hy_w_out"][0]}
            routed = hyena_mixer(x, mod, g1, hp, batch, seq_len, route)
        elif kind == 2:
            routed = pool_mixer(x, mod, g1, p["pool_w"][0], p["pool_scale"][0], seq_len, route)
        else:
            routed = sconv_mixer(x, mod, g1, p["sc_w_in"][0], p["sc_conv_w"][0],
                                 p["sc_conv_b"][0], p["sc_w_out"][0], seq_len, route)
        x = hier_moe(routed, mod, p["moe_w_gate"], p["moe_w_up"], p["moe_w_down"], i, seq_len)
    return x.reshape(batch, seq_len, D_MODEL)


def kernel(x_prompt, x_sample, c_prompt, c_sample, norm1_g, norm2_g, ada_w, ada_b, attn_wqkv, attn_q_norm, attn_k_norm, attn_wo, hy_w_in, hy_conv_w, hy_conv_b, hy_ffn_w1, hy_ffn_b1, hy_ffn_w2, hy_ffn_b2, hy_ffn_w3, hy_freq, hy_skip, hy_w_out, pool_w, pool_scale, sc_w_in, sc_conv_w, sc_conv_b, sc_w_out, moe_w_group, moe_b_group, moe_w_router, moe_b_router, moe_w_gate, moe_w_up, moe_w_down):
    p = dict(norm1_g=norm1_g, norm2_g=norm2_g, attn_wqkv=attn_wqkv, attn_q_norm=attn_q_norm,
             attn_k_norm=attn_k_norm, attn_wo=attn_wo, hy_w_in=hy_w_in, hy_conv_w=hy_conv_w,
             hy_conv_b=hy_conv_b, hy_ffn_w1=hy_ffn_w1, hy_ffn_b1=hy_ffn_b1, hy_ffn_w2=hy_ffn_w2,
             hy_ffn_b2=hy_ffn_b2, hy_ffn_w3=hy_ffn_w3, hy_freq=hy_freq, hy_skip=hy_skip,
             hy_w_out=hy_w_out, pool_w=pool_w, pool_scale=pool_scale, sc_w_in=sc_w_in,
             sc_conv_w=sc_conv_w, sc_conv_b=sc_conv_b, sc_w_out=sc_w_out, moe_w_group=moe_w_group,
             moe_b_group=moe_b_group, moe_w_router=moe_w_router, moe_b_router=moe_b_router,
             moe_w_gate=moe_w_gate, moe_w_up=moe_w_up, moe_w_down=moe_w_down)
    nb = c_prompt.shape[0]
    ns = c_sample.shape[0]
    rows = -(-(nb + ns) // HALO) * HALO
    c_all = jnp.pad(jnp.concatenate([c_prompt, c_sample], axis=0), ((0, rows - nb - ns), (0, 0)))
    mod = ada_mod(c_all, ada_w, ada_b).reshape(DEPTH, rows, 6, D_MODEL)
    mods_prompt = [mod[i, :nb] for i in range(DEPTH)]
    mods_sample = [mod[i, nb:nb + ns] for i in range(DEPTH)]
    return _trunk(x_prompt, mods_prompt, p), _trunk(x_sample, mods_sample, p)
```

```python
import functools
import math
from typing import NamedTuple

import jax
import jax.numpy as jnp
import numpy as np
from jax import lax
from jax.experimental import pallas as pl
from jax.experimental.pallas import tpu as pltpu
from jax.experimental.pallas import tpu_sc as plsc

F32 = jnp.float32
BF16 = jnp.bfloat16

D_MODEL = 1024
DEPTH = 4
EPS = 1e-6
GRID_W = 64
HEAD_DIM = 64
N_HEADS = 16
N_KV_HEADS = 4
Q_PER_KV = 4
ROPE_THETA = 10000.0
ROPE_FREQS = 16
HY_EMB_DIM = 33
HY_BANDS = 16
HY_FILTER_WIDTH = 64
HY_FAST_DECAY = 0.3
HY_SLOW_DECAY = 1.5
HY_TARGET = 1e-2
POOL_WINDOWS = (2, 4, 8, 16)
POOL_GROUP_DIM = 256
MOE_GROUPS = 4
MOE_EXPERTS_PER_GROUP = 8
MOE_N_EXPERTS = 32
MOE_D_FF = 256

LANES = 128
HALO = 8
DFT_N2 = 256
VMEM_LIMIT = 56 * 1024 * 1024

ROW_TILE = 512
STREAM_TILE = 1024


def _params(sem):
    return pltpu.CompilerParams(dimension_semantics=sem, vmem_limit_bytes=VMEM_LIMIT)


def _dot(a, b):
    return jnp.dot(a, b, preferred_element_type=F32)


def _split(a):
    hi = a.astype(BF16)
    lo = (a - hi.astype(F32)).astype(BF16)
    return hi, lo


def _dot3(a, b):
    ah, al = _split(a)
    bh, bl = _split(b)
    return _dot(ah, bh) + (_dot(ah, bl) + _dot(al, bh))


def _modulate(x, g, shift, scale):
    ms = jnp.mean(x * x, axis=-1, keepdims=True)
    return x * lax.rsqrt(ms + EPS) * g * (1.0 + scale) + shift


def _silu(x):
    return x * (1.0 / (1.0 + jnp.exp(-x)))


def _ada_kernel(c_ref, w_ref, b_ref, o_ref):
    c = c_ref[...]
    o_ref[0] = _dot3(_silu(c), w_ref[0]) + b_ref[0]


def ada_mod(c_all, ada_w, ada_b):
    rows = c_all.shape[0]
    n = ada_w.shape[2]
    tn = 1536
    return pl.pallas_call(
        _ada_kernel,
        grid=(DEPTH, n // tn),
        in_specs=[
            pl.BlockSpec((rows, D_MODEL), lambda l, j: (0, 0)),
            pl.BlockSpec((1, D_MODEL, tn), lambda l, j: (l, 0, j)),
            pl.BlockSpec((1, 1, tn), lambda l, j: (l, 0, j)),
        ],
        out_specs=pl.BlockSpec((1, rows, tn), lambda l, j: (l, 0, j)),
        out_shape=jax.ShapeDtypeStruct((DEPTH, rows, n), F32),
        compiler_params=_params(("parallel", "parallel")),
        name="ada_mod",
    )(c_all, ada_w, ada_b.reshape(DEPTH, 1, n))


def _row_spec(tm, width=D_MODEL):
    return pl.BlockSpec((tm, width), lambda i: (i, 0))


def _mod_spec(tm, seq_len):
    return pl.BlockSpec((1, 6, D_MODEL), lambda i: ((i * tm) // seq_len, 0, 0))


def _const_spec(shape):
    nd = len(shape)
    return pl.BlockSpec(shape, lambda i: (0,) * nd)


def _halo_specs(tm, n_rows, width=D_MODEL):
    per = tm // HALO
    last = n_rows // HALO - 1
    prev = pl.BlockSpec((HALO, width), lambda i: (jnp.maximum(i * per - 1, 0), 0))
    nxt = pl.BlockSpec((HALO, width), lambda i: (jnp.minimum((i + 1) * per, last), 0))
    return prev, nxt


class _Pending(NamedTuple):
    x: jax.Array
    ya: jax.Array
    yb: jax.Array
    meta: jax.Array
    mod: jax.Array


N_PLAIN_REFS = 3
N_PENDING_REFS = 13


def _stream_io(src, tm, n_rows, seq_len):
    if not isinstance(src, _Pending):
        prev, nxt = _halo_specs(tm, n_rows)
        return [src] * 3, [prev, _row_spec(tm), nxt]
    operands, specs = [], []
    for arr in (src.x, src.ya, src.yb, src.meta):
        width = arr.shape[1]
        prev, nxt = _halo_specs(tm, n_rows, width)
        operands += [arr] * 3
        specs += [prev, _row_spec(tm, width), nxt]
    return operands + [src.mod], specs + [_mod_spec(tm, seq_len)]


def _moe_combined(x_ref, ya_ref, yb_ref, meta_ref, gate):
    meta = meta_ref[...]
    y = (meta[:, META_W1:META_W1 + 1] * _unpack_bf16(ya_ref[...])
         + meta[:, META_W2:META_W2 + 1] * _unpack_bf16(yb_ref[...]))
    return x_ref[...] + gate * y


def _stream_rows(refs):
    if len(refs) == N_PLAIN_REFS:
        prev_ref, x_ref, next_ref = refs
        x = x_ref[...]
        return x, jnp.concatenate([prev_ref[...], x, next_ref[...]], axis=0)
    gate = refs[-1][0][5:6]
    parts = [_moe_combined(refs[k], refs[3 + k], refs[6 + k], refs[9 + k], gate) for k in range(3)]
    return parts[1], jnp.concatenate(parts, axis=0)


def _edge_flags(tm, seq_len):
    i = pl.program_id(0)
    per_seq = seq_len // tm
    pos = i % per_seq
    return pos == 0, pos == per_seq - 1


def _shift_rows(u, tm):
    n = u.shape[0]
    up = pltpu.roll(u, 1, axis=0)[HALO:HALO + tm]
    dn = pltpu.roll(u, n - 1, axis=0)[HALO:HALO + tm]
    return up, u[HALO:HALO + tm], dn


def _conv3(u, w_ref, b_ref, cols, tm, first, last, stage_ref=None):
    if stage_ref is None:
        up, mid, dn = _shift_rows(u, tm)
    else:
        stage_ref[...] = u
        up = stage_ref[pl.ds(HALO - 1, tm), :]
        mid = stage_ref[pl.ds(HALO, tm), :]
        dn = stage_ref[pl.ds(HALO + 1, tm), :]
    row = lax.broadcasted_iota(jnp.int32, (tm, 1), 0)
    up = jnp.where(jnp.logical_and(first, row == 0), 0.0, up)
    dn = jnp.where(jnp.logical_and(last, row == tm - 1), 0.0, dn)
    w = w_ref[:, cols]
    return up * w[0:1] + mid * w[1:2] + dn * w[2:3] + b_ref[:, cols]


def _norm_rope(t, gain, headmean, cos, sin_signed):
    width = t.shape[1]
    ms = _dot((t * t).astype(BF16), headmean[:width, :width])
    y = t * lax.rsqrt(ms + EPS) * gain
    lane = lax.broadcasted_iota(jnp.int32, y.shape, 1)
    first = (lane % 32) < ROPE_FREQS
    partner = jnp.where(first, pltpu.roll(y, width - ROPE_FREQS, axis=1),
                        pltpu.roll(y, ROPE_FREQS, axis=1))
    reps = width // LANES
    return y * jnp.tile(cos, (1, reps)) + partner * jnp.tile(sin_signed, (1, reps))


def _qkv_kernel(x_ref, mod_ref, g_ref, w_ref, qg_ref, kg_ref, hm_ref, cos_ref, sin_ref,
                q_ref, k_ref, v_ref):
    m = mod_ref[0]
    h = _modulate(x_ref[...], g_ref[...], m[0:1], m[1:2]).astype(BF16)
    qkv = _dot(h, w_ref[...])
    nq = N_HEADS * HEAD_DIM
    nk = N_KV_HEADS * HEAD_DIM
    cos = cos_ref[...]
    sin = sin_ref[...]
    hm = hm_ref[...]
    q = _norm_rope(qkv[:, :nq], qg_ref[...], hm, cos, sin)
    k = _norm_rope(qkv[:, nq:nq + nk], kg_ref[...], hm, cos, sin)
    v = qkv[:, nq + nk:]
    q_ref[...] = q.astype(BF16)
    ones = jnp.ones((v.shape[0], HEAD_DIM), F32)
    for g in range(N_KV_HEADS):
        sl = slice(g * HEAD_DIM, (g + 1) * HEAD_DIM)
        k_ref[g] = k[:, sl].astype(BF16)
        v_ref[g] = jnp.concatenate([v[:, sl], ones], axis=1).astype(BF16)


def _rope_tables(seq_len):
    rows = seq_len // GRID_W
    r = jnp.broadcast_to(jnp.arange(rows)[:, None], (rows, GRID_W)).reshape(-1)
    c = jnp.broadcast_to(jnp.arange(GRID_W)[None, :], (rows, GRID_W)).reshape(-1)
    inv_freq = ROPE_THETA ** (-jnp.arange(ROPE_FREQS, dtype=F32) / ROPE_FREQS)
    pos = jnp.stack([r, c], axis=-1).astype(F32)
    ang = pos[:, :, None] * inv_freq[None, None, :]
    cos = jnp.cos(ang)
    sin = jnp.sin(ang)
    cos64 = jnp.concatenate([cos, cos], axis=-1).reshape(seq_len, HEAD_DIM)
    sin64 = jnp.concatenate([-sin, sin], axis=-1).reshape(seq_len, HEAD_DIM)
    return jnp.tile(cos64, (1, 2)), jnp.tile(sin64, (1, 2))


def attn_qkv(x, mod, norm_g, wqkv, q_norm, k_norm, seq_len):
    n_rows = x.shape[0]
    tm = ROW_TILE
    nq = N_HEADS * HEAD_DIM
    nk = N_KV_HEADS * HEAD_DIM
    cos, sin = _rope_tables(seq_len)
    qg = jnp.tile(q_norm, N_HEADS)[None, :] * (HEAD_DIM ** -0.5 * math.log2(math.e))
    kg = jnp.tile(k_norm, N_KV_HEADS)[None, :]
    head = np.arange(nq) // HEAD_DIM
    headmean = jnp.asarray((head[:, None] == head[None, :]).astype(np.float32) / HEAD_DIM, BF16)
    per_seq = seq_len // tm
    tab_spec = pl.BlockSpec((tm, LANES), lambda i: (i % per_seq, 0))
    return pl.pallas_call(
        _qkv_kernel,
        grid=(n_rows // tm,),
        in_specs=[
            _row_spec(tm), _mod_spec(tm, seq_len), _const_spec((1, D_MODEL)),
            _const_spec((D_MODEL, nq + 2 * nk)), _const_spec((1, nq)), _const_spec((1, nk)),
            _const_spec((nq, nq)), tab_spec, tab_spec,
        ],
        out_specs=[
            _row_spec(tm, nq),
            pl.BlockSpec((N_KV_HEADS, tm, HEAD_DIM), lambda i: (0, i, 0)),
            pl.BlockSpec((N_KV_HEADS, tm, 2 * HEAD_DIM), lambda i: (0, i, 0)),
        ],
        out_shape=[
            jax.ShapeDtypeStruct((n_rows, nq), BF16),
            jax.ShapeDtypeStruct((N_KV_HEADS, n_rows, HEAD_DIM), BF16),
            jax.ShapeDtypeStruct((N_KV_HEADS, n_rows, 2 * HEAD_DIM), BF16),
        ],
        compiler_params=_params(("parallel",)),
        name="attn_qkv",
    )(x, mod, norm_g[None, :], wqkv.astype(BF16), qg, kg, headmean, cos, sin)


def _flash_kernel(q_ref, k_ref, v_ref, o_ref, *, tq, tk, n_chunks):
    q = q_ref[...]
    qs = jnp.concatenate([q[:, j * HEAD_DIM:(j + 1) * HEAD_DIM] for j in range(Q_PER_KV)], axis=0)
    rows = Q_PER_KV * tq

    def body(c, carry):
        m, acc = carry
        start = pl.multiple_of(c * tk, tk)
        kc = k_ref[0, pl.ds(start, tk), :]
        vc = v_ref[0, pl.ds(start, tk), :]
        s = lax.dot_general(qs, kc, (((1,), (1,)), ((), ())), preferred_element_type=F32)
        m_new = jnp.maximum(m, jnp.max(s, axis=-1, keepdims=True))
        alpha = jnp.exp2(m - m_new)
        p = jnp.exp2(s - m_new)
        acc = acc * alpha + _dot(p.astype(BF16), vc)
        return m_new, acc

    m0 = jnp.full((rows, 1), -jnp.inf, F32)
    acc0 = jnp.zeros((rows, 2 * HEAD_DIM), F32)
    _, acc = lax.fori_loop(0, n_chunks, body, (m0, acc0))
    o = acc[:, :HEAD_DIM] / acc[:, HEAD_DIM:HEAD_DIM + 1]
    o_ref[...] = jnp.concatenate([o[j * tq:(j + 1) * tq] for j in range(Q_PER_KV)],
                                 axis=1).astype(BF16)


def attn_flash(q, k, v, batch, seq_len):
    n_rows = q.shape[0]
    tq = 256
    tk = min(seq_len, 2048)
    per_seq = seq_len // tq
    width = Q_PER_KV * HEAD_DIM
    kern = functools.partial(_flash_kernel, tq=tq, tk=tk, n_chunks=seq_len // tk)
    return pl.pallas_call(
        kern,
        grid=(batch, N_KV_HEADS, per_seq),
        in_specs=[
            pl.BlockSpec((tq, width), lambda b, g, i: (b * per_seq + i, g)),
            pl.BlockSpec((1, seq_len, HEAD_DIM), lambda b, g, i: (g, b, 0)),
            pl.BlockSpec((1, seq_len, 2 * HEAD_DIM), lambda b, g, i: (g, b, 0)),
        ],
        out_specs=pl.BlockSpec((tq, width), lambda b, g, i: (b * per_seq + i, g)),
        out_shape=jax.ShapeDtypeStruct((n_rows, N_HEADS * HEAD_DIM), BF16),
        compiler_params=_params(("parallel", "parallel", "parallel")),
        name="attn_flash",
    )(q, k, v)


def _proj_res_kernel(y_ref, w_ref, x_ref, mod_ref, *refs, gate_row):
    route_in, o_ref, route_out = refs[:N_ROUTE_IN], refs[N_ROUTE_IN], refs[N_ROUTE_IN + 1:]
    m = mod_ref[0]
    x1 = x_ref[...] + m[gate_row:gate_row + 1] * _dot(y_ref[...], w_ref[...])
    o_ref[...] = x1
    _route_tile(x1, m, *route_in, *route_out)


def proj_residual(y, w, x, mod, gate_row, seq_len, route):
    n_rows = x.shape[0]
    tm = STREAM_TILE
    r_ops, r_in, r_out, r_shape = _router_io(route, tm, n_rows)
    return pl.pallas_call(
        functools.partial(_proj_res_kernel, gate_row=gate_row),
        grid=(n_rows // tm,),
        in_specs=[_row_spec(tm, y.shape[1]), _const_spec(w.shape), _row_spec(tm),
                  _mod_spec(tm, seq_len)] + r_in,
        out_specs=[_row_spec(tm)] + r_out,
        out_shape=[jax.ShapeDtypeStruct((n_rows, D_MODEL), F32)] + r_shape,
        compiler_params=_params(("arbitrary",)),
        name="proj_residual",
    )(y, w.astype(BF16), x, mod, *r_ops)


CONV_COLS = 256


def _hy_in_kernel(*refs, tm, seq_len, n_stream):
    stream, refs = refs[:n_stream], refs[n_stream:]
    mod_ref, g_ref, w_ref, cw_ref, cb_ref, z_ref, x0_ref = refs[:7]
    stage_refs = refs[-3:]
    first, last = _edge_flags(tm, seq_len)
    m = mod_ref[0]
    x, ext = _stream_rows(stream)
    if n_stream == N_PENDING_REFS:
        refs[7][...] = x
    h = _modulate(ext, g_ref[...], m[0:1], m[1:2]).astype(BF16)
    tn = CONV_COLS
    for j in range(D_MODEL // tn):
        part = []
        for s in range(3):
            cols = slice(s * D_MODEL + j * tn, s * D_MODEL + (j + 1) * tn)
            part.append(_conv3(_dot(h, w_ref[:, cols]), cw_ref, cb_ref, cols, tm, first, last,
                               stage_refs[s]))
        out_cols = slice(j * tn, (j + 1) * tn)
        x0_ref[:, out_cols] = part[0].astype(BF16)
        z_ref[:, out_cols] = part[2] * part[1]


def hyena_in(src, mod, norm_g, w_in, conv_w, conv_b, seq_len):
    pending = isinstance(src, _Pending)
    n_rows = (src.x if pending else src).shape[0]
    tm = ROW_TILE
    s_ops, s_specs = _stream_io(src, tm, n_rows, seq_len)
    x_out = [jax.ShapeDtypeStruct((n_rows, D_MODEL), F32)] if pending else []
    return pl.pallas_call(
        functools.partial(_hy_in_kernel, tm=tm, seq_len=seq_len, n_stream=len(s_ops)),
        grid=(n_rows // tm,),
        in_specs=s_specs + [_mod_spec(tm, seq_len), _const_spec((1, D_MODEL)),
                            _const_spec((D_MODEL, 3 * D_MODEL)), _const_spec((3, 3 * D_MODEL)),
                            _const_spec((1, 3 * D_MODEL))],
        out_specs=[_row_spec(tm)] * (2 + len(x_out)),
        out_shape=[jax.ShapeDtypeStruct((n_rows, D_MODEL), F32),
                   jax.ShapeDtypeStruct((n_rows, D_MODEL), BF16)] + x_out,
        scratch_shapes=[pltpu.VMEM((tm + 2 * HALO, CONV_COLS), F32)] * 3,
        compiler_params=_params(("parallel",)),
        name="hyena_in",
    )(*s_ops, mod, norm_g[None, :], w_in.astype(BF16), conv_w, conv_b[None, :])


def _hy_filter_kernel(feat_ref, w1_ref, b1_ref, w2_ref, b2_ref, w3_ref, fr_ref, dl_ref, o_ref):
    feat = feat_ref[...]
    fr = fr_ref[...]
    a = jnp.sin(fr * (_dot3(feat, w1_ref[...]) + b1_ref[...]))
    a = jnp.sin(fr * (_dot3(a, w2_ref[...]) + b2_ref[...]))
    hf = _dot3(a, w3_ref[...])
    decay = jnp.exp(-feat[:, 0:1] * dl_ref[...])
    o_ref[0] = hf[:, :D_MODEL] * decay
    o_ref[1] = hf[:, D_MODEL:] * decay


def _pad_to(a, rows, cols):
    return jnp.pad(a.astype(F32), ((0, rows - a.shape[0]), (0, cols - a.shape[1])))


def hyena_filter(seq_len, w1, b1, w2, b2, w3, freq):
    t = jnp.linspace(0.0, 1.0, seq_len, dtype=F32)[:, None]
    w = 2.0 * math.pi * jnp.arange(seq_len, dtype=F32)[:, None] / seq_len
    f = jnp.linspace(1e-4, HY_BANDS - 1, HY_BANDS, dtype=F32)[None, :]
    feat = _pad_to(jnp.concatenate([t, jnp.cos(f * w), -jnp.sin(f * w)], axis=-1), seq_len, LANES)
    max_decay = math.log(HY_TARGET) / HY_FAST_DECAY
    min_decay = math.log(HY_TARGET) / HY_SLOW_DECAY
    absdelta = jnp.abs(jnp.linspace(min_decay, max_decay, D_MODEL, dtype=F32))[None, :]
    tl = 512
    return pl.pallas_call(
        _hy_filter_kernel,
        grid=(seq_len // tl,),
        in_specs=[_row_spec(tl, LANES), _const_spec((LANES, LANES)), _const_spec((1, LANES)),
                  _const_spec((LANES, LANES)), _const_spec((1, LANES)),
                  _const_spec((LANES, 2 * D_MODEL)), _const_spec((1, LANES)),
                  _const_spec((1, D_MODEL))],
        out_specs=pl.BlockSpec((2, tl, D_MODEL), lambda i: (0, i, 0)),
        out_shape=jax.ShapeDtypeStruct((2, seq_len, D_MODEL), F32),
        compiler_params=_params(("parallel",)),
        name="hyena_filter",
    )(feat, _pad_to(w1, LANES, LANES), _pad_to(b1[None, :], 1, LANES), _pad_to(w2, LANES, LANES),
      _pad_to(b2[None, :], 1, LANES), _pad_to(w3, LANES, 2 * D_MODEL),
      _pad_to(freq[None, :], 1, LANES), absdelta)


class _FFTPlan:
    def __init__(self, seq_len):
        self.n = 2 * seq_len
        self.n1 = self.n // DFT_N2
        self.r = self.n1 // 2
        self.k1n = self.n1 // 2 + 1
        self.kron = max(HALO, LANES // self.r)
        ang = 2.0 * np.pi * np.outer(np.arange(self.k1n), np.arange(self.r)) / self.n1
        eye = np.eye(self.kron)
        self.fwd_cos = jnp.asarray(np.kron(np.cos(ang), eye), BF16)
        self.fwd_sin = jnp.asarray(np.kron(-np.sin(ang), eye), BF16)
        wgt = np.full((self.k1n,), 2.0)
        wgt[0] = wgt[-1] = 1.0
        scale = (wgt / self.n)[None, :]
        self.inv_cos = jnp.asarray(np.kron(np.cos(ang).T * scale, eye), BF16)
        self.inv_sin = jnp.asarray(np.kron(-np.sin(ang).T * scale, eye), BF16)
        a2 = 2.0 * np.pi * np.outer(np.arange(DFT_N2), np.arange(DFT_N2)) / DFT_N2
        self.f_cos = jnp.asarray(np.cos(a2), F32)
        self.f_sin = jnp.asarray(-np.sin(a2), F32)
        tw = 2.0 * np.pi * np.outer(np.arange(self.k1n), np.arange(DFT_N2)) / self.n
        self.tw_cos = jnp.asarray(np.cos(tw), F32)
        self.tw_sin = jnp.asarray(-np.sin(tw), F32)


def _fft_a_kernel(z_ref, wc_ref, ws_ref, ar_ref, ai_ref, *, rq):
    shape = ar_ref.shape[1:2] + ar_ref.shape[3:]
    for t in range(z_ref.shape[2]):
        z = z_ref[0, :, t].reshape(rq, D_MODEL).astype(BF16)
        ar_ref[0, :, t] = _dot(wc_ref[...], z).reshape(shape).astype(BF16)
        ai_ref[0, :, t] = _dot(ws_ref[...], z).reshape(shape).astype(BF16)


def _fft_group(plan):
    nhi = DFT_N2 // plan.kron
    per_group = plan.r * plan.kron * D_MODEL * 4
    return max(1, min(nhi, (2 << 20) // per_group))


def fft_stage_a(z, plan, batch):
    q = plan.kron
    nhi = DFT_N2 // q
    hb = _fft_group(plan)
    zv = z.reshape(batch, plan.r, nhi, q, D_MODEL)
    out = jax.ShapeDtypeStruct((batch, plan.k1n, nhi, q, D_MODEL), BF16)
    ospec = pl.BlockSpec((1, plan.k1n, hb, q, D_MODEL), lambda b, h: (b, 0, h, 0, 0))
    wspec = pl.BlockSpec(plan.fwd_cos.shape, lambda b, h: (0, 0))
    ar, ai = pl.pallas_call(
        functools.partial(_fft_a_kernel, rq=plan.r * q),
        grid=(batch, nhi // hb),
        in_specs=[pl.BlockSpec((1, plan.r, hb, q, D_MODEL), lambda b, h: (b, 0, h, 0, 0)),
                  wspec, wspec],
        out_specs=[ospec, ospec],
        out_shape=[out, out],
        compiler_params=_params(("parallel", "parallel")),
        name="fft_stage_a",
    )(zv, plan.fwd_cos, plan.fwd_sin)
    shape = (batch, plan.k1n, DFT_N2, D_MODEL)
    return ar.reshape(shape), ai.reshape(shape)


def _twiddled_dft(fr, fi, tr, ti):
    return (fr * tr - fi * ti).astype(BF16), (fr * ti + fi * tr).astype(BF16)


def _fft_b_fwd_kernel(ar_ref, ai_ref, fr_ref, fi_ref, twr_ref, twi_ref, br_ref, bi_ref,
                      gr_ref, gi_ref):
    @pl.when(pl.program_id(1) == 0)
    def _():
        gr, gi = _twiddled_dft(fr_ref[...], fi_ref[...], twr_ref[0], twi_ref[0])
        gr_ref[...] = gr
        gi_ref[...] = gi

    gr = gr_ref[...]
    gi = gi_ref[...]
    ar = ar_ref[0, 0].astype(BF16)
    ai = ai_ref[0, 0].astype(BF16)
    br_ref[0, 0] = _dot(gr, ar) - _dot(gi, ai)
    bi_ref[0, 0] = _dot(gr, ai) + _dot(gi, ar)


def fft_stage_b_fwd(ar, ai, plan):
    batch = ar.shape[0]
    blk = pl.BlockSpec((1, 1, DFT_N2, D_MODEL), lambda k, b: (b, k, 0, 0))
    cst = pl.BlockSpec((DFT_N2, DFT_N2), lambda k, b: (0, 0))
    tws = pl.BlockSpec((1, 1, DFT_N2), lambda k, b: (k, 0, 0))
    out = jax.ShapeDtypeStruct(ar.shape, F32)
    return pl.pallas_call(
        _fft_b_fwd_kernel,
        grid=(plan.k1n, batch),
        in_specs=[blk, blk, cst, cst, tws, tws],
        out_specs=[blk, blk],
        out_shape=[out, out],
        scratch_shapes=[pltpu.VMEM((DFT_N2, DFT_N2), BF16)] * 2,
        compiler_params=_params(("parallel", "arbitrary")),
        name="fft_stage_b_fwd",
    )(ar, ai, plan.f_cos, plan.f_sin, plan.tw_cos[:, None, :], plan.tw_sin[:, None, :])


def _fft_b_conv_kernel(ar_ref, ai_ref, hr_ref, hi_ref, hb0_ref, fr_ref, fi_ref,
                       twr_ref, twi_ref, tcr_ref, tci_ref, cr_ref, ci_ref,
                       gr_ref, gi_ref, gtr_ref, gti_ref, kr_ref, ki_ref):
    @pl.when(pl.program_id(1) == 0)
    def _():
        fr = fr_ref[...]
        fi = fi_ref[...]
        gr, gi = _twiddled_dft(fr, fi, twr_ref[0], twi_ref[0])
        gr_ref[...] = gr
        gi_ref[...] = gi
        gtr, gti = _twiddled_dft(fr, fi, tcr_ref[0], tci_ref[0])
        gtr_ref[...] = gtr
        gti_ref[...] = gti
        kr_ref[...] = hr_ref[0, 0] + hr_ref[1, 0] - hb0_ref[...]
        ki_ref[...] = hi_ref[0, 0] - hi_ref[1, 0]

    gr = gr_ref[...]
    gi = gi_ref[...]
    ar = ar_ref[0, 0].astype(BF16)
    ai = ai_ref[0, 0].astype(BF16)
    br = _dot(gr, ar) - _dot(gi, ai)
    bi = _dot(gr, ai) + _dot(gi, ar)
    kr = kr_ref[...]
    ki = ki_ref[...]
    pr = (br * kr - bi * ki).astype(BF16)
    pi = (br * ki + bi * kr).astype(BF16)
    gtr = gtr_ref[...]
    gti = gti_ref[...]
    cr_ref[0, 0] = (_dot(gtr, pr) + _dot(gti, pi)).astype(BF16)
    ci_ref[0, 0] = (_dot(gtr, pi) - _dot(gti, pr)).astype(BF16)


def fft_stage_b_conv(ar, ai, hr, hi, hb0, plan):
    batch = ar.shape[0]
    blk = pl.BlockSpec((1, 1, DFT_N2, D_MODEL), lambda k, b: (b, k, 0, 0))
    hblk = pl.BlockSpec((2, 1, DFT_N2, D_MODEL), lambda k, b: (0, k, 0, 0))
    cst = pl.BlockSpec((DFT_N2, DFT_N2), lambda k, b: (0, 0))
    tws = pl.BlockSpec((1, 1, DFT_N2), lambda k, b: (k, 0, 0))
    twc = pl.BlockSpec((1, DFT_N2, 1), lambda k, b: (k, 0, 0))
    out = jax.ShapeDtypeStruct(ar.shape, BF16)
    return pl.pallas_call(
        _fft_b_conv_kernel,
        grid=(plan.k1n, batch),
        in_specs=[blk, blk, hblk, hblk, pl.BlockSpec((1, D_MODEL), lambda k, b: (0, 0)),
                  cst, cst, tws, tws, twc, twc],
        out_specs=[blk, blk],
        out_shape=[out, out],
        scratch_shapes=[pltpu.VMEM((DFT_N2, DFT_N2), BF16)] * 4
        + [pltpu.VMEM((DFT_N2, D_MODEL), F32)] * 2,
        compiler_params=_params(("parallel", "arbitrary")),
        name="fft_stage_b_conv",
    )(ar, ai, hr, hi, hb0, plan.f_cos, plan.f_sin,
      plan.tw_cos[:, None, :], plan.tw_sin[:, None, :],
      plan.tw_cos[:, :, None], plan.tw_sin[:, :, None])


def _fft_a_inv_kernel(cr_ref, ci_ref, vc_ref, vs_ref, z_ref, x0_ref, skip_ref, y_ref, *, kq):
    shape = z_ref.shape[1:2] + z_ref.shape[3:]
    for t in range(z_ref.shape[2]):
        cr = cr_ref[0, :, t].reshape(kq, D_MODEL).astype(BF16)
        ci = ci_ref[0, :, t].reshape(kq, D_MODEL).astype(BF16)
        conv = _dot(vc_ref[...], cr) + _dot(vs_ref[...], ci)
        y = conv.reshape(shape) + z_ref[0, :, t] * skip_ref[...]
        y_ref[0, :, t] = (y * x0_ref[0, :, t].astype(F32)).astype(BF16)


def fft_stage_a_inv(cr, ci, z, x0, skip, plan, batch):
    q = plan.kron
    nhi = DFT_N2 // q
    hb = _fft_group(plan)
    cshape = (batch, plan.k1n, nhi, q, D_MODEL)
    tshape = (batch, plan.r, nhi, q, D_MODEL)
    cspec = pl.BlockSpec((1, plan.k1n, hb, q, D_MODEL), lambda b, h: (b, 0, h, 0, 0))
    tspec = pl.BlockSpec((1, plan.r, hb, q, D_MODEL), lambda b, h: (b, 0, h, 0, 0))
    wspec = pl.BlockSpec(plan.inv_cos.shape, lambda b, h: (0, 0))
    y = pl.pallas_call(
        functools.partial(_fft_a_inv_kernel, kq=plan.k1n * q),
        grid=(batch, nhi // hb),
        in_specs=[cspec, cspec, wspec, wspec, tspec, tspec,
                  pl.BlockSpec((1, D_MODEL), lambda b, h: (0, 0))],
        out_specs=tspec,
        out_shape=jax.ShapeDtypeStruct(tshape, BF16),
        compiler_params=_params(("parallel", "parallel")),
        name="fft_stage_a_inv",
    )(cr.reshape(cshape), ci.reshape(cshape), plan.inv_cos, plan.inv_sin,
      z.reshape(tshape), x0.reshape(tshape), skip[None, :])
    return y.reshape(z.shape)


def hyena_mixer(src, mod, norm_g, p, batch, seq_len, route):
    z, x0, *rest = hyena_in(src, mod, norm_g, p["w_in"], p["conv_w"], p["conv_b"], seq_len)
    x = rest[0] if rest else src
    plan = _FFTPlan(seq_len)
    filt = hyena_filter(seq_len, p["w1"], p["b1"], p["w2"], p["b2"], p["w3"], p["freq"])
    fr, fi = fft_stage_a(filt.reshape(2 * seq_len, D_MODEL), plan, 2)
    hr, hi = fft_stage_b_fwd(fr, fi, plan)
    ar, ai = fft_stage_a(z, plan, batch)
    cr, ci = fft_stage_b_conv(ar, ai, hr, hi, filt[1, 0:1, :], plan)
    y = fft_stage_a_inv(cr, ci, z, x0, p["skip"], plan, batch)
    return proj_residual(y, p["w_out"], x, mod, 2, seq_len, route)


def _pool_kernel(*refs, tm, seq_len, n_stream):
    stream, refs = refs[:n_stream], refs[n_stream:]
    mod_ref, g_ref, w_ref, s_ref = refs[:4]
    refs = refs[4:]
    route_in, o_ref, route_out = refs[:N_ROUTE_IN], refs[N_ROUTE_IN], refs[N_ROUTE_IN + 1:]
    first, last = _edge_flags(tm, seq_len)
    m = mod_ref[0]
    x, ext = _stream_rows(stream)
    h = _modulate(ext, g_ref[...], m[0:1], m[1:2])
    n = tm + 2 * HALO
    row = lax.broadcasted_iota(jnp.int32, (n, 1), 0)
    outside = jnp.logical_or(jnp.logical_and(first, row < HALO),
                             jnp.logical_and(last, row >= HALO + tm))
    h = jnp.where(outside, 0.0, h)
    pos = (pl.program_id(0) * tm) % seq_len + lax.broadcasted_iota(jnp.int32, (tm, 1), 0)
    ys = []
    for gi, win in enumerate(POOL_WINDOWS):
        cols = slice(gi * POOL_GROUP_DIM, (gi + 1) * POOL_GROUP_DIM)
        hg = h[:, cols]
        acc = hg
        span = 1
        while span < win:
            acc = acc + pltpu.roll(acc, span, axis=0)
            span *= 2
        lead = win // 2 - 1
        if lead:
            acc = pltpu.roll(acc, n - lead, axis=0)
        half = win // 2
        cnt = jnp.minimum(pos + half, seq_len) - jnp.maximum(pos - half, 0)
        pooled = acc[HALO:HALO + tm] / cnt.astype(F32) - hg[HALO:HALO + tm]
        ys.append(_dot(pooled.astype(BF16), w_ref[gi]))
    y = jnp.concatenate(ys, axis=1) * s_ref[...]
    x1 = x + m[2:3] * y
    o_ref[...] = x1
    _route_tile(x1, m, *route_in, *route_out)


def pool_mixer(src, mod, norm_g, w_group, scale, seq_len, route):
    n_rows = (src.x if isinstance(src, _Pending) else src).shape[0]
    tm = ROW_TILE
    s_ops, s_specs = _stream_io(src, tm, n_rows, seq_len)
    r_ops, r_in, r_out, r_shape = _router_io(route, tm, n_rows)
    return pl.pallas_call(
        functools.partial(_pool_kernel, tm=tm, seq_len=seq_len, n_stream=len(s_ops)),
        grid=(n_rows // tm,),
        in_specs=s_specs + [_mod_spec(tm, seq_len), _const_spec((1, D_MODEL)),
                            _const_spec(w_group.shape), _const_spec((1, D_MODEL))] + r_in,
        out_specs=[_row_spec(tm)] + r_out,
        out_shape=[jax.ShapeDtypeStruct((n_rows, D_MODEL), F32)] + r_shape,
        compiler_params=_params(("arbitrary",)),
        name="pool_mixer",
    )(*s_ops, mod, norm_g[None, :], w_group.astype(BF16), scale[None, :], *r_ops)


def _sconv_kernel(*refs, tm, seq_len, n_stream):
    stream, refs = refs[:n_stream], refs[n_stream:]
    mod_ref, g_ref, w_ref, cw_ref, cb_ref, wo_ref = refs[:6]
    refs = refs[6:]
    route_in, o_ref = refs[:N_ROUTE_IN], refs[N_ROUTE_IN]
    route_out, y_ref = refs[N_ROUTE_IN + 1:N_ROUTE_IN + 1 + N_ROUTE_OUT], refs[-1]
    first, last = _edge_flags(tm, seq_len)
    m = mod_ref[0]
    x, ext = _stream_rows(stream)
    h = _modulate(ext, g_ref[...], m[0:1], m[1:2]).astype(BF16)
    tn = 256
    for j in range(D_MODEL // tn):
        cols = slice(j * tn, (j + 1) * tn)
        bg = _dot(h, w_ref[:, cols])[HALO:HALO + tm]
        cg = _dot(h, w_ref[:, D_MODEL + j * tn:D_MODEL + (j + 1) * tn])
        hp = _dot(h, w_ref[:, 2 * D_MODEL + j * tn:2 * D_MODEL + (j + 1) * tn])
        y_ref[:, cols] = (bg * _conv3(cg * hp, cw_ref, cb_ref, cols, tm, first, last)).astype(BF16)
    x1 = x + m[2:3] * _dot(y_ref[...], wo_ref[...])
    o_ref[...] = x1
    _route_tile(x1, m, *route_in, *route_out)


def sconv_mixer(src, mod, norm_g, w_in, conv_w, conv_b, w_out, seq_len, route):
    n_rows = (src.x if isinstance(src, _Pending) else src).shape[0]
    tm = ROW_TILE
    s_ops, s_specs = _stream_io(src, tm, n_rows, seq_len)
    r_ops, r_in, r_out, r_shape = _router_io(route, tm, n_rows)
    return pl.pallas_call(
        functools.partial(_sconv_kernel, tm=tm, seq_len=seq_len, n_stream=len(s_ops)),
        grid=(n_rows // tm,),
        in_specs=s_specs + [_mod_spec(tm, seq_len), _const_spec((1, D_MODEL)),
                            _const_spec((D_MODEL, 3 * D_MODEL)), _const_spec((3, D_MODEL)),
                            _const_spec((1, D_MODEL)), _const_spec((D_MODEL, D_MODEL))] + r_in,
        out_specs=[_row_spec(tm)] + r_out,
        out_shape=[jax.ShapeDtypeStruct((n_rows, D_MODEL), F32)] + r_shape,
        scratch_shapes=[pltpu.VMEM((tm, D_MODEL), BF16)],
        compiler_params=_params(("arbitrary",)),
        name="sconv_mixer",
    )(*s_ops, mod, norm_g[None, :], w_in.astype(BF16), conv_w, conv_b[None, :],
      w_out.astype(BF16), *r_ops)


def _pack_bf16(x):
    w = x.shape[1] // 2
    bits = pltpu.bitcast(x.astype(BF16).astype(F32), jnp.uint32)
    return (bits[:, :w] >> 16) | bits[:, w:]


def _unpack_bf16(p):
    lo = pltpu.bitcast(p << 16, F32)
    hi = pltpu.bitcast(p & jnp.uint32(0xFFFF0000), F32)
    return jnp.concatenate([lo, hi], axis=1)


META_E1, META_E2, META_W1, META_W2, META_R1, META_R2 = range(6)
META_ROWS = 8


def _route_tile(x, m, g_ref, wh_ref, b_ref, tri_ref, h_ref, meta_ref, meta_t_ref, cnt_ref):
    @pl.when(pl.program_id(0) == 0)
    def _():
        cnt_ref[...] = jnp.zeros_like(cnt_ref)

    ms = jnp.mean(x * x, axis=-1, keepdims=True)
    h = x * lax.rsqrt(ms + EPS) * (g_ref[...] * (1.0 + m[4:5])) + m[3:4]
    hi = h.astype(BF16)
    hi32 = hi.astype(F32)
    lo = (h - hi32).astype(BF16)
    half = D_MODEL // 2
    bits = pltpu.bitcast(hi32, jnp.uint32)
    h_ref[...] = (bits[:, :half] >> 16) | bits[:, half:]
    part = _dot(hi, wh_ref[...])
    lg = part[:, :LANES] + (part[:, LANES:] + _dot(lo, wh_ref[:, :LANES])) + b_ref[...]
    lgt = lg.T
    tm = lgt.shape[1]
    neg = -jnp.inf
    sub = lax.broadcasted_iota(jnp.int32, (HALO, tm), 0).astype(F32)

    def first_argmax(vals):
        top = jnp.max(vals, axis=0, keepdims=True)
        idx = jnp.min(jnp.where(vals == top, sub, float(HALO)), axis=0, keepdims=True)
        return top, idx

    gl = jnp.where(sub < MOE_GROUPS, lgt[MOE_N_EXPERTS:MOE_N_EXPERTS + HALO], neg)
    gmax, grp = first_argmax(gl)
    g_w = 1.0 / jnp.sum(jnp.exp(gl - gmax), axis=0, keepdims=True)
    el = lgt[:MOE_EXPERTS_PER_GROUP]
    for g in range(1, MOE_GROUPS):
        el = jnp.where(grp == float(g),
                       lgt[g * MOE_EXPERTS_PER_GROUP:(g + 1) * MOE_EXPERTS_PER_GROUP], el)
    v1, i1 = first_argmax(el)
    v2, i2 = first_argmax(jnp.where(sub == i1, neg, el))
    ex = jnp.exp(v2 - v1)
    w1 = 1.0 / (1.0 + ex)
    w2 = ex * w1
    e1 = grp * MOE_EXPERTS_PER_GROUP + i1
    e2 = grp * MOE_EXPERTS_PER_GROUP + i2
    expert = lax.broadcasted_iota(jnp.int32, (MOE_N_EXPERTS, tm), 0).astype(F32)
    onehot = jnp.where(jnp.logical_or(expert == e1, expert == e2), 1.0, 0.0)
    before = _dot(onehot.astype(BF16), tri_ref[...]) + cnt_ref[...]
    cnt_ref[...] += jnp.sum(onehot, axis=1, keepdims=True)
    r1 = jnp.sum(jnp.where(expert == e1, before, 0.0), axis=0, keepdims=True)
    r2 = jnp.sum(jnp.where(expert == e2, before, 0.0), axis=0, keepdims=True)
    fields = ((META_E1, e1), (META_E2, e2), (META_W1, w1 * g_w), (META_W2, w2 * g_w),
              (META_R1, r1), (META_R2, r2))
    field = lax.broadcasted_iota(jnp.int32, (LANES, tm), 0)
    meta_t = jnp.zeros((LANES, tm), F32)
    for row, val in fields:
        meta_t = jnp.where(field == row, val, meta_t)
    meta_t_ref[...] = meta_t[:META_ROWS]
    meta_ref[...] = meta_t.T


N_ROUTE_IN = 4
N_ROUTE_OUT = 4


def _router_io(route, tm, n_rows):
    norm_g, w_group, b_group, w_router, b_router = route
    w = _pad_to(jnp.concatenate([w_router, w_group], axis=1), D_MODEL, LANES)
    wh = w.astype(BF16)
    wl = (w - wh.astype(F32)).astype(BF16)
    whl = jnp.concatenate([wh, wl], axis=1)
    b = _pad_to(jnp.concatenate([b_router, b_group])[None, :], 1, LANES)
    tri = jnp.asarray(np.triu(np.ones((tm, tm), np.float32), 1), BF16)
    operands = (norm_g[None, :], whl, b, tri)
    in_specs = [_const_spec((1, D_MODEL)), _const_spec((D_MODEL, 2 * LANES)),
                _const_spec((1, LANES)), _const_spec((tm, tm))]
    out_specs = [_row_spec(tm, D_MODEL // 2), _row_spec(tm, LANES),
                 pl.BlockSpec((META_ROWS, tm), lambda i: (0, i)),
                 _const_spec((MOE_N_EXPERTS, 1))]
    out_shape = [jax.ShapeDtypeStruct((n_rows, D_MODEL // 2), jnp.uint32),
                 jax.ShapeDtypeStruct((n_rows, LANES), F32),
                 jax.ShapeDtypeStruct((META_ROWS, n_rows), F32),
                 jax.ShapeDtypeStruct((MOE_N_EXPERTS, 1), F32)]
    return operands, in_specs, out_specs, out_shape


def _slot_kernel(offs_ref, meta_t_ref, pos_ref):
    meta = meta_t_ref[...]
    start = jnp.zeros_like(meta)
    for e in range(MOE_N_EXPERTS):
        start = jnp.where(meta == float(e), offs_ref[e], start)
    shift = META_ROWS - (META_R1 - META_E1)
    pos_ref[...] = (start + pltpu.roll(meta, shift, axis=0)).astype(jnp.int32)


def moe_slots(meta_t, offsets):
    n_rows = meta_t.shape[1]
    tn = min(n_rows, 8192)
    blk = pl.BlockSpec((META_ROWS, tn), lambda i, offs: (0, i))
    return pl.pallas_call(
        _slot_kernel,
        grid_spec=pltpu.PrefetchScalarGridSpec(
            num_scalar_prefetch=1, grid=(n_rows // tn,), in_specs=[blk], out_specs=blk),
        out_shape=jax.ShapeDtypeStruct((META_ROWS, n_rows), jnp.int32),
        compiler_params=_params(("parallel",)),
        name="moe_slots",
    )(offsets, meta_t)


def _expert_kernel(te_ref, nu_ref, nv_ref, xs_ref, wg_ref, wu_ref, wd_ref, o_ref,
                   wgu_ref, wdb_ref):
    j = pl.program_id(0)

    @pl.when(jnp.logical_or(j == 0, te_ref[j] != te_ref[jnp.maximum(j - 1, 0)]))
    def _():
        wgu_ref[:, :MOE_D_FF] = wg_ref[0, 0].astype(BF16)
        wgu_ref[:, MOE_D_FF:] = wu_ref[0, 0].astype(BF16)
        wdb_ref[...] = wd_ref[0, 0].astype(BF16)

    @pl.when(j < nu_ref[0])
    def _():
        x = _unpack_bf16(xs_ref[...])
        row = lax.broadcasted_iota(jnp.int32, (x.shape[0], 1), 0)
        x = jnp.where(row < nv_ref[j], x, 0.0).astype(BF16)
        au = _dot(x, wgu_ref[...])
        hh = (_silu(au[:, :MOE_D_FF]) * au[:, MOE_D_FF:]).astype(BF16)
        o_ref[...] = _pack_bf16(_dot(hh, wdb_ref[...]))


def moe_experts(xs, tile_expert, n_used, n_valid, w_gate, w_up, w_down, layer, tm):
    n_slots, half = xs.shape
    wspec = lambda shape: pl.BlockSpec((1, 1) + shape, lambda j, te, nu, nv: (layer, te[j], 0, 0))
    row = pl.BlockSpec((tm, half), lambda j, te, nu, nv: (j, 0))
    grid_spec = pltpu.PrefetchScalarGridSpec(
        num_scalar_prefetch=3,
        grid=(n_slots // tm,),
        in_specs=[row, wspec((D_MODEL, MOE_D_FF)), wspec((D_MODEL, MOE_D_FF)),
                  wspec((MOE_D_FF, D_MODEL))],
        out_specs=row,
        scratch_shapes=[pltpu.VMEM((D_MODEL, 2 * MOE_D_FF), BF16),
                        pltpu.VMEM((MOE_D_FF, D_MODEL), BF16)],
    )
    return pl.pallas_call(
        _expert_kernel,
        grid_spec=grid_spec,
        out_shape=jax.ShapeDtypeStruct((n_slots, half), jnp.uint32),
        compiler_params=_params(("arbitrary",)),
        name="moe_experts",
    )(tile_expert, n_used, n_valid, xs, w_gate, w_up, w_down)


def _combine_kernel(x_ref, ya_ref, yb_ref, meta_ref, mod_ref, o_ref):
    o_ref[...] = _moe_combined(x_ref, ya_ref, yb_ref, meta_ref, mod_ref[0][5:6])


def moe_combine(pending, seq_len):
    x, ya, yb, meta, mod = pending
    n_rows = x.shape[0]
    tm = STREAM_TILE
    half = D_MODEL // 2
    return pl.pallas_call(
        _combine_kernel,
        grid=(n_rows // tm,),
        in_specs=[_row_spec(tm), _row_spec(tm, half), _row_spec(tm, half), _row_spec(tm, LANES),
                  _mod_spec(tm, seq_len)],
        out_specs=_row_spec(tm),
        out_shape=jax.ShapeDtypeStruct((n_rows, D_MODEL), F32),
        compiler_params=_params(("parallel",)),
        name="moe_combine",
    )(x, ya, yb, meta, mod)


MOE_TILE = 512
MOE_TILE_LARGE = 1024
MOE_LARGE_ROWS = 32768


def hier_moe(routed, mod, w_gate, w_up, w_down, layer, seq_len):
    x, hp, meta, meta_t, counts = routed
    n_rows = x.shape[0]
    tm = MOE_TILE_LARGE if n_rows >= MOE_LARGE_ROWS else MOE_TILE
    cnt = counts[:, 0].astype(jnp.int32)
    padded = (cnt + tm - 1) // tm * tm
    ends = jnp.cumsum(padded)
    starts = ends - padded
    n_slots = 2 * n_rows + MOE_N_EXPERTS * tm
    tile_start = jnp.arange(n_slots // tm, dtype=jnp.int32) * tm
    tile_expert = jnp.minimum(jnp.sum(tile_start[:, None] >= ends[None, :], axis=1),
                              MOE_N_EXPERTS - 1).astype(jnp.int32)
    n_used = (ends[-1:] // tm).astype(jnp.int32)
    n_valid = jnp.clip(starts[tile_expert] + cnt[tile_expert] - tile_start, 0, tm).astype(jnp.int32)
    pos = moe_slots(meta_t, starts.astype(F32))
    idx0 = pos[META_E1].reshape(n_rows // SC_BLOCK, SC_BLOCK)
    idx1 = pos[META_E2].reshape(n_rows // SC_BLOCK, SC_BLOCK)
    xs = sc_scatter_rows(hp, idx0, idx1, n_slots)
    ys = moe_experts(xs, tile_expert, n_used, n_valid, w_gate, w_up, w_down, layer, tm)
    ya, yb = sc_gather_rows(ys, idx0, idx1)
    return _Pending(x, ya, yb, meta, mod)


SC_CORES = 2
SC_SUBCORES = 16
SC_WORKERS = SC_CORES * SC_SUBCORES
SC_BLOCK = 128


def _sc_mesh():
    return plsc.VectorSubcoreMesh(core_axis_name="c", subcore_axis_name="s")


def _sc_worker():
    return lax.axis_index("s") * SC_CORES + lax.axis_index("c")


def sc_scatter_rows(rows, idx0, idx1, n_slots):
    n_rows, width = rows.shape
    per_worker = n_rows // SC_BLOCK // SC_WORKERS

    @functools.partial(
        pl.kernel, mesh=_sc_mesh(),
        out_type=jax.ShapeDtypeStruct((n_slots, width), rows.dtype),
        scratch_types=[pltpu.VMEM((SC_BLOCK,), jnp.int32), pltpu.VMEM((SC_BLOCK,), jnp.int32),
                       pltpu.VMEM((SC_BLOCK, width), rows.dtype)],
        name="sc_scatter_rows",
    )
    def scatter(rows_hbm, i0_hbm, i1_hbm, out_hbm, i0_v, i1_v, rows_v):
        first = _sc_worker() * per_worker

        @pl.loop(0, per_worker)
        def _(j):
            blk = first + j
            pltpu.sync_copy(i0_hbm.at[blk], i0_v)
            pltpu.sync_copy(i1_hbm.at[blk], i1_v)
            pltpu.sync_copy(rows_hbm.at[pl.ds(blk * SC_BLOCK, SC_BLOCK)], rows_v)
            pltpu.sync_copy(rows_v, out_hbm.at[i0_v])
            pltpu.sync_copy(rows_v, out_hbm.at[i1_v])

    return scatter(rows, idx0, idx1)


def sc_gather_rows(src, idx0, idx1):
    width = src.shape[1]
    n_rows = idx0.shape[0] * SC_BLOCK
    per_worker = n_rows // SC_BLOCK // SC_WORKERS
    out = jax.ShapeDtypeStruct((n_rows, width), src.dtype)

    @functools.partial(
        pl.kernel, mesh=_sc_mesh(), out_type=(out, out),
        scratch_types=[pltpu.VMEM((SC_BLOCK,), jnp.int32), pltpu.VMEM((SC_BLOCK, width), src.dtype)],
        name="sc_gather_rows",
    )
    def gather(src_hbm, i0_hbm, i1_hbm, a_hbm, b_hbm, idx_v, rows_v):
        first = _sc_worker() * per_worker

        @pl.loop(0, per_worker)
        def _(j):
            blk = first + j
            dst = pl.ds(blk * SC_BLOCK, SC_BLOCK)
            pltpu.sync_copy(i0_hbm.at[blk], idx_v)
            pltpu.sync_copy(src_hbm.at[idx_v], rows_v)
            pltpu.sync_copy(rows_v, a_hbm.at[dst])
            pltpu.sync_copy(i1_hbm.at[blk], idx_v)
            pltpu.sync_copy(src_hbm.at[idx_v], rows_v)
            pltpu.sync_copy(rows_v, b_hbm.at[dst])

    return gather(src, idx0, idx1)


def _trunk(x3, mods, p):
    batch, seq_len, _ = x3.shape
    x = x3.reshape(batch * seq_len, D_MODEL)
    for i in range(DEPTH):
        mod = mods[i]
        g1 = p["norm1_g"][i]
        route = (p["norm2_g"][i], p["moe_w_group"][i], p["moe_b_group"][i], p["moe_w_router"][i],
                 p["moe_b_router"][i])
        kind = i % 4
        if kind == 0:
            if isinstance(x, _Pending):
                x = moe_combine(x, seq_len)
            q, k, v = attn_qkv(x, mod, g1, p["attn_wqkv"][0], p["attn_q_norm"][0],
                               p["attn_k_norm"][0], seq_len)
            o = attn_flash(q, k, v, batch, seq_len)
            routed = proj_residual(o, p["attn_wo"][0], x, mod, 2, seq_len, route)
        elif kind == 1:
            hp = {"w_in": p["hy_w_in"][0], "conv_w": p["hy_conv_w"][0], "conv_b": p["hy_conv_b"][0],
                  "w1": p["hy_ffn_w1"][0], "b1": p["hy_ffn_b1"][0], "w2": p["hy_ffn_w2"][0],
                  "b2": p["hy_ffn_b2"][0], "w3": p["hy_ffn_w3"][0], "freq": p["hy_freq"][0],
                  "skip": p["hy_skip"][0], "w_out": p["hy_w_out"][0]}
            routed = hyena_mixer(x, mod, g1, hp, batch, seq_len, route)
        elif kind == 2:
            routed = pool_mixer(x, mod, g1, p["pool_w"][0], p["pool_scale"][0], seq_len, route)
        else:
            routed = sconv_mixer(x, mod, g1, p["sc_w_in"][0], p["sc_conv_w"][0],
                                 p["sc_conv_b"][0], p["sc_w_out"][0], seq_len, route)
        x = hier_moe(routed, mod, p["moe_w_gate"], p["moe_w_up"], p["moe_w_down"], i, seq_len)
    return moe_combine(x, seq_len).reshape(batch, seq_len, D_MODEL)


def kernel(x_prompt, x_sample, c_prompt, c_sample, norm1_g, norm2_g, ada_w, ada_b, attn_wqkv, attn_q_norm, attn_k_norm, attn_wo, hy_w_in, hy_conv_w, hy_conv_b, hy_ffn_w1, hy_ffn_b1, hy_ffn_w2, hy_ffn_b2, hy_ffn_w3, hy_freq, hy_skip, hy_w_out, pool_w, pool_scale, sc_w_in, sc_conv_w, sc_conv_b, sc_w_out, moe_w_group, moe_b_group, moe_w_router, moe_b_router, moe_w_gate, moe_w_up, moe_w_down):
    p = dict(norm1_g=norm1_g, norm2_g=norm2_g, attn_wqkv=attn_wqkv, attn_q_norm=attn_q_norm,
             attn_k_norm=attn_k_norm, attn_wo=attn_wo, hy_w_in=hy_w_in, hy_conv_w=hy_conv_w,
             hy_conv_b=hy_conv_b, hy_ffn_w1=hy_ffn_w1, hy_ffn_b1=hy_ffn_b1, hy_ffn_w2=hy_ffn_w2,
             hy_ffn_b2=hy_ffn_b2, hy_ffn_w3=hy_ffn_w3, hy_freq=hy_freq, hy_skip=hy_skip,
             hy_w_out=hy_w_out, pool_w=pool_w, pool_scale=pool_scale, sc_w_in=sc_w_in,
             sc_conv_w=sc_conv_w, sc_conv_b=sc_conv_b, sc_w_out=sc_w_out, moe_w_group=moe_w_group,
             moe_b_group=moe_b_group, moe_w_router=moe_w_router, moe_b_router=moe_b_router,
             moe_w_gate=moe_w_gate, moe_w_up=moe_w_up, moe_w_down=moe_w_down)
    nb = c_prompt.shape[0]
    ns = c_sample.shape[0]
    rows = -(-(nb + ns) // HALO) * HALO
    c_all = jnp.pad(jnp.concatenate([c_prompt, c_sample], axis=0), ((0, rows - nb - ns), (0, 0)))
    mod = ada_mod(c_all, ada_w, ada_b).reshape(DEPTH, rows, 6, D_MODEL)
    mods_prompt = [mod[i, :nb] for i in range(DEPTH)]
    mods_sample = [mod[i, nb:nb + ns] for i in range(DEPTH)]
    return _trunk(x_prompt, mods_prompt, p), _trunk(x_sample, mods_sample, p)
```

```python
import functools
import math
from typing import NamedTuple

import jax
import jax.numpy as jnp
import numpy as np
from jax import lax
from jax.experimental import pallas as pl
from jax.experimental.pallas import tpu as pltpu
from jax.experimental.pallas import tpu_sc as plsc

F32 = jnp.float32
BF16 = jnp.bfloat16
F8 = jnp.float8_e4m3fn

D_MODEL = 1024
DEPTH = 4
EPS = 1e-6
GRID_W = 64
HEAD_DIM = 64
N_HEADS = 16
N_KV_HEADS = 4
Q_PER_KV = 4
ROPE_THETA = 10000.0
ROPE_FREQS = 16
HY_EMB_DIM = 33
HY_BANDS = 16
HY_FILTER_WIDTH = 64
HY_FAST_DECAY = 0.3
HY_SLOW_DECAY = 1.5
HY_TARGET = 1e-2
POOL_WINDOWS = (2, 4, 8, 16)
POOL_GROUP_DIM = 256
MOE_GROUPS = 4
MOE_EXPERTS_PER_GROUP = 8
MOE_N_EXPERTS = 32
MOE_D_FF = 256

LANES = 128
HALO = 8
DFT_N2 = 256
VMEM_LIMIT = 56 * 1024 * 1024

ROW_TILE = 512
STREAM_TILE = 1024


def _params(sem):
    return pltpu.CompilerParams(dimension_semantics=sem, vmem_limit_bytes=VMEM_LIMIT)


def _dot(a, b):
    return jnp.dot(a, b, preferred_element_type=F32)


def _split(a):
    hi = a.astype(BF16)
    lo = (a - hi.astype(F32)).astype(BF16)
    return hi, lo


def _dot3(a, b):
    ah, al = _split(a)
    bh, bl = _split(b)
    return _dot(ah, bh) + (_dot(ah, bl) + _dot(al, bh))


def _modulate(x, g, shift, scale):
    ms = jnp.mean(x * x, axis=-1, keepdims=True)
    return x * lax.rsqrt(ms + EPS) * g * (1.0 + scale) + shift


def _silu(x):
    return x * (1.0 / (1.0 + jnp.exp(-x)))


def _ada_kernel(c_ref, w_ref, b_ref, o_ref):
    c = c_ref[...]
    o_ref[0] = _dot3(_silu(c), w_ref[0]) + b_ref[0]


def ada_mod(c_all, ada_w, ada_b):
    rows = c_all.shape[0]
    n = ada_w.shape[2]
    tn = 1536
    return pl.pallas_call(
        _ada_kernel,
        grid=(DEPTH, n // tn),
        in_specs=[
            pl.BlockSpec((rows, D_MODEL), lambda l, j: (0, 0)),
            pl.BlockSpec((1, D_MODEL, tn), lambda l, j: (l, 0, j)),
            pl.BlockSpec((1, 1, tn), lambda l, j: (l, 0, j)),
        ],
        out_specs=pl.BlockSpec((1, rows, tn), lambda l, j: (l, 0, j)),
        out_shape=jax.ShapeDtypeStruct((DEPTH, rows, n), F32),
        compiler_params=_params(("parallel", "parallel")),
        name="ada_mod",
    )(c_all, ada_w, ada_b.reshape(DEPTH, 1, n))


def _row_spec(tm, width=D_MODEL):
    return pl.BlockSpec((tm, width), lambda i: (i, 0))


def _mod_spec(tm, seq_len):
    return pl.BlockSpec((1, 6, D_MODEL), lambda i: ((i * tm) // seq_len, 0, 0))


def _const_spec(shape):
    nd = len(shape)
    return pl.BlockSpec(shape, lambda i: (0,) * nd)


def _halo_specs(tm, n_rows, width=D_MODEL):
    per = tm // HALO
    last = n_rows // HALO - 1
    prev = pl.BlockSpec((HALO, width), lambda i: (jnp.maximum(i * per - 1, 0), 0))
    nxt = pl.BlockSpec((HALO, width), lambda i: (jnp.minimum((i + 1) * per, last), 0))
    return prev, nxt


class _Pending(NamedTuple):
    x: jax.Array
    ya: jax.Array
    yb: jax.Array
    meta: jax.Array
    mod: jax.Array


N_PLAIN_REFS = 3
N_PENDING_REFS = 13


def _stream_io(src, tm, n_rows, seq_len):
    if not isinstance(src, _Pending):
        prev, nxt = _halo_specs(tm, n_rows)
        return [src] * 3, [prev, _row_spec(tm), nxt]
    operands, specs = [], []
    for arr in (src.x, src.ya, src.yb, src.meta):
        width = arr.shape[1]
        prev, nxt = _halo_specs(tm, n_rows, width)
        operands += [arr] * 3
        specs += [prev, _row_spec(tm, width), nxt]
    return operands + [src.mod], specs + [_mod_spec(tm, seq_len)]


def _moe_combined(x_ref, ya_ref, yb_ref, meta_ref, gate):
    meta = meta_ref[...]
    y = (meta[:, META_W1:META_W1 + 1] * _unpack_bf16(ya_ref[...])
         + meta[:, META_W2:META_W2 + 1] * _unpack_bf16(yb_ref[...]))
    return x_ref[...] + gate * y


def _stream_rows(refs):
    if len(refs) == N_PLAIN_REFS:
        prev_ref, x_ref, next_ref = refs
        x = x_ref[...]
        return x, jnp.concatenate([prev_ref[...], x, next_ref[...]], axis=0)
    gate = refs[-1][0][5:6]
    parts = [_moe_combined(refs[k], refs[3 + k], refs[6 + k], refs[9 + k], gate) for k in range(3)]
    return parts[1], jnp.concatenate(parts, axis=0)


def _edge_flags(tm, seq_len):
    i = pl.program_id(0)
    per_seq = seq_len // tm
    pos = i % per_seq
    return pos == 0, pos == per_seq - 1


def _shift_rows(u, tm):
    n = u.shape[0]
    up = pltpu.roll(u, 1, axis=0)[HALO:HALO + tm]
    dn = pltpu.roll(u, n - 1, axis=0)[HALO:HALO + tm]
    return up, u[HALO:HALO + tm], dn


def _conv3(u, w_ref, b_ref, cols, tm, first, last, stage_ref=None):
    if stage_ref is None:
        up, mid, dn = _shift_rows(u, tm)
    else:
        stage_ref[...] = u
        up = stage_ref[pl.ds(HALO - 1, tm), :]
        mid = stage_ref[pl.ds(HALO, tm), :]
        dn = stage_ref[pl.ds(HALO + 1, tm), :]
    row = lax.broadcasted_iota(jnp.int32, (tm, 1), 0)
    up = jnp.where(jnp.logical_and(first, row == 0), 0.0, up)
    dn = jnp.where(jnp.logical_and(last, row == tm - 1), 0.0, dn)
    w = w_ref[:, cols]
    return up * w[0:1] + mid * w[1:2] + dn * w[2:3] + b_ref[:, cols]


def _norm_rope(t, gain, headmean, cos, sin_signed):
    width = t.shape[1]
    ms = _dot((t * t).astype(BF16), headmean[:width, :width])
    y = t * lax.rsqrt(ms + EPS) * gain
    lane = lax.broadcasted_iota(jnp.int32, y.shape, 1)
    first = (lane % 32) < ROPE_FREQS
    partner = jnp.where(first, pltpu.roll(y, width - ROPE_FREQS, axis=1),
                        pltpu.roll(y, ROPE_FREQS, axis=1))
    reps = width // LANES
    return y * jnp.tile(cos, (1, reps)) + partner * jnp.tile(sin_signed, (1, reps))


def _qkv_kernel(x_ref, mod_ref, g_ref, w_ref, qg_ref, kg_ref, hm_ref, cos_ref, sin_ref,
                q_ref, k_ref, v_ref):
    m = mod_ref[0]
    h = _modulate(x_ref[...], g_ref[...], m[0:1], m[1:2]).astype(BF16)
    qkv = _dot(h, w_ref[...])
    nq = N_HEADS * HEAD_DIM
    nk = N_KV_HEADS * HEAD_DIM
    cos = cos_ref[...]
    sin = sin_ref[...]
    hm = hm_ref[...]
    q = _norm_rope(qkv[:, :nq], qg_ref[...], hm, cos, sin)
    k = _norm_rope(qkv[:, nq:nq + nk], kg_ref[...], hm, cos, sin)
    v = qkv[:, nq + nk:]
    q_ref[...] = q.astype(BF16)
    ones = jnp.ones((v.shape[0], HEAD_DIM), F32)
    for g in range(N_KV_HEADS):
        sl = slice(g * HEAD_DIM, (g + 1) * HEAD_DIM)
        k_ref[g] = k[:, sl].astype(F8)
        v_ref[g] = jnp.concatenate([v[:, sl], ones], axis=1).astype(BF16)


def _rope_tables(seq_len):
    rows = seq_len // GRID_W
    r = jnp.broadcast_to(jnp.arange(rows)[:, None], (rows, GRID_W)).reshape(-1)
    c = jnp.broadcast_to(jnp.arange(GRID_W)[None, :], (rows, GRID_W)).reshape(-1)
    inv_freq = ROPE_THETA ** (-jnp.arange(ROPE_FREQS, dtype=F32) / ROPE_FREQS)
    pos = jnp.stack([r, c], axis=-1).astype(F32)
    ang = pos[:, :, None] * inv_freq[None, None, :]
    cos = jnp.cos(ang)
    sin = jnp.sin(ang)
    cos64 = jnp.concatenate([cos, cos], axis=-1).reshape(seq_len, HEAD_DIM)
    sin64 = jnp.concatenate([-sin, sin], axis=-1).reshape(seq_len, HEAD_DIM)
    return jnp.tile(cos64, (1, 2)), jnp.tile(sin64, (1, 2))


def attn_qkv(x, mod, norm_g, wqkv, q_norm, k_norm, seq_len):
    n_rows = x.shape[0]
    tm = ROW_TILE
    nq = N_HEADS * HEAD_DIM
    nk = N_KV_HEADS * HEAD_DIM
    cos, sin = _rope_tables(seq_len)
    qg = jnp.tile(q_norm, N_HEADS)[None, :] * (HEAD_DIM ** -0.5 * math.log2(math.e))
    kg = jnp.tile(k_norm, N_KV_HEADS)[None, :]
    head = np.arange(nq) // HEAD_DIM
    headmean = jnp.asarray((head[:, None] == head[None, :]).astype(np.float32) / HEAD_DIM, BF16)
    per_seq = seq_len // tm
    tab_spec = pl.BlockSpec((tm, LANES), lambda i: (i % per_seq, 0))
    return pl.pallas_call(
        _qkv_kernel,
        grid=(n_rows // tm,),
        in_specs=[
            _row_spec(tm), _mod_spec(tm, seq_len), _const_spec((1, D_MODEL)),
            _const_spec((D_MODEL, nq + 2 * nk)), _const_spec((1, nq)), _const_spec((1, nk)),
            _const_spec((nq, nq)), tab_spec, tab_spec,
        ],
        out_specs=[
            _row_spec(tm, nq),
            pl.BlockSpec((N_KV_HEADS, tm, HEAD_DIM), lambda i: (0, i, 0)),
            pl.BlockSpec((N_KV_HEADS, tm, 2 * HEAD_DIM), lambda i: (0, i, 0)),
        ],
        out_shape=[
            jax.ShapeDtypeStruct((n_rows, nq), BF16),
            jax.ShapeDtypeStruct((N_KV_HEADS, n_rows, HEAD_DIM), F8),
            jax.ShapeDtypeStruct((N_KV_HEADS, n_rows, 2 * HEAD_DIM), BF16),
        ],
        compiler_params=_params(("parallel",)),
        name="attn_qkv",
    )(x, mod, norm_g[None, :], wqkv.astype(BF16), qg, kg, headmean, cos, sin)


def _flash_kernel(q_ref, k_ref, v_ref, o_ref, *, tq, tk, n_chunks):
    q = q_ref[...]
    qs = jnp.concatenate([q[:, j * HEAD_DIM:(j + 1) * HEAD_DIM] for j in range(Q_PER_KV)], axis=0)
    qs = qs.astype(F8)
    rows = Q_PER_KV * tq

    def body(c, carry):
        m, acc = carry
        start = pl.multiple_of(c * tk, tk)
        kc = k_ref[0, pl.ds(start, tk), :]
        vc = v_ref[0, pl.ds(start, tk), :]
        s = lax.dot_general(qs, kc, (((1,), (1,)), ((), ())), preferred_element_type=F32)
        m_new = jnp.maximum(m, jnp.max(s, axis=-1, keepdims=True))
        alpha = jnp.exp2(m - m_new)
        p = jnp.exp2(s - m_new)
        acc = acc * alpha + _dot(p.astype(BF16), vc)
        return m_new, acc

    m0 = jnp.full((rows, 1), -jnp.inf, F32)
    acc0 = jnp.zeros((rows, 2 * HEAD_DIM), F32)
    _, acc = lax.fori_loop(0, n_chunks, body, (m0, acc0))
    o = acc[:, :HEAD_DIM] / acc[:, HEAD_DIM:HEAD_DIM + 1]
    o_ref[...] = jnp.concatenate([o[j * tq:(j + 1) * tq] for j in range(Q_PER_KV)],
                                 axis=1).astype(BF16)


def attn_flash(q, k, v, batch, seq_len):
    n_rows = q.shape[0]
    tq = 256
    tk = min(seq_len, 2048)
    per_seq = seq_len // tq
    width = Q_PER_KV * HEAD_DIM
    kern = functools.partial(_flash_kernel, tq=tq, tk=tk, n_chunks=seq_len // tk)
    return pl.pallas_call(
        kern,
        grid=(batch, N_KV_HEADS, per_seq),
        in_specs=[
            pl.BlockSpec((tq, width), lambda b, g, i: (b * per_seq + i, g)),
            pl.BlockSpec((1, seq_len, HEAD_DIM), lambda b, g, i: (g, b, 0)),
            pl.BlockSpec((1, seq_len, 2 * HEAD_DIM), lambda b, g, i: (g, b, 0)),
        ],
        out_specs=pl.BlockSpec((tq, width), lambda b, g, i: (b * per_seq + i, g)),
        out_shape=jax.ShapeDtypeStruct((n_rows, N_HEADS * HEAD_DIM), BF16),
        compiler_params=_params(("parallel", "parallel", "parallel")),
        name="attn_flash",
    )(q, k, v)


def _proj_res_kernel(y_ref, w_ref, x_ref, mod_ref, *refs, gate_row):
    route_in, o_ref, route_out = refs[:N_ROUTE_IN], refs[N_ROUTE_IN], refs[N_ROUTE_IN + 1:]
    m = mod_ref[0]
    x1 = x_ref[...] + m[gate_row:gate_row + 1] * _dot(y_ref[...], w_ref[...])
    o_ref[...] = x1
    _route_tile(x1, m, *route_in, *route_out)


def proj_residual(y, w, x, mod, gate_row, seq_len, route):
    n_rows = x.shape[0]
    tm = STREAM_TILE
    r_ops, r_in, r_out, r_shape = _router_io(route, tm, n_rows)
    return pl.pallas_call(
        functools.partial(_proj_res_kernel, gate_row=gate_row),
        grid=(n_rows // tm,),
        in_specs=[_row_spec(tm, y.shape[1]), _const_spec(w.shape), _row_spec(tm),
                  _mod_spec(tm, seq_len)] + r_in,
        out_specs=[_row_spec(tm)] + r_out,
        out_shape=[jax.ShapeDtypeStruct((n_rows, D_MODEL), F32)] + r_shape,
        compiler_params=_params(("arbitrary",)),
        name="proj_residual",
    )(y, w.astype(BF16), x, mod, *r_ops)


CONV_COLS = 256


def _hy_in_kernel(*refs, tm, seq_len, n_stream):
    stream, refs = refs[:n_stream], refs[n_stream:]
    mod_ref, g_ref, w_ref, cw_ref, cb_ref, z_ref, x0_ref = refs[:7]
    stage_refs = refs[-3:]
    first, last = _edge_flags(tm, seq_len)
    m = mod_ref[0]
    x, ext = _stream_rows(stream)
    if n_stream == N_PENDING_REFS:
        refs[7][...] = x
    h = _modulate(ext, g_ref[...], m[0:1], m[1:2]).astype(BF16)
    tn = CONV_COLS
    for j in range(D_MODEL // tn):
        part = []
        for s in range(3):
            cols = slice(s * D_MODEL + j * tn, s * D_MODEL + (j + 1) * tn)
            part.append(_conv3(_dot(h, w_ref[:, cols]), cw_ref, cb_ref, cols, tm, first, last,
                               stage_refs[s]))
        out_cols = slice(j * tn, (j + 1) * tn)
        x0_ref[:, out_cols] = part[0].astype(BF16)
        z_ref[:, out_cols] = (part[2] * part[1]).astype(BF16)


def hyena_in(src, mod, norm_g, w_in, conv_w, conv_b, seq_len):
    pending = isinstance(src, _Pending)
    n_rows = (src.x if pending else src).shape[0]
    tm = ROW_TILE
    s_ops, s_specs = _stream_io(src, tm, n_rows, seq_len)
    x_out = [jax.ShapeDtypeStruct((n_rows, D_MODEL), F32)] if pending else []
    return pl.pallas_call(
        functools.partial(_hy_in_kernel, tm=tm, seq_len=seq_len, n_stream=len(s_ops)),
        grid=(n_rows // tm,),
        in_specs=s_specs + [_mod_spec(tm, seq_len), _const_spec((1, D_MODEL)),
                            _const_spec((D_MODEL, 3 * D_MODEL)), _const_spec((3, 3 * D_MODEL)),
                            _const_spec((1, 3 * D_MODEL))],
        out_specs=[_row_spec(tm)] * (2 + len(x_out)),
        out_shape=[jax.ShapeDtypeStruct((n_rows, D_MODEL), BF16),
                   jax.ShapeDtypeStruct((n_rows, D_MODEL), BF16)] + x_out,
        scratch_shapes=[pltpu.VMEM((tm + 2 * HALO, CONV_COLS), F32)] * 3,
        compiler_params=_params(("parallel",)),
        name="hyena_in",
    )(*s_ops, mod, norm_g[None, :], w_in.astype(BF16), conv_w, conv_b[None, :])


def _hy_filter_kernel(feat_ref, w1_ref, b1_ref, w2_ref, b2_ref, w3_ref, fr_ref, dl_ref, o_ref):
    feat = feat_ref[...]
    fr = fr_ref[...]
    a = jnp.sin(fr * (_dot3(feat, w1_ref[...]) + b1_ref[...]))
    a = jnp.sin(fr * (_dot3(a, w2_ref[...]) + b2_ref[...]))
    hf = _dot3(a, w3_ref[...])
    decay = jnp.exp(-feat[:, 0:1] * dl_ref[...])
    o_ref[0] = hf[:, :D_MODEL] * decay
    o_ref[1] = hf[:, D_MODEL:] * decay


def _pad_to(a, rows, cols):
    return jnp.pad(a.astype(F32), ((0, rows - a.shape[0]), (0, cols - a.shape[1])))


def hyena_filter(seq_len, w1, b1, w2, b2, w3, freq):
    t = jnp.linspace(0.0, 1.0, seq_len, dtype=F32)[:, None]
    w = 2.0 * math.pi * jnp.arange(seq_len, dtype=F32)[:, None] / seq_len
    f = jnp.linspace(1e-4, HY_BANDS - 1, HY_BANDS, dtype=F32)[None, :]
    feat = _pad_to(jnp.concatenate([t, jnp.cos(f * w), -jnp.sin(f * w)], axis=-1), seq_len, LANES)
    max_decay = math.log(HY_TARGET) / HY_FAST_DECAY
    min_decay = math.log(HY_TARGET) / HY_SLOW_DECAY
    absdelta = jnp.abs(jnp.linspace(min_decay, max_decay, D_MODEL, dtype=F32))[None, :]
    tl = 512
    return pl.pallas_call(
        _hy_filter_kernel,
        grid=(seq_len // tl,),
        in_specs=[_row_spec(tl, LANES), _const_spec((LANES, LANES)), _const_spec((1, LANES)),
                  _const_spec((LANES, LANES)), _const_spec((1, LANES)),
                  _const_spec((LANES, 2 * D_MODEL)), _const_spec((1, LANES)),
                  _const_spec((1, D_MODEL))],
        out_specs=pl.BlockSpec((2, tl, D_MODEL), lambda i: (0, i, 0)),
        out_shape=jax.ShapeDtypeStruct((2, seq_len, D_MODEL), F32),
        compiler_params=_params(("parallel",)),
        name="hyena_filter",
    )(feat, _pad_to(w1, LANES, LANES), _pad_to(b1[None, :], 1, LANES), _pad_to(w2, LANES, LANES),
      _pad_to(b2[None, :], 1, LANES), _pad_to(w3, LANES, 2 * D_MODEL),
      _pad_to(freq[None, :], 1, LANES), absdelta)


class _FFTPlan:
    def __init__(self, seq_len):
        self.n = 2 * seq_len
        self.n1 = self.n // DFT_N2
        self.r = self.n1 // 2
        self.k1n = self.n1 // 2 + 1
        self.kron = max(2 * HALO, LANES // self.r)
        ang = 2.0 * np.pi * np.outer(np.arange(self.k1n), np.arange(self.r)) / self.n1
        eye = np.eye(self.kron)
        self.fwd_cos = jnp.asarray(np.kron(np.cos(ang), eye), BF16)
        self.fwd_sin = jnp.asarray(np.kron(-np.sin(ang), eye), BF16)
        wgt = np.full((self.k1n,), 2.0)
        wgt[0] = wgt[-1] = 1.0
        scale = (wgt / self.n)[None, :]
        self.inv_cos = jnp.asarray(np.kron(np.cos(ang).T * scale, eye), BF16)
        self.inv_sin = jnp.asarray(np.kron(-np.sin(ang).T * scale, eye), BF16)
        a2 = 2.0 * np.pi * np.outer(np.arange(DFT_N2), np.arange(DFT_N2)) / DFT_N2
        self.f_cos = jnp.asarray(np.cos(a2), F32)
        self.f_sin = jnp.asarray(-np.sin(a2), F32)
        tw = 2.0 * np.pi * np.outer(np.arange(self.k1n), np.arange(DFT_N2)) / self.n
        self.tw_cos = jnp.asarray(np.cos(tw), F32)
        self.tw_sin = jnp.asarray(-np.sin(tw), F32)


def _fft_a_kernel(z_ref, wc_ref, ws_ref, ar_ref, ai_ref, *, rq):
    shape = ar_ref.shape[1:2] + ar_ref.shape[3:]
    for t in range(z_ref.shape[2]):
        z = z_ref[0, :, t].reshape(rq, D_MODEL).astype(BF16)
        ar_ref[0, :, t] = _dot(wc_ref[...], z).reshape(shape).astype(BF16)
        ai_ref[0, :, t] = _dot(ws_ref[...], z).reshape(shape).astype(BF16)


def _fft_group(plan):
    nhi = DFT_N2 // plan.kron
    per_group = plan.r * plan.kron * D_MODEL * 4
    return max(1, min(nhi, (2 << 20) // per_group))


def fft_stage_a(z, plan, batch):
    q = plan.kron
    nhi = DFT_N2 // q
    hb = _fft_group(plan)
    zv = z.reshape(batch, plan.r, nhi, q, D_MODEL)
    out = jax.ShapeDtypeStruct((batch, plan.k1n, nhi, q, D_MODEL), BF16)
    ospec = pl.BlockSpec((1, plan.k1n, hb, q, D_MODEL), lambda b, h: (b, 0, h, 0, 0))
    wspec = pl.BlockSpec(plan.fwd_cos.shape, lambda b, h: (0, 0))
    ar, ai = pl.pallas_call(
        functools.partial(_fft_a_kernel, rq=plan.r * q),
        grid=(batch, nhi // hb),
        in_specs=[pl.BlockSpec((1, plan.r, hb, q, D_MODEL), lambda b, h: (b, 0, h, 0, 0)),
                  wspec, wspec],
        out_specs=[ospec, ospec],
        out_shape=[out, out],
        compiler_params=_params(("parallel", "parallel")),
        name="fft_stage_a",
    )(zv, plan.fwd_cos, plan.fwd_sin)
    shape = (batch, plan.k1n, DFT_N2, D_MODEL)
    return ar.reshape(shape), ai.reshape(shape)


def _twiddled_dft(fr, fi, tr, ti):
    return (fr * tr - fi * ti).astype(BF16), (fr * ti + fi * tr).astype(BF16)


def _fft_b_fwd_kernel(ar_ref, ai_ref, fr_ref, fi_ref, twr_ref, twi_ref, br_ref, bi_ref,
                      gr_ref, gi_ref):
    @pl.when(pl.program_id(1) == 0)
    def _():
        gr, gi = _twiddled_dft(fr_ref[...], fi_ref[...], twr_ref[0], twi_ref[0])
        gr_ref[...] = gr
        gi_ref[...] = gi

    gr = gr_ref[...]
    gi = gi_ref[...]
    ar = ar_ref[0, 0].astype(BF16)
    ai = ai_ref[0, 0].astype(BF16)
    br_ref[0, 0] = _dot(gr, ar) - _dot(gi, ai)
    bi_ref[0, 0] = _dot(gr, ai) + _dot(gi, ar)


def fft_stage_b_fwd(ar, ai, plan):
    batch = ar.shape[0]
    blk = pl.BlockSpec((1, 1, DFT_N2, D_MODEL), lambda k, b: (b, k, 0, 0))
    cst = pl.BlockSpec((DFT_N2, DFT_N2), lambda k, b: (0, 0))
    tws = pl.BlockSpec((1, 1, DFT_N2), lambda k, b: (k, 0, 0))
    out = jax.ShapeDtypeStruct(ar.shape, F32)
    return pl.pallas_call(
        _fft_b_fwd_kernel,
        grid=(plan.k1n, batch),
        in_specs=[blk, blk, cst, cst, tws, tws],
        out_specs=[blk, blk],
        out_shape=[out, out],
        scratch_shapes=[pltpu.VMEM((DFT_N2, DFT_N2), BF16)] * 2,
        compiler_params=_params(("parallel", "arbitrary")),
        name="fft_stage_b_fwd",
    )(ar, ai, plan.f_cos, plan.f_sin, plan.tw_cos[:, None, :], plan.tw_sin[:, None, :])


def _fft_b_conv_kernel(ar_ref, ai_ref, hr_ref, hi_ref, hb0_ref, fr_ref, fi_ref,
                       twr_ref, twi_ref, tcr_ref, tci_ref, cr_ref, ci_ref,
                       gr_ref, gi_ref, gtr_ref, gti_ref, kr_ref, ki_ref):
    @pl.when(pl.program_id(1) == 0)
    def _():
        fr = fr_ref[...]
        fi = fi_ref[...]
        gr, gi = _twiddled_dft(fr, fi, twr_ref[0], twi_ref[0])
        gr_ref[...] = gr
        gi_ref[...] = gi
        gtr, gti = _twiddled_dft(fr, fi, tcr_ref[0], tci_ref[0])
        gtr_ref[...] = gtr
        gti_ref[...] = gti
        kr_ref[...] = hr_ref[0, 0] + hr_ref[1, 0] - hb0_ref[...]
        ki_ref[...] = hi_ref[0, 0] - hi_ref[1, 0]

    gr = gr_ref[...]
    gi = gi_ref[...]
    ar = ar_ref[0, 0].astype(BF16)
    ai = ai_ref[0, 0].astype(BF16)
    br = _dot(gr, ar) - _dot(gi, ai)
    bi = _dot(gr, ai) + _dot(gi, ar)
    kr = kr_ref[...]
    ki = ki_ref[...]
    pr = (br * kr - bi * ki).astype(BF16)
    pi = (br * ki + bi * kr).astype(BF16)
    gtr = gtr_ref[...]
    gti = gti_ref[...]
    cr_ref[0, 0] = (_dot(gtr, pr) + _dot(gti, pi)).astype(BF16)
    ci_ref[0, 0] = (_dot(gtr, pi) - _dot(gti, pr)).astype(BF16)


def fft_stage_b_conv(ar, ai, hr, hi, hb0, plan):
    batch = ar.shape[0]
    blk = pl.BlockSpec((1, 1, DFT_N2, D_MODEL), lambda k, b: (b, k, 0, 0))
    hblk = pl.BlockSpec((2, 1, DFT_N2, D_MODEL), lambda k, b: (0, k, 0, 0))
    cst = pl.BlockSpec((DFT_N2, DFT_N2), lambda k, b: (0, 0))
    tws = pl.BlockSpec((1, 1, DFT_N2), lambda k, b: (k, 0, 0))
    twc = pl.BlockSpec((1, DFT_N2, 1), lambda k, b: (k, 0, 0))
    out = jax.ShapeDtypeStruct(ar.shape, BF16)
    return pl.pallas_call(
        _fft_b_conv_kernel,
        grid=(plan.k1n, batch),
        in_specs=[blk, blk, hblk, hblk, pl.BlockSpec((1, D_MODEL), lambda k, b: (0, 0)),
                  cst, cst, tws, tws, twc, twc],
        out_specs=[blk, blk],
        out_shape=[out, out],
        scratch_shapes=[pltpu.VMEM((DFT_N2, DFT_N2), BF16)] * 4
        + [pltpu.VMEM((DFT_N2, D_MODEL), F32)] * 2,
        compiler_params=_params(("parallel", "arbitrary")),
        name="fft_stage_b_conv",
    )(ar, ai, hr, hi, hb0, plan.f_cos, plan.f_sin,
      plan.tw_cos[:, None, :], plan.tw_sin[:, None, :],
      plan.tw_cos[:, :, None], plan.tw_sin[:, :, None])


def _fft_a_inv_kernel(cr_ref, ci_ref, vc_ref, vs_ref, z_ref, x0_ref, skip_ref, y_ref, *, kq):
    shape = z_ref.shape[1:2] + z_ref.shape[3:]
    for t in range(z_ref.shape[2]):
        cr = cr_ref[0, :, t].reshape(kq, D_MODEL).astype(BF16)
        ci = ci_ref[0, :, t].reshape(kq, D_MODEL).astype(BF16)
        conv = _dot(vc_ref[...], cr) + _dot(vs_ref[...], ci)
        y = conv.reshape(shape) + z_ref[0, :, t].astype(F32) * skip_ref[...]
        y_ref[0, :, t] = (y * x0_ref[0, :, t].astype(F32)).astype(BF16)


def fft_stage_a_inv(cr, ci, z, x0, skip, plan, batch):
    q = plan.kron
    nhi = DFT_N2 // q
    hb = _fft_group(plan)
    cshape = (batch, plan.k1n, nhi, q, D_MODEL)
    tshape = (batch, plan.r, nhi, q, D_MODEL)
    cspec = pl.BlockSpec((1, plan.k1n, hb, q, D_MODEL), lambda b, h: (b, 0, h, 0, 0))
    tspec = pl.BlockSpec((1, plan.r, hb, q, D_MODEL), lambda b, h: (b, 0, h, 0, 0))
    wspec = pl.BlockSpec(plan.inv_cos.shape, lambda b, h: (0, 0))
    y = pl.pallas_call(
        functools.partial(_fft_a_inv_kernel, kq=plan.k1n * q),
        grid=(batch, nhi // hb),
        in_specs=[cspec, cspec, wspec, wspec, tspec, tspec,
                  pl.BlockSpec((1, D_MODEL), lambda b, h: (0, 0))],
        out_specs=tspec,
        out_shape=jax.ShapeDtypeStruct(tshape, BF16),
        compiler_params=_params(("parallel", "parallel")),
        name="fft_stage_a_inv",
    )(cr.reshape(cshape), ci.reshape(cshape), plan.inv_cos, plan.inv_sin,
      z.reshape(tshape), x0.reshape(tshape), skip[None, :])
    return y.reshape(z.shape)


def hyena_mixer(src, mod, norm_g, p, batch, seq_len, route):
    z, x0, *rest = hyena_in(src, mod, norm_g, p["w_in"], p["conv_w"], p["conv_b"], seq_len)
    x = rest[0] if rest else src
    plan = _FFTPlan(seq_len)
    filt = hyena_filter(seq_len, p["w1"], p["b1"], p["w2"], p["b2"], p["w3"], p["freq"])
    fr, fi = fft_stage_a(filt.reshape(2 * seq_len, D_MODEL), plan, 2)
    hr, hi = fft_stage_b_fwd(fr, fi, plan)
    ar, ai = fft_stage_a(z, plan, batch)
    cr, ci = fft_stage_b_conv(ar, ai, hr, hi, filt[1, 0:1, :], plan)
    y = fft_stage_a_inv(cr, ci, z, x0, p["skip"], plan, batch)
    return proj_residual(y, p["w_out"], x, mod, 2, seq_len, route)


def _pool_kernel(*refs, tm, seq_len, n_stream):
    stream, refs = refs[:n_stream], refs[n_stream:]
    mod_ref, g_ref, w_ref, s_ref = refs[:4]
    refs = refs[4:]
    route_in, o_ref, route_out = refs[:N_ROUTE_IN], refs[N_ROUTE_IN], refs[N_ROUTE_IN + 1:]
    first, last = _edge_flags(tm, seq_len)
    m = mod_ref[0]
    x, ext = _stream_rows(stream)
    h = _modulate(ext, g_ref[...], m[0:1], m[1:2])
    n = tm + 2 * HALO
    row = lax.broadcasted_iota(jnp.int32, (n, 1), 0)
    outside = jnp.logical_or(jnp.logical_and(first, row < HALO),
                             jnp.logical_and(last, row >= HALO + tm))
    h = jnp.where(outside, 0.0, h)
    pos = (pl.program_id(0) * tm) % seq_len + lax.broadcasted_iota(jnp.int32, (tm, 1), 0)
    ys = []
    for gi, win in enumerate(POOL_WINDOWS):
        cols = slice(gi * POOL_GROUP_DIM, (gi + 1) * POOL_GROUP_DIM)
        hg = h[:, cols]
        acc = hg
        span = 1
        while span < win:
            acc = acc + pltpu.roll(acc, span, axis=0)
            span *= 2
        lead = win // 2 - 1
        if lead:
            acc = pltpu.roll(acc, n - lead, axis=0)
        half = win // 2
        cnt = jnp.minimum(pos + half, seq_len) - jnp.maximum(pos - half, 0)
        pooled = acc[HALO:HALO + tm] / cnt.astype(F32) - hg[HALO:HALO + tm]
        ys.append(_dot(pooled.astype(BF16), w_ref[gi]))
    y = jnp.concatenate(ys, axis=1) * s_ref[...]
    x1 = x + m[2:3] * y
    o_ref[...] = x1
    _route_tile(x1, m, *route_in, *route_out)


def pool_mixer(src, mod, norm_g, w_group, scale, seq_len, route):
    n_rows = (src.x if isinstance(src, _Pending) else src).shape[0]
    tm = ROW_TILE
    s_ops, s_specs = _stream_io(src, tm, n_rows, seq_len)
    r_ops, r_in, r_out, r_shape = _router_io(route, tm, n_rows)
    return pl.pallas_call(
        functools.partial(_pool_kernel, tm=tm, seq_len=seq_len, n_stream=len(s_ops)),
        grid=(n_rows // tm,),
        in_specs=s_specs + [_mod_spec(tm, seq_len), _const_spec((1, D_MODEL)),
                            _const_spec(w_group.shape), _const_spec((1, D_MODEL))] + r_in,
        out_specs=[_row_spec(tm)] + r_out,
        out_shape=[jax.ShapeDtypeStruct((n_rows, D_MODEL), F32)] + r_shape,
        compiler_params=_params(("arbitrary",)),
        name="pool_mixer",
    )(*s_ops, mod, norm_g[None, :], w_group.astype(BF16), scale[None, :], *r_ops)


def _sconv_kernel(*refs, tm, seq_len, n_stream):
    stream, refs = refs[:n_stream], refs[n_stream:]
    mod_ref, g_ref, w_ref, cw_ref, cb_ref, wo_ref = refs[:6]
    refs = refs[6:]
    route_in, o_ref = refs[:N_ROUTE_IN], refs[N_ROUTE_IN]
    route_out, y_ref = refs[N_ROUTE_IN + 1:N_ROUTE_IN + 1 + N_ROUTE_OUT], refs[-1]
    first, last = _edge_flags(tm, seq_len)
    m = mod_ref[0]
    x, ext = _stream_rows(stream)
    h = _modulate(ext, g_ref[...], m[0:1], m[1:2]).astype(BF16)
    tn = 256
    for j in range(D_MODEL // tn):
        cols = slice(j * tn, (j + 1) * tn)
        bg = _dot(h, w_ref[:, cols])[HALO:HALO + tm]
        cg = _dot(h, w_ref[:, D_MODEL + j * tn:D_MODEL + (j + 1) * tn])
        hp = _dot(h, w_ref[:, 2 * D_MODEL + j * tn:2 * D_MODEL + (j + 1) * tn])
        y_ref[:, cols] = (bg * _conv3(cg * hp, cw_ref, cb_ref, cols, tm, first, last)).astype(BF16)
    x1 = x + m[2:3] * _dot(y_ref[...], wo_ref[...])
    o_ref[...] = x1
    _route_tile(x1, m, *route_in, *route_out)


def sconv_mixer(src, mod, norm_g, w_in, conv_w, conv_b, w_out, seq_len, route):
    n_rows = (src.x if isinstance(src, _Pending) else src).shape[0]
    tm = ROW_TILE
    s_ops, s_specs = _stream_io(src, tm, n_rows, seq_len)
    r_ops, r_in, r_out, r_shape = _router_io(route, tm, n_rows)
    return pl.pallas_call(
        functools.partial(_sconv_kernel, tm=tm, seq_len=seq_len, n_stream=len(s_ops)),
        grid=(n_rows // tm,),
        in_specs=s_specs + [_mod_spec(tm, seq_len), _const_spec((1, D_MODEL)),
                            _const_spec((D_MODEL, 3 * D_MODEL)), _const_spec((3, D_MODEL)),
                            _const_spec((1, D_MODEL)), _const_spec((D_MODEL, D_MODEL))] + r_in,
        out_specs=[_row_spec(tm)] + r_out,
        out_shape=[jax.ShapeDtypeStruct((n_rows, D_MODEL), F32)] + r_shape,
        scratch_shapes=[pltpu.VMEM((tm, D_MODEL), BF16)],
        compiler_params=_params(("arbitrary",)),
        name="sconv_mixer",
    )(*s_ops, mod, norm_g[None, :], w_in.astype(BF16), conv_w, conv_b[None, :],
      w_out.astype(BF16), *r_ops)


def _pack_bf16(x):
    w = x.shape[1] // 2
    bits = pltpu.bitcast(x.astype(BF16).astype(F32), jnp.uint32)
    return (bits[:, :w] >> 16) | bits[:, w:]


def _unpack_bf16(p):
    lo = pltpu.bitcast(p << 16, F32)
    hi = pltpu.bitcast(p & jnp.uint32(0xFFFF0000), F32)
    return jnp.concatenate([lo, hi], axis=1)


META_E1, META_E2, META_W1, META_W2, META_R1, META_R2 = range(6)
META_ROWS = 8


def _route_tile(x, m, g_ref, wh_ref, b_ref, tri_ref, h_ref, meta_ref, meta_t_ref, cnt_ref):
    @pl.when(pl.program_id(0) == 0)
    def _():
        cnt_ref[...] = jnp.zeros_like(cnt_ref)

    ms = jnp.mean(x * x, axis=-1, keepdims=True)
    h = x * lax.rsqrt(ms + EPS) * (g_ref[...] * (1.0 + m[4:5])) + m[3:4]
    hi = h.astype(BF16)
    hi32 = hi.astype(F32)
    lo = (h - hi32).astype(BF16)
    half = D_MODEL // 2
    bits = pltpu.bitcast(hi32, jnp.uint32)
    h_ref[...] = (bits[:, :half] >> 16) | bits[:, half:]
    part = _dot(hi, wh_ref[...])
    lg = part[:, :LANES] + (part[:, LANES:] + _dot(lo, wh_ref[:, :LANES])) + b_ref[...]
    lgt = lg.T
    tm = lgt.shape[1]
    neg = -jnp.inf
    sub = lax.broadcasted_iota(jnp.int32, (HALO, tm), 0).astype(F32)

    def first_argmax(vals):
        top = jnp.max(vals, axis=0, keepdims=True)
        idx = jnp.min(jnp.where(vals == top, sub, float(HALO)), axis=0, keepdims=True)
        return top, idx

    gl = jnp.where(sub < MOE_GROUPS, lgt[MOE_N_EXPERTS:MOE_N_EXPERTS + HALO], neg)
    gmax, grp = first_argmax(gl)
    g_w = 1.0 / jnp.sum(jnp.exp(gl - gmax), axis=0, keepdims=True)
    el = lgt[:MOE_EXPERTS_PER_GROUP]
    for g in range(1, MOE_GROUPS):
        el = jnp.where(grp == float(g),
                       lgt[g * MOE_EXPERTS_PER_GROUP:(g + 1) * MOE_EXPERTS_PER_GROUP], el)
    v1, i1 = first_argmax(el)
    v2, i2 = first_argmax(jnp.where(sub == i1, neg, el))
    ex = jnp.exp(v2 - v1)
    w1 = 1.0 / (1.0 + ex)
    w2 = ex * w1
    e1 = grp * MOE_EXPERTS_PER_GROUP + i1
    e2 = grp * MOE_EXPERTS_PER_GROUP + i2
    expert = lax.broadcasted_iota(jnp.int32, (MOE_N_EXPERTS, tm), 0).astype(F32)
    onehot = jnp.where(jnp.logical_or(expert == e1, expert == e2), 1.0, 0.0)
    before = _dot(onehot.astype(BF16), tri_ref[...]) + cnt_ref[...]
    cnt_ref[...] += jnp.sum(onehot, axis=1, keepdims=True)
    r1 = jnp.sum(jnp.where(expert == e1, before, 0.0), axis=0, keepdims=True)
    r2 = jnp.sum(jnp.where(expert == e2, before, 0.0), axis=0, keepdims=True)
    fields = ((META_E1, e1), (META_E2, e2), (META_W1, w1 * g_w), (META_W2, w2 * g_w),
              (META_R1, r1), (META_R2, r2))
    field = lax.broadcasted_iota(jnp.int32, (LANES, tm), 0)
    meta_t = jnp.zeros((LANES, tm), F32)
    for row, val in fields:
        meta_t = jnp.where(field == row, val, meta_t)
    meta_t_ref[...] = meta_t[:META_ROWS]
    meta_ref[...] = meta_t.T


N_ROUTE_IN = 4
N_ROUTE_OUT = 4


def _router_io(route, tm, n_rows):
    norm_g, w_group, b_group, w_router, b_router = route
    w = _pad_to(jnp.concatenate([w_router, w_group], axis=1), D_MODEL, LANES)
    wh = w.astype(BF16)
    wl = (w - wh.astype(F32)).astype(BF16)
    whl = jnp.concatenate([wh, wl], axis=1)
    b = _pad_to(jnp.concatenate([b_router, b_group])[None, :], 1, LANES)
    tri = jnp.asarray(np.triu(np.ones((tm, tm), np.float32), 1), BF16)
    operands = (norm_g[None, :], whl, b, tri)
    in_specs = [_const_spec((1, D_MODEL)), _const_spec((D_MODEL, 2 * LANES)),
                _const_spec((1, LANES)), _const_spec((tm, tm))]
    out_specs = [_row_spec(tm, D_MODEL // 2), _row_spec(tm, LANES),
                 pl.BlockSpec((META_ROWS, tm), lambda i: (0, i)),
                 _const_spec((MOE_N_EXPERTS, 1))]
    out_shape = [jax.ShapeDtypeStruct((n_rows, D_MODEL // 2), jnp.uint32),
                 jax.ShapeDtypeStruct((n_rows, LANES), F32),
                 jax.ShapeDtypeStruct((META_ROWS, n_rows), F32),
                 jax.ShapeDtypeStruct((MOE_N_EXPERTS, 1), F32)]
    return operands, in_specs, out_specs, out_shape


def _slot_kernel(offs_ref, meta_t_ref, pos_ref):
    meta = meta_t_ref[...]
    start = jnp.zeros_like(meta)
    for e in range(MOE_N_EXPERTS):
        start = jnp.where(meta == float(e), offs_ref[e], start)
    shift = META_ROWS - (META_R1 - META_E1)
    pos_ref[...] = (start + pltpu.roll(meta, shift, axis=0)).astype(jnp.int32)


def moe_slots(meta_t, offsets):
    n_rows = meta_t.shape[1]
    tn = min(n_rows, 8192)
    blk = pl.BlockSpec((META_ROWS, tn), lambda i, offs: (0, i))
    return pl.pallas_call(
        _slot_kernel,
        grid_spec=pltpu.PrefetchScalarGridSpec(
            num_scalar_prefetch=1, grid=(n_rows // tn,), in_specs=[blk], out_specs=blk),
        out_shape=jax.ShapeDtypeStruct((META_ROWS, n_rows), jnp.int32),
        compiler_params=_params(("parallel",)),
        name="moe_slots",
    )(offsets, meta_t)


def _expert_kernel(te_ref, nu_ref, nv_ref, xs_ref, wg_ref, wu_ref, wd_ref, o_ref,
                   wgu_ref, wdb_ref):
    j = pl.program_id(0)

    @pl.when(jnp.logical_or(j == 0, te_ref[j] != te_ref[jnp.maximum(j - 1, 0)]))
    def _():
        wgu_ref[:, :MOE_D_FF] = wg_ref[0, 0].astype(BF16)
        wgu_ref[:, MOE_D_FF:] = wu_ref[0, 0].astype(BF16)
        wdb_ref[...] = wd_ref[0, 0].astype(BF16)

    @pl.when(j < nu_ref[0])
    def _():
        x = _unpack_bf16(xs_ref[...])
        row = lax.broadcasted_iota(jnp.int32, (x.shape[0], 1), 0)
        x = jnp.where(row < nv_ref[j], x, 0.0).astype(BF16)
        au = _dot(x, wgu_ref[...])
        hh = (_silu(au[:, :MOE_D_FF]) * au[:, MOE_D_FF:]).astype(BF16)
        o_ref[...] = _pack_bf16(_dot(hh, wdb_ref[...]))


def moe_experts(xs, tile_expert, n_used, n_valid, w_gate, w_up, w_down, layer, tm):
    n_slots, half = xs.shape
    wspec = lambda shape: pl.BlockSpec((1, 1) + shape, lambda j, te, nu, nv: (layer, te[j], 0, 0))
    row = pl.BlockSpec((tm, half), lambda j, te, nu, nv: (j, 0))
    grid_spec = pltpu.PrefetchScalarGridSpec(
        num_scalar_prefetch=3,
        grid=(n_slots // tm,),
        in_specs=[row, wspec((D_MODEL, MOE_D_FF)), wspec((D_MODEL, MOE_D_FF)),
                  wspec((MOE_D_FF, D_MODEL))],
        out_specs=row,
        scratch_shapes=[pltpu.VMEM((D_MODEL, 2 * MOE_D_FF), BF16),
                        pltpu.VMEM((MOE_D_FF, D_MODEL), BF16)],
    )
    return pl.pallas_call(
        _expert_kernel,
        grid_spec=grid_spec,
        out_shape=jax.ShapeDtypeStruct((n_slots, half), jnp.uint32),
        compiler_params=_params(("arbitrary",)),
        name="moe_experts",
    )(tile_expert, n_used, n_valid, xs, w_gate, w_up, w_down)


def _combine_kernel(x_ref, ya_ref, yb_ref, meta_ref, mod_ref, o_ref):
    o_ref[...] = _moe_combined(x_ref, ya_ref, yb_ref, meta_ref, mod_ref[0][5:6])


def moe_combine(pending, seq_len):
    x, ya, yb, meta, mod = pending
    n_rows = x.shape[0]
    tm = STREAM_TILE
    half = D_MODEL // 2
    return pl.pallas_call(
        _combine_kernel,
        grid=(n_rows // tm,),
        in_specs=[_row_spec(tm), _row_spec(tm, half), _row_spec(tm, half), _row_spec(tm, LANES),
                  _mod_spec(tm, seq_len)],
        out_specs=_row_spec(tm),
        out_shape=jax.ShapeDtypeStruct((n_rows, D_MODEL), F32),
        compiler_params=_params(("parallel",)),
        name="moe_combine",
    )(x, ya, yb, meta, mod)


MOE_TILE = 512
MOE_TILE_LARGE = 1024
MOE_LARGE_ROWS = 32768


def hier_moe(routed, mod, w_gate, w_up, w_down, layer, seq_len):
    x, hp, meta, meta_t, counts = routed
    n_rows = x.shape[0]
    tm = MOE_TILE_LARGE if n_rows >= MOE_LARGE_ROWS else MOE_TILE
    cnt = counts[:, 0].astype(jnp.int32)
    padded = (cnt + tm - 1) // tm * tm
    ends = jnp.cumsum(padded)
    starts = ends - padded
    n_slots = 2 * n_rows + MOE_N_EXPERTS * tm
    tile_start = jnp.arange(n_slots // tm, dtype=jnp.int32) * tm
    tile_expert = jnp.minimum(jnp.sum(tile_start[:, None] >= ends[None, :], axis=1),
                              MOE_N_EXPERTS - 1).astype(jnp.int32)
    n_used = (ends[-1:] // tm).astype(jnp.int32)
    n_valid = jnp.clip(starts[tile_expert] + cnt[tile_expert] - tile_start, 0, tm).astype(jnp.int32)
    pos = moe_slots(meta_t, starts.astype(F32))
    idx0 = pos[META_E1].reshape(n_rows // SC_BLOCK, SC_BLOCK)
    idx1 = pos[META_E2].reshape(n_rows // SC_BLOCK, SC_BLOCK)
    xs = sc_scatter_rows(hp, idx0, idx1, n_slots)
    ys = moe_experts(xs, tile_expert, n_used, n_valid, w_gate, w_up, w_down, layer, tm)
    ya, yb = sc_gather_rows(ys, idx0, idx1)
    return _Pending(x, ya, yb, meta, mod)


SC_CORES = 2
SC_SUBCORES = 16
SC_WORKERS = SC_CORES * SC_SUBCORES
SC_BLOCK = 128


def _sc_mesh():
    return plsc.VectorSubcoreMesh(core_axis_name="c", subcore_axis_name="s")


def _sc_worker():
    return lax.axis_index("s") * SC_CORES + lax.axis_index("c")


def sc_scatter_rows(rows, idx0, idx1, n_slots):
    n_rows, width = rows.shape
    per_worker = n_rows // SC_BLOCK // SC_WORKERS

    @functools.partial(
        pl.kernel, mesh=_sc_mesh(),
        out_type=jax.ShapeDtypeStruct((n_slots, width), rows.dtype),
        scratch_types=[pltpu.VMEM((SC_BLOCK,), jnp.int32), pltpu.VMEM((SC_BLOCK,), jnp.int32),
                       pltpu.VMEM((SC_BLOCK, width), rows.dtype)],
        name="sc_scatter_rows",
    )
    def scatter(rows_hbm, i0_hbm, i1_hbm, out_hbm, i0_v, i1_v, rows_v):
        first = _sc_worker() * per_worker

        @pl.loop(0, per_worker)
        def _(j):
            blk = first + j
            pltpu.sync_copy(i0_hbm.at[blk], i0_v)
            pltpu.sync_copy(i1_hbm.at[blk], i1_v)
            pltpu.sync_copy(rows_hbm.at[pl.ds(blk * SC_BLOCK, SC_BLOCK)], rows_v)
            pltpu.sync_copy(rows_v, out_hbm.at[i0_v])
            pltpu.sync_copy(rows_v, out_hbm.at[i1_v])

    return scatter(rows, idx0, idx1)


def sc_gather_rows(src, idx0, idx1):
    width = src.shape[1]
    n_rows = idx0.shape[0] * SC_BLOCK
    per_worker = n_rows // SC_BLOCK // SC_WORKERS
    out = jax.ShapeDtypeStruct((n_rows, width), src.dtype)

    @functools.partial(
        pl.kernel, mesh=_sc_mesh(), out_type=(out, out),
        scratch_types=[pltpu.VMEM((SC_BLOCK,), jnp.int32), pltpu.VMEM((SC_BLOCK, width), src.dtype)],
        name="sc_gather_rows",
    )
    def gather(src_hbm, i0_hbm, i1_hbm, a_hbm, b_hbm, idx_v, rows_v):
        first = _sc_worker() * per_worker

        @pl.loop(0, per_worker)
        def _(j):
            blk = first + j
            dst = pl.ds(blk * SC_BLOCK, SC_BLOCK)
            pltpu.sync_copy(i0_hbm.at[blk], idx_v)
            pltpu.sync_copy(src_hbm.at[idx_v], rows_v)
            pltpu.sync_copy(rows_v, a_hbm.at[dst])
            pltpu.sync_copy(i1_hbm.at[blk], idx_v)
            pltpu.sync_copy(src_hbm.at[idx_v], rows_v)
            pltpu.sync_copy(rows_v, b_hbm.at[dst])

    return gather(src, idx0, idx1)


def _trunk(x3, mods, p):
    batch, seq_len, _ = x3.shape
    x = x3.reshape(batch * seq_len, D_MODEL)
    for i in range(DEPTH):
        mod = mods[i]
        g1 = p["norm1_g"][i]
        route = (p["norm2_g"][i], p["moe_w_group"][i], p["moe_b_group"][i], p["moe_w_router"][i],
                 p["moe_b_router"][i])
        kind = i % 4
        if kind == 0:
            if isinstance(x, _Pending):
                x = moe_combine(x, seq_len)
            q, k, v = attn_qkv(x, mod, g1, p["attn_wqkv"][0], p["attn_q_norm"][0],
                               p["attn_k_norm"][0], seq_len)
            o = attn_flash(q, k, v, batch, seq_len)
            routed = proj_residual(o, p["attn_wo"][0], x, mod, 2, seq_len, route)
        elif kind == 1:
            hp = {"w_in": p["hy_w_in"][0], "conv_w": p["hy_conv_w"][0], "conv_b": p["hy_conv_b"][0],
                  "w1": p["hy_ffn_w1"][0], "b1": p["hy_ffn_b1"][0], "w2": p["hy_ffn_w2"][0],
                  "b2": p["hy_ffn_b2"][0], "w3": p["hy_ffn_w3"][0], "freq": p["hy_freq"][0],
                  "skip": p["hy_skip"][0], "w_out": p["hy_w_out"][0]}
            routed = hyena_mixer(x, mod, g1, hp, batch, seq_len, route)
        elif kind == 2:
            routed = pool_mixer(x, mod, g1, p["pool_w"][0], p["pool_scale"][0], seq_len, route)
        else:
            routed = sconv_mixer(x, mod, g1, p["sc_w_in"][0], p["sc_conv_w"][0],
                                 p["sc_conv_b"][0], p["sc_w_out"][0], seq_len, route)
        x = hier_moe(routed, mod, p["moe_w_gate"], p["moe_w_up"], p["moe_w_down"], i, seq_len)
    return moe_combine(x, seq_len).reshape(batch, seq_len, D_MODEL)


def kernel(x_prompt, x_sample, c_prompt, c_sample, norm1_g, norm2_g, ada_w, ada_b, attn_wqkv, attn_q_norm, attn_k_norm, attn_wo, hy_w_in, hy_conv_w, hy_conv_b, hy_ffn_w1, hy_ffn_b1, hy_ffn_w2, hy_ffn_b2, hy_ffn_w3, hy_freq, hy_skip, hy_w_out, pool_w, pool_scale, sc_w_in, sc_conv_w, sc_conv_b, sc_w_out, moe_w_group, moe_b_group, moe_w_router, moe_b_router, moe_w_gate, moe_w_up, moe_w_down):
    p = dict(norm1_g=norm1_g, norm2_g=norm2_g, attn_wqkv=attn_wqkv, attn_q_norm=attn_q_norm,
             attn_k_norm=attn_k_norm, attn_wo=attn_wo, hy_w_in=hy_w_in, hy_conv_w=hy_conv_w,
             hy_conv_b=hy_conv_b, hy_ffn_w1=hy_ffn_w1, hy_ffn_b1=hy_ffn_b1, hy_ffn_w2=hy_ffn_w2,
             hy_ffn_b2=hy_ffn_b2, hy_ffn_w3=hy_ffn_w3, hy_freq=hy_freq, hy_skip=hy_skip,
             hy_w_out=hy_w_out, pool_w=pool_w, pool_scale=pool_scale, sc_w_in=sc_w_in,
             sc_conv_w=sc_conv_w, sc_conv_b=sc_conv_b, sc_w_out=sc_w_out, moe_w_group=moe_w_group,
             moe_b_group=moe_b_group, moe_w_router=moe_w_router, moe_b_router=moe_b_router,
             moe_w_gate=moe_w_gate, moe_w_up=moe_w_up, moe_w_down=moe_w_down)
    nb = c_prompt.shape[0]
    ns = c_sample.shape[0]
    rows = -(-(nb + ns) // HALO) * HALO
    c_all = jnp.pad(jnp.concatenate([c_prompt, c_sample], axis=0), ((0, rows - nb - ns), (0, 0)))
    mod = ada_mod(c_all, ada_w, ada_b).reshape(DEPTH, rows, 6, D_MODEL)
    mods_prompt = [mod[i, :nb] for i in range(DEPTH)]
    mods_sample = [mod[i, nb:nb + ns] for i in range(DEPTH)]
    return _trunk(x_prompt, mods_prompt, p), _trunk(x_sample, mods_sample, p)
```

```python
import functools
import math
from typing import NamedTuple

import jax
import jax.numpy as jnp
import numpy as np
from jax import lax
from jax.experimental import pallas as pl
from jax.experimental.pallas import tpu as pltpu
from jax.experimental.pallas import tpu_sc as plsc

F32 = jnp.float32
BF16 = jnp.bfloat16
F8 = jnp.float8_e4m3fn

D_MODEL = 1024
DEPTH = 4
EPS = 1e-6
GRID_W = 64
HEAD_DIM = 64
N_HEADS = 16
N_KV_HEADS = 4
Q_PER_KV = 4
ROPE_THETA = 10000.0
ROPE_FREQS = 16
HY_EMB_DIM = 33
HY_BANDS = 16
HY_FILTER_WIDTH = 64
HY_FAST_DECAY = 0.3
HY_SLOW_DECAY = 1.5
HY_TARGET = 1e-2
POOL_WINDOWS = (2, 4, 8, 16)
POOL_GROUP_DIM = 256
MOE_GROUPS = 4
MOE_EXPERTS_PER_GROUP = 8
MOE_N_EXPERTS = 32
MOE_D_FF = 256

LANES = 128
HALO = 8
DFT_N2 = 256
VMEM_LIMIT = 56 * 1024 * 1024

ROW_TILE = 512
STREAM_TILE = 1024


def _params(sem):
    return pltpu.CompilerParams(dimension_semantics=sem, vmem_limit_bytes=VMEM_LIMIT)


def _dot(a, b):
    return jnp.dot(a, b, preferred_element_type=F32)


def _split(a):
    hi = a.astype(BF16)
    lo = (a - hi.astype(F32)).astype(BF16)
    return hi, lo


def _dot3(a, b):
    ah, al = _split(a)
    bh, bl = _split(b)
    return _dot(ah, bh) + (_dot(ah, bl) + _dot(al, bh))


def _modulate(x, g, shift, scale):
    ms = jnp.mean(x * x, axis=-1, keepdims=True)
    return x * lax.rsqrt(ms + EPS) * g * (1.0 + scale) + shift


def _silu(x):
    return x * (1.0 / (1.0 + jnp.exp(-x)))


def _ada_kernel(c_ref, w_ref, b_ref, o_ref):
    c = c_ref[...]
    o_ref[0] = _dot3(_silu(c), w_ref[0]) + b_ref[0]


def ada_mod(c_all, ada_w, ada_b):
    rows = c_all.shape[0]
    n = ada_w.shape[2]
    tn = 1536
    return pl.pallas_call(
        _ada_kernel,
        grid=(DEPTH, n // tn),
        in_specs=[
            pl.BlockSpec((rows, D_MODEL), lambda l, j: (0, 0)),
            pl.BlockSpec((1, D_MODEL, tn), lambda l, j: (l, 0, j)),
            pl.BlockSpec((1, 1, tn), lambda l, j: (l, 0, j)),
        ],
        out_specs=pl.BlockSpec((1, rows, tn), lambda l, j: (l, 0, j)),
        out_shape=jax.ShapeDtypeStruct((DEPTH, rows, n), F32),
        compiler_params=_params(("parallel", "parallel")),
        name="ada_mod",
    )(c_all, ada_w, ada_b.reshape(DEPTH, 1, n))


def _row_spec(tm, width=D_MODEL):
    return pl.BlockSpec((tm, width), lambda i: (i, 0))


def _mod_spec(tm, seq_len):
    return pl.BlockSpec((1, 6, D_MODEL), lambda i: ((i * tm) // seq_len, 0, 0))


def _const_spec(shape):
    nd = len(shape)
    return pl.BlockSpec(shape, lambda i: (0,) * nd)


def _halo_specs(tm, n_rows, width=D_MODEL):
    per = tm // HALO
    last = n_rows // HALO - 1
    prev = pl.BlockSpec((HALO, width), lambda i: (jnp.maximum(i * per - 1, 0), 0))
    nxt = pl.BlockSpec((HALO, width), lambda i: (jnp.minimum((i + 1) * per, last), 0))
    return prev, nxt


class _Pending(NamedTuple):
    x: jax.Array
    ya: jax.Array
    yb: jax.Array
    meta: jax.Array
    mod: jax.Array


N_PLAIN_REFS = 3
N_PENDING_REFS = 13


def _stream_io(src, tm, n_rows, seq_len):
    if not isinstance(src, _Pending):
        prev, nxt = _halo_specs(tm, n_rows)
        return [src] * 3, [prev, _row_spec(tm), nxt]
    operands, specs = [], []
    for arr in (src.x, src.ya, src.yb, src.meta):
        width = arr.shape[1]
        prev, nxt = _halo_specs(tm, n_rows, width)
        operands += [arr] * 3
        specs += [prev, _row_spec(tm, width), nxt]
    return operands + [src.mod], specs + [_mod_spec(tm, seq_len)]


def _moe_combined(x_ref, ya_ref, yb_ref, meta_ref, gate):
    meta = meta_ref[...]
    y = (meta[:, META_W1:META_W1 + 1] * _unpack_bf16(ya_ref[...])
         + meta[:, META_W2:META_W2 + 1] * _unpack_bf16(yb_ref[...]))
    return x_ref[...] + gate * y


def _stream_rows(refs):
    if len(refs) == N_PLAIN_REFS:
        prev_ref, x_ref, next_ref = refs
        x = x_ref[...]
        return x, jnp.concatenate([prev_ref[...], x, next_ref[...]], axis=0)
    gate = refs[-1][0][5:6]
    parts = [_moe_combined(refs[k], refs[3 + k], refs[6 + k], refs[9 + k], gate) for k in range(3)]
    return parts[1], jnp.concatenate(parts, axis=0)


def _edge_flags(tm, seq_len):
    i = pl.program_id(0)
    per_seq = seq_len // tm
    pos = i % per_seq
    return pos == 0, pos == per_seq - 1


def _shift_rows(u, tm):
    n = u.shape[0]
    up = pltpu.roll(u, 1, axis=0)[HALO:HALO + tm]
    dn = pltpu.roll(u, n - 1, axis=0)[HALO:HALO + tm]
    return up, u[HALO:HALO + tm], dn


def _conv3(u, w_ref, b_ref, cols, tm, first, last, stage_ref=None):
    if stage_ref is None:
        up, mid, dn = _shift_rows(u, tm)
    else:
        stage_ref[...] = u
        up = stage_ref[pl.ds(HALO - 1, tm), :]
        mid = stage_ref[pl.ds(HALO, tm), :]
        dn = stage_ref[pl.ds(HALO + 1, tm), :]
    row = lax.broadcasted_iota(jnp.int32, (tm, 1), 0)
    up = jnp.where(jnp.logical_and(first, row == 0), 0.0, up)
    dn = jnp.where(jnp.logical_and(last, row == tm - 1), 0.0, dn)
    w = w_ref[:, cols]
    return up * w[0:1] + mid * w[1:2] + dn * w[2:3] + b_ref[:, cols]


def _norm_rope(t, gain, headmean, cos, sin_signed):
    width = t.shape[1]
    ms = _dot((t * t).astype(BF16), headmean[:width, :width])
    y = t * lax.rsqrt(ms + EPS) * gain
    lane = lax.broadcasted_iota(jnp.int32, y.shape, 1)
    first = (lane % 32) < ROPE_FREQS
    partner = jnp.where(first, pltpu.roll(y, width - ROPE_FREQS, axis=1),
                        pltpu.roll(y, ROPE_FREQS, axis=1))
    reps = width // LANES
    return y * jnp.tile(cos, (1, reps)) + partner * jnp.tile(sin_signed, (1, reps))


def _qkv_kernel(x_ref, mod_ref, g_ref, w_ref, qg_ref, kg_ref, hm_ref, cos_ref, sin_ref,
                q_ref, k_ref, v_ref):
    m = mod_ref[0]
    h = _modulate(x_ref[...], g_ref[...], m[0:1], m[1:2]).astype(BF16)
    qkv = _dot(h, w_ref[...])
    nq = N_HEADS * HEAD_DIM
    nk = N_KV_HEADS * HEAD_DIM
    cos = cos_ref[...]
    sin = sin_ref[...]
    hm = hm_ref[...]
    q = _norm_rope(qkv[:, :nq], qg_ref[...], hm, cos, sin)
    k = _norm_rope(qkv[:, nq:nq + nk], kg_ref[...], hm, cos, sin)
    v = qkv[:, nq + nk:]
    q_ref[...] = q.astype(BF16)
    ones = jnp.ones((v.shape[0], HEAD_DIM), F32)
    for g in range(N_KV_HEADS):
        sl = slice(g * HEAD_DIM, (g + 1) * HEAD_DIM)
        k_ref[g] = k[:, sl].astype(F8)
        v_ref[g] = jnp.concatenate([v[:, sl], ones], axis=1).astype(BF16)


def _rope_tables(seq_len):
    rows = seq_len // GRID_W
    r = jnp.broadcast_to(jnp.arange(rows)[:, None], (rows, GRID_W)).reshape(-1)
    c = jnp.broadcast_to(jnp.arange(GRID_W)[None, :], (rows, GRID_W)).reshape(-1)
    inv_freq = ROPE_THETA ** (-jnp.arange(ROPE_FREQS, dtype=F32) / ROPE_FREQS)
    pos = jnp.stack([r, c], axis=-1).astype(F32)
    ang = pos[:, :, None] * inv_freq[None, None, :]
    cos = jnp.cos(ang)
    sin = jnp.sin(ang)
    cos64 = jnp.concatenate([cos, cos], axis=-1).reshape(seq_len, HEAD_DIM)
    sin64 = jnp.concatenate([-sin, sin], axis=-1).reshape(seq_len, HEAD_DIM)
    return jnp.tile(cos64, (1, 2)), jnp.tile(sin64, (1, 2))


def attn_qkv(x, mod, norm_g, wqkv, q_norm, k_norm, seq_len):
    n_rows = x.shape[0]
    tm = ROW_TILE
    nq = N_HEADS * HEAD_DIM
    nk = N_KV_HEADS * HEAD_DIM
    cos, sin = _rope_tables(seq_len)
    qg = jnp.tile(q_norm, N_HEADS)[None, :] * (HEAD_DIM ** -0.5 * math.log2(math.e))
    kg = jnp.tile(k_norm, N_KV_HEADS)[None, :]
    head = np.arange(nq) // HEAD_DIM
    headmean = jnp.asarray((head[:, None] == head[None, :]).astype(np.float32) / HEAD_DIM, BF16)
    per_seq = seq_len // tm
    tab_spec = pl.BlockSpec((tm, LANES), lambda i: (i % per_seq, 0))
    return pl.pallas_call(
        _qkv_kernel,
        grid=(n_rows // tm,),
        in_specs=[
            _row_spec(tm), _mod_spec(tm, seq_len), _const_spec((1, D_MODEL)),
            _const_spec((D_MODEL, nq + 2 * nk)), _const_spec((1, nq)), _const_spec((1, nk)),
            _const_spec((nq, nq)), tab_spec, tab_spec,
        ],
        out_specs=[
            _row_spec(tm, nq),
            pl.BlockSpec((N_KV_HEADS, tm, HEAD_DIM), lambda i: (0, i, 0)),
            pl.BlockSpec((N_KV_HEADS, tm, 2 * HEAD_DIM), lambda i: (0, i, 0)),
        ],
        out_shape=[
            jax.ShapeDtypeStruct((n_rows, nq), BF16),
            jax.ShapeDtypeStruct((N_KV_HEADS, n_rows, HEAD_DIM), F8),
            jax.ShapeDtypeStruct((N_KV_HEADS, n_rows, 2 * HEAD_DIM), BF16),
        ],
        compiler_params=_params(("parallel",)),
        name="attn_qkv",
    )(x, mod, norm_g[None, :], wqkv.astype(BF16), qg, kg, headmean, cos, sin)


def _flash_kernel(q_ref, k_ref, v_ref, o_ref, *, tq, tk, n_chunks):
    q = q_ref[...]
    qs = jnp.concatenate([q[:, j * HEAD_DIM:(j + 1) * HEAD_DIM] for j in range(Q_PER_KV)], axis=0)
    qs = qs.astype(F8)
    rows = Q_PER_KV * tq

    def body(c, carry):
        m, acc = carry
        start = pl.multiple_of(c * tk, tk)
        kc = k_ref[0, pl.ds(start, tk), :]
        vc = v_ref[0, pl.ds(start, tk), :]
        s = lax.dot_general(qs, kc, (((1,), (1,)), ((), ())), preferred_element_type=F32)
        m_new = jnp.maximum(m, jnp.max(s, axis=-1, keepdims=True))
        alpha = jnp.exp2(m - m_new)
        p = jnp.exp2(s - m_new)
        acc = acc * alpha + _dot(p.astype(BF16), vc)
        return m_new, acc

    m0 = jnp.full((rows, 1), -jnp.inf, F32)
    acc0 = jnp.zeros((rows, 2 * HEAD_DIM), F32)
    _, acc = lax.fori_loop(0, n_chunks, body, (m0, acc0))
    o = acc[:, :HEAD_DIM] / acc[:, HEAD_DIM:HEAD_DIM + 1]
    o_ref[...] = jnp.concatenate([o[j * tq:(j + 1) * tq] for j in range(Q_PER_KV)],
                                 axis=1).astype(BF16)


def attn_flash(q, k, v, batch, seq_len):
    n_rows = q.shape[0]
    tq = 256
    tk = min(seq_len, 2048)
    per_seq = seq_len // tq
    width = Q_PER_KV * HEAD_DIM
    kern = functools.partial(_flash_kernel, tq=tq, tk=tk, n_chunks=seq_len // tk)
    return pl.pallas_call(
        kern,
        grid=(batch, N_KV_HEADS, per_seq),
        in_specs=[
            pl.BlockSpec((tq, width), lambda b, g, i: (b * per_seq + i, g)),
            pl.BlockSpec((1, seq_len, HEAD_DIM), lambda b, g, i: (g, b, 0)),
            pl.BlockSpec((1, seq_len, 2 * HEAD_DIM), lambda b, g, i: (g, b, 0)),
        ],
        out_specs=pl.BlockSpec((tq, width), lambda b, g, i: (b * per_seq + i, g)),
        out_shape=jax.ShapeDtypeStruct((n_rows, N_HEADS * HEAD_DIM), BF16),
        compiler_params=_params(("parallel", "parallel", "parallel")),
        name="attn_flash",
    )(q, k, v)


def _proj_res_kernel(y_ref, w_ref, x_ref, mod_ref, *refs, gate_row):
    route_in, o_ref, route_out = refs[:N_ROUTE_IN], refs[N_ROUTE_IN], refs[N_ROUTE_IN + 1:]
    m = mod_ref[0]
    x1 = x_ref[...] + m[gate_row:gate_row + 1] * _dot(y_ref[...], w_ref[...])
    o_ref[...] = x1
    _route_tile(x1, m, *route_in, *route_out)


def proj_residual(y, w, x, mod, gate_row, seq_len, route):
    n_rows = x.shape[0]
    tm = STREAM_TILE
    r_ops, r_in, r_out, r_shape = _router_io(route, tm, n_rows)
    return pl.pallas_call(
        functools.partial(_proj_res_kernel, gate_row=gate_row),
        grid=(n_rows // tm,),
        in_specs=[_row_spec(tm, y.shape[1]), _const_spec(w.shape), _row_spec(tm),
                  _mod_spec(tm, seq_len)] + r_in,
        out_specs=[_row_spec(tm)] + r_out,
        out_shape=[jax.ShapeDtypeStruct((n_rows, D_MODEL), F32)] + r_shape,
        compiler_params=_params(("arbitrary",)),
        name="proj_residual",
    )(y, w.astype(BF16), x, mod, *r_ops)


CONV_COLS = 256


def _hy_in_kernel(*refs, tm, seq_len, n_stream):
    stream, refs = refs[:n_stream], refs[n_stream:]
    mod_ref, g_ref, w_ref, cw_ref, cb_ref, z_ref, x0_ref = refs[:7]
    stage_refs = refs[-3:]
    first, last = _edge_flags(tm, seq_len)
    m = mod_ref[0]
    x, ext = _stream_rows(stream)
    if n_stream == N_PENDING_REFS:
        refs[7][...] = x
    h = _modulate(ext, g_ref[...], m[0:1], m[1:2]).astype(BF16)
    tn = CONV_COLS
    for j in range(D_MODEL // tn):
        part = []
        for s in range(3):
            cols = slice(s * D_MODEL + j * tn, s * D_MODEL + (j + 1) * tn)
            part.append(_conv3(_dot(h, w_ref[:, cols]), cw_ref, cb_ref, cols, tm, first, last,
                               stage_refs[s]))
        out_cols = slice(j * tn, (j + 1) * tn)
        x0_ref[:, out_cols] = part[0].astype(BF16)
        z_ref[:, out_cols] = (part[2] * part[1]).astype(BF16)


def hyena_in(src, mod, norm_g, w_in, conv_w, conv_b, seq_len):
    pending = isinstance(src, _Pending)
    n_rows = (src.x if pending else src).shape[0]
    tm = ROW_TILE
    s_ops, s_specs = _stream_io(src, tm, n_rows, seq_len)
    x_out = [jax.ShapeDtypeStruct((n_rows, D_MODEL), F32)] if pending else []
    return pl.pallas_call(
        functools.partial(_hy_in_kernel, tm=tm, seq_len=seq_len, n_stream=len(s_ops)),
        grid=(n_rows // tm,),
        in_specs=s_specs + [_mod_spec(tm, seq_len), _const_spec((1, D_MODEL)),
                            _const_spec((D_MODEL, 3 * D_MODEL)), _const_spec((3, 3 * D_MODEL)),
                            _const_spec((1, 3 * D_MODEL))],
        out_specs=[_row_spec(tm)] * (2 + len(x_out)),
        out_shape=[jax.ShapeDtypeStruct((n_rows, D_MODEL), BF16),
                   jax.ShapeDtypeStruct((n_rows, D_MODEL), BF16)] + x_out,
        scratch_shapes=[pltpu.VMEM((tm + 2 * HALO, CONV_COLS), F32)] * 3,
        compiler_params=_params(("parallel",)),
        name="hyena_in",
    )(*s_ops, mod, norm_g[None, :], w_in.astype(BF16), conv_w, conv_b[None, :])


def _hy_filter_kernel(feat_ref, w1_ref, b1_ref, w2_ref, b2_ref, w3_ref, fr_ref, dl_ref, o_ref):
    feat = feat_ref[...]
    fr = fr_ref[...]
    a = jnp.sin(fr * (_dot3(feat, w1_ref[...]) + b1_ref[...]))
    a = jnp.sin(fr * (_dot3(a, w2_ref[...]) + b2_ref[...]))
    hf = _dot3(a, w3_ref[...])
    decay = jnp.exp(-feat[:, 0:1] * dl_ref[...])
    o_ref[0] = hf[:, :D_MODEL] * decay
    o_ref[1] = hf[:, D_MODEL:] * decay


def _pad_to(a, rows, cols):
    return jnp.pad(a.astype(F32), ((0, rows - a.shape[0]), (0, cols - a.shape[1])))


def hyena_filter(seq_len, w1, b1, w2, b2, w3, freq):
    t = jnp.linspace(0.0, 1.0, seq_len, dtype=F32)[:, None]
    w = 2.0 * math.pi * jnp.arange(seq_len, dtype=F32)[:, None] / seq_len
    f = jnp.linspace(1e-4, HY_BANDS - 1, HY_BANDS, dtype=F32)[None, :]
    feat = _pad_to(jnp.concatenate([t, jnp.cos(f * w), -jnp.sin(f * w)], axis=-1), seq_len, LANES)
    max_decay = math.log(HY_TARGET) / HY_FAST_DECAY
    min_decay = math.log(HY_TARGET) / HY_SLOW_DECAY
    absdelta = jnp.abs(jnp.linspace(min_decay, max_decay, D_MODEL, dtype=F32))[None, :]
    tl = 512
    return pl.pallas_call(
        _hy_filter_kernel,
        grid=(seq_len // tl,),
        in_specs=[_row_spec(tl, LANES), _const_spec((LANES, LANES)), _const_spec((1, LANES)),
                  _const_spec((LANES, LANES)), _const_spec((1, LANES)),
                  _const_spec((LANES, 2 * D_MODEL)), _const_spec((1, LANES)),
                  _const_spec((1, D_MODEL))],
        out_specs=pl.BlockSpec((2, tl, D_MODEL), lambda i: (0, i, 0)),
        out_shape=jax.ShapeDtypeStruct((2, seq_len, D_MODEL), F32),
        compiler_params=_params(("parallel",)),
        name="hyena_filter",
    )(feat, _pad_to(w1, LANES, LANES), _pad_to(b1[None, :], 1, LANES), _pad_to(w2, LANES, LANES),
      _pad_to(b2[None, :], 1, LANES), _pad_to(w3, LANES, 2 * D_MODEL),
      _pad_to(freq[None, :], 1, LANES), absdelta)


class _FFTPlan:
    def __init__(self, seq_len):
        self.n = 2 * seq_len
        self.n1 = self.n // DFT_N2
        self.r = self.n1 // 2
        self.k1n = self.n1 // 2 + 1
        self.kron = 1 if self.r >= FFT_PLAIN_ROWS else max(2 * HALO, LANES // self.r)
        ang = 2.0 * np.pi * np.outer(np.arange(self.k1n), np.arange(self.r)) / self.n1
        eye = np.eye(self.kron)
        self.fwd_cos = jnp.asarray(np.kron(np.cos(ang), eye), BF16)
        self.fwd_sin = jnp.asarray(np.kron(-np.sin(ang), eye), BF16)
        wgt = np.full((self.k1n,), 2.0)
        wgt[0] = wgt[-1] = 1.0
        scale = (wgt / self.n)[None, :]
        self.inv_cos = jnp.asarray(np.kron(np.cos(ang).T * scale, eye), BF16)
        self.inv_sin = jnp.asarray(np.kron(-np.sin(ang).T * scale, eye), BF16)
        a2 = 2.0 * np.pi * np.outer(np.arange(DFT_N2), np.arange(DFT_N2)) / DFT_N2
        self.f_cos = jnp.asarray(np.cos(a2), F32)
        self.f_sin = jnp.asarray(-np.sin(a2), F32)
        tw = 2.0 * np.pi * np.outer(np.arange(self.k1n), np.arange(DFT_N2)) / self.n
        self.tw_cos = jnp.asarray(np.cos(tw), F32)
        self.tw_sin = jnp.asarray(-np.sin(tw), F32)


def _fft_a_kernel(z_ref, wc_ref, ws_ref, ar_ref, ai_ref, *, rq):
    shape = ar_ref.shape[1:2] + ar_ref.shape[3:]
    for t in range(z_ref.shape[2]):
        z = z_ref[0, :, t].reshape(rq, D_MODEL).astype(BF16)
        ar_ref[0, :, t] = _dot(wc_ref[...], z).reshape(shape).astype(BF16)
        ai_ref[0, :, t] = _dot(ws_ref[...], z).reshape(shape).astype(BF16)


def _fft_group(plan):
    nhi = DFT_N2 // plan.kron
    per_group = plan.r * plan.kron * D_MODEL * 4
    return max(1, min(nhi, (2 << 20) // per_group))


FFT_PLAIN_ROWS = 64
FFT_PLAIN_COLS = 16384


def _fft_a_plain_kernel(z_ref, wc_ref, ws_ref, ar_ref, ai_ref):
    z = z_ref[0].astype(BF16)
    ar_ref[0] = _dot(wc_ref[...], z).astype(BF16)
    ai_ref[0] = _dot(ws_ref[...], z).astype(BF16)


def _fft_stage_a_plain(z, plan, batch):
    cols = DFT_N2 * D_MODEL
    tc = FFT_PLAIN_COLS
    out = jax.ShapeDtypeStruct((batch, plan.k1n, cols), BF16)
    ospec = pl.BlockSpec((1, plan.k1n, tc), lambda b, h: (b, 0, h))
    wspec = pl.BlockSpec(plan.fwd_cos.shape, lambda b, h: (0, 0))
    ar, ai = pl.pallas_call(
        _fft_a_plain_kernel,
        grid=(batch, cols // tc),
        in_specs=[pl.BlockSpec((1, plan.r, tc), lambda b, h: (b, 0, h)), wspec, wspec],
        out_specs=[ospec, ospec],
        out_shape=[out, out],
        compiler_params=_params(("parallel", "parallel")),
        name="fft_stage_a",
    )(z.reshape(batch, plan.r, cols), plan.fwd_cos, plan.fwd_sin)
    shape = (batch, plan.k1n, DFT_N2, D_MODEL)
    return ar.reshape(shape), ai.reshape(shape)


def fft_stage_a(z, plan, batch):
    if plan.kron == 1:
        return _fft_stage_a_plain(z, plan, batch)
    q = plan.kron
    nhi = DFT_N2 // q
    hb = _fft_group(plan)
    zv = z.reshape(batch, plan.r, nhi, q, D_MODEL)
    out = jax.ShapeDtypeStruct((batch, plan.k1n, nhi, q, D_MODEL), BF16)
    ospec = pl.BlockSpec((1, plan.k1n, hb, q, D_MODEL), lambda b, h: (b, 0, h, 0, 0))
    wspec = pl.BlockSpec(plan.fwd_cos.shape, lambda b, h: (0, 0))
    ar, ai = pl.pallas_call(
        functools.partial(_fft_a_kernel, rq=plan.r * q),
        grid=(batch, nhi // hb),
        in_specs=[pl.BlockSpec((1, plan.r, hb, q, D_MODEL), lambda b, h: (b, 0, h, 0, 0)),
                  wspec, wspec],
        out_specs=[ospec, ospec],
        out_shape=[out, out],
        compiler_params=_params(("parallel", "parallel")),
        name="fft_stage_a",
    )(zv, plan.fwd_cos, plan.fwd_sin)
    shape = (batch, plan.k1n, DFT_N2, D_MODEL)
    return ar.reshape(shape), ai.reshape(shape)


def _twiddled_dft(fr, fi, tr, ti):
    return (fr * tr - fi * ti).astype(BF16), (fr * ti + fi * tr).astype(BF16)


def _fft_b_fwd_kernel(ar_ref, ai_ref, fr_ref, fi_ref, twr_ref, twi_ref, br_ref, bi_ref,
                      gr_ref, gi_ref):
    @pl.when(pl.program_id(1) == 0)
    def _():
        gr, gi = _twiddled_dft(fr_ref[...], fi_ref[...], twr_ref[0], twi_ref[0])
        gr_ref[...] = gr
        gi_ref[...] = gi

    gr = gr_ref[...]
    gi = gi_ref[...]
    ar = ar_ref[0, 0].astype(BF16)
    ai = ai_ref[0, 0].astype(BF16)
    br_ref[0, 0] = _dot(gr, ar) - _dot(gi, ai)
    bi_ref[0, 0] = _dot(gr, ai) + _dot(gi, ar)


def fft_stage_b_fwd(ar, ai, plan):
    batch = ar.shape[0]
    blk = pl.BlockSpec((1, 1, DFT_N2, D_MODEL), lambda k, b: (b, k, 0, 0))
    cst = pl.BlockSpec((DFT_N2, DFT_N2), lambda k, b: (0, 0))
    tws = pl.BlockSpec((1, 1, DFT_N2), lambda k, b: (k, 0, 0))
    out = jax.ShapeDtypeStruct(ar.shape, F32)
    return pl.pallas_call(
        _fft_b_fwd_kernel,
        grid=(plan.k1n, batch),
        in_specs=[blk, blk, cst, cst, tws, tws],
        out_specs=[blk, blk],
        out_shape=[out, out],
        scratch_shapes=[pltpu.VMEM((DFT_N2, DFT_N2), BF16)] * 2,
        compiler_params=_params(("parallel", "arbitrary")),
        name="fft_stage_b_fwd",
    )(ar, ai, plan.f_cos, plan.f_sin, plan.tw_cos[:, None, :], plan.tw_sin[:, None, :])


def _fft_b_conv_kernel(ar_ref, ai_ref, hr_ref, hi_ref, hb0_ref, fr_ref, fi_ref,
                       twr_ref, twi_ref, tcr_ref, tci_ref, cr_ref, ci_ref,
                       fwd_ref, inv_ref, kr_ref, ki_ref):
    n = DFT_N2

    @pl.when(pl.program_id(1) == 0)
    def _():
        fr = fr_ref[...]
        fi = fi_ref[...]
        gr, gi = _twiddled_dft(fr, fi, twr_ref[0], twi_ref[0])
        fwd_ref[:n, :n] = gr
        fwd_ref[:n, n:] = -gi
        fwd_ref[n:, :n] = gi
        fwd_ref[n:, n:] = gr
        gtr, gti = _twiddled_dft(fr, fi, tcr_ref[0], tci_ref[0])
        inv_ref[:n, :n] = gtr
        inv_ref[:n, n:] = gti
        inv_ref[n:, :n] = -gti
        inv_ref[n:, n:] = gtr
        kr_ref[...] = hr_ref[0, 0] + hr_ref[1, 0] - hb0_ref[...]
        ki_ref[...] = hi_ref[0, 0] - hi_ref[1, 0]

    a = jnp.concatenate([ar_ref[0, 0], ai_ref[0, 0]], axis=0)
    b = _dot(fwd_ref[...], a)
    br, bi = b[:n], b[n:]
    kr = kr_ref[...]
    ki = ki_ref[...]
    p = jnp.concatenate([(br * kr - bi * ki).astype(BF16), (br * ki + bi * kr).astype(BF16)], axis=0)
    c = _dot(inv_ref[...], p)
    cr_ref[0, 0] = c[:n].astype(BF16)
    ci_ref[0, 0] = c[n:].astype(BF16)


def fft_stage_b_conv(ar, ai, hr, hi, hb0, plan):
    batch = ar.shape[0]
    blk = pl.BlockSpec((1, 1, DFT_N2, D_MODEL), lambda k, b: (b, k, 0, 0))
    hblk = pl.BlockSpec((2, 1, DFT_N2, D_MODEL), lambda k, b: (0, k, 0, 0))
    cst = pl.BlockSpec((DFT_N2, DFT_N2), lambda k, b: (0, 0))
    tws = pl.BlockSpec((1, 1, DFT_N2), lambda k, b: (k, 0, 0))
    twc = pl.BlockSpec((1, DFT_N2, 1), lambda k, b: (k, 0, 0))
    out = jax.ShapeDtypeStruct(ar.shape, BF16)
    return pl.pallas_call(
        _fft_b_conv_kernel,
        grid=(plan.k1n, batch),
        in_specs=[blk, blk, hblk, hblk, pl.BlockSpec((1, D_MODEL), lambda k, b: (0, 0)),
                  cst, cst, tws, tws, twc, twc],
        out_specs=[blk, blk],
        out_shape=[out, out],
        scratch_shapes=[pltpu.VMEM((2 * DFT_N2, 2 * DFT_N2), BF16)] * 2
        + [pltpu.VMEM((DFT_N2, D_MODEL), F32)] * 2,
        compiler_params=_params(("parallel", "arbitrary")),
        name="fft_stage_b_conv",
    )(ar, ai, hr, hi, hb0, plan.f_cos, plan.f_sin,
      plan.tw_cos[:, None, :], plan.tw_sin[:, None, :],
      plan.tw_cos[:, :, None], plan.tw_sin[:, :, None])


def _fft_a_inv_kernel(cr_ref, ci_ref, vc_ref, vs_ref, z_ref, x0_ref, skip_ref, y_ref, *, kq):
    shape = z_ref.shape[1:2] + z_ref.shape[3:]
    for t in range(z_ref.shape[2]):
        cr = cr_ref[0, :, t].reshape(kq, D_MODEL).astype(BF16)
        ci = ci_ref[0, :, t].reshape(kq, D_MODEL).astype(BF16)
        conv = _dot(vc_ref[...], cr) + _dot(vs_ref[...], ci)
        y = conv.reshape(shape) + z_ref[0, :, t].astype(F32) * skip_ref[...]
        y_ref[0, :, t] = (y * x0_ref[0, :, t].astype(F32)).astype(BF16)


def _fft_a_inv_plain_kernel(cr_ref, ci_ref, vc_ref, vs_ref, z_ref, x0_ref, skip_ref, y_ref):
    conv = _dot(vc_ref[...], cr_ref[0]) + _dot(vs_ref[...], ci_ref[0])
    y = conv + z_ref[0].astype(F32) * skip_ref[...]
    y_ref[0] = (y * x0_ref[0].astype(F32)).astype(BF16)


def _fft_stage_a_inv_plain(cr, ci, z, x0, skip, plan, batch):
    cols = DFT_N2 * D_MODEL
    tc = FFT_PLAIN_COLS
    cspec = pl.BlockSpec((1, plan.k1n, tc), lambda b, h: (b, 0, h))
    tspec = pl.BlockSpec((1, plan.r, tc), lambda b, h: (b, 0, h))
    wspec = pl.BlockSpec(plan.inv_cos.shape, lambda b, h: (0, 0))
    y = pl.pallas_call(
        _fft_a_inv_plain_kernel,
        grid=(batch, cols // tc),
        in_specs=[cspec, cspec, wspec, wspec, tspec, tspec,
                  pl.BlockSpec((1, tc), lambda b, h: (0, 0))],
        out_specs=tspec,
        out_shape=jax.ShapeDtypeStruct((batch, plan.r, cols), BF16),
        compiler_params=_params(("parallel", "parallel")),
        name="fft_stage_a_inv",
    )(cr.reshape(batch, plan.k1n, cols), ci.reshape(batch, plan.k1n, cols), plan.inv_cos,
      plan.inv_sin, z.reshape(batch, plan.r, cols), x0.reshape(batch, plan.r, cols),
      jnp.tile(skip, tc // D_MODEL)[None, :])
    return y.reshape(z.shape)


def fft_stage_a_inv(cr, ci, z, x0, skip, plan, batch):
    if plan.kron == 1:
        return _fft_stage_a_inv_plain(cr, ci, z, x0, skip, plan, batch)
    q = plan.kron
    nhi = DFT_N2 // q
    hb = _fft_group(plan)
    cshape = (batch, plan.k1n, nhi, q, D_MODEL)
    tshape = (batch, plan.r, nhi, q, D_MODEL)
    cspec = pl.BlockSpec((1, plan.k1n, hb, q, D_MODEL), lambda b, h: (b, 0, h, 0, 0))
    tspec = pl.BlockSpec((1, plan.r, hb, q, D_MODEL), lambda b, h: (b, 0, h, 0, 0))
    wspec = pl.BlockSpec(plan.inv_cos.shape, lambda b, h: (0, 0))
    y = pl.pallas_call(
        functools.partial(_fft_a_inv_kernel, kq=plan.k1n * q),
        grid=(batch, nhi // hb),
        in_specs=[cspec, cspec, wspec, wspec, tspec, tspec,
                  pl.BlockSpec((1, D_MODEL), lambda b, h: (0, 0))],
        out_specs=tspec,
        out_shape=jax.ShapeDtypeStruct(tshape, BF16),
        compiler_params=_params(("parallel", "parallel")),
        name="fft_stage_a_inv",
    )(cr.reshape(cshape), ci.reshape(cshape), plan.inv_cos, plan.inv_sin,
      z.reshape(tshape), x0.reshape(tshape), skip[None, :])
    return y.reshape(z.shape)


def hyena_mixer(src, mod, norm_g, p, batch, seq_len, route):
    z, x0, *rest = hyena_in(src, mod, norm_g, p["w_in"], p["conv_w"], p["conv_b"], seq_len)
    x = rest[0] if rest else src
    plan = _FFTPlan(seq_len)
    filt = hyena_filter(seq_len, p["w1"], p["b1"], p["w2"], p["b2"], p["w3"], p["freq"])
    fr, fi = fft_stage_a(filt.reshape(2 * seq_len, D_MODEL), plan, 2)
    hr, hi = fft_stage_b_fwd(fr, fi, plan)
    ar, ai = fft_stage_a(z, plan, batch)
    cr, ci = fft_stage_b_conv(ar, ai, hr, hi, filt[1, 0:1, :], plan)
    y = fft_stage_a_inv(cr, ci, z, x0, p["skip"], plan, batch)
    return proj_residual(y, p["w_out"], x, mod, 2, seq_len, route)


def _pool_kernel(*refs, tm, seq_len, n_stream):
    stream, refs = refs[:n_stream], refs[n_stream:]
    mod_ref, g_ref, w_ref, s_ref = refs[:4]
    refs = refs[4:]
    route_in, o_ref, route_out = refs[:N_ROUTE_IN], refs[N_ROUTE_IN], refs[N_ROUTE_IN + 1:]
    first, last = _edge_flags(tm, seq_len)
    m = mod_ref[0]
    x, ext = _stream_rows(stream)
    h = _modulate(ext, g_ref[...], m[0:1], m[1:2])
    n = tm + 2 * HALO
    row = lax.broadcasted_iota(jnp.int32, (n, 1), 0)
    outside = jnp.logical_or(jnp.logical_and(first, row < HALO),
                             jnp.logical_and(last, row >= HALO + tm))
    h = jnp.where(outside, 0.0, h)
    pos = (pl.program_id(0) * tm) % seq_len + lax.broadcasted_iota(jnp.int32, (tm, 1), 0)
    ys = []
    for gi, win in enumerate(POOL_WINDOWS):
        cols = slice(gi * POOL_GROUP_DIM, (gi + 1) * POOL_GROUP_DIM)
        hg = h[:, cols]
        acc = hg
        span = 1
        while span < win:
            acc = acc + pltpu.roll(acc, span, axis=0)
            span *= 2
        lead = win // 2 - 1
        if lead:
            acc = pltpu.roll(acc, n - lead, axis=0)
        half = win // 2
        cnt = jnp.minimum(pos + half, seq_len) - jnp.maximum(pos - half, 0)
        pooled = acc[HALO:HALO + tm] / cnt.astype(F32) - hg[HALO:HALO + tm]
        ys.append(_dot(pooled.astype(BF16), w_ref[gi]))
    y = jnp.concatenate(ys, axis=1) * s_ref[...]
    x1 = x + m[2:3] * y
    o_ref[...] = x1
    _route_tile(x1, m, *route_in, *route_out)


def pool_mixer(src, mod, norm_g, w_group, scale, seq_len, route):
    n_rows = (src.x if isinstance(src, _Pending) else src).shape[0]
    tm = ROW_TILE
    s_ops, s_specs = _stream_io(src, tm, n_rows, seq_len)
    r_ops, r_in, r_out, r_shape = _router_io(route, tm, n_rows)
    return pl.pallas_call(
        functools.partial(_pool_kernel, tm=tm, seq_len=seq_len, n_stream=len(s_ops)),
        grid=(n_rows // tm,),
        in_specs=s_specs + [_mod_spec(tm, seq_len), _const_spec((1, D_MODEL)),
                            _const_spec(w_group.shape), _const_spec((1, D_MODEL))] + r_in,
        out_specs=[_row_spec(tm)] + r_out,
        out_shape=[jax.ShapeDtypeStruct((n_rows, D_MODEL), F32)] + r_shape,
        compiler_params=_params(("arbitrary",)),
        name="pool_mixer",
    )(*s_ops, mod, norm_g[None, :], w_group.astype(BF16), scale[None, :], *r_ops)


def _sconv_kernel(*refs, tm, seq_len, n_stream):
    stream, refs = refs[:n_stream], refs[n_stream:]
    mod_ref, g_ref, w_ref, cw_ref, cb_ref, wo_ref = refs[:6]
    refs = refs[6:]
    route_in, o_ref = refs[:N_ROUTE_IN], refs[N_ROUTE_IN]
    route_out, y_ref = refs[N_ROUTE_IN + 1:N_ROUTE_IN + 1 + N_ROUTE_OUT], refs[-1]
    first, last = _edge_flags(tm, seq_len)
    m = mod_ref[0]
    x, ext = _stream_rows(stream)
    h = _modulate(ext, g_ref[...], m[0:1], m[1:2]).astype(BF16)
    tn = 256
    for j in range(D_MODEL // tn):
        cols = slice(j * tn, (j + 1) * tn)
        bg = _dot(h, w_ref[:, cols])[HALO:HALO + tm]
        cg = _dot(h, w_ref[:, D_MODEL + j * tn:D_MODEL + (j + 1) * tn])
        hp = _dot(h, w_ref[:, 2 * D_MODEL + j * tn:2 * D_MODEL + (j + 1) * tn])
        y_ref[:, cols] = (bg * _conv3(cg * hp, cw_ref, cb_ref, cols, tm, first, last)).astype(BF16)
    x1 = x + m[2:3] * _dot(y_ref[...], wo_ref[...])
    o_ref[...] = x1
    _route_tile(x1, m, *route_in, *route_out)


def sconv_mixer(src, mod, norm_g, w_in, conv_w, conv_b, w_out, seq_len, route):
    n_rows = (src.x if isinstance(src, _Pending) else src).shape[0]
    tm = ROW_TILE
    s_ops, s_specs = _stream_io(src, tm, n_rows, seq_len)
    r_ops, r_in, r_out, r_shape = _router_io(route, tm, n_rows)
    return pl.pallas_call(
        functools.partial(_sconv_kernel, tm=tm, seq_len=seq_len, n_stream=len(s_ops)),
        grid=(n_rows // tm,),
        in_specs=s_specs + [_mod_spec(tm, seq_len), _const_spec((1, D_MODEL)),
                            _const_spec((D_MODEL, 3 * D_MODEL)), _const_spec((3, D_MODEL)),
                            _const_spec((1, D_MODEL)), _const_spec((D_MODEL, D_MODEL))] + r_in,
        out_specs=[_row_spec(tm)] + r_out,
        out_shape=[jax.ShapeDtypeStruct((n_rows, D_MODEL), F32)] + r_shape,
        scratch_shapes=[pltpu.VMEM((tm, D_MODEL), BF16)],
        compiler_params=_params(("arbitrary",)),
        name="sconv_mixer",
    )(*s_ops, mod, norm_g[None, :], w_in.astype(BF16), conv_w, conv_b[None, :],
      w_out.astype(BF16), *r_ops)


def _pack_bf16(x):
    w = x.shape[1] // 2
    bits = pltpu.bitcast(x.astype(BF16).astype(F32), jnp.uint32)
    return (bits[:, :w] >> 16) | bits[:, w:]


def _unpack_bf16(p):
    lo = pltpu.bitcast(p << 16, F32)
    hi = pltpu.bitcast(p & jnp.uint32(0xFFFF0000), F32)
    return jnp.concatenate([lo, hi], axis=1)


META_E1, META_E2, META_W1, META_W2, META_R1, META_R2 = range(6)
META_ROWS = 8


def _route_tile(x, m, g_ref, wh_ref, b_ref, tri_ref, h_ref, meta_ref, meta_t_ref, cnt_ref):
    @pl.when(pl.program_id(0) == 0)
    def _():
        cnt_ref[...] = jnp.zeros_like(cnt_ref)

    ms = jnp.mean(x * x, axis=-1, keepdims=True)
    h = x * lax.rsqrt(ms + EPS) * (g_ref[...] * (1.0 + m[4:5])) + m[3:4]
    hi = h.astype(BF16)
    hi32 = hi.astype(F32)
    lo = (h - hi32).astype(BF16)
    half = D_MODEL // 2
    bits = pltpu.bitcast(hi32, jnp.uint32)
    h_ref[...] = (bits[:, :half] >> 16) | bits[:, half:]
    part = _dot(hi, wh_ref[...])
    lg = part[:, :LANES] + (part[:, LANES:] + _dot(lo, wh_ref[:, :LANES])) + b_ref[...]
    lgt = lg.T
    tm = lgt.shape[1]
    neg = -jnp.inf
    sub = lax.broadcasted_iota(jnp.int32, (HALO, tm), 0).astype(F32)

    def first_argmax(vals):
        top = jnp.max(vals, axis=0, keepdims=True)
        idx = jnp.min(jnp.where(vals == top, sub, float(HALO)), axis=0, keepdims=True)
        return top, idx

    gl = jnp.where(sub < MOE_GROUPS, lgt[MOE_N_EXPERTS:MOE_N_EXPERTS + HALO], neg)
    gmax, grp = first_argmax(gl)
    g_w = 1.0 / jnp.sum(jnp.exp(gl - gmax), axis=0, keepdims=True)
    el = lgt[:MOE_EXPERTS_PER_GROUP]
    for g in range(1, MOE_GROUPS):
        el = jnp.where(grp == float(g),
                       lgt[g * MOE_EXPERTS_PER_GROUP:(g + 1) * MOE_EXPERTS_PER_GROUP], el)
    v1, i1 = first_argmax(el)
    v2, i2 = first_argmax(jnp.where(sub == i1, neg, el))
    ex = jnp.exp(v2 - v1)
    w1 = 1.0 / (1.0 + ex)
    w2 = ex * w1
    e1 = grp * MOE_EXPERTS_PER_GROUP + i1
    e2 = grp * MOE_EXPERTS_PER_GROUP + i2
    expert = lax.broadcasted_iota(jnp.int32, (MOE_N_EXPERTS, tm), 0).astype(F32)
    onehot = jnp.where(jnp.logical_or(expert == e1, expert == e2), 1.0, 0.0)
    before = _dot(onehot.astype(BF16), tri_ref[...]) + cnt_ref[...]
    cnt_ref[...] += jnp.sum(onehot, axis=1, keepdims=True)
    r1 = jnp.sum(jnp.where(expert == e1, before, 0.0), axis=0, keepdims=True)
    r2 = jnp.sum(jnp.where(expert == e2, before, 0.0), axis=0, keepdims=True)
    fields = ((META_E1, e1), (META_E2, e2), (META_W1, w1 * g_w), (META_W2, w2 * g_w),
              (META_R1, r1), (META_R2, r2))
    field = lax.broadcasted_iota(jnp.int32, (LANES, tm), 0)
    meta_t = jnp.zeros((LANES, tm), F32)
    for row, val in fields:
        meta_t = jnp.where(field == row, val, meta_t)
    meta_t_ref[...] = meta_t[:META_ROWS]
    meta_ref[...] = meta_t.T


N_ROUTE_IN = 4
N_ROUTE_OUT = 4


def _router_io(route, tm, n_rows):
    norm_g, w_group, b_group, w_router, b_router = route
    w = _pad_to(jnp.concatenate([w_router, w_group], axis=1), D_MODEL, LANES)
    wh = w.astype(BF16)
    wl = (w - wh.astype(F32)).astype(BF16)
    whl = jnp.concatenate([wh, wl], axis=1)
    b = _pad_to(jnp.concatenate([b_router, b_group])[None, :], 1, LANES)
    tri = jnp.asarray(np.triu(np.ones((tm, tm), np.float32), 1), BF16)
    operands = (norm_g[None, :], whl, b, tri)
    in_specs = [_const_spec((1, D_MODEL)), _const_spec((D_MODEL, 2 * LANES)),
                _const_spec((1, LANES)), _const_spec((tm, tm))]
    out_specs = [_row_spec(tm, D_MODEL // 2), _row_spec(tm, LANES),
                 pl.BlockSpec((META_ROWS, tm), lambda i: (0, i)),
                 _const_spec((MOE_N_EXPERTS, 1))]
    out_shape = [jax.ShapeDtypeStruct((n_rows, D_MODEL // 2), jnp.uint32),
                 jax.ShapeDtypeStruct((n_rows, LANES), F32),
                 jax.ShapeDtypeStruct((META_ROWS, n_rows), F32),
                 jax.ShapeDtypeStruct((MOE_N_EXPERTS, 1), F32)]
    return operands, in_specs, out_specs, out_shape


def _slot_kernel(offs_ref, meta_t_ref, pos_ref):
    meta = meta_t_ref[...]
    start = jnp.zeros_like(meta)
    for e in range(MOE_N_EXPERTS):
        start = jnp.where(meta == float(e), offs_ref[e], start)
    shift = META_ROWS - (META_R1 - META_E1)
    pos_ref[...] = (start + pltpu.roll(meta, shift, axis=0)).astype(jnp.int32)


def moe_slots(meta_t, offsets):
    n_rows = meta_t.shape[1]
    tn = min(n_rows, 8192)
    blk = pl.BlockSpec((META_ROWS, tn), lambda i, offs: (0, i))
    return pl.pallas_call(
        _slot_kernel,
        grid_spec=pltpu.PrefetchScalarGridSpec(
            num_scalar_prefetch=1, grid=(n_rows // tn,), in_specs=[blk], out_specs=blk),
        out_shape=jax.ShapeDtypeStruct((META_ROWS, n_rows), jnp.int32),
        compiler_params=_params(("parallel",)),
        name="moe_slots",
    )(offsets, meta_t)


def _expert_kernel(te_ref, nu_ref, nv_ref, xs_ref, wg_ref, wu_ref, wd_ref, o_ref,
                   wgu_ref, wdb_ref):
    j = pl.program_id(0)

    @pl.when(jnp.logical_or(j == 0, te_ref[j] != te_ref[jnp.maximum(j - 1, 0)]))
    def _():
        wgu_ref[:, :MOE_D_FF] = wg_ref[0, 0].astype(BF16)
        wgu_ref[:, MOE_D_FF:] = wu_ref[0, 0].astype(BF16)
        wdb_ref[...] = wd_ref[0, 0].astype(BF16)

    @pl.when(j < nu_ref[0])
    def _():
        x = _unpack_bf16(xs_ref[...])
        row = lax.broadcasted_iota(jnp.int32, (x.shape[0], 1), 0)
        x = jnp.where(row < nv_ref[j], x, 0.0).astype(BF16)
        au = _dot(x, wgu_ref[...])
        hh = (_silu(au[:, :MOE_D_FF]) * au[:, MOE_D_FF:]).astype(BF16)
        o_ref[...] = _pack_bf16(_dot(hh, wdb_ref[...]))


def moe_experts(xs, tile_expert, n_used, n_valid, w_gate, w_up, w_down, layer, tm):
    n_slots, half = xs.shape
    wspec = lambda shape: pl.BlockSpec((1, 1) + shape, lambda j, te, nu, nv: (layer, te[j], 0, 0))
    row = pl.BlockSpec((tm, half), lambda j, te, nu, nv: (j, 0))
    grid_spec = pltpu.PrefetchScalarGridSpec(
        num_scalar_prefetch=3,
        grid=(n_slots // tm,),
        in_specs=[row, wspec((D_MODEL, MOE_D_FF)), wspec((D_MODEL, MOE_D_FF)),
                  wspec((MOE_D_FF, D_MODEL))],
        out_specs=row,
        scratch_shapes=[pltpu.VMEM((D_MODEL, 2 * MOE_D_FF), BF16),
                        pltpu.VMEM((MOE_D_FF, D_MODEL), BF16)],
    )
    return pl.pallas_call(
        _expert_kernel,
        grid_spec=grid_spec,
        out_shape=jax.ShapeDtypeStruct((n_slots, half), jnp.uint32),
        compiler_params=_params(("arbitrary",)),
        name="moe_experts",
    )(tile_expert, n_used, n_valid, xs, w_gate, w_up, w_down)


def _combine_kernel(x_ref, ya_ref, yb_ref, meta_ref, mod_ref, o_ref):
    o_ref[...] = _moe_combined(x_ref, ya_ref, yb_ref, meta_ref, mod_ref[0][5:6])


def moe_combine(pending, seq_len):
    x, ya, yb, meta, mod = pending
    n_rows = x.shape[0]
    tm = STREAM_TILE
    half = D_MODEL // 2
    return pl.pallas_call(
        _combine_kernel,
        grid=(n_rows // tm,),
        in_specs=[_row_spec(tm), _row_spec(tm, half), _row_spec(tm, half), _row_spec(tm, LANES),
                  _mod_spec(tm, seq_len)],
        out_specs=_row_spec(tm),
        out_shape=jax.ShapeDtypeStruct((n_rows, D_MODEL), F32),
        compiler_params=_params(("parallel",)),
        name="moe_combine",
    )(x, ya, yb, meta, mod)


MOE_TILE = 512
MOE_TILE_LARGE = 1024
MOE_LARGE_ROWS = 32768


def hier_moe(routed, mod, w_gate, w_up, w_down, layer, seq_len):
    x, hp, meta, meta_t, counts = routed
    n_rows = x.shape[0]
    tm = MOE_TILE_LARGE if n_rows >= MOE_LARGE_ROWS else MOE_TILE
    cnt = counts[:, 0].astype(jnp.int32)
    padded = (cnt + tm - 1) // tm * tm
    ends = jnp.cumsum(padded)
    starts = ends - padded
    n_slots = 2 * n_rows + MOE_N_EXPERTS * tm
    tile_start = jnp.arange(n_slots // tm, dtype=jnp.int32) * tm
    tile_expert = jnp.minimum(jnp.sum(tile_start[:, None] >= ends[None, :], axis=1),
                              MOE_N_EXPERTS - 1).astype(jnp.int32)
    n_used = (ends[-1:] // tm).astype(jnp.int32)
    n_valid = jnp.clip(starts[tile_expert] + cnt[tile_expert] - tile_start, 0, tm).astype(jnp.int32)
    pos = moe_slots(meta_t, starts.astype(F32))
    idx0 = pos[META_E1].reshape(n_rows // SC_BLOCK, SC_BLOCK)
    idx1 = pos[META_E2].reshape(n_rows // SC_BLOCK, SC_BLOCK)
    xs = sc_scatter_rows(hp, idx0, idx1, n_slots)
    ys = moe_experts(xs, tile_expert, n_used, n_valid, w_gate, w_up, w_down, layer, tm)
    ya, yb = sc_gather_rows(ys, idx0, idx1)
    return _Pending(x, ya, yb, meta, mod)


SC_CORES = 2
SC_SUBCORES = 16
SC_WORKERS = SC_CORES * SC_SUBCORES
SC_BLOCK = 128


def _sc_mesh():
    return plsc.VectorSubcoreMesh(core_axis_name="c", subcore_axis_name="s")


def _sc_worker():
    return lax.axis_index("s") * SC_CORES + lax.axis_index("c")


def sc_scatter_rows(rows, idx0, idx1, n_slots):
    n_rows, width = rows.shape
    per_worker = n_rows // SC_BLOCK // SC_WORKERS

    @functools.partial(
        pl.kernel, mesh=_sc_mesh(),
        out_type=jax.ShapeDtypeStruct((n_slots, width), rows.dtype),
        scratch_types=[pltpu.VMEM((SC_BLOCK,), jnp.int32), pltpu.VMEM((SC_BLOCK,), jnp.int32),
                       pltpu.VMEM((SC_BLOCK, width), rows.dtype)],
        name="sc_scatter_rows",
    )
    def scatter(rows_hbm, i0_hbm, i1_hbm, out_hbm, i0_v, i1_v, rows_v):
        first = _sc_worker() * per_worker

        @pl.loop(0, per_worker)
        def _(j):
            blk = first + j
            pltpu.sync_copy(i0_hbm.at[blk], i0_v)
            pltpu.sync_copy(i1_hbm.at[blk], i1_v)
            pltpu.sync_copy(rows_hbm.at[pl.ds(blk * SC_BLOCK, SC_BLOCK)], rows_v)
            pltpu.sync_copy(rows_v, out_hbm.at[i0_v])
            pltpu.sync_copy(rows_v, out_hbm.at[i1_v])

    return scatter(rows, idx0, idx1)


def sc_gather_rows(src, idx0, idx1):
    width = src.shape[1]
    n_rows = idx0.shape[0] * SC_BLOCK
    per_worker = n_rows // SC_BLOCK // SC_WORKERS
    out = jax.ShapeDtypeStruct((n_rows, width), src.dtype)

    @functools.partial(
        pl.kernel, mesh=_sc_mesh(), out_type=(out, out),
        scratch_types=[pltpu.VMEM((SC_BLOCK,), jnp.int32), pltpu.VMEM((SC_BLOCK, width), src.dtype)],
        name="sc_gather_rows",
    )
    def gather(src_hbm, i0_hbm, i1_hbm, a_hbm, b_hbm, idx_v, rows_v):
        first = _sc_worker() * per_worker

        @pl.loop(0, per_worker)
        def _(j):
            blk = first + j
            dst = pl.ds(blk * SC_BLOCK, SC_BLOCK)
            pltpu.sync_copy(i0_hbm.at[blk], idx_v)
            pltpu.sync_copy(src_hbm.at[idx_v], rows_v)
            pltpu.sync_copy(rows_v, a_hbm.at[dst])
            pltpu.sync_copy(i1_hbm.at[blk], idx_v)
            pltpu.sync_copy(src_hbm.at[idx_v], rows_v)
            pltpu.sync_copy(rows_v, b_hbm.at[dst])

    return gather(src, idx0, idx1)


def _trunk(x3, mods, p):
    batch, seq_len, _ = x3.shape
    x = x3.reshape(batch * seq_len, D_MODEL)
    for i in range(DEPTH):
        mod = mods[i]
        g1 = p["norm1_g"][i]
        route = (p["norm2_g"][i], p["moe_w_group"][i], p["moe_b_group"][i], p["moe_w_router"][i],
                 p["moe_b_router"][i])
        kind = i % 4
        if kind == 0:
            if isinstance(x, _Pending):
                x = moe_combine(x, seq_len)
            q, k, v = attn_qkv(x, mod, g1, p["attn_wqkv"][0], p["attn_q_norm"][0],
                               p["attn_k_norm"][0], seq_len)
            o = attn_flash(q, k, v, batch, seq_len)
            routed = proj_residual(o, p["attn_wo"][0], x, mod, 2, seq_len, route)
        elif kind == 1:
            hp = {"w_in": p["hy_w_in"][0], "conv_w": p["hy_conv_w"][0], "conv_b": p["hy_conv_b"][0],
                  "w1": p["hy_ffn_w1"][0], "b1": p["hy_ffn_b1"][0], "w2": p["hy_ffn_w2"][0],
                  "b2": p["hy_ffn_b2"][0], "w3": p["hy_ffn_w3"][0], "freq": p["hy_freq"][0],
                  "skip": p["hy_skip"][0], "w_out": p["hy_w_out"][0]}
            routed = hyena_mixer(x, mod, g1, hp, batch, seq_len, route)
        elif kind == 2:
            routed = pool_mixer(x, mod, g1, p["pool_w"][0], p["pool_scale"][0], seq_len, route)
        else:
            routed = sconv_mixer(x, mod, g1, p["sc_w_in"][0], p["sc_conv_w"][0],
                                 p["sc_conv_b"][0], p["sc_w_out"][0], seq_len, route)
        x = hier_moe(routed, mod, p["moe_w_gate"], p["moe_w_up"], p["moe_w_down"], i, seq_len)
    return moe_combine(x, seq_len).reshape(batch, seq_len, D_MODEL)


def kernel(x_prompt, x_sample, c_prompt, c_sample, norm1_g, norm2_g, ada_w, ada_b, attn_wqkv, attn_q_norm, attn_k_norm, attn_wo, hy_w_in, hy_conv_w, hy_conv_b, hy_ffn_w1, hy_ffn_b1, hy_ffn_w2, hy_ffn_b2, hy_ffn_w3, hy_freq, hy_skip, hy_w_out, pool_w, pool_scale, sc_w_in, sc_conv_w, sc_conv_b, sc_w_out, moe_w_group, moe_b_group, moe_w_router, moe_b_router, moe_w_gate, moe_w_up, moe_w_down):
    p = dict(norm1_g=norm1_g, norm2_g=norm2_g, attn_wqkv=attn_wqkv, attn_q_norm=attn_q_norm,
             attn_k_norm=attn_k_norm, attn_wo=attn_wo, hy_w_in=hy_w_in, hy_conv_w=hy_conv_w,
             hy_conv_b=hy_conv_b, hy_ffn_w1=hy_ffn_w1, hy_ffn_b1=hy_ffn_b1, hy_ffn_w2=hy_ffn_w2,
             hy_ffn_b2=hy_ffn_b2, hy_ffn_w3=hy_ffn_w3, hy_freq=hy_freq, hy_skip=hy_skip,
             hy_w_out=hy_w_out, pool_w=pool_w, pool_scale=pool_scale, sc_w_in=sc_w_in,
             sc_conv_w=sc_conv_w, sc_conv_b=sc_conv_b, sc_w_out=sc_w_out, moe_w_group=moe_w_group,
             moe_b_group=moe_b_group, moe_w_router=moe_w_router, moe_b_router=moe_b_router,
             moe_w_gate=moe_w_gate, moe_w_up=moe_w_up, moe_w_down=moe_w_down)
    nb = c_prompt.shape[0]
    ns = c_sample.shape[0]
    rows = -(-(nb + ns) // HALO) * HALO
    c_all = jnp.pad(jnp.concatenate([c_prompt, c_sample], axis=0), ((0, rows - nb - ns), (0, 0)))
    mod = ada_mod(c_all, ada_w, ada_b).reshape(DEPTH, rows, 6, D_MODEL)
    mods_prompt = [mod[i, :nb] for i in range(DEPTH)]
    mods_sample = [mod[i, nb:nb + ns] for i in range(DEPTH)]
    return _trunk(x_prompt, mods_prompt, p), _trunk(x_sample, mods_sample, p)
```

```python
import functools
import math
from typing import NamedTuple

import jax
import jax.numpy as jnp
import numpy as np
from jax import lax
from jax.experimental import pallas as pl
from jax.experimental.pallas import tpu as pltpu
from jax.experimental.pallas import tpu_sc as plsc

F32 = jnp.float32
BF16 = jnp.bfloat16
F8 = jnp.float8_e4m3fn

D_MODEL = 1024
DEPTH = 4
EPS = 1e-6
GRID_W = 64
HEAD_DIM = 64
N_HEADS = 16
N_KV_HEADS = 4
Q_PER_KV = 4
ROPE_THETA = 10000.0
ROPE_FREQS = 16
HY_EMB_DIM = 33
HY_BANDS = 16
HY_FILTER_WIDTH = 64
HY_FAST_DECAY = 0.3
HY_SLOW_DECAY = 1.5
HY_TARGET = 1e-2
POOL_WINDOWS = (2, 4, 8, 16)
POOL_GROUP_DIM = 256
MOE_GROUPS = 4
MOE_EXPERTS_PER_GROUP = 8
MOE_N_EXPERTS = 32
MOE_D_FF = 256

LANES = 128
HALO = 8
DFT_N2 = 256
VMEM_LIMIT = 56 * 1024 * 1024

ROW_TILE = 512
STREAM_TILE = 1024


def _params(sem):
    return pltpu.CompilerParams(dimension_semantics=sem, vmem_limit_bytes=VMEM_LIMIT)


def _dot(a, b):
    return jnp.dot(a, b, preferred_element_type=F32)


def _split(a):
    hi = a.astype(BF16)
    lo = (a - hi.astype(F32)).astype(BF16)
    return hi, lo


def _dot3(a, b):
    ah, al = _split(a)
    bh, bl = _split(b)
    return _dot(ah, bh) + (_dot(ah, bl) + _dot(al, bh))


def _modulate(x, g, shift, scale):
    ms = jnp.mean(x * x, axis=-1, keepdims=True)
    return x * lax.rsqrt(ms + EPS) * g * (1.0 + scale) + shift


def _silu(x):
    return x * (1.0 / (1.0 + jnp.exp(-x)))


def _ada_kernel(c_ref, w_ref, b_ref, o_ref):
    c = c_ref[...]
    o_ref[0] = _dot3(_silu(c), w_ref[0]) + b_ref[0]


def ada_mod(c_all, ada_w, ada_b):
    rows = c_all.shape[0]
    n = ada_w.shape[2]
    tn = 1536
    return pl.pallas_call(
        _ada_kernel,
        grid=(DEPTH, n // tn),
        in_specs=[
            pl.BlockSpec((rows, D_MODEL), lambda l, j: (0, 0)),
            pl.BlockSpec((1, D_MODEL, tn), lambda l, j: (l, 0, j)),
            pl.BlockSpec((1, 1, tn), lambda l, j: (l, 0, j)),
        ],
        out_specs=pl.BlockSpec((1, rows, tn), lambda l, j: (l, 0, j)),
        out_shape=jax.ShapeDtypeStruct((DEPTH, rows, n), F32),
        compiler_params=_params(("parallel", "parallel")),
        name="ada_mod",
    )(c_all, ada_w, ada_b.reshape(DEPTH, 1, n))


def _row_spec(tm, width=D_MODEL):
    return pl.BlockSpec((tm, width), lambda i: (i, 0))


def _mod_spec(tm, seq_len):
    return pl.BlockSpec((1, 6, D_MODEL), lambda i: ((i * tm) // seq_len, 0, 0))


def _const_spec(shape):
    nd = len(shape)
    return pl.BlockSpec(shape, lambda i: (0,) * nd)


def _halo_specs(tm, n_rows, width=D_MODEL):
    per = tm // HALO
    last = n_rows // HALO - 1
    prev = pl.BlockSpec((HALO, width), lambda i: (jnp.maximum(i * per - 1, 0), 0))
    nxt = pl.BlockSpec((HALO, width), lambda i: (jnp.minimum((i + 1) * per, last), 0))
    return prev, nxt


class _Pending(NamedTuple):
    x: jax.Array
    ya: jax.Array
    yb: jax.Array
    meta: jax.Array
    mod: jax.Array


N_PLAIN_REFS = 3
N_PENDING_REFS = 13


def _stream_io(src, tm, n_rows, seq_len):
    if not isinstance(src, _Pending):
        prev, nxt = _halo_specs(tm, n_rows)
        return [src] * 3, [prev, _row_spec(tm), nxt]
    operands, specs = [], []
    for arr in (src.x, src.ya, src.yb, src.meta):
        width = arr.shape[1]
        prev, nxt = _halo_specs(tm, n_rows, width)
        operands += [arr] * 3
        specs += [prev, _row_spec(tm, width), nxt]
    return operands + [src.mod], specs + [_mod_spec(tm, seq_len)]


def _moe_combined(x_ref, ya_ref, yb_ref, meta_ref, gate):
    meta = meta_ref[...]
    y = (meta[:, META_W1:META_W1 + 1] * _unpack_bf16(ya_ref[...])
         + meta[:, META_W2:META_W2 + 1] * _unpack_bf16(yb_ref[...]))
    return x_ref[...] + gate * y


def _stream_rows(refs):
    if len(refs) == N_PLAIN_REFS:
        prev_ref, x_ref, next_ref = refs
        x = x_ref[...]
        return x, jnp.concatenate([prev_ref[...], x, next_ref[...]], axis=0)
    gate = refs[-1][0][5:6]
    parts = [_moe_combined(refs[k], refs[3 + k], refs[6 + k], refs[9 + k], gate) for k in range(3)]
    return parts[1], jnp.concatenate(parts, axis=0)


def _edge_flags(tm, seq_len):
    i = pl.program_id(0)
    per_seq = seq_len // tm
    pos = i % per_seq
    return pos == 0, pos == per_seq - 1


def _shift_rows(u, tm):
    n = u.shape[0]
    up = pltpu.roll(u, 1, axis=0)[HALO:HALO + tm]
    dn = pltpu.roll(u, n - 1, axis=0)[HALO:HALO + tm]
    return up, u[HALO:HALO + tm], dn


def _conv3(u, w_ref, b_ref, cols, tm, first, last, stage_ref=None):
    if stage_ref is None:
        up, mid, dn = _shift_rows(u, tm)
    else:
        stage_ref[...] = u
        up = stage_ref[pl.ds(HALO - 1, tm), :]
        mid = stage_ref[pl.ds(HALO, tm), :]
        dn = stage_ref[pl.ds(HALO + 1, tm), :]
    row = lax.broadcasted_iota(jnp.int32, (tm, 1), 0)
    up = jnp.where(jnp.logical_and(first, row == 0), 0.0, up)
    dn = jnp.where(jnp.logical_and(last, row == tm - 1), 0.0, dn)
    w = w_ref[:, cols]
    return up * w[0:1] + mid * w[1:2] + dn * w[2:3] + b_ref[:, cols]


def _norm_rope(t, gain, headmean, cos, sin_signed):
    width = t.shape[1]
    ms = _dot((t * t).astype(BF16), headmean[:width, :width])
    y = t * lax.rsqrt(ms + EPS) * gain
    lane = lax.broadcasted_iota(jnp.int32, y.shape, 1)
    first = (lane % 32) < ROPE_FREQS
    partner = jnp.where(first, pltpu.roll(y, width - ROPE_FREQS, axis=1),
                        pltpu.roll(y, ROPE_FREQS, axis=1))
    reps = width // LANES
    return y * jnp.tile(cos, (1, reps)) + partner * jnp.tile(sin_signed, (1, reps))


def _qkv_kernel(x_ref, mod_ref, g_ref, w_ref, qg_ref, kg_ref, hm_ref, cos_ref, sin_ref,
                q_ref, k_ref, v_ref):
    m = mod_ref[0]
    h = _modulate(x_ref[...], g_ref[...], m[0:1], m[1:2]).astype(BF16)
    qkv = _dot(h, w_ref[...])
    nq = N_HEADS * HEAD_DIM
    nk = N_KV_HEADS * HEAD_DIM
    cos = cos_ref[...]
    sin = sin_ref[...]
    hm = hm_ref[...]
    q = _norm_rope(qkv[:, :nq], qg_ref[...], hm, cos, sin)
    k = _norm_rope(qkv[:, nq:nq + nk], kg_ref[...], hm, cos, sin)
    v = qkv[:, nq + nk:]
    q_ref[...] = q.astype(BF16)
    ones = jnp.ones((v.shape[0], HEAD_DIM), F32)
    for g in range(N_KV_HEADS):
        sl = slice(g * HEAD_DIM, (g + 1) * HEAD_DIM)
        k_ref[g] = k[:, sl].astype(F8)
        v_ref[g] = jnp.concatenate([v[:, sl], ones], axis=1).astype(BF16)


def _rope_tables(seq_len):
    rows = seq_len // GRID_W
    r = jnp.broadcast_to(jnp.arange(rows)[:, None], (rows, GRID_W)).reshape(-1)
    c = jnp.broadcast_to(jnp.arange(GRID_W)[None, :], (rows, GRID_W)).reshape(-1)
    inv_freq = ROPE_THETA ** (-jnp.arange(ROPE_FREQS, dtype=F32) / ROPE_FREQS)
    pos = jnp.stack([r, c], axis=-1).astype(F32)
    ang = pos[:, :, None] * inv_freq[None, None, :]
    cos = jnp.cos(ang)
    sin = jnp.sin(ang)
    cos64 = jnp.concatenate([cos, cos], axis=-1).reshape(seq_len, HEAD_DIM)
    sin64 = jnp.concatenate([-sin, sin], axis=-1).reshape(seq_len, HEAD_DIM)
    return jnp.tile(cos64, (1, 2)), jnp.tile(sin64, (1, 2))


def attn_qkv(x, mod, norm_g, wqkv, q_norm, k_norm, seq_len):
    n_rows = x.shape[0]
    tm = ROW_TILE
    nq = N_HEADS * HEAD_DIM
    nk = N_KV_HEADS * HEAD_DIM
    cos, sin = _rope_tables(seq_len)
    qg = jnp.tile(q_norm, N_HEADS)[None, :] * (HEAD_DIM ** -0.5 * math.log2(math.e))
    kg = jnp.tile(k_norm, N_KV_HEADS)[None, :]
    head = np.arange(nq) // HEAD_DIM
    headmean = jnp.asarray((head[:, None] == head[None, :]).astype(np.float32) / HEAD_DIM, BF16)
    per_seq = seq_len // tm
    tab_spec = pl.BlockSpec((tm, LANES), lambda i: (i % per_seq, 0))
    return pl.pallas_call(
        _qkv_kernel,
        grid=(n_rows // tm,),
        in_specs=[
            _row_spec(tm), _mod_spec(tm, seq_len), _const_spec((1, D_MODEL)),
            _const_spec((D_MODEL, nq + 2 * nk)), _const_spec((1, nq)), _const_spec((1, nk)),
            _const_spec((nq, nq)), tab_spec, tab_spec,
        ],
        out_specs=[
            _row_spec(tm, nq),
            pl.BlockSpec((N_KV_HEADS, tm, HEAD_DIM), lambda i: (0, i, 0)),
            pl.BlockSpec((N_KV_HEADS, tm, 2 * HEAD_DIM), lambda i: (0, i, 0)),
        ],
        out_shape=[
            jax.ShapeDtypeStruct((n_rows, nq), BF16),
            jax.ShapeDtypeStruct((N_KV_HEADS, n_rows, HEAD_DIM), F8),
            jax.ShapeDtypeStruct((N_KV_HEADS, n_rows, 2 * HEAD_DIM), BF16),
        ],
        compiler_params=_params(("parallel",)),
        name="attn_qkv",
    )(x, mod, norm_g[None, :], wqkv.astype(BF16), qg, kg, headmean, cos, sin)


def _flash_kernel(q_ref, k_ref, v_ref, o_ref, *, tq, tk, n_chunks):
    q = q_ref[...]
    qs = jnp.concatenate([q[:, j * HEAD_DIM:(j + 1) * HEAD_DIM] for j in range(Q_PER_KV)], axis=0)
    qs = qs.astype(F8)
    rows = Q_PER_KV * tq

    def body(c, carry):
        m, acc = carry
        start = pl.multiple_of(c * tk, tk)
        kc = k_ref[0, pl.ds(start, tk), :]
        vc = v_ref[0, pl.ds(start, tk), :]
        s = lax.dot_general(qs, kc, (((1,), (1,)), ((), ())), preferred_element_type=F32)
        m_new = jnp.maximum(m, jnp.max(s, axis=-1, keepdims=True))
        alpha = jnp.exp2(m - m_new)
        p = jnp.exp2(s - m_new)
        acc = acc * alpha + _dot(p.astype(BF16), vc)
        return m_new, acc

    m0 = jnp.full((rows, 1), -jnp.inf, F32)
    acc0 = jnp.zeros((rows, 2 * HEAD_DIM), F32)
    _, acc = lax.fori_loop(0, n_chunks, body, (m0, acc0))
    o = acc[:, :HEAD_DIM] / acc[:, HEAD_DIM:HEAD_DIM + 1]
    o_ref[...] = jnp.concatenate([o[j * tq:(j + 1) * tq] for j in range(Q_PER_KV)],
                                 axis=1).astype(BF16)


def attn_flash(q, k, v, batch, seq_len):
    n_rows = q.shape[0]
    tq = 256
    tk = min(seq_len, 2048)
    per_seq = seq_len // tq
    width = Q_PER_KV * HEAD_DIM
    kern = functools.partial(_flash_kernel, tq=tq, tk=tk, n_chunks=seq_len // tk)
    return pl.pallas_call(
        kern,
        grid=(batch, N_KV_HEADS, per_seq),
        in_specs=[
            pl.BlockSpec((tq, width), lambda b, g, i: (b * per_seq + i, g)),
            pl.BlockSpec((1, seq_len, HEAD_DIM), lambda b, g, i: (g, b, 0)),
            pl.BlockSpec((1, seq_len, 2 * HEAD_DIM), lambda b, g, i: (g, b, 0)),
        ],
        out_specs=pl.BlockSpec((tq, width), lambda b, g, i: (b * per_seq + i, g)),
        out_shape=jax.ShapeDtypeStruct((n_rows, N_HEADS * HEAD_DIM), BF16),
        compiler_params=_params(("parallel", "parallel", "parallel")),
        name="attn_flash",
    )(q, k, v)


def _proj_res_kernel(y_ref, w_ref, x_ref, mod_ref, *refs, gate_row):
    route_in, o_ref, route_out = refs[:N_ROUTE_IN], refs[N_ROUTE_IN], refs[N_ROUTE_IN + 1:]
    m = mod_ref[0]
    x1 = x_ref[...] + m[gate_row:gate_row + 1] * _dot(y_ref[...], w_ref[...])
    o_ref[...] = x1
    _route_tile(x1, m, *route_in, *route_out)


def proj_residual(y, w, x, mod, gate_row, seq_len, route):
    n_rows = x.shape[0]
    tm = STREAM_TILE
    r_ops, r_in, r_out, r_shape = _router_io(route, tm, n_rows)
    return pl.pallas_call(
        functools.partial(_proj_res_kernel, gate_row=gate_row),
        grid=(n_rows // tm,),
        in_specs=[_row_spec(tm, y.shape[1]), _const_spec(w.shape), _row_spec(tm),
                  _mod_spec(tm, seq_len)] + r_in,
        out_specs=[_row_spec(tm)] + r_out,
        out_shape=[jax.ShapeDtypeStruct((n_rows, D_MODEL), F32)] + r_shape,
        compiler_params=_params(("arbitrary",)),
        name="proj_residual",
    )(y, w.astype(BF16), x, mod, *r_ops)


CONV_COLS = 256


def _hy_in_kernel(*refs, tm, seq_len, n_stream):
    stream, refs = refs[:n_stream], refs[n_stream:]
    mod_ref, g_ref, w_ref, cw_ref, cb_ref, z_ref, x0_ref = refs[:7]
    stage_refs = refs[-3:]
    first, last = _edge_flags(tm, seq_len)
    m = mod_ref[0]
    x, ext = _stream_rows(stream)
    if n_stream == N_PENDING_REFS:
        refs[7][...] = x
    h = _modulate(ext, g_ref[...], m[0:1], m[1:2]).astype(BF16)
    tn = CONV_COLS
    for j in range(D_MODEL // tn):
        part = []
        for s in range(3):
            cols = slice(s * D_MODEL + j * tn, s * D_MODEL + (j + 1) * tn)
            part.append(_conv3(_dot(h, w_ref[:, cols]), cw_ref, cb_ref, cols, tm, first, last,
                               stage_refs[s]))
        out_cols = slice(j * tn, (j + 1) * tn)
        x0_ref[:, out_cols] = part[0].astype(BF16)
        z_ref[:, out_cols] = (part[2] * part[1]).astype(BF16)


def hyena_in(src, mod, norm_g, w_in, conv_w, conv_b, seq_len):
    pending = isinstance(src, _Pending)
    n_rows = (src.x if pending else src).shape[0]
    tm = ROW_TILE
    s_ops, s_specs = _stream_io(src, tm, n_rows, seq_len)
    x_out = [jax.ShapeDtypeStruct((n_rows, D_MODEL), F32)] if pending else []
    return pl.pallas_call(
        functools.partial(_hy_in_kernel, tm=tm, seq_len=seq_len, n_stream=len(s_ops)),
        grid=(n_rows // tm,),
        in_specs=s_specs + [_mod_spec(tm, seq_len), _const_spec((1, D_MODEL)),
                            _const_spec((D_MODEL, 3 * D_MODEL)), _const_spec((3, 3 * D_MODEL)),
                            _const_spec((1, 3 * D_MODEL))],
        out_specs=[_row_spec(tm)] * (2 + len(x_out)),
        out_shape=[jax.ShapeDtypeStruct((n_rows, D_MODEL), BF16),
                   jax.ShapeDtypeStruct((n_rows, D_MODEL), BF16)] + x_out,
        scratch_shapes=[pltpu.VMEM((tm + 2 * HALO, CONV_COLS), F32)] * 3,
        compiler_params=_params(("parallel",)),
        name="hyena_in",
    )(*s_ops, mod, norm_g[None, :], w_in.astype(BF16), conv_w, conv_b[None, :])


def _hy_filter_kernel(feat_ref, w1_ref, b1_ref, w2_ref, b2_ref, w3_ref, fr_ref, dl_ref, o_ref):
    feat = feat_ref[...]
    fr = fr_ref[...]
    a = jnp.sin(fr * (_dot3(feat, w1_ref[...]) + b1_ref[...]))
    a = jnp.sin(fr * (_dot3(a, w2_ref[...]) + b2_ref[...]))
    hf = _dot3(a, w3_ref[...])
    decay = jnp.exp(-feat[:, 0:1] * dl_ref[...])
    o_ref[0] = hf[:, :D_MODEL] * decay
    o_ref[1] = hf[:, D_MODEL:] * decay


def _pad_to(a, rows, cols):
    return jnp.pad(a.astype(F32), ((0, rows - a.shape[0]), (0, cols - a.shape[1])))


def hyena_filter(seq_len, w1, b1, w2, b2, w3, freq):
    t = jnp.linspace(0.0, 1.0, seq_len, dtype=F32)[:, None]
    w = 2.0 * math.pi * jnp.arange(seq_len, dtype=F32)[:, None] / seq_len
    f = jnp.linspace(1e-4, HY_BANDS - 1, HY_BANDS, dtype=F32)[None, :]
    feat = _pad_to(jnp.concatenate([t, jnp.cos(f * w), -jnp.sin(f * w)], axis=-1), seq_len, LANES)
    max_decay = math.log(HY_TARGET) / HY_FAST_DECAY
    min_decay = math.log(HY_TARGET) / HY_SLOW_DECAY
    absdelta = jnp.abs(jnp.linspace(min_decay, max_decay, D_MODEL, dtype=F32))[None, :]
    tl = 512
    return pl.pallas_call(
        _hy_filter_kernel,
        grid=(seq_len // tl,),
        in_specs=[_row_spec(tl, LANES), _const_spec((LANES, LANES)), _const_spec((1, LANES)),
                  _const_spec((LANES, LANES)), _const_spec((1, LANES)),
                  _const_spec((LANES, 2 * D_MODEL)), _const_spec((1, LANES)),
                  _const_spec((1, D_MODEL))],
        out_specs=pl.BlockSpec((2, tl, D_MODEL), lambda i: (0, i, 0)),
        out_shape=jax.ShapeDtypeStruct((2, seq_len, D_MODEL), F32),
        compiler_params=_params(("parallel",)),
        name="hyena_filter",
    )(feat, _pad_to(w1, LANES, LANES), _pad_to(b1[None, :], 1, LANES), _pad_to(w2, LANES, LANES),
      _pad_to(b2[None, :], 1, LANES), _pad_to(w3, LANES, 2 * D_MODEL),
      _pad_to(freq[None, :], 1, LANES), absdelta)


class _FFTPlan:
    def __init__(self, seq_len):
        self.n = 2 * seq_len
        self.n1 = self.n // DFT_N2
        self.r = self.n1 // 2
        self.k1n = self.n1 // 2 + 1
        self.kron = max(2 * HALO, LANES // self.r)
        ang = 2.0 * np.pi * np.outer(np.arange(self.k1n), np.arange(self.r)) / self.n1
        eye = np.eye(self.kron)
        self.fwd_cos = jnp.asarray(np.kron(np.cos(ang), eye), BF16)
        self.fwd_sin = jnp.asarray(np.kron(-np.sin(ang), eye), BF16)
        wgt = np.full((self.k1n,), 2.0)
        wgt[0] = wgt[-1] = 1.0
        scale = (wgt / self.n)[None, :]
        self.inv_cos = jnp.asarray(np.kron(np.cos(ang).T * scale, eye), BF16)
        self.inv_sin = jnp.asarray(np.kron(-np.sin(ang).T * scale, eye), BF16)
        a2 = 2.0 * np.pi * np.outer(np.arange(DFT_N2), np.arange(DFT_N2)) / DFT_N2
        self.f_cos = jnp.asarray(np.cos(a2), F32)
        self.f_sin = jnp.asarray(-np.sin(a2), F32)
        tw = 2.0 * np.pi * np.outer(np.arange(self.k1n), np.arange(DFT_N2)) / self.n
        self.tw_cos = jnp.asarray(np.cos(tw), F32)
        self.tw_sin = jnp.asarray(-np.sin(tw), F32)


def _fft_a_kernel(z_ref, wc_ref, ws_ref, ar_ref, ai_ref, *, rq):
    shape = ar_ref.shape[1:2] + ar_ref.shape[3:]
    for t in range(z_ref.shape[2]):
        z = z_ref[0, :, t].reshape(rq, D_MODEL).astype(BF16)
        ar_ref[0, :, t] = _dot(wc_ref[...], z).reshape(shape).astype(BF16)
        ai_ref[0, :, t] = _dot(ws_ref[...], z).reshape(shape).astype(BF16)


def _fft_group(plan):
    nhi = DFT_N2 // plan.kron
    per_group = plan.r * plan.kron * D_MODEL * 4
    return max(1, min(nhi, (2 << 20) // per_group))


def fft_stage_a(z, plan, batch):
    q = plan.kron
    nhi = DFT_N2 // q
    hb = _fft_group(plan)
    zv = z.reshape(batch, plan.r, nhi, q, D_MODEL)
    out = jax.ShapeDtypeStruct((batch, plan.k1n, nhi, q, D_MODEL), BF16)
    ospec = pl.BlockSpec((1, plan.k1n, hb, q, D_MODEL), lambda b, h: (b, 0, h, 0, 0))
    wspec = pl.BlockSpec(plan.fwd_cos.shape, lambda b, h: (0, 0))
    ar, ai = pl.pallas_call(
        functools.partial(_fft_a_kernel, rq=plan.r * q),
        grid=(batch, nhi // hb),
        in_specs=[pl.BlockSpec((1, plan.r, hb, q, D_MODEL), lambda b, h: (b, 0, h, 0, 0)),
                  wspec, wspec],
        out_specs=[ospec, ospec],
        out_shape=[out, out],
        compiler_params=_params(("parallel", "parallel")),
        name="fft_stage_a",
    )(zv, plan.fwd_cos, plan.fwd_sin)
    shape = (batch, plan.k1n, DFT_N2, D_MODEL)
    return ar.reshape(shape), ai.reshape(shape)


def _twiddled_dft(fr, fi, tr, ti):
    return (fr * tr - fi * ti).astype(BF16), (fr * ti + fi * tr).astype(BF16)


def _fft_b_fwd_kernel(ar_ref, ai_ref, fr_ref, fi_ref, twr_ref, twi_ref, br_ref, bi_ref):
    n = DFT_N2
    gr, gi = _twiddled_dft(fr_ref[...], fi_ref[...], twr_ref[0], twi_ref[0])
    fwd = jnp.concatenate([jnp.concatenate([gr, -gi], axis=1),
                           jnp.concatenate([gi, gr], axis=1)], axis=0)
    for s in range(ar_ref.shape[0]):
        b = _dot(fwd, jnp.concatenate([ar_ref[s, 0], ai_ref[s, 0]], axis=0))
        br_ref[s, 0] = b[:n]
        bi_ref[s, 0] = b[n:]


def fft_stage_b_fwd(ar, ai, plan):
    batch = ar.shape[0]
    blk = pl.BlockSpec((batch, 1, DFT_N2, D_MODEL), lambda k: (0, k, 0, 0))
    cst = pl.BlockSpec((DFT_N2, DFT_N2), lambda k: (0, 0))
    tws = pl.BlockSpec((1, 1, DFT_N2), lambda k: (k, 0, 0))
    out = jax.ShapeDtypeStruct(ar.shape, F32)
    return pl.pallas_call(
        _fft_b_fwd_kernel,
        grid=(plan.k1n,),
        in_specs=[blk, blk, cst, cst, tws, tws],
        out_specs=[blk, blk],
        out_shape=[out, out],
        compiler_params=_params(("parallel",)),
        name="fft_stage_b_fwd",
    )(ar, ai, plan.f_cos, plan.f_sin, plan.tw_cos[:, None, :], plan.tw_sin[:, None, :])


def _fft_b_conv_kernel(ar_ref, ai_ref, hr_ref, hi_ref, hb0_ref, fr_ref, fi_ref,
                       twr_ref, twi_ref, tcr_ref, tci_ref, cr_ref, ci_ref,
                       fwd_ref, inv_ref, kr_ref, ki_ref):
    n = DFT_N2

    @pl.when(pl.program_id(1) == 0)
    def _():
        fr = fr_ref[...]
        fi = fi_ref[...]
        gr, gi = _twiddled_dft(fr, fi, twr_ref[0], twi_ref[0])
        fwd_ref[:n, :n] = gr
        fwd_ref[:n, n:] = -gi
        fwd_ref[n:, :n] = gi
        fwd_ref[n:, n:] = gr
        gtr, gti = _twiddled_dft(fr, fi, tcr_ref[0], tci_ref[0])
        inv_ref[:n, :n] = gtr
        inv_ref[:n, n:] = gti
        inv_ref[n:, :n] = -gti
        inv_ref[n:, n:] = gtr
        kr_ref[...] = hr_ref[0, 0] + hr_ref[1, 0] - hb0_ref[...]
        ki_ref[...] = hi_ref[0, 0] - hi_ref[1, 0]

    a = jnp.concatenate([ar_ref[0, 0], ai_ref[0, 0]], axis=0)
    b = _dot(fwd_ref[...], a)
    br, bi = b[:n], b[n:]
    kr = kr_ref[...]
    ki = ki_ref[...]
    p = jnp.concatenate([(br * kr - bi * ki).astype(BF16), (br * ki + bi * kr).astype(BF16)], axis=0)
    c = _dot(inv_ref[...], p)
    cr_ref[0, 0] = c[:n].astype(BF16)
    ci_ref[0, 0] = c[n:].astype(BF16)


def fft_stage_b_conv(ar, ai, hr, hi, hb0, plan):
    batch = ar.shape[0]
    blk = pl.BlockSpec((1, 1, DFT_N2, D_MODEL), lambda k, b: (b, k, 0, 0))
    hblk = pl.BlockSpec((2, 1, DFT_N2, D_MODEL), lambda k, b: (0, k, 0, 0))
    cst = pl.BlockSpec((DFT_N2, DFT_N2), lambda k, b: (0, 0))
    tws = pl.BlockSpec((1, 1, DFT_N2), lambda k, b: (k, 0, 0))
    twc = pl.BlockSpec((1, DFT_N2, 1), lambda k, b: (k, 0, 0))
    out = jax.ShapeDtypeStruct(ar.shape, BF16)
    return pl.pallas_call(
        _fft_b_conv_kernel,
        grid=(plan.k1n, batch),
        in_specs=[blk, blk, hblk, hblk, pl.BlockSpec((1, D_MODEL), lambda k, b: (0, 0)),
                  cst, cst, tws, tws, twc, twc],
        out_specs=[blk, blk],
        out_shape=[out, out],
        scratch_shapes=[pltpu.VMEM((2 * DFT_N2, 2 * DFT_N2), BF16)] * 2
        + [pltpu.VMEM((DFT_N2, D_MODEL), F32)] * 2,
        compiler_params=_params(("parallel", "arbitrary")),
        name="fft_stage_b_conv",
    )(ar, ai, hr, hi, hb0, plan.f_cos, plan.f_sin,
      plan.tw_cos[:, None, :], plan.tw_sin[:, None, :],
      plan.tw_cos[:, :, None], plan.tw_sin[:, :, None])


def _fft_a_inv_kernel(cr_ref, ci_ref, vc_ref, vs_ref, z_ref, x0_ref, skip_ref, y_ref, *, kq):
    shape = z_ref.shape[1:2] + z_ref.shape[3:]
    for t in range(z_ref.shape[2]):
        cr = cr_ref[0, :, t].reshape(kq, D_MODEL).astype(BF16)
        ci = ci_ref[0, :, t].reshape(kq, D_MODEL).astype(BF16)
        conv = _dot(vc_ref[...], cr) + _dot(vs_ref[...], ci)
        y = conv.reshape(shape) + z_ref[0, :, t].astype(F32) * skip_ref[...]
        y_ref[0, :, t] = (y * x0_ref[0, :, t].astype(F32)).astype(BF16)


def fft_stage_a_inv(cr, ci, z, x0, skip, plan, batch):
    q = plan.kron
    nhi = DFT_N2 // q
    hb = _fft_group(plan)
    cshape = (batch, plan.k1n, nhi, q, D_MODEL)
    tshape = (batch, plan.r, nhi, q, D_MODEL)
    cspec = pl.BlockSpec((1, plan.k1n, hb, q, D_MODEL), lambda b, h: (b, 0, h, 0, 0))
    tspec = pl.BlockSpec((1, plan.r, hb, q, D_MODEL), lambda b, h: (b, 0, h, 0, 0))
    wspec = pl.BlockSpec(plan.inv_cos.shape, lambda b, h: (0, 0))
    y = pl.pallas_call(
        functools.partial(_fft_a_inv_kernel, kq=plan.k1n * q),
        grid=(batch, nhi // hb),
        in_specs=[cspec, cspec, wspec, wspec, tspec, tspec,
                  pl.BlockSpec((1, D_MODEL), lambda b, h: (0, 0))],
        out_specs=tspec,
        out_shape=jax.ShapeDtypeStruct(tshape, BF16),
        compiler_params=_params(("parallel", "parallel")),
        name="fft_stage_a_inv",
    )(cr.reshape(cshape), ci.reshape(cshape), plan.inv_cos, plan.inv_sin,
      z.reshape(tshape), x0.reshape(tshape), skip[None, :])
    return y.reshape(z.shape)


def hyena_mixer(src, mod, norm_g, p, batch, seq_len, route):
    z, x0, *rest = hyena_in(src, mod, norm_g, p["w_in"], p["conv_w"], p["conv_b"], seq_len)
    x = rest[0] if rest else src
    plan = _FFTPlan(seq_len)
    filt = hyena_filter(seq_len, p["w1"], p["b1"], p["w2"], p["b2"], p["w3"], p["freq"])
    fr, fi = fft_stage_a(filt.reshape(2 * seq_len, D_MODEL), plan, 2)
    hr, hi = fft_stage_b_fwd(fr, fi, plan)
    ar, ai = fft_stage_a(z, plan, batch)
    cr, ci = fft_stage_b_conv(ar, ai, hr, hi, filt[1, 0:1, :], plan)
    y = fft_stage_a_inv(cr, ci, z, x0, p["skip"], plan, batch)
    return proj_residual(y, p["w_out"], x, mod, 2, seq_len, route)


def _pool_kernel(*refs, tm, seq_len, n_stream):
    stream, refs = refs[:n_stream], refs[n_stream:]
    mod_ref, g_ref, w_ref, s_ref = refs[:4]
    refs = refs[4:]
    route_in, o_ref, route_out = refs[:N_ROUTE_IN], refs[N_ROUTE_IN], refs[N_ROUTE_IN + 1:]
    first, last = _edge_flags(tm, seq_len)
    m = mod_ref[0]
    x, ext = _stream_rows(stream)
    h = _modulate(ext, g_ref[...], m[0:1], m[1:2])
    n = tm + 2 * HALO
    row = lax.broadcasted_iota(jnp.int32, (n, 1), 0)
    outside = jnp.logical_or(jnp.logical_and(first, row < HALO),
                             jnp.logical_and(last, row >= HALO + tm))
    h = jnp.where(outside, 0.0, h)
    pos = (pl.program_id(0) * tm) % seq_len + lax.broadcasted_iota(jnp.int32, (tm, 1), 0)
    ys = []
    for gi, win in enumerate(POOL_WINDOWS):
        cols = slice(gi * POOL_GROUP_DIM, (gi + 1) * POOL_GROUP_DIM)
        hg = h[:, cols]
        acc = hg
        span = 1
        while span < win:
            acc = acc + pltpu.roll(acc, span, axis=0)
            span *= 2
        lead = win // 2 - 1
        if lead:
            acc = pltpu.roll(acc, n - lead, axis=0)
        half = win // 2
        cnt = jnp.minimum(pos + half, seq_len) - jnp.maximum(pos - half, 0)
        pooled = acc[HALO:HALO + tm] / cnt.astype(F32) - hg[HALO:HALO + tm]
        ys.append(_dot(pooled.astype(BF16), w_ref[gi]))
    y = jnp.concatenate(ys, axis=1) * s_ref[...]
    x1 = x + m[2:3] * y
    o_ref[...] = x1
    _route_tile(x1, m, *route_in, *route_out)


def pool_mixer(src, mod, norm_g, w_group, scale, seq_len, route):
    n_rows = (src.x if isinstance(src, _Pending) else src).shape[0]
    tm = ROW_TILE
    s_ops, s_specs = _stream_io(src, tm, n_rows, seq_len)
    r_ops, r_in, r_out, r_shape = _router_io(route, tm, n_rows)
    return pl.pallas_call(
        functools.partial(_pool_kernel, tm=tm, seq_len=seq_len, n_stream=len(s_ops)),
        grid=(n_rows // tm,),
        in_specs=s_specs + [_mod_spec(tm, seq_len), _const_spec((1, D_MODEL)),
                            _const_spec(w_group.shape), _const_spec((1, D_MODEL))] + r_in,
        out_specs=[_row_spec(tm)] + r_out,
        out_shape=[jax.ShapeDtypeStruct((n_rows, D_MODEL), F32)] + r_shape,
        compiler_params=_params(("arbitrary",)),
        name="pool_mixer",
    )(*s_ops, mod, norm_g[None, :], w_group.astype(BF16), scale[None, :], *r_ops)


def _sconv_kernel(*refs, tm, seq_len, n_stream):
    stream, refs = refs[:n_stream], refs[n_stream:]
    mod_ref, g_ref, w_ref, cw_ref, cb_ref, wo_ref = refs[:6]
    refs = refs[6:]
    route_in, o_ref = refs[:N_ROUTE_IN], refs[N_ROUTE_IN]
    route_out, y_ref = refs[N_ROUTE_IN + 1:N_ROUTE_IN + 1 + N_ROUTE_OUT], refs[-1]
    first, last = _edge_flags(tm, seq_len)
    m = mod_ref[0]
    x, ext = _stream_rows(stream)
    h = _modulate(ext, g_ref[...], m[0:1], m[1:2]).astype(BF16)
    tn = 256
    for j in range(D_MODEL // tn):
        cols = slice(j * tn, (j + 1) * tn)
        bg = _dot(h, w_ref[:, cols])[HALO:HALO + tm]
        cg = _dot(h, w_ref[:, D_MODEL + j * tn:D_MODEL + (j + 1) * tn])
        hp = _dot(h, w_ref[:, 2 * D_MODEL + j * tn:2 * D_MODEL + (j + 1) * tn])
        y_ref[:, cols] = (bg * _conv3(cg * hp, cw_ref, cb_ref, cols, tm, first, last)).astype(BF16)
    x1 = x + m[2:3] * _dot(y_ref[...], wo_ref[...])
    o_ref[...] = x1
    _route_tile(x1, m, *route_in, *route_out)


def sconv_mixer(src, mod, norm_g, w_in, conv_w, conv_b, w_out, seq_len, route):
    n_rows = (src.x if isinstance(src, _Pending) else src).shape[0]
    tm = ROW_TILE
    s_ops, s_specs = _stream_io(src, tm, n_rows, seq_len)
    r_ops, r_in, r_out, r_shape = _router_io(route, tm, n_rows)
    return pl.pallas_call(
        functools.partial(_sconv_kernel, tm=tm, seq_len=seq_len, n_stream=len(s_ops)),
        grid=(n_rows // tm,),
        in_specs=s_specs + [_mod_spec(tm, seq_len), _const_spec((1, D_MODEL)),
                            _const_spec((D_MODEL, 3 * D_MODEL)), _const_spec((3, D_MODEL)),
                            _const_spec((1, D_MODEL)), _const_spec((D_MODEL, D_MODEL))] + r_in,
        out_specs=[_row_spec(tm)] + r_out,
        out_shape=[jax.ShapeDtypeStruct((n_rows, D_MODEL), F32)] + r_shape,
        scratch_shapes=[pltpu.VMEM((tm, D_MODEL), BF16)],
        compiler_params=_params(("arbitrary",)),
        name="sconv_mixer",
    )(*s_ops, mod, norm_g[None, :], w_in.astype(BF16), conv_w, conv_b[None, :],
      w_out.astype(BF16), *r_ops)


def _pack_bf16(x):
    w = x.shape[1] // 2
    bits = pltpu.bitcast(x.astype(BF16).astype(F32), jnp.uint32)
    return (bits[:, :w] >> 16) | bits[:, w:]


def _unpack_bf16(p):
    lo = pltpu.bitcast(p << 16, F32)
    hi = pltpu.bitcast(p & jnp.uint32(0xFFFF0000), F32)
    return jnp.concatenate([lo, hi], axis=1)


META_E1, META_E2, META_W1, META_W2, META_R1, META_R2 = range(6)
META_ROWS = 8


def _route_tile(x, m, g_ref, wh_ref, b_ref, tri_ref, h_ref, meta_ref, meta_t_ref, cnt_ref):
    @pl.when(pl.program_id(0) == 0)
    def _():
        cnt_ref[...] = jnp.zeros_like(cnt_ref)

    ms = jnp.mean(x * x, axis=-1, keepdims=True)
    h = x * lax.rsqrt(ms + EPS) * (g_ref[...] * (1.0 + m[4:5])) + m[3:4]
    hi = h.astype(BF16)
    hi32 = hi.astype(F32)
    lo = (h - hi32).astype(BF16)
    half = D_MODEL // 2
    bits = pltpu.bitcast(hi32, jnp.uint32)
    h_ref[...] = (bits[:, :half] >> 16) | bits[:, half:]
    part = _dot(hi, wh_ref[...])
    lg = part[:, :LANES] + (part[:, LANES:] + _dot(lo, wh_ref[:, :LANES])) + b_ref[...]
    lgt = lg.T
    tm = lgt.shape[1]
    neg = -jnp.inf
    sub = lax.broadcasted_iota(jnp.int32, (HALO, tm), 0).astype(F32)

    def first_argmax(vals):
        top = jnp.max(vals, axis=0, keepdims=True)
        idx = jnp.min(jnp.where(vals == top, sub, float(HALO)), axis=0, keepdims=True)
        return top, idx

    gl = jnp.where(sub < MOE_GROUPS, lgt[MOE_N_EXPERTS:MOE_N_EXPERTS + HALO], neg)
    gmax, grp = first_argmax(gl)
    g_w = 1.0 / jnp.sum(jnp.exp(gl - gmax), axis=0, keepdims=True)
    el = lgt[:MOE_EXPERTS_PER_GROUP]
    for g in range(1, MOE_GROUPS):
        el = jnp.where(grp == float(g),
                       lgt[g * MOE_EXPERTS_PER_GROUP:(g + 1) * MOE_EXPERTS_PER_GROUP], el)
    v1, i1 = first_argmax(el)
    v2, i2 = first_argmax(jnp.where(sub == i1, neg, el))
    ex = jnp.exp(v2 - v1)
    w1 = 1.0 / (1.0 + ex)
    w2 = ex * w1
    e1 = grp * MOE_EXPERTS_PER_GROUP + i1
    e2 = grp * MOE_EXPERTS_PER_GROUP + i2
    expert = lax.broadcasted_iota(jnp.int32, (MOE_N_EXPERTS, tm), 0).astype(F32)
    onehot = jnp.where(jnp.logical_or(expert == e1, expert == e2), 1.0, 0.0)
    before = _dot(onehot.astype(BF16), tri_ref[...]) + cnt_ref[...]
    cnt_ref[...] += jnp.sum(onehot, axis=1, keepdims=True)
    r1 = jnp.sum(jnp.where(expert == e1, before, 0.0), axis=0, keepdims=True)
    r2 = jnp.sum(jnp.where(expert == e2, before, 0.0), axis=0, keepdims=True)
    fields = ((META_E1, e1), (META_E2, e2), (META_W1, w1 * g_w), (META_W2, w2 * g_w),
              (META_R1, r1), (META_R2, r2))
    field = lax.broadcasted_iota(jnp.int32, (LANES, tm), 0)
    meta_t = jnp.zeros((LANES, tm), F32)
    for row, val in fields:
        meta_t = jnp.where(field == row, val, meta_t)
    meta_t_ref[...] = meta_t[:META_ROWS]
    meta_ref[...] = meta_t.T


N_ROUTE_IN = 4
N_ROUTE_OUT = 4


def _router_io(route, tm, n_rows):
    norm_g, w_group, b_group, w_router, b_router = route
    w = _pad_to(jnp.concatenate([w_router, w_group], axis=1), D_MODEL, LANES)
    wh = w.astype(BF16)
    wl = (w - wh.astype(F32)).astype(BF16)
    whl = jnp.concatenate([wh, wl], axis=1)
    b = _pad_to(jnp.concatenate([b_router, b_group])[None, :], 1, LANES)
    tri = jnp.asarray(np.triu(np.ones((tm, tm), np.float32), 1), BF16)
    operands = (norm_g[None, :], whl, b, tri)
    in_specs = [_const_spec((1, D_MODEL)), _const_spec((D_MODEL, 2 * LANES)),
                _const_spec((1, LANES)), _const_spec((tm, tm))]
    out_specs = [_row_spec(tm, D_MODEL // 2), _row_spec(tm, LANES),
                 pl.BlockSpec((META_ROWS, tm), lambda i: (0, i)),
                 _const_spec((MOE_N_EXPERTS, 1))]
    out_shape = [jax.ShapeDtypeStruct((n_rows, D_MODEL // 2), jnp.uint32),
                 jax.ShapeDtypeStruct((n_rows, LANES), F32),
                 jax.ShapeDtypeStruct((META_ROWS, n_rows), F32),
                 jax.ShapeDtypeStruct((MOE_N_EXPERTS, 1), F32)]
    return operands, in_specs, out_specs, out_shape


def _slot_kernel(offs_ref, meta_t_ref, pos_ref):
    meta = meta_t_ref[...]
    start = jnp.zeros_like(meta)
    for e in range(MOE_N_EXPERTS):
        start = jnp.where(meta == float(e), offs_ref[e], start)
    shift = META_ROWS - (META_R1 - META_E1)
    pos_ref[...] = (start + pltpu.roll(meta, shift, axis=0)).astype(jnp.int32)


def moe_slots(meta_t, offsets):
    n_rows = meta_t.shape[1]
    tn = min(n_rows, 8192)
    blk = pl.BlockSpec((META_ROWS, tn), lambda i, offs: (0, i))
    return pl.pallas_call(
        _slot_kernel,
        grid_spec=pltpu.PrefetchScalarGridSpec(
            num_scalar_prefetch=1, grid=(n_rows // tn,), in_specs=[blk], out_specs=blk),
        out_shape=jax.ShapeDtypeStruct((META_ROWS, n_rows), jnp.int32),
        compiler_params=_params(("parallel",)),
        name="moe_slots",
    )(offsets, meta_t)


def _expert_kernel(te_ref, nu_ref, nv_ref, xs_ref, wg_ref, wu_ref, wd_ref, o_ref,
                   wgu_ref, wdb_ref):
    j = pl.program_id(0)

    @pl.when(jnp.logical_or(j == 0, te_ref[j] != te_ref[jnp.maximum(j - 1, 0)]))
    def _():
        wgu_ref[:, :MOE_D_FF] = wg_ref[0, 0].astype(BF16)
        wgu_ref[:, MOE_D_FF:] = wu_ref[0, 0].astype(BF16)
        wdb_ref[...] = wd_ref[0, 0].astype(BF16)

    @pl.when(j < nu_ref[0])
    def _():
        x = _unpack_bf16(xs_ref[...])
        row = lax.broadcasted_iota(jnp.int32, (x.shape[0], 1), 0)
        x = jnp.where(row < nv_ref[j], x, 0.0).astype(BF16)
        au = _dot(x, wgu_ref[...])
        hh = (_silu(au[:, :MOE_D_FF]) * au[:, MOE_D_FF:]).astype(BF16)
        o_ref[...] = _pack_bf16(_dot(hh, wdb_ref[...]))


def moe_experts(xs, tile_expert, n_used, n_valid, w_gate, w_up, w_down, layer, tm):
    n_slots, half = xs.shape
    wspec = lambda shape: pl.BlockSpec((1, 1) + shape, lambda j, te, nu, nv: (layer, te[j], 0, 0))
    row = pl.BlockSpec((tm, half), lambda j, te, nu, nv: (j, 0))
    row_in = pl.BlockSpec((tm, half), lambda j, te, nu, nv: (jnp.minimum(j, nu[0] - 1), 0))
    grid_spec = pltpu.PrefetchScalarGridSpec(
        num_scalar_prefetch=3,
        grid=(n_slots // tm,),
        in_specs=[row_in, wspec((D_MODEL, MOE_D_FF)), wspec((D_MODEL, MOE_D_FF)),
                  wspec((MOE_D_FF, D_MODEL))],
        out_specs=row,
        scratch_shapes=[pltpu.VMEM((D_MODEL, 2 * MOE_D_FF), BF16),
                        pltpu.VMEM((MOE_D_FF, D_MODEL), BF16)],
    )
    return pl.pallas_call(
        _expert_kernel,
        grid_spec=grid_spec,
        out_shape=jax.ShapeDtypeStruct((n_slots, half), jnp.uint32),
        compiler_params=_params(("arbitrary",)),
        name="moe_experts",
    )(tile_expert, n_used, n_valid, xs, w_gate, w_up, w_down)


def _combine_kernel(x_ref, ya_ref, yb_ref, meta_ref, mod_ref, o_ref):
    o_ref[...] = _moe_combined(x_ref, ya_ref, yb_ref, meta_ref, mod_ref[0][5:6])


def moe_combine(pending, seq_len):
    x, ya, yb, meta, mod = pending
    n_rows = x.shape[0]
    tm = STREAM_TILE
    half = D_MODEL // 2
    return pl.pallas_call(
        _combine_kernel,
        grid=(n_rows // tm,),
        in_specs=[_row_spec(tm), _row_spec(tm, half), _row_spec(tm, half), _row_spec(tm, LANES),
                  _mod_spec(tm, seq_len)],
        out_specs=_row_spec(tm),
        out_shape=jax.ShapeDtypeStruct((n_rows, D_MODEL), F32),
        compiler_params=_params(("parallel",)),
        name="moe_combine",
    )(x, ya, yb, meta, mod)


MOE_TILE = 512
MOE_TILE_LARGE = 1024
MOE_LARGE_ROWS = 32768


def hier_moe(routed, mod, w_gate, w_up, w_down, layer, seq_len):
    x, hp, meta, meta_t, counts = routed
    n_rows = x.shape[0]
    tm = MOE_TILE_LARGE if n_rows >= MOE_LARGE_ROWS else MOE_TILE
    cnt = counts[:, 0].astype(jnp.int32)
    padded = (cnt + tm - 1) // tm * tm
    ends = jnp.cumsum(padded)
    starts = ends - padded
    n_slots = 2 * n_rows + MOE_N_EXPERTS * tm
    tile_start = jnp.arange(n_slots // tm, dtype=jnp.int32) * tm
    tile_expert = jnp.minimum(jnp.sum(tile_start[:, None] >= ends[None, :], axis=1),
                              MOE_N_EXPERTS - 1).astype(jnp.int32)
    n_used = (ends[-1:] // tm).astype(jnp.int32)
    n_valid = jnp.clip(starts[tile_expert] + cnt[tile_expert] - tile_start, 0, tm).astype(jnp.int32)
    pos = moe_slots(meta_t, starts.astype(F32))
    idx0 = pos[META_E1].reshape(n_rows // SC_BLOCK, SC_BLOCK)
    idx1 = pos[META_E2].reshape(n_rows // SC_BLOCK, SC_BLOCK)
    xs = sc_scatter_rows(hp, idx0, idx1, n_slots)
    ys = moe_experts(xs, tile_expert, n_used, n_valid, w_gate, w_up, w_down, layer, tm)
    ya, yb = sc_gather_rows(ys, idx0, idx1)
    return _Pending(x, ya, yb, meta, mod)


SC_CORES = 2
SC_SUBCORES = 16
SC_WORKERS = SC_CORES * SC_SUBCORES
SC_BLOCK = 128


def _sc_mesh():
    return plsc.VectorSubcoreMesh(core_axis_name="c", subcore_axis_name="s")


def _sc_worker():
    return lax.axis_index("s") * SC_CORES + lax.axis_index("c")


def sc_scatter_rows(rows, idx0, idx1, n_slots):
    n_rows, width = rows.shape
    per_worker = n_rows // SC_BLOCK // SC_WORKERS

    @functools.partial(
        pl.kernel, mesh=_sc_mesh(),
        out_type=jax.ShapeDtypeStruct((n_slots, width), rows.dtype),
        scratch_types=[pltpu.VMEM((SC_BLOCK,), jnp.int32), pltpu.VMEM((SC_BLOCK,), jnp.int32),
                       pltpu.VMEM((SC_BLOCK, width), rows.dtype)],
        name="sc_scatter_rows",
    )
    def scatter(rows_hbm, i0_hbm, i1_hbm, out_hbm, i0_v, i1_v, rows_v):
        first = _sc_worker() * per_worker

        @pl.loop(0, per_worker)
        def _(j):
            blk = first + j
            pltpu.sync_copy(i0_hbm.at[blk], i0_v)
            pltpu.sync_copy(i1_hbm.at[blk], i1_v)
            pltpu.sync_copy(rows_hbm.at[pl.ds(blk * SC_BLOCK, SC_BLOCK)], rows_v)
            pltpu.sync_copy(rows_v, out_hbm.at[i0_v])
            pltpu.sync_copy(rows_v, out_hbm.at[i1_v])

    return scatter(rows, idx0, idx1)


def sc_gather_rows(src, idx0, idx1):
    width = src.shape[1]
    n_rows = idx0.shape[0] * SC_BLOCK
    per_worker = n_rows // SC_BLOCK // SC_WORKERS
    out = jax.ShapeDtypeStruct((n_rows, width), src.dtype)

    @functools.partial(
        pl.kernel, mesh=_sc_mesh(), out_type=(out, out),
        scratch_types=[pltpu.VMEM((SC_BLOCK,), jnp.int32), pltpu.VMEM((SC_BLOCK, width), src.dtype)],
        name="sc_gather_rows",
    )
    def gather(src_hbm, i0_hbm, i1_hbm, a_hbm, b_hbm, idx_v, rows_v):
        first = _sc_worker() * per_worker

        @pl.loop(0, per_worker)
        def _(j):
            blk = first + j
            dst = pl.ds(blk * SC_BLOCK, SC_BLOCK)
            pltpu.sync_copy(i0_hbm.at[blk], idx_v)
            pltpu.sync_copy(src_hbm.at[idx_v], rows_v)
            pltpu.sync_copy(rows_v, a_hbm.at[dst])
            pltpu.sync_copy(i1_hbm.at[blk], idx_v)
            pltpu.sync_copy(src_hbm.at[idx_v], rows_v)
            pltpu.sync_copy(rows_v, b_hbm.at[dst])

    return gather(src, idx0, idx1)


def _trunk(x3, mods, p):
    batch, seq_len, _ = x3.shape
    x = x3.reshape(batch * seq_len, D_MODEL)
    for i in range(DEPTH):
        mod = mods[i]
        g1 = p["norm1_g"][i]
        route = (p["norm2_g"][i], p["moe_w_group"][i], p["moe_b_group"][i], p["moe_w_router"][i],
                 p["moe_b_router"][i])
        kind = i % 4
        if kind == 0:
            if isinstance(x, _Pending):
                x = moe_combine(x, seq_len)
            q, k, v = attn_qkv(x, mod, g1, p["attn_wqkv"][0], p["attn_q_norm"][0],
                               p["attn_k_norm"][0], seq_len)
            o = attn_flash(q, k, v, batch, seq_len)
            routed = proj_residual(o, p["attn_wo"][0], x, mod, 2, seq_len, route)
        elif kind == 1:
            hp = {"w_in": p["hy_w_in"][0], "conv_w": p["hy_conv_w"][0], "conv_b": p["hy_conv_b"][0],
                  "w1": p["hy_ffn_w1"][0], "b1": p["hy_ffn_b1"][0], "w2": p["hy_ffn_w2"][0],
                  "b2": p["hy_ffn_b2"][0], "w3": p["hy_ffn_w3"][0], "freq": p["hy_freq"][0],
                  "skip": p["hy_skip"][0], "w_out": p["hy_w_out"][0]}
            routed = hyena_mixer(x, mod, g1, hp, batch, seq_len, route)
        elif kind == 2:
            routed = pool_mixer(x, mod, g1, p["pool_w"][0], p["pool_scale"][0], seq_len, route)
        else:
            routed = sconv_mixer(x, mod, g1, p["sc_w_in"][0], p["sc_conv_w"][0],
                                 p["sc_conv_b"][0], p["sc_w_out"][0], seq_len, route)
        x = hier_moe(routed, mod, p["moe_w_gate"], p["moe_w_up"], p["moe_w_down"], i, seq_len)
    return moe_combine(x, seq_len).reshape(batch, seq_len, D_MODEL)


def kernel(x_prompt, x_sample, c_prompt, c_sample, norm1_g, norm2_g, ada_w, ada_b, attn_wqkv, attn_q_norm, attn_k_norm, attn_wo, hy_w_in, hy_conv_w, hy_conv_b, hy_ffn_w1, hy_ffn_b1, hy_ffn_w2, hy_ffn_b2, hy_ffn_w3, hy_freq, hy_skip, hy_w_out, pool_w, pool_scale, sc_w_in, sc_conv_w, sc_conv_b, sc_w_out, moe_w_group, moe_b_group, moe_w_router, moe_b_router, moe_w_gate, moe_w_up, moe_w_down):
    p = dict(norm1_g=norm1_g, norm2_g=norm2_g, attn_wqkv=attn_wqkv, attn_q_norm=attn_q_norm,
             attn_k_norm=attn_k_norm, attn_wo=attn_wo, hy_w_in=hy_w_in, hy_conv_w=hy_conv_w,
             hy_conv_b=hy_conv_b, hy_ffn_w1=hy_ffn_w1, hy_ffn_b1=hy_ffn_b1, hy_ffn_w2=hy_ffn_w2,
             hy_ffn_b2=hy_ffn_b2, hy_ffn_w3=hy_ffn_w3, hy_freq=hy_freq, hy_skip=hy_skip,
             hy_w_out=hy_w_out, pool_w=pool_w, pool_scale=pool_scale, sc_w_in=sc_w_in,
             sc_conv_w=sc_conv_w, sc_conv_b=sc_conv_b, sc_w_out=sc_w_out, moe_w_group=moe_w_group,
             moe_b_group=moe_b_group, moe_w_router=moe_w_router, moe_b_router=moe_b_router,
             moe_w_gate=moe_w_gate, moe_w_up=moe_w_up, moe_w_down=moe_w_down)
    nb = c_prompt.shape[0]
    ns = c_sample.shape[0]
    rows = -(-(nb + ns) // HALO) * HALO
    c_all = jnp.pad(jnp.concatenate([c_prompt, c_sample], axis=0), ((0, rows - nb - ns), (0, 0)))
    mod = ada_mod(c_all, ada_w, ada_b).reshape(DEPTH, rows, 6, D_MODEL)
    mods_prompt = [mod[i, :nb] for i in range(DEPTH)]
    mods_sample = [mod[i, nb:nb + ns] for i in range(DEPTH)]
    return _trunk(x_prompt, mods_prompt, p), _trunk(x_sample, mods_sample, p)
```

```python
import functools
import math
from typing import NamedTuple

import jax
import jax.numpy as jnp
import numpy as np
from jax import lax
from jax.experimental import pallas as pl
from jax.experimental.pallas import tpu as pltpu
from jax.experimental.pallas import tpu_sc as plsc

F32 = jnp.float32
BF16 = jnp.bfloat16
F8 = jnp.float8_e4m3fn

D_MODEL = 1024
DEPTH = 4
EPS = 1e-6
GRID_W = 64
HEAD_DIM = 64
N_HEADS = 16
N_KV_HEADS = 4
Q_PER_KV = 4
ROPE_THETA = 10000.0
ROPE_FREQS = 16
HY_EMB_DIM = 33
HY_BANDS = 16
HY_FILTER_WIDTH = 64
HY_FAST_DECAY = 0.3
HY_SLOW_DECAY = 1.5
HY_TARGET = 1e-2
POOL_WINDOWS = (2, 4, 8, 16)
POOL_GROUP_DIM = 256
MOE_GROUPS = 4
MOE_EXPERTS_PER_GROUP = 8
MOE_N_EXPERTS = 32
MOE_D_FF = 256

LANES = 128
HALO = 8
DFT_N2 = 256
VMEM_LIMIT = 56 * 1024 * 1024

ROW_TILE = 512
STREAM_TILE = 1024


def _params(sem):
    return pltpu.CompilerParams(dimension_semantics=sem, vmem_limit_bytes=VMEM_LIMIT)


def _dot(a, b):
    return jnp.dot(a, b, preferred_element_type=F32)


def _split(a):
    hi = a.astype(BF16)
    lo = (a - hi.astype(F32)).astype(BF16)
    return hi, lo


def _dot3(a, b):
    ah, al = _split(a)
    bh, bl = _split(b)
    return _dot(ah, bh) + (_dot(ah, bl) + _dot(al, bh))


def _modulate(x, g, shift, scale):
    ms = jnp.mean(x * x, axis=-1, keepdims=True)
    return x * lax.rsqrt(ms + EPS) * g * (1.0 + scale) + shift


def _silu(x):
    return x * (1.0 / (1.0 + jnp.exp(-x)))


def _ada_kernel(c_ref, w_ref, b_ref, o_ref):
    c = c_ref[...]
    o_ref[0] = _dot3(_silu(c), w_ref[0]) + b_ref[0]


def ada_mod(c_all, ada_w, ada_b):
    rows = c_all.shape[0]
    n = ada_w.shape[2]
    tn = 1536
    return pl.pallas_call(
        _ada_kernel,
        grid=(DEPTH, n // tn),
        in_specs=[
            pl.BlockSpec((rows, D_MODEL), lambda l, j: (0, 0)),
            pl.BlockSpec((1, D_MODEL, tn), lambda l, j: (l, 0, j)),
            pl.BlockSpec((1, 1, tn), lambda l, j: (l, 0, j)),
        ],
        out_specs=pl.BlockSpec((1, rows, tn), lambda l, j: (l, 0, j)),
        out_shape=jax.ShapeDtypeStruct((DEPTH, rows, n), F32),
        compiler_params=_params(("parallel", "parallel")),
        name="ada_mod",
    )(c_all, ada_w, ada_b.reshape(DEPTH, 1, n))


def _row_spec(tm, width=D_MODEL):
    return pl.BlockSpec((tm, width), lambda i: (i, 0))


def _mod_spec(tm, seq_len):
    return pl.BlockSpec((1, 6, D_MODEL), lambda i: ((i * tm) // seq_len, 0, 0))


def _const_spec(shape):
    nd = len(shape)
    return pl.BlockSpec(shape, lambda i: (0,) * nd)


def _halo_specs(tm, n_rows, width=D_MODEL):
    per = tm // HALO
    last = n_rows // HALO - 1
    prev = pl.BlockSpec((HALO, width), lambda i: (jnp.maximum(i * per - 1, 0), 0))
    nxt = pl.BlockSpec((HALO, width), lambda i: (jnp.minimum((i + 1) * per, last), 0))
    return prev, nxt


class _Pending(NamedTuple):
    x: jax.Array
    ya: jax.Array
    yb: jax.Array
    meta: jax.Array
    mod: jax.Array


N_PLAIN_REFS = 3
N_PENDING_REFS = 13


def _stream_io(src, tm, n_rows, seq_len):
    if not isinstance(src, _Pending):
        prev, nxt = _halo_specs(tm, n_rows)
        return [src] * 3, [prev, _row_spec(tm), nxt]
    operands, specs = [], []
    for arr in (src.x, src.ya, src.yb, src.meta):
        width = arr.shape[1]
        prev, nxt = _halo_specs(tm, n_rows, width)
        operands += [arr] * 3
        specs += [prev, _row_spec(tm, width), nxt]
    return operands + [src.mod], specs + [_mod_spec(tm, seq_len)]


def _moe_combined(x_ref, ya_ref, yb_ref, meta_ref, gate):
    meta = meta_ref[...]
    y = (meta[:, META_W1:META_W1 + 1] * _unpack_bf16(ya_ref[...])
         + meta[:, META_W2:META_W2 + 1] * _unpack_bf16(yb_ref[...]))
    return x_ref[...] + gate * y


def _stream_rows(refs):
    if len(refs) == N_PLAIN_REFS:
        prev_ref, x_ref, next_ref = refs
        x = x_ref[...]
        return x, jnp.concatenate([prev_ref[...], x, next_ref[...]], axis=0)
    gate = refs[-1][0][5:6]
    parts = [_moe_combined(refs[k], refs[3 + k], refs[6 + k], refs[9 + k], gate) for k in range(3)]
    return parts[1], jnp.concatenate(parts, axis=0)


def _edge_flags(tm, seq_len):
    i = pl.program_id(0)
    per_seq = seq_len // tm
    pos = i % per_seq
    return pos == 0, pos == per_seq - 1


def _shift_rows(u, tm):
    n = u.shape[0]
    up = pltpu.roll(u, 1, axis=0)[HALO:HALO + tm]
    dn = pltpu.roll(u, n - 1, axis=0)[HALO:HALO + tm]
    return up, u[HALO:HALO + tm], dn


def _conv3(u, w_ref, b_ref, cols, tm, first, last, stage_ref=None):
    if stage_ref is None:
        up, mid, dn = _shift_rows(u, tm)
    else:
        stage_ref[...] = u
        up = stage_ref[pl.ds(HALO - 1, tm), :]
        mid = stage_ref[pl.ds(HALO, tm), :]
        dn = stage_ref[pl.ds(HALO + 1, tm), :]
    row = lax.broadcasted_iota(jnp.int32, (tm, 1), 0)
    up = jnp.where(jnp.logical_and(first, row == 0), 0.0, up)
    dn = jnp.where(jnp.logical_and(last, row == tm - 1), 0.0, dn)
    w = w_ref[:, cols]
    return up * w[0:1] + mid * w[1:2] + dn * w[2:3] + b_ref[:, cols]


def _norm_rope(t, gain, headmean, cos, sin_signed):
    width = t.shape[1]
    ms = _dot((t * t).astype(BF16), headmean[:width, :width])
    y = t * lax.rsqrt(ms + EPS) * gain
    lane = lax.broadcasted_iota(jnp.int32, y.shape, 1)
    first = (lane % 32) < ROPE_FREQS
    partner = jnp.where(first, pltpu.roll(y, width - ROPE_FREQS, axis=1),
                        pltpu.roll(y, ROPE_FREQS, axis=1))
    reps = width // LANES
    return y * jnp.tile(cos, (1, reps)) + partner * jnp.tile(sin_signed, (1, reps))


def _qkv_kernel(x_ref, mod_ref, g_ref, w_ref, qg_ref, kg_ref, hm_ref, cos_ref, sin_ref,
                q_ref, k_ref, v_ref):
    m = mod_ref[0]
    h = _modulate(x_ref[...], g_ref[...], m[0:1], m[1:2]).astype(BF16)
    qkv = _dot(h, w_ref[...])
    nq = N_HEADS * HEAD_DIM
    nk = N_KV_HEADS * HEAD_DIM
    cos = cos_ref[...]
    sin = sin_ref[...]
    hm = hm_ref[...]
    q = _norm_rope(qkv[:, :nq], qg_ref[...], hm, cos, sin)
    k = _norm_rope(qkv[:, nq:nq + nk], kg_ref[...], hm, cos, sin)
    v = qkv[:, nq + nk:]
    q_ref[...] = q.astype(BF16)
    ones = jnp.ones((v.shape[0], HEAD_DIM), F32)
    for g in range(N_KV_HEADS):
        sl = slice(g * HEAD_DIM, (g + 1) * HEAD_DIM)
        k_ref[g] = k[:, sl].astype(F8)
        v_ref[g] = jnp.concatenate([v[:, sl], ones], axis=1).astype(BF16)


def _rope_tables(seq_len):
    rows = seq_len // GRID_W
    r = jnp.broadcast_to(jnp.arange(rows)[:, None], (rows, GRID_W)).reshape(-1)
    c = jnp.broadcast_to(jnp.arange(GRID_W)[None, :], (rows, GRID_W)).reshape(-1)
    inv_freq = ROPE_THETA ** (-jnp.arange(ROPE_FREQS, dtype=F32) / ROPE_FREQS)
    pos = jnp.stack([r, c], axis=-1).astype(F32)
    ang = pos[:, :, None] * inv_freq[None, None, :]
    cos = jnp.cos(ang)
    sin = jnp.sin(ang)
    cos64 = jnp.concatenate([cos, cos], axis=-1).reshape(seq_len, HEAD_DIM)
    sin64 = jnp.concatenate([-sin, sin], axis=-1).reshape(seq_len, HEAD_DIM)
    return jnp.tile(cos64, (1, 2)), jnp.tile(sin64, (1, 2))


def attn_qkv(x, mod, norm_g, wqkv, q_norm, k_norm, seq_len):
    n_rows = x.shape[0]
    tm = ROW_TILE
    nq = N_HEADS * HEAD_DIM
    nk = N_KV_HEADS * HEAD_DIM
    cos, sin = _rope_tables(seq_len)
    qg = jnp.tile(q_norm, N_HEADS)[None, :] * (HEAD_DIM ** -0.5 * math.log2(math.e))
    kg = jnp.tile(k_norm, N_KV_HEADS)[None, :]
    head = np.arange(nq) // HEAD_DIM
    headmean = jnp.asarray((head[:, None] == head[None, :]).astype(np.float32) / HEAD_DIM, BF16)
    per_seq = seq_len // tm
    tab_spec = pl.BlockSpec((tm, LANES), lambda i: (i % per_seq, 0))
    return pl.pallas_call(
        _qkv_kernel,
        grid=(n_rows // tm,),
        in_specs=[
            _row_spec(tm), _mod_spec(tm, seq_len), _const_spec((1, D_MODEL)),
            _const_spec((D_MODEL, nq + 2 * nk)), _const_spec((1, nq)), _const_spec((1, nk)),
            _const_spec((nq, nq)), tab_spec, tab_spec,
        ],
        out_specs=[
            _row_spec(tm, nq),
            pl.BlockSpec((N_KV_HEADS, tm, HEAD_DIM), lambda i: (0, i, 0)),
            pl.BlockSpec((N_KV_HEADS, tm, 2 * HEAD_DIM), lambda i: (0, i, 0)),
        ],
        out_shape=[
            jax.ShapeDtypeStruct((n_rows, nq), BF16),
            jax.ShapeDtypeStruct((N_KV_HEADS, n_rows, HEAD_DIM), F8),
            jax.ShapeDtypeStruct((N_KV_HEADS, n_rows, 2 * HEAD_DIM), BF16),
        ],
        compiler_params=_params(("parallel",)),
        name="attn_qkv",
    )(x, mod, norm_g[None, :], wqkv.astype(BF16), qg, kg, headmean, cos, sin)


def _flash_kernel(q_ref, k_ref, v_ref, o_ref, *, tq, tk, n_chunks):
    q = q_ref[...]
    qs = jnp.concatenate([q[:, j * HEAD_DIM:(j + 1) * HEAD_DIM] for j in range(Q_PER_KV)], axis=0)
    qs = qs.astype(F8)
    rows = Q_PER_KV * tq

    def body(c, carry):
        m, acc = carry
        start = pl.multiple_of(c * tk, tk)
        kc = k_ref[0, pl.ds(start, tk), :]
        vc = v_ref[0, pl.ds(start, tk), :]
        s = lax.dot_general(qs, kc, (((1,), (1,)), ((), ())), preferred_element_type=F32)
        m_new = jnp.maximum(m, jnp.max(s, axis=-1, keepdims=True))
        alpha = jnp.exp2(m - m_new)
        p = jnp.exp2(s - m_new)
        acc = acc * alpha + _dot(p.astype(BF16), vc)
        return m_new, acc

    m0 = jnp.full((rows, 1), -jnp.inf, F32)
    acc0 = jnp.zeros((rows, 2 * HEAD_DIM), F32)
    _, acc = lax.fori_loop(0, n_chunks, body, (m0, acc0))
    o = acc[:, :HEAD_DIM] / acc[:, HEAD_DIM:HEAD_DIM + 1]
    o_ref[...] = jnp.concatenate([o[j * tq:(j + 1) * tq] for j in range(Q_PER_KV)],
                                 axis=1).astype(BF16)


def attn_flash(q, k, v, batch, seq_len):
    n_rows = q.shape[0]
    tq = 256
    tk = min(seq_len, 2048)
    per_seq = seq_len // tq
    width = Q_PER_KV * HEAD_DIM
    kern = functools.partial(_flash_kernel, tq=tq, tk=tk, n_chunks=seq_len // tk)
    return pl.pallas_call(
        kern,
        grid=(batch, N_KV_HEADS, per_seq),
        in_specs=[
            pl.BlockSpec((tq, width), lambda b, g, i: (b * per_seq + i, g)),
            pl.BlockSpec((1, seq_len, HEAD_DIM), lambda b, g, i: (g, b, 0)),
            pl.BlockSpec((1, seq_len, 2 * HEAD_DIM), lambda b, g, i: (g, b, 0)),
        ],
        out_specs=pl.BlockSpec((tq, width), lambda b, g, i: (b * per_seq + i, g)),
        out_shape=jax.ShapeDtypeStruct((n_rows, N_HEADS * HEAD_DIM), BF16),
        compiler_params=_params(("parallel", "parallel", "parallel")),
        name="attn_flash",
    )(q, k, v)


def _proj_res_kernel(y_ref, w_ref, x_ref, mod_ref, *refs, gate_row):
    route_in, o_ref, route_out = refs[:N_ROUTE_IN], refs[N_ROUTE_IN], refs[N_ROUTE_IN + 1:]
    m = mod_ref[0]
    x1 = x_ref[...] + m[gate_row:gate_row + 1] * _dot(y_ref[...], w_ref[...])
    o_ref[...] = x1
    _route_tile(x1, m, *route_in, *route_out)


def proj_residual(y, w, x, mod, gate_row, seq_len, route):
    n_rows = x.shape[0]
    tm = STREAM_TILE
    r_ops, r_in, r_out, r_shape = _router_io(route, tm, n_rows)
    return pl.pallas_call(
        functools.partial(_proj_res_kernel, gate_row=gate_row),
        grid=(n_rows // tm,),
        in_specs=[_row_spec(tm, y.shape[1]), _const_spec(w.shape), _row_spec(tm),
                  _mod_spec(tm, seq_len)] + r_in,
        out_specs=[_row_spec(tm)] + r_out,
        out_shape=[jax.ShapeDtypeStruct((n_rows, D_MODEL), F32)] + r_shape,
        compiler_params=_params(("arbitrary",)),
        name="proj_residual",
    )(y, w.astype(BF16), x, mod, *r_ops)


CONV_COLS = 256


def _hy_in_kernel(*refs, tm, seq_len, n_stream):
    stream, refs = refs[:n_stream], refs[n_stream:]
    mod_ref, g_ref, w_ref, cw_ref, cb_ref, z_ref, x0_ref = refs[:7]
    stage_refs = refs[-3:]
    first, last = _edge_flags(tm, seq_len)
    m = mod_ref[0]
    x, ext = _stream_rows(stream)
    if n_stream == N_PENDING_REFS:
        refs[7][...] = x
    h = _modulate(ext, g_ref[...], m[0:1], m[1:2]).astype(BF16)
    tn = CONV_COLS
    for j in range(D_MODEL // tn):
        part = []
        for s in range(3):
            cols = slice(s * D_MODEL + j * tn, s * D_MODEL + (j + 1) * tn)
            part.append(_conv3(_dot(h, w_ref[:, cols]), cw_ref, cb_ref, cols, tm, first, last,
                               stage_refs[s]))
        out_cols = slice(j * tn, (j + 1) * tn)
        x0_ref[:, out_cols] = part[0].astype(BF16)
        z_ref[:, out_cols] = (part[2] * part[1]).astype(BF16)


def hyena_in(src, mod, norm_g, w_in, conv_w, conv_b, seq_len):
    pending = isinstance(src, _Pending)
    n_rows = (src.x if pending else src).shape[0]
    tm = ROW_TILE
    s_ops, s_specs = _stream_io(src, tm, n_rows, seq_len)
    x_out = [jax.ShapeDtypeStruct((n_rows, D_MODEL), F32)] if pending else []
    return pl.pallas_call(
        functools.partial(_hy_in_kernel, tm=tm, seq_len=seq_len, n_stream=len(s_ops)),
        grid=(n_rows // tm,),
        in_specs=s_specs + [_mod_spec(tm, seq_len), _const_spec((1, D_MODEL)),
                            _const_spec((D_MODEL, 3 * D_MODEL)), _const_spec((3, 3 * D_MODEL)),
                            _const_spec((1, 3 * D_MODEL))],
        out_specs=[_row_spec(tm)] * (2 + len(x_out)),
        out_shape=[jax.ShapeDtypeStruct((n_rows, D_MODEL), BF16),
                   jax.ShapeDtypeStruct((n_rows, D_MODEL), BF16)] + x_out,
        scratch_shapes=[pltpu.VMEM((tm + 2 * HALO, CONV_COLS), F32)] * 3,
        compiler_params=_params(("parallel",)),
        name="hyena_in",
    )(*s_ops, mod, norm_g[None, :], w_in.astype(BF16), conv_w, conv_b[None, :])


def _hy_filter_kernel(feat_ref, w1_ref, b1_ref, w2_ref, b2_ref, w3_ref, fr_ref, dl_ref, o_ref):
    feat = feat_ref[...]
    fr = fr_ref[...]
    a = jnp.sin(fr * (_dot3(feat, w1_ref[...]) + b1_ref[...]))
    a = jnp.sin(fr * (_dot3(a, w2_ref[...]) + b2_ref[...]))
    hf = _dot3(a, w3_ref[...])
    decay = jnp.exp(-feat[:, 0:1] * dl_ref[...])
    o_ref[0] = hf[:, :D_MODEL] * decay
    o_ref[1] = hf[:, D_MODEL:] * decay


def _pad_to(a, rows, cols):
    return jnp.pad(a.astype(F32), ((0, rows - a.shape[0]), (0, cols - a.shape[1])))


def hyena_filter(seq_len, w1, b1, w2, b2, w3, freq):
    t = jnp.linspace(0.0, 1.0, seq_len, dtype=F32)[:, None]
    w = 2.0 * math.pi * jnp.arange(seq_len, dtype=F32)[:, None] / seq_len
    f = jnp.linspace(1e-4, HY_BANDS - 1, HY_BANDS, dtype=F32)[None, :]
    feat = _pad_to(jnp.concatenate([t, jnp.cos(f * w), -jnp.sin(f * w)], axis=-1), seq_len, LANES)
    max_decay = math.log(HY_TARGET) / HY_FAST_DECAY
    min_decay = math.log(HY_TARGET) / HY_SLOW_DECAY
    absdelta = jnp.abs(jnp.linspace(min_decay, max_decay, D_MODEL, dtype=F32))[None, :]
    tl = 512
    return pl.pallas_call(
        _hy_filter_kernel,
        grid=(seq_len // tl,),
        in_specs=[_row_spec(tl, LANES), _const_spec((LANES, LANES)), _const_spec((1, LANES)),
                  _const_spec((LANES, LANES)), _const_spec((1, LANES)),
                  _const_spec((LANES, 2 * D_MODEL)), _const_spec((1, LANES)),
                  _const_spec((1, D_MODEL))],
        out_specs=pl.BlockSpec((2, tl, D_MODEL), lambda i: (0, i, 0)),
        out_shape=jax.ShapeDtypeStruct((2, seq_len, D_MODEL), F32),
        compiler_params=_params(("parallel",)),
        name="hyena_filter",
    )(feat, _pad_to(w1, LANES, LANES), _pad_to(b1[None, :], 1, LANES), _pad_to(w2, LANES, LANES),
      _pad_to(b2[None, :], 1, LANES), _pad_to(w3, LANES, 2 * D_MODEL),
      _pad_to(freq[None, :], 1, LANES), absdelta)


class _FFTPlan:
    def __init__(self, seq_len):
        self.n = 2 * seq_len
        self.n1 = self.n // DFT_N2
        self.r = self.n1 // 2
        self.k1n = self.n1 // 2 + 1
        self.kron = max(HALO, LANES // self.r)
        ang = 2.0 * np.pi * np.outer(np.arange(self.k1n), np.arange(self.r)) / self.n1
        eye = np.eye(self.kron)
        self.fwd_cos = jnp.asarray(np.kron(np.cos(ang), eye), BF16)
        self.fwd_sin = jnp.asarray(np.kron(-np.sin(ang), eye), BF16)
        wgt = np.full((self.k1n,), 2.0)
        wgt[0] = wgt[-1] = 1.0
        scale = (wgt / self.n)[None, :]
        self.inv_cos = jnp.asarray(np.kron(np.cos(ang).T * scale, eye), BF16)
        self.inv_sin = jnp.asarray(np.kron(-np.sin(ang).T * scale, eye), BF16)
        a2 = 2.0 * np.pi * np.outer(np.arange(DFT_N2), np.arange(DFT_N2)) / DFT_N2
        self.f_cos = jnp.asarray(np.cos(a2), F32)
        self.f_sin = jnp.asarray(-np.sin(a2), F32)
        tw = 2.0 * np.pi * np.outer(np.arange(self.k1n), np.arange(DFT_N2)) / self.n
        self.tw_cos = jnp.asarray(np.cos(tw), F32)
        self.tw_sin = jnp.asarray(-np.sin(tw), F32)


def _fft_a_kernel(z_ref, wc_ref, ws_ref, ar_ref, ai_ref, *, rq):
    shape = ar_ref.shape[1:2] + ar_ref.shape[3:]
    for t in range(z_ref.shape[2]):
        z = z_ref[0, :, t].astype(F32).reshape(rq, D_MODEL).astype(BF16)
        ar_ref[0, :, t] = _dot(wc_ref[...], z).reshape(shape).astype(BF16)
        ai_ref[0, :, t] = _dot(ws_ref[...], z).reshape(shape).astype(BF16)


def _fft_group(plan):
    nhi = DFT_N2 // plan.kron
    per_group = plan.r * plan.kron * D_MODEL * 4
    return max(1, min(nhi, (2 << 20) // per_group))


def fft_stage_a(z, plan, batch):
    q = plan.kron
    nhi = DFT_N2 // q
    hb = _fft_group(plan)
    zv = z.reshape(batch, plan.r, nhi, q, D_MODEL)
    out = jax.ShapeDtypeStruct((batch, plan.k1n, nhi, q, D_MODEL), BF16)
    ospec = pl.BlockSpec((1, plan.k1n, hb, q, D_MODEL), lambda b, h: (b, 0, h, 0, 0))
    wspec = pl.BlockSpec(plan.fwd_cos.shape, lambda b, h: (0, 0))
    ar, ai = pl.pallas_call(
        functools.partial(_fft_a_kernel, rq=plan.r * q),
        grid=(batch, nhi // hb),
        in_specs=[pl.BlockSpec((1, plan.r, hb, q, D_MODEL), lambda b, h: (b, 0, h, 0, 0)),
                  wspec, wspec],
        out_specs=[ospec, ospec],
        out_shape=[out, out],
        compiler_params=_params(("parallel", "parallel")),
        name="fft_stage_a",
    )(zv, plan.fwd_cos, plan.fwd_sin)
    shape = (batch, plan.k1n, DFT_N2, D_MODEL)
    return ar.reshape(shape), ai.reshape(shape)


def _twiddled_dft(fr, fi, tr, ti):
    return (fr * tr - fi * ti).astype(BF16), (fr * ti + fi * tr).astype(BF16)


def _fft_b_fwd_kernel(ar_ref, ai_ref, fr_ref, fi_ref, twr_ref, twi_ref, br_ref, bi_ref):
    n = DFT_N2
    gr, gi = _twiddled_dft(fr_ref[...], fi_ref[...], twr_ref[0], twi_ref[0])
    fwd = jnp.concatenate([jnp.concatenate([gr, -gi], axis=1),
                           jnp.concatenate([gi, gr], axis=1)], axis=0)
    for s in range(ar_ref.shape[0]):
        b = _dot(fwd, jnp.concatenate([ar_ref[s, 0], ai_ref[s, 0]], axis=0))
        br_ref[s, 0] = b[:n].astype(BF16)
        bi_ref[s, 0] = b[n:].astype(BF16)


def fft_stage_b_fwd(ar, ai, plan):
    batch = ar.shape[0]
    blk = pl.BlockSpec((batch, 1, DFT_N2, D_MODEL), lambda k: (0, k, 0, 0))
    cst = pl.BlockSpec((DFT_N2, DFT_N2), lambda k: (0, 0))
    tws = pl.BlockSpec((1, 1, DFT_N2), lambda k: (k, 0, 0))
    out = jax.ShapeDtypeStruct(ar.shape, BF16)
    return pl.pallas_call(
        _fft_b_fwd_kernel,
        grid=(plan.k1n,),
        in_specs=[blk, blk, cst, cst, tws, tws],
        out_specs=[blk, blk],
        out_shape=[out, out],
        compiler_params=_params(("parallel",)),
        name="fft_stage_b_fwd",
    )(ar, ai, plan.f_cos, plan.f_sin, plan.tw_cos[:, None, :], plan.tw_sin[:, None, :])


def _fft_b_conv_kernel(ar_ref, ai_ref, hr_ref, hi_ref, hb0_ref, fr_ref, fi_ref,
                       twr_ref, twi_ref, tcr_ref, tci_ref, cr_ref, ci_ref,
                       fwd_ref, inv_ref, kr_ref, ki_ref):
    n = DFT_N2

    @pl.when(pl.program_id(1) == 0)
    def _():
        fr = fr_ref[...]
        fi = fi_ref[...]
        gr, gi = _twiddled_dft(fr, fi, twr_ref[0], twi_ref[0])
        fwd_ref[:n, :n] = gr
        fwd_ref[:n, n:] = -gi
        fwd_ref[n:, :n] = gi
        fwd_ref[n:, n:] = gr
        gtr, gti = _twiddled_dft(fr, fi, tcr_ref[0], tci_ref[0])
        inv_ref[:n, :n] = gtr
        inv_ref[:n, n:] = gti
        inv_ref[n:, :n] = -gti
        inv_ref[n:, n:] = gtr
        kr_ref[...] = hr_ref[0, 0].astype(F32) + hr_ref[1, 0].astype(F32) - hb0_ref[...]
        ki_ref[...] = hi_ref[0, 0].astype(F32) - hi_ref[1, 0].astype(F32)

    a = jnp.concatenate([ar_ref[0, 0], ai_ref[0, 0]], axis=0)
    b = _dot(fwd_ref[...], a)
    br, bi = b[:n], b[n:]
    kr = kr_ref[...]
    ki = ki_ref[...]
    p = jnp.concatenate([(br * kr - bi * ki).astype(BF16), (br * ki + bi * kr).astype(BF16)], axis=0)
    c = _dot(inv_ref[...], p)
    cr_ref[0, 0] = c[:n].astype(BF16)
    ci_ref[0, 0] = c[n:].astype(BF16)


def fft_stage_b_conv(ar, ai, hr, hi, hb0, plan):
    batch = ar.shape[0]
    blk = pl.BlockSpec((1, 1, DFT_N2, D_MODEL), lambda k, b: (b, k, 0, 0))
    hblk = pl.BlockSpec((2, 1, DFT_N2, D_MODEL), lambda k, b: (0, k, 0, 0))
    cst = pl.BlockSpec((DFT_N2, DFT_N2), lambda k, b: (0, 0))
    tws = pl.BlockSpec((1, 1, DFT_N2), lambda k, b: (k, 0, 0))
    twc = pl.BlockSpec((1, DFT_N2, 1), lambda k, b: (k, 0, 0))
    out = jax.ShapeDtypeStruct(ar.shape, BF16)
    return pl.pallas_call(
        _fft_b_conv_kernel,
        grid=(plan.k1n, batch),
        in_specs=[blk, blk, hblk, hblk, pl.BlockSpec((1, D_MODEL), lambda k, b: (0, 0)),
                  cst, cst, tws, tws, twc, twc],
        out_specs=[blk, blk],
        out_shape=[out, out],
        scratch_shapes=[pltpu.VMEM((2 * DFT_N2, 2 * DFT_N2), BF16)] * 2
        + [pltpu.VMEM((DFT_N2, D_MODEL), F32)] * 2,
        compiler_params=_params(("parallel", "arbitrary")),
        name="fft_stage_b_conv",
    )(ar, ai, hr, hi, hb0, plan.f_cos, plan.f_sin,
      plan.tw_cos[:, None, :], plan.tw_sin[:, None, :],
      plan.tw_cos[:, :, None], plan.tw_sin[:, :, None])


def _fft_a_inv_kernel(cr_ref, ci_ref, vc_ref, vs_ref, z_ref, x0_ref, skip_ref, y_ref, *, kq):
    shape = z_ref.shape[1:2] + z_ref.shape[3:]
    for t in range(z_ref.shape[2]):
        cr = cr_ref[0, :, t].astype(F32).reshape(kq, D_MODEL).astype(BF16)
        ci = ci_ref[0, :, t].astype(F32).reshape(kq, D_MODEL).astype(BF16)
        conv = _dot(vc_ref[...], cr) + _dot(vs_ref[...], ci)
        y = conv.reshape(shape) + z_ref[0, :, t].astype(F32) * skip_ref[...]
        y_ref[0, :, t] = (y * x0_ref[0, :, t].astype(F32)).astype(BF16)


def fft_stage_a_inv(cr, ci, z, x0, skip, plan, batch):
    q = plan.kron
    nhi = DFT_N2 // q
    hb = _fft_group(plan)
    cshape = (batch, plan.k1n, nhi, q, D_MODEL)
    tshape = (batch, plan.r, nhi, q, D_MODEL)
    cspec = pl.BlockSpec((1, plan.k1n, hb, q, D_MODEL), lambda b, h: (b, 0, h, 0, 0))
    tspec = pl.BlockSpec((1, plan.r, hb, q, D_MODEL), lambda b, h: (b, 0, h, 0, 0))
    wspec = pl.BlockSpec(plan.inv_cos.shape, lambda b, h: (0, 0))
    y = pl.pallas_call(
        functools.partial(_fft_a_inv_kernel, kq=plan.k1n * q),
        grid=(batch, nhi // hb),
        in_specs=[cspec, cspec, wspec, wspec, tspec, tspec,
                  pl.BlockSpec((1, D_MODEL), lambda b, h: (0, 0))],
        out_specs=tspec,
        out_shape=jax.ShapeDtypeStruct(tshape, BF16),
        compiler_params=_params(("parallel", "parallel")),
        name="fft_stage_a_inv",
    )(cr.reshape(cshape), ci.reshape(cshape), plan.inv_cos, plan.inv_sin,
      z.reshape(tshape), x0.reshape(tshape), skip[None, :])
    return y.reshape(z.shape)


def hyena_mixer(src, mod, norm_g, p, batch, seq_len, route):
    z, x0, *rest = hyena_in(src, mod, norm_g, p["w_in"], p["conv_w"], p["conv_b"], seq_len)
    x = rest[0] if rest else src
    plan = _FFTPlan(seq_len)
    filt = hyena_filter(seq_len, p["w1"], p["b1"], p["w2"], p["b2"], p["w3"], p["freq"])
    fr, fi = fft_stage_a(filt.reshape(2 * seq_len, D_MODEL), plan, 2)
    hr, hi = fft_stage_b_fwd(fr, fi, plan)
    ar, ai = fft_stage_a(z, plan, batch)
    cr, ci = fft_stage_b_conv(ar, ai, hr, hi, filt[1, 0:1, :], plan)
    y = fft_stage_a_inv(cr, ci, z, x0, p["skip"], plan, batch)
    return proj_residual(y, p["w_out"], x, mod, 2, seq_len, route)


def _pool_kernel(*refs, tm, seq_len, n_stream):
    stream, refs = refs[:n_stream], refs[n_stream:]
    mod_ref, g_ref, w_ref, s_ref = refs[:4]
    refs = refs[4:]
    route_in, o_ref, route_out = refs[:N_ROUTE_IN], refs[N_ROUTE_IN], refs[N_ROUTE_IN + 1:]
    first, last = _edge_flags(tm, seq_len)
    m = mod_ref[0]
    x, ext = _stream_rows(stream)
    h = _modulate(ext, g_ref[...], m[0:1], m[1:2])
    n = tm + 2 * HALO
    row = lax.broadcasted_iota(jnp.int32, (n, 1), 0)
    outside = jnp.logical_or(jnp.logical_and(first, row < HALO),
                             jnp.logical_and(last, row >= HALO + tm))
    h = jnp.where(outside, 0.0, h)
    pos = (pl.program_id(0) * tm) % seq_len + lax.broadcasted_iota(jnp.int32, (tm, 1), 0)
    ys = []
    for gi, win in enumerate(POOL_WINDOWS):
        cols = slice(gi * POOL_GROUP_DIM, (gi + 1) * POOL_GROUP_DIM)
        hg = h[:, cols]
        acc = hg
        span = 1
        while span < win:
            acc = acc + pltpu.roll(acc, span, axis=0)
            span *= 2
        lead = win // 2 - 1
        if lead:
            acc = pltpu.roll(acc, n - lead, axis=0)
        half = win // 2
        cnt = jnp.minimum(pos + half, seq_len) - jnp.maximum(pos - half, 0)
        pooled = acc[HALO:HALO + tm] / cnt.astype(F32) - hg[HALO:HALO + tm]
        ys.append(_dot(pooled.astype(BF16), w_ref[gi]))
    y = jnp.concatenate(ys, axis=1) * s_ref[...]
    x1 = x + m[2:3] * y
    o_ref[...] = x1
    _route_tile(x1, m, *route_in, *route_out)


def pool_mixer(src, mod, norm_g, w_group, scale, seq_len, route):
    n_rows = (src.x if isinstance(src, _Pending) else src).shape[0]
    tm = ROW_TILE
    s_ops, s_specs = _stream_io(src, tm, n_rows, seq_len)
    r_ops, r_in, r_out, r_shape = _router_io(route, tm, n_rows)
    return pl.pallas_call(
        functools.partial(_pool_kernel, tm=tm, seq_len=seq_len, n_stream=len(s_ops)),
        grid=(n_rows // tm,),
        in_specs=s_specs + [_mod_spec(tm, seq_len), _const_spec((1, D_MODEL)),
                            _const_spec(w_group.shape), _const_spec((1, D_MODEL))] + r_in,
        out_specs=[_row_spec(tm)] + r_out,
        out_shape=[jax.ShapeDtypeStruct((n_rows, D_MODEL), F32)] + r_shape,
        compiler_params=_params(("arbitrary",)),
        name="pool_mixer",
    )(*s_ops, mod, norm_g[None, :], w_group.astype(BF16), scale[None, :], *r_ops)


def _sconv_kernel(*refs, tm, seq_len, n_stream):
    stream, refs = refs[:n_stream], refs[n_stream:]
    mod_ref, g_ref, w_ref, cw_ref, cb_ref, wo_ref = refs[:6]
    refs = refs[6:]
    route_in, o_ref = refs[:N_ROUTE_IN], refs[N_ROUTE_IN]
    route_out, y_ref = refs[N_ROUTE_IN + 1:N_ROUTE_IN + 1 + N_ROUTE_OUT], refs[-1]
    first, last = _edge_flags(tm, seq_len)
    m = mod_ref[0]
    x, ext = _stream_rows(stream)
    h = _modulate(ext, g_ref[...], m[0:1], m[1:2]).astype(BF16)
    tn = 256
    for j in range(D_MODEL // tn):
        cols = slice(j * tn, (j + 1) * tn)
        bg = _dot(h, w_ref[:, cols])[HALO:HALO + tm]
        cg = _dot(h, w_ref[:, D_MODEL + j * tn:D_MODEL + (j + 1) * tn])
        hp = _dot(h, w_ref[:, 2 * D_MODEL + j * tn:2 * D_MODEL + (j + 1) * tn])
        y_ref[:, cols] = (bg * _conv3(cg * hp, cw_ref, cb_ref, cols, tm, first, last)).astype(BF16)
    x1 = x + m[2:3] * _dot(y_ref[...], wo_ref[...])
    o_ref[...] = x1
    _route_tile(x1, m, *route_in, *route_out)


def sconv_mixer(src, mod, norm_g, w_in, conv_w, conv_b, w_out, seq_len, route):
    n_rows = (src.x if isinstance(src, _Pending) else src).shape[0]
    tm = ROW_TILE
    s_ops, s_specs = _stream_io(src, tm, n_rows, seq_len)
    r_ops, r_in, r_out, r_shape = _router_io(route, tm, n_rows)
    return pl.pallas_call(
        functools.partial(_sconv_kernel, tm=tm, seq_len=seq_len, n_stream=len(s_ops)),
        grid=(n_rows // tm,),
        in_specs=s_specs + [_mod_spec(tm, seq_len), _const_spec((1, D_MODEL)),
                            _const_spec((D_MODEL, 3 * D_MODEL)), _const_spec((3, D_MODEL)),
                            _const_spec((1, D_MODEL)), _const_spec((D_MODEL, D_MODEL))] + r_in,
        out_specs=[_row_spec(tm)] + r_out,
        out_shape=[jax.ShapeDtypeStruct((n_rows, D_MODEL), F32)] + r_shape,
        scratch_shapes=[pltpu.VMEM((tm, D_MODEL), BF16)],
        compiler_params=_params(("arbitrary",)),
        name="sconv_mixer",
    )(*s_ops, mod, norm_g[None, :], w_in.astype(BF16), conv_w, conv_b[None, :],
      w_out.astype(BF16), *r_ops)


def _pack_bf16(x):
    w = x.shape[1] // 2
    bits = pltpu.bitcast(x.astype(BF16).astype(F32), jnp.uint32)
    return (bits[:, :w] >> 16) | bits[:, w:]


def _unpack_bf16(p):
    lo = pltpu.bitcast(p << 16, F32)
    hi = pltpu.bitcast(p & jnp.uint32(0xFFFF0000), F32)
    return jnp.concatenate([lo, hi], axis=1)


META_E1, META_E2, META_W1, META_W2, META_R1, META_R2 = range(6)
META_ROWS = 8


def _route_tile(x, m, g_ref, wh_ref, b_ref, tri_ref, h_ref, meta_ref, meta_t_ref, cnt_ref):
    @pl.when(pl.program_id(0) == 0)
    def _():
        cnt_ref[...] = jnp.zeros_like(cnt_ref)

    ms = jnp.mean(x * x, axis=-1, keepdims=True)
    h = x * lax.rsqrt(ms + EPS) * (g_ref[...] * (1.0 + m[4:5])) + m[3:4]
    hi = h.astype(BF16)
    hi32 = hi.astype(F32)
    lo = (h - hi32).astype(BF16)
    half = D_MODEL // 2
    bits = pltpu.bitcast(hi32, jnp.uint32)
    h_ref[...] = (bits[:, :half] >> 16) | bits[:, half:]
    part = _dot(hi, wh_ref[...])
    lg = part[:, :LANES] + (part[:, LANES:] + _dot(lo, wh_ref[:, :LANES])) + b_ref[...]
    lgt = lg.T
    tm = lgt.shape[1]
    neg = -jnp.inf
    sub = lax.broadcasted_iota(jnp.int32, (HALO, tm), 0).astype(F32)

    def first_argmax(vals):
        top = jnp.max(vals, axis=0, keepdims=True)
        idx = jnp.min(jnp.where(vals == top, sub, float(HALO)), axis=0, keepdims=True)
        return top, idx

    gl = jnp.where(sub < MOE_GROUPS, lgt[MOE_N_EXPERTS:MOE_N_EXPERTS + HALO], neg)
    gmax, grp = first_argmax(gl)
    g_w = 1.0 / jnp.sum(jnp.exp(gl - gmax), axis=0, keepdims=True)
    el = lgt[:MOE_EXPERTS_PER_GROUP]
    for g in range(1, MOE_GROUPS):
        el = jnp.where(grp == float(g),
                       lgt[g * MOE_EXPERTS_PER_GROUP:(g + 1) * MOE_EXPERTS_PER_GROUP], el)
    v1, i1 = first_argmax(el)
    v2, i2 = first_argmax(jnp.where(sub == i1, neg, el))
    ex = jnp.exp(v2 - v1)
    w1 = 1.0 / (1.0 + ex)
    w2 = ex * w1
    e1 = grp * MOE_EXPERTS_PER_GROUP + i1
    e2 = grp * MOE_EXPERTS_PER_GROUP + i2
    expert = lax.broadcasted_iota(jnp.int32, (MOE_N_EXPERTS, tm), 0).astype(F32)
    onehot = jnp.where(jnp.logical_or(expert == e1, expert == e2), 1.0, 0.0)
    before = _dot(onehot.astype(BF16), tri_ref[...]) + cnt_ref[...]
    cnt_ref[...] += jnp.sum(onehot, axis=1, keepdims=True)
    r1 = jnp.sum(jnp.where(expert == e1, before, 0.0), axis=0, keepdims=True)
    r2 = jnp.sum(jnp.where(expert == e2, before, 0.0), axis=0, keepdims=True)
    fields = ((META_E1, e1), (META_E2, e2), (META_W1, w1 * g_w), (META_W2, w2 * g_w),
              (META_R1, r1), (META_R2, r2))
    field = lax.broadcasted_iota(jnp.int32, (LANES, tm), 0)
    meta_t = jnp.zeros((LANES, tm), F32)
    for row, val in fields:
        meta_t = jnp.where(field == row, val, meta_t)
    meta_t_ref[...] = meta_t[:META_ROWS]
    meta_ref[...] = meta_t.T


N_ROUTE_IN = 4
N_ROUTE_OUT = 4


def _router_io(route, tm, n_rows):
    norm_g, w_group, b_group, w_router, b_router = route
    w = _pad_to(jnp.concatenate([w_router, w_group], axis=1), D_MODEL, LANES)
    wh = w.astype(BF16)
    wl = (w - wh.astype(F32)).astype(BF16)
    whl = jnp.concatenate([wh, wl], axis=1)
    b = _pad_to(jnp.concatenate([b_router, b_group])[None, :], 1, LANES)
    tri = jnp.asarray(np.triu(np.ones((tm, tm), np.float32), 1), BF16)
    operands = (norm_g[None, :], whl, b, tri)
    in_specs = [_const_spec((1, D_MODEL)), _const_spec((D_MODEL, 2 * LANES)),
                _const_spec((1, LANES)), _const_spec((tm, tm))]
    out_specs = [_row_spec(tm, D_MODEL // 2), _row_spec(tm, LANES),
                 pl.BlockSpec((META_ROWS, tm), lambda i: (0, i)),
                 _const_spec((MOE_N_EXPERTS, 1))]
    out_shape = [jax.ShapeDtypeStruct((n_rows, D_MODEL // 2), jnp.uint32),
                 jax.ShapeDtypeStruct((n_rows, LANES), F32),
                 jax.ShapeDtypeStruct((META_ROWS, n_rows), F32),
                 jax.ShapeDtypeStruct((MOE_N_EXPERTS, 1), F32)]
    return operands, in_specs, out_specs, out_shape


def _slot_kernel(offs_ref, meta_t_ref, pos_ref):
    meta = meta_t_ref[...]
    start = jnp.zeros_like(meta)
    for e in range(MOE_N_EXPERTS):
        start = jnp.where(meta == float(e), offs_ref[e], start)
    shift = META_ROWS - (META_R1 - META_E1)
    pos_ref[...] = (start + pltpu.roll(meta, shift, axis=0)).astype(jnp.int32)


def moe_slots(meta_t, offsets):
    n_rows = meta_t.shape[1]
    tn = min(n_rows, 8192)
    blk = pl.BlockSpec((META_ROWS, tn), lambda i, offs: (0, i))
    return pl.pallas_call(
        _slot_kernel,
        grid_spec=pltpu.PrefetchScalarGridSpec(
            num_scalar_prefetch=1, grid=(n_rows // tn,), in_specs=[blk], out_specs=blk),
        out_shape=jax.ShapeDtypeStruct((META_ROWS, n_rows), jnp.int32),
        compiler_params=_params(("parallel",)),
        name="moe_slots",
    )(offsets, meta_t)


def _expert_kernel(te_ref, nu_ref, nv_ref, xs_ref, wg_ref, wu_ref, wd_ref, o_ref,
                   wgu_ref, wdb_ref):
    j = pl.program_id(0)

    @pl.when(jnp.logical_or(j == 0, te_ref[j] != te_ref[jnp.maximum(j - 1, 0)]))
    def _():
        wgu_ref[:, :MOE_D_FF] = wg_ref[0, 0].astype(BF16)
        wgu_ref[:, MOE_D_FF:] = wu_ref[0, 0].astype(BF16)
        wdb_ref[...] = wd_ref[0, 0].astype(BF16)

    @pl.when(j < nu_ref[0])
    def _():
        x = _unpack_bf16(xs_ref[...])
        row = lax.broadcasted_iota(jnp.int32, (x.shape[0], 1), 0)
        x = jnp.where(row < nv_ref[j], x, 0.0).astype(BF16)
        au = _dot(x, wgu_ref[...])
        hh = (_silu(au[:, :MOE_D_FF]) * au[:, MOE_D_FF:]).astype(BF16)
        o_ref[...] = _pack_bf16(_dot(hh, wdb_ref[...]))


def moe_experts(xs, tile_expert, n_used, n_valid, w_gate, w_up, w_down, layer, tm):
    n_slots, half = xs.shape
    wspec = lambda shape: pl.BlockSpec((1, 1) + shape, lambda j, te, nu, nv: (layer, te[j], 0, 0))
    row = pl.BlockSpec((tm, half), lambda j, te, nu, nv: (j, 0))
    row_in = pl.BlockSpec((tm, half), lambda j, te, nu, nv: (jnp.minimum(j, nu[0] - 1), 0))
    grid_spec = pltpu.PrefetchScalarGridSpec(
        num_scalar_prefetch=3,
        grid=(n_slots // tm,),
        in_specs=[row_in, wspec((D_MODEL, MOE_D_FF)), wspec((D_MODEL, MOE_D_FF)),
                  wspec((MOE_D_FF, D_MODEL))],
        out_specs=row,
        scratch_shapes=[pltpu.VMEM((D_MODEL, 2 * MOE_D_FF), BF16),
                        pltpu.VMEM((MOE_D_FF, D_MODEL), BF16)],
    )
    return pl.pallas_call(
        _expert_kernel,
        grid_spec=grid_spec,
        out_shape=jax.ShapeDtypeStruct((n_slots, half), jnp.uint32),
        compiler_params=_params(("arbitrary",)),
        name="moe_experts",
    )(tile_expert, n_used, n_valid, xs, w_gate, w_up, w_down)


def _combine_kernel(x_ref, ya_ref, yb_ref, meta_ref, mod_ref, o_ref):
    o_ref[...] = _moe_combined(x_ref, ya_ref, yb_ref, meta_ref, mod_ref[0][5:6])


def moe_combine(pending, seq_len):
    x, ya, yb, meta, mod = pending
    n_rows = x.shape[0]
    tm = STREAM_TILE
    half = D_MODEL // 2
    return pl.pallas_call(
        _combine_kernel,
        grid=(n_rows // tm,),
        in_specs=[_row_spec(tm), _row_spec(tm, half), _row_spec(tm, half), _row_spec(tm, LANES),
                  _mod_spec(tm, seq_len)],
        out_specs=_row_spec(tm),
        out_shape=jax.ShapeDtypeStruct((n_rows, D_MODEL), F32),
        compiler_params=_params(("parallel",)),
        name="moe_combine",
    )(x, ya, yb, meta, mod)


MOE_TILE = 512
MOE_TILE_LARGE = 1024
MOE_LARGE_ROWS = 32768


def hier_moe(routed, mod, w_gate, w_up, w_down, layer, seq_len):
    x, hp, meta, meta_t, counts = routed
    n_rows = x.shape[0]
    tm = MOE_TILE_LARGE if n_rows >= MOE_LARGE_ROWS else MOE_TILE
    cnt = counts[:, 0].astype(jnp.int32)
    padded = (cnt + tm - 1) // tm * tm
    ends = jnp.cumsum(padded)
    starts = ends - padded
    n_slots = 2 * n_rows + MOE_N_EXPERTS * tm
    tile_start = jnp.arange(n_slots // tm, dtype=jnp.int32) * tm
    tile_expert = jnp.minimum(jnp.sum(tile_start[:, None] >= ends[None, :], axis=1),
                              MOE_N_EXPERTS - 1).astype(jnp.int32)
    n_used = (ends[-1:] // tm).astype(jnp.int32)
    n_valid = jnp.clip(starts[tile_expert] + cnt[tile_expert] - tile_start, 0, tm).astype(jnp.int32)
    pos = moe_slots(meta_t, starts.astype(F32))
    idx0 = pos[META_E1].reshape(n_rows // SC_BLOCK, SC_BLOCK)
    idx1 = pos[META_E2].reshape(n_rows // SC_BLOCK, SC_BLOCK)
    xs = sc_scatter_rows(hp, idx0, idx1, n_slots)
    ys = moe_experts(xs, tile_expert, n_used, n_valid, w_gate, w_up, w_down, layer, tm)
    ya, yb = sc_gather_rows(ys, idx0, idx1)
    return _Pending(x, ya, yb, meta, mod)


SC_CORES = 2
SC_SUBCORES = 16
SC_WORKERS = SC_CORES * SC_SUBCORES
SC_BLOCK = 128


def _sc_mesh():
    return plsc.VectorSubcoreMesh(core_axis_name="c", subcore_axis_name="s")


def _sc_worker():
    return lax.axis_index("s") * SC_CORES + lax.axis_index("c")


def sc_scatter_rows(rows, idx0, idx1, n_slots):
    n_rows, width = rows.shape
    per_worker = n_rows // SC_BLOCK // SC_WORKERS

    @functools.partial(
        pl.kernel, mesh=_sc_mesh(),
        out_type=jax.ShapeDtypeStruct((n_slots, width), rows.dtype),
        scratch_types=[pltpu.VMEM((SC_BLOCK,), jnp.int32), pltpu.VMEM((SC_BLOCK,), jnp.int32),
                       pltpu.VMEM((SC_BLOCK, width), rows.dtype)],
        name="sc_scatter_rows",
    )
    def scatter(rows_hbm, i0_hbm, i1_hbm, out_hbm, i0_v, i1_v, rows_v):
        first = _sc_worker() * per_worker

        @pl.loop(0, per_worker)
        def _(j):
            blk = first + j
            pltpu.sync_copy(i0_hbm.at[blk], i0_v)
            pltpu.sync_copy(i1_hbm.at[blk], i1_v)
            pltpu.sync_copy(rows_hbm.at[pl.ds(blk * SC_BLOCK, SC_BLOCK)], rows_v)
            pltpu.sync_copy(rows_v, out_hbm.at[i0_v])
            pltpu.sync_copy(rows_v, out_hbm.at[i1_v])

    return scatter(rows, idx0, idx1)


def sc_gather_rows(src, idx0, idx1):
    width = src.shape[1]
    n_rows = idx0.shape[0] * SC_BLOCK
    per_worker = n_rows // SC_BLOCK // SC_WORKERS
    out = jax.ShapeDtypeStruct((n_rows, width), src.dtype)

    @functools.partial(
        pl.kernel, mesh=_sc_mesh(), out_type=(out, out),
        scratch_types=[pltpu.VMEM((SC_BLOCK,), jnp.int32), pltpu.VMEM((SC_BLOCK, width), src.dtype)],
        name="sc_gather_rows",
    )
    def gather(src_hbm, i0_hbm, i1_hbm, a_hbm, b_hbm, idx_v, rows_v):
        first = _sc_worker() * per_worker

        @pl.loop(0, per_worker)
        def _(j):
            blk = first + j
            dst = pl.ds(blk * SC_BLOCK, SC_BLOCK)
            pltpu.sync_copy(i0_hbm.at[blk], idx_v)
            pltpu.sync_copy(src_hbm.at[idx_v], rows_v)
            pltpu.sync_copy(rows_v, a_hbm.at[dst])
            pltpu.sync_copy(i1_hbm.at[blk], idx_v)
            pltpu.sync_copy(src_hbm.at[idx_v], rows_v)
            pltpu.sync_copy(rows_v, b_hbm.at[dst])

    return gather(src, idx0, idx1)


def _trunk(x3, mods, p):
    batch, seq_len, _ = x3.shape
    x = x3.reshape(batch * seq_len, D_MODEL)
    for i in range(DEPTH):
        mod = mods[i]
        g1 = p["norm1_g"][i]
        route = (p["norm2_g"][i], p["moe_w_group"][i], p["moe_b_group"][i], p["moe_w_router"][i],
                 p["moe_b_router"][i])
        kind = i % 4
        if kind == 0:
            if isinstance(x, _Pending):
                x = moe_combine(x, seq_len)
            q, k, v = attn_qkv(x, mod, g1, p["attn_wqkv"][0], p["attn_q_norm"][0],
                               p["attn_k_norm"][0], seq_len)
            o = attn_flash(q, k, v, batch, seq_len)
            routed = proj_residual(o, p["attn_wo"][0], x, mod, 2, seq_len, route)
        elif kind == 1:
            hp = {"w_in": p["hy_w_in"][0], "conv_w": p["hy_conv_w"][0], "conv_b": p["hy_conv_b"][0],
                  "w1": p["hy_ffn_w1"][0], "b1": p["hy_ffn_b1"][0], "w2": p["hy_ffn_w2"][0],
                  "b2": p["hy_ffn_b2"][0], "w3": p["hy_ffn_w3"][0], "freq": p["hy_freq"][0],
                  "skip": p["hy_skip"][0], "w_out": p["hy_w_out"][0]}
            routed = hyena_mixer(x, mod, g1, hp, batch, seq_len, route)
        elif kind == 2:
            routed = pool_mixer(x, mod, g1, p["pool_w"][0], p["pool_scale"][0], seq_len, route)
        else:
            routed = sconv_mixer(x, mod, g1, p["sc_w_in"][0], p["sc_conv_w"][0],
                                 p["sc_conv_b"][0], p["sc_w_out"][0], seq_len, route)
        x = hier_moe(routed, mod, p["moe_w_gate"], p["moe_w_up"], p["moe_w_down"], i, seq_len)
    return moe_combine(x, seq_len).reshape(batch, seq_len, D_MODEL)


def kernel(x_prompt, x_sample, c_prompt, c_sample, norm1_g, norm2_g, ada_w, ada_b, attn_wqkv, attn_q_norm, attn_k_norm, attn_wo, hy_w_in, hy_conv_w, hy_conv_b, hy_ffn_w1, hy_ffn_b1, hy_ffn_w2, hy_ffn_b2, hy_ffn_w3, hy_freq, hy_skip, hy_w_out, pool_w, pool_scale, sc_w_in, sc_conv_w, sc_conv_b, sc_w_out, moe_w_group, moe_b_group, moe_w_router, moe_b_router, moe_w_gate, moe_w_up, moe_w_down):
    p = dict(norm1_g=norm1_g, norm2_g=norm2_g, attn_wqkv=attn_wqkv, attn_q_norm=attn_q_norm,
             attn_k_norm=attn_k_norm, attn_wo=attn_wo, hy_w_in=hy_w_in, hy_conv_w=hy_conv_w,
             hy_conv_b=hy_conv_b, hy_ffn_w1=hy_ffn_w1, hy_ffn_b1=hy_ffn_b1, hy_ffn_w2=hy_ffn_w2,
             hy_ffn_b2=hy_ffn_b2, hy_ffn_w3=hy_ffn_w3, hy_freq=hy_freq, hy_skip=hy_skip,
             hy_w_out=hy_w_out, pool_w=pool_w, pool_scale=pool_scale, sc_w_in=sc_w_in,
             sc_conv_w=sc_conv_w, sc_conv_b=sc_conv_b, sc_w_out=sc_w_out, moe_w_group=moe_w_group,
             moe_b_group=moe_b_group, moe_w_router=moe_w_router, moe_b_router=moe_b_router,
             moe_w_gate=moe_w_gate, moe_w_up=moe_w_up, moe_w_down=moe_w_down)
    nb = c_prompt.shape[0]
    ns = c_sample.shape[0]
    rows = -(-(nb + ns) // HALO) * HALO
    c_all = jnp.pad(jnp.concatenate([c_prompt, c_sample], axis=0), ((0, rows - nb - ns), (0, 0)))
    mod = ada_mod(c_all, ada_w, ada_b).reshape(DEPTH, rows, 6, D_MODEL)
    mods_prompt = [mod[i, :nb] for i in range(DEPTH)]
    mods_sample = [mod[i, nb:nb + ns] for i in range(DEPTH)]
    return _trunk(x_prompt, mods_prompt, p), _trunk(x_sample, mods_sample, p)
```

```python
import functools
import math
from typing import NamedTuple

import jax
import jax.numpy as jnp
import numpy as np
from jax import lax
from jax.experimental import pallas as pl
from jax.experimental.pallas import tpu as pltpu
from jax.experimental.pallas import tpu_sc as plsc

F32 = jnp.float32
BF16 = jnp.bfloat16
F8 = jnp.float8_e4m3fn

D_MODEL = 1024
DEPTH = 4
EPS = 1e-6
GRID_W = 64
HEAD_DIM = 64
N_HEADS = 16
N_KV_HEADS = 4
Q_PER_KV = 4
ROPE_THETA = 10000.0
ROPE_FREQS = 16
HY_EMB_DIM = 33
HY_BANDS = 16
HY_FILTER_WIDTH = 64
HY_FAST_DECAY = 0.3
HY_SLOW_DECAY = 1.5
HY_TARGET = 1e-2
POOL_WINDOWS = (2, 4, 8, 16)
POOL_GROUP_DIM = 256
MOE_GROUPS = 4
MOE_EXPERTS_PER_GROUP = 8
MOE_N_EXPERTS = 32
MOE_D_FF = 256

LANES = 128
HALO = 8
DFT_N2 = 256
VMEM_LIMIT = 56 * 1024 * 1024

ROW_TILE = 512
STREAM_TILE = 1024


def _params(sem):
    return pltpu.CompilerParams(dimension_semantics=sem, vmem_limit_bytes=VMEM_LIMIT)


def _dot(a, b):
    return jnp.dot(a, b, preferred_element_type=F32)


def _split(a):
    hi = a.astype(BF16)
    lo = (a - hi.astype(F32)).astype(BF16)
    return hi, lo


def _dot3(a, b):
    ah, al = _split(a)
    bh, bl = _split(b)
    return _dot(ah, bh) + (_dot(ah, bl) + _dot(al, bh))


def _modulate(x, g, shift, scale):
    ms = jnp.mean(x * x, axis=-1, keepdims=True)
    return x * lax.rsqrt(ms + EPS) * g * (1.0 + scale) + shift


def _silu(x):
    return x * (1.0 / (1.0 + jnp.exp(-x)))


def _ada_kernel(c_ref, w_ref, b_ref, o_ref):
    c = c_ref[...]
    o_ref[0] = _dot3(_silu(c), w_ref[0]) + b_ref[0]


def ada_mod(c_all, ada_w, ada_b):
    rows = c_all.shape[0]
    n = ada_w.shape[2]
    tn = 1536
    return pl.pallas_call(
        _ada_kernel,
        grid=(DEPTH, n // tn),
        in_specs=[
            pl.BlockSpec((rows, D_MODEL), lambda l, j: (0, 0)),
            pl.BlockSpec((1, D_MODEL, tn), lambda l, j: (l, 0, j)),
            pl.BlockSpec((1, 1, tn), lambda l, j: (l, 0, j)),
        ],
        out_specs=pl.BlockSpec((1, rows, tn), lambda l, j: (l, 0, j)),
        out_shape=jax.ShapeDtypeStruct((DEPTH, rows, n), F32),
        compiler_params=_params(("parallel", "parallel")),
        name="ada_mod",
    )(c_all, ada_w, ada_b.reshape(DEPTH, 1, n))


def _row_spec(tm, width=D_MODEL):
    return pl.BlockSpec((tm, width), lambda i: (i, 0))


def _mod_spec(tm, seq_len):
    return pl.BlockSpec((1, 6, D_MODEL), lambda i: ((i * tm) // seq_len, 0, 0))


def _const_spec(shape):
    nd = len(shape)
    return pl.BlockSpec(shape, lambda i: (0,) * nd)


def _halo_specs(tm, n_rows, width=D_MODEL):
    per = tm // HALO
    last = n_rows // HALO - 1
    prev = pl.BlockSpec((HALO, width), lambda i: (jnp.maximum(i * per - 1, 0), 0))
    nxt = pl.BlockSpec((HALO, width), lambda i: (jnp.minimum((i + 1) * per, last), 0))
    return prev, nxt


class _Pending(NamedTuple):
    x: jax.Array
    ya: jax.Array
    yb: jax.Array
    meta: jax.Array
    mod: jax.Array


N_PLAIN_REFS = 3
N_PENDING_REFS = 13


def _stream_io(src, tm, n_rows, seq_len):
    if not isinstance(src, _Pending):
        prev, nxt = _halo_specs(tm, n_rows)
        return [src] * 3, [prev, _row_spec(tm), nxt]
    operands, specs = [], []
    for arr in (src.x, src.ya, src.yb, src.meta):
        width = arr.shape[1]
        prev, nxt = _halo_specs(tm, n_rows, width)
        operands += [arr] * 3
        specs += [prev, _row_spec(tm, width), nxt]
    return operands + [src.mod], specs + [_mod_spec(tm, seq_len)]


def _moe_combined(x_ref, ya_ref, yb_ref, meta_ref, gate):
    meta = meta_ref[...]
    y = (meta[:, META_W1:META_W1 + 1] * _unpack_bf16(ya_ref[...])
         + meta[:, META_W2:META_W2 + 1] * _unpack_bf16(yb_ref[...]))
    return x_ref[...] + gate * y


def _stream_rows(refs):
    if len(refs) == N_PLAIN_REFS:
        prev_ref, x_ref, next_ref = refs
        x = x_ref[...]
        return x, jnp.concatenate([prev_ref[...], x, next_ref[...]], axis=0)
    gate = refs[-1][0][5:6]
    parts = [_moe_combined(refs[k], refs[3 + k], refs[6 + k], refs[9 + k], gate) for k in range(3)]
    return parts[1], jnp.concatenate(parts, axis=0)


def _edge_flags(tm, seq_len):
    i = pl.program_id(0)
    per_seq = seq_len // tm
    pos = i % per_seq
    return pos == 0, pos == per_seq - 1


def _shift_rows(u, tm):
    n = u.shape[0]
    up = pltpu.roll(u, 1, axis=0)[HALO:HALO + tm]
    dn = pltpu.roll(u, n - 1, axis=0)[HALO:HALO + tm]
    return up, u[HALO:HALO + tm], dn


def _conv3(u, w_ref, b_ref, cols, tm, first, last, stage_ref=None):
    if stage_ref is None:
        up, mid, dn = _shift_rows(u, tm)
    else:
        stage_ref[...] = u
        up = stage_ref[pl.ds(HALO - 1, tm), :]
        mid = stage_ref[pl.ds(HALO, tm), :]
        dn = stage_ref[pl.ds(HALO + 1, tm), :]
    row = lax.broadcasted_iota(jnp.int32, (tm, 1), 0)
    up = jnp.where(jnp.logical_and(first, row == 0), 0.0, up)
    dn = jnp.where(jnp.logical_and(last, row == tm - 1), 0.0, dn)
    w = w_ref[:, cols]
    return up * w[0:1] + mid * w[1:2] + dn * w[2:3] + b_ref[:, cols]


def _norm_rope(t, gain, headmean, cos, sin_signed):
    width = t.shape[1]
    ms = _dot((t * t).astype(BF16), headmean[:width, :width])
    y = t * lax.rsqrt(ms + EPS) * gain
    lane = lax.broadcasted_iota(jnp.int32, y.shape, 1)
    first = (lane % 32) < ROPE_FREQS
    partner = jnp.where(first, pltpu.roll(y, width - ROPE_FREQS, axis=1),
                        pltpu.roll(y, ROPE_FREQS, axis=1))
    reps = width // LANES
    return y * jnp.tile(cos, (1, reps)) + partner * jnp.tile(sin_signed, (1, reps))


def _qkv_kernel(x_ref, mod_ref, g_ref, w_ref, qg_ref, kg_ref, hm_ref, cos_ref, sin_ref,
                q_ref, k_ref, v_ref):
    m = mod_ref[0]
    h = _modulate(x_ref[...], g_ref[...], m[0:1], m[1:2]).astype(BF16)
    qkv = _dot(h, w_ref[...])
    nq = N_HEADS * HEAD_DIM
    nk = N_KV_HEADS * HEAD_DIM
    cos = cos_ref[...]
    sin = sin_ref[...]
    hm = hm_ref[...]
    q = _norm_rope(qkv[:, :nq], qg_ref[...], hm, cos, sin)
    k = _norm_rope(qkv[:, nq:nq + nk], kg_ref[...], hm, cos, sin)
    v = qkv[:, nq + nk:]
    q_ref[...] = q.astype(BF16)
    ones = jnp.ones((v.shape[0], HEAD_DIM), F32)
    for g in range(N_KV_HEADS):
        sl = slice(g * HEAD_DIM, (g + 1) * HEAD_DIM)
        k_ref[g] = k[:, sl].astype(F8)
        v_ref[g] = jnp.concatenate([v[:, sl], ones], axis=1).astype(BF16)


def _rope_tables(seq_len):
    rows = seq_len // GRID_W
    r = jnp.broadcast_to(jnp.arange(rows)[:, None], (rows, GRID_W)).reshape(-1)
    c = jnp.broadcast_to(jnp.arange(GRID_W)[None, :], (rows, GRID_W)).reshape(-1)
    inv_freq = ROPE_THETA ** (-jnp.arange(ROPE_FREQS, dtype=F32) / ROPE_FREQS)
    pos = jnp.stack([r, c], axis=-1).astype(F32)
    ang = pos[:, :, None] * inv_freq[None, None, :]
    cos = jnp.cos(ang)
    sin = jnp.sin(ang)
    cos64 = jnp.concatenate([cos, cos], axis=-1).reshape(seq_len, HEAD_DIM)
    sin64 = jnp.concatenate([-sin, sin], axis=-1).reshape(seq_len, HEAD_DIM)
    return jnp.tile(cos64, (1, 2)), jnp.tile(sin64, (1, 2))


def attn_qkv(x, mod, norm_g, wqkv, q_norm, k_norm, seq_len):
    n_rows = x.shape[0]
    tm = ROW_TILE
    nq = N_HEADS * HEAD_DIM
    nk = N_KV_HEADS * HEAD_DIM
    cos, sin = _rope_tables(seq_len)
    qg = jnp.tile(q_norm, N_HEADS)[None, :] * (HEAD_DIM ** -0.5 * math.log2(math.e))
    kg = jnp.tile(k_norm, N_KV_HEADS)[None, :]
    head = np.arange(nq) // HEAD_DIM
    headmean = jnp.asarray((head[:, None] == head[None, :]).astype(np.float32) / HEAD_DIM, BF16)
    per_seq = seq_len // tm
    tab_spec = pl.BlockSpec((tm, LANES), lambda i: (i % per_seq, 0))
    return pl.pallas_call(
        _qkv_kernel,
        grid=(n_rows // tm,),
        in_specs=[
            _row_spec(tm), _mod_spec(tm, seq_len), _const_spec((1, D_MODEL)),
            _const_spec((D_MODEL, nq + 2 * nk)), _const_spec((1, nq)), _const_spec((1, nk)),
            _const_spec((nq, nq)), tab_spec, tab_spec,
        ],
        out_specs=[
            _row_spec(tm, nq),
            pl.BlockSpec((N_KV_HEADS, tm, HEAD_DIM), lambda i: (0, i, 0)),
            pl.BlockSpec((N_KV_HEADS, tm, 2 * HEAD_DIM), lambda i: (0, i, 0)),
        ],
        out_shape=[
            jax.ShapeDtypeStruct((n_rows, nq), BF16),
            jax.ShapeDtypeStruct((N_KV_HEADS, n_rows, HEAD_DIM), F8),
            jax.ShapeDtypeStruct((N_KV_HEADS, n_rows, 2 * HEAD_DIM), BF16),
        ],
        compiler_params=_params(("parallel",)),
        name="attn_qkv",
    )(x, mod, norm_g[None, :], wqkv.astype(BF16), qg, kg, headmean, cos, sin)


def _flash_kernel(q_ref, k_ref, v_ref, o_ref, *, tq, tk, n_chunks):
    q = q_ref[...]
    qs = jnp.concatenate([q[:, j * HEAD_DIM:(j + 1) * HEAD_DIM] for j in range(Q_PER_KV)], axis=0)
    qs = qs.astype(F8)
    rows = Q_PER_KV * tq

    def body(c, carry):
        m, acc = carry
        start = pl.multiple_of(c * tk, tk)
        kc = k_ref[0, pl.ds(start, tk), :]
        vc = v_ref[0, pl.ds(start, tk), :]
        s = lax.dot_general(qs, kc, (((1,), (1,)), ((), ())), preferred_element_type=F32)
        m_new = jnp.maximum(m, jnp.max(s, axis=-1, keepdims=True))
        alpha = jnp.exp2(m - m_new)
        p = jnp.exp2(s - m_new)
        acc = acc * alpha + _dot(p.astype(BF16), vc)
        return m_new, acc

    m0 = jnp.full((rows, 1), -jnp.inf, F32)
    acc0 = jnp.zeros((rows, 2 * HEAD_DIM), F32)
    _, acc = lax.fori_loop(0, n_chunks, body, (m0, acc0))
    o = acc[:, :HEAD_DIM] / acc[:, HEAD_DIM:HEAD_DIM + 1]
    o_ref[...] = jnp.concatenate([o[j * tq:(j + 1) * tq] for j in range(Q_PER_KV)],
                                 axis=1).astype(BF16)


def attn_flash(q, k, v, batch, seq_len):
    n_rows = q.shape[0]
    tq = 512
    tk = min(seq_len, 2048)
    per_seq = seq_len // tq
    width = Q_PER_KV * HEAD_DIM
    kern = functools.partial(_flash_kernel, tq=tq, tk=tk, n_chunks=seq_len // tk)
    return pl.pallas_call(
        kern,
        grid=(batch, N_KV_HEADS, per_seq),
        in_specs=[
            pl.BlockSpec((tq, width), lambda b, g, i: (b * per_seq + i, g)),
            pl.BlockSpec((1, seq_len, HEAD_DIM), lambda b, g, i: (g, b, 0)),
            pl.BlockSpec((1, seq_len, 2 * HEAD_DIM), lambda b, g, i: (g, b, 0)),
        ],
        out_specs=pl.BlockSpec((tq, width), lambda b, g, i: (b * per_seq + i, g)),
        out_shape=jax.ShapeDtypeStruct((n_rows, N_HEADS * HEAD_DIM), BF16),
        compiler_params=_params(("parallel", "parallel", "parallel")),
        name="attn_flash",
    )(q, k, v)


def _proj_res_kernel(y_ref, w_ref, x_ref, mod_ref, *refs, gate_row):
    route_in, o_ref, route_out = refs[:N_ROUTE_IN], refs[N_ROUTE_IN], refs[N_ROUTE_IN + 1:]
    m = mod_ref[0]
    x1 = x_ref[...] + m[gate_row:gate_row + 1] * _dot(y_ref[...], w_ref[...])
    o_ref[...] = x1
    _route_tile(x1, m, *route_in, *route_out)


def proj_residual(y, w, x, mod, gate_row, seq_len, route):
    n_rows = x.shape[0]
    tm = STREAM_TILE
    r_ops, r_in, r_out, r_shape = _router_io(route, tm, n_rows)
    return pl.pallas_call(
        functools.partial(_proj_res_kernel, gate_row=gate_row),
        grid=(n_rows // tm,),
        in_specs=[_row_spec(tm, y.shape[1]), _const_spec(w.shape), _row_spec(tm),
                  _mod_spec(tm, seq_len)] + r_in,
        out_specs=[_row_spec(tm)] + r_out,
        out_shape=[jax.ShapeDtypeStruct((n_rows, D_MODEL), F32)] + r_shape,
        compiler_params=_params(("arbitrary",)),
        name="proj_residual",
    )(y, w.astype(BF16), x, mod, *r_ops)


CONV_COLS = 256


def _hy_in_kernel(*refs, tm, seq_len, n_stream):
    stream, refs = refs[:n_stream], refs[n_stream:]
    mod_ref, g_ref, w_ref, cw_ref, cb_ref, z_ref, x0_ref = refs[:7]
    stage_refs = refs[-3:]
    first, last = _edge_flags(tm, seq_len)
    m = mod_ref[0]
    x, ext = _stream_rows(stream)
    if n_stream == N_PENDING_REFS:
        refs[7][...] = x
    h = _modulate(ext, g_ref[...], m[0:1], m[1:2]).astype(BF16)
    tn = CONV_COLS
    for j in range(D_MODEL // tn):
        part = []
        for s in range(3):
            cols = slice(s * D_MODEL + j * tn, s * D_MODEL + (j + 1) * tn)
            part.append(_conv3(_dot(h, w_ref[:, cols]), cw_ref, cb_ref, cols, tm, first, last,
                               stage_refs[s]))
        out_cols = slice(j * tn, (j + 1) * tn)
        x0_ref[:, out_cols] = part[0].astype(BF16)
        z_ref[:, out_cols] = (part[2] * part[1]).astype(BF16)


def hyena_in(src, mod, norm_g, w_in, conv_w, conv_b, seq_len):
    pending = isinstance(src, _Pending)
    n_rows = (src.x if pending else src).shape[0]
    tm = ROW_TILE
    s_ops, s_specs = _stream_io(src, tm, n_rows, seq_len)
    x_out = [jax.ShapeDtypeStruct((n_rows, D_MODEL), F32)] if pending else []
    return pl.pallas_call(
        functools.partial(_hy_in_kernel, tm=tm, seq_len=seq_len, n_stream=len(s_ops)),
        grid=(n_rows // tm,),
        in_specs=s_specs + [_mod_spec(tm, seq_len), _const_spec((1, D_MODEL)),
                            _const_spec((D_MODEL, 3 * D_MODEL)), _const_spec((3, 3 * D_MODEL)),
                            _const_spec((1, 3 * D_MODEL))],
        out_specs=[_row_spec(tm)] * (2 + len(x_out)),
        out_shape=[jax.ShapeDtypeStruct((n_rows, D_MODEL), BF16),
                   jax.ShapeDtypeStruct((n_rows, D_MODEL), BF16)] + x_out,
        scratch_shapes=[pltpu.VMEM((tm + 2 * HALO, CONV_COLS), F32)] * 3,
        compiler_params=_params(("parallel",)),
        name="hyena_in",
    )(*s_ops, mod, norm_g[None, :], w_in.astype(BF16), conv_w, conv_b[None, :])


def _hy_filter_kernel(feat_ref, w1_ref, b1_ref, w2_ref, b2_ref, w3_ref, fr_ref, dl_ref, o_ref):
    feat = feat_ref[...]
    fr = fr_ref[...]
    a = jnp.sin(fr * (_dot3(feat, w1_ref[...]) + b1_ref[...]))
    a = jnp.sin(fr * (_dot3(a, w2_ref[...]) + b2_ref[...]))
    hf = _dot3(a, w3_ref[...])
    decay = jnp.exp(-feat[:, 0:1] * dl_ref[...])
    o_ref[0] = hf[:, :D_MODEL] * decay
    o_ref[1] = hf[:, D_MODEL:] * decay


def _pad_to(a, rows, cols):
    return jnp.pad(a.astype(F32), ((0, rows - a.shape[0]), (0, cols - a.shape[1])))


def hyena_filter(seq_len, w1, b1, w2, b2, w3, freq):
    t = jnp.linspace(0.0, 1.0, seq_len, dtype=F32)[:, None]
    w = 2.0 * math.pi * jnp.arange(seq_len, dtype=F32)[:, None] / seq_len
    f = jnp.linspace(1e-4, HY_BANDS - 1, HY_BANDS, dtype=F32)[None, :]
    feat = _pad_to(jnp.concatenate([t, jnp.cos(f * w), -jnp.sin(f * w)], axis=-1), seq_len, LANES)
    max_decay = math.log(HY_TARGET) / HY_FAST_DECAY
    min_decay = math.log(HY_TARGET) / HY_SLOW_DECAY
    absdelta = jnp.abs(jnp.linspace(min_decay, max_decay, D_MODEL, dtype=F32))[None, :]
    tl = 512
    return pl.pallas_call(
        _hy_filter_kernel,
        grid=(seq_len // tl,),
        in_specs=[_row_spec(tl, LANES), _const_spec((LANES, LANES)), _const_spec((1, LANES)),
                  _const_spec((LANES, LANES)), _const_spec((1, LANES)),
                  _const_spec((LANES, 2 * D_MODEL)), _const_spec((1, LANES)),
                  _const_spec((1, D_MODEL))],
        out_specs=pl.BlockSpec((2, tl, D_MODEL), lambda i: (0, i, 0)),
        out_shape=jax.ShapeDtypeStruct((2, seq_len, D_MODEL), F32),
        compiler_params=_params(("parallel",)),
        name="hyena_filter",
    )(feat, _pad_to(w1, LANES, LANES), _pad_to(b1[None, :], 1, LANES), _pad_to(w2, LANES, LANES),
      _pad_to(b2[None, :], 1, LANES), _pad_to(w3, LANES, 2 * D_MODEL),
      _pad_to(freq[None, :], 1, LANES), absdelta)


class _FFTPlan:
    def __init__(self, seq_len):
        self.n = 2 * seq_len
        self.n1 = self.n // DFT_N2
        self.r = self.n1 // 2
        self.k1n = self.n1 // 2 + 1
        self.kron = max(HALO, LANES // self.r)
        ang = 2.0 * np.pi * np.outer(np.arange(self.k1n), np.arange(self.r)) / self.n1
        eye = np.eye(self.kron)
        self.fwd_cos = jnp.asarray(np.kron(np.cos(ang), eye), BF16)
        self.fwd_sin = jnp.asarray(np.kron(-np.sin(ang), eye), BF16)
        wgt = np.full((self.k1n,), 2.0)
        wgt[0] = wgt[-1] = 1.0
        scale = (wgt / self.n)[None, :]
        self.inv_cos = jnp.asarray(np.kron(np.cos(ang).T * scale, eye), BF16)
        self.inv_sin = jnp.asarray(np.kron(-np.sin(ang).T * scale, eye), BF16)
        a2 = 2.0 * np.pi * np.outer(np.arange(DFT_N2), np.arange(DFT_N2)) / DFT_N2
        self.f_cos = jnp.asarray(np.cos(a2), F32)
        self.f_sin = jnp.asarray(-np.sin(a2), F32)
        tw = 2.0 * np.pi * np.outer(np.arange(self.k1n), np.arange(DFT_N2)) / self.n
        self.tw_cos = jnp.asarray(np.cos(tw), F32)
        self.tw_sin = jnp.asarray(-np.sin(tw), F32)


def _fft_a_kernel(z_ref, wc_ref, ws_ref, ar_ref, ai_ref, *, rq):
    shape = ar_ref.shape[1:2] + ar_ref.shape[3:]
    for t in range(z_ref.shape[2]):
        z = z_ref[0, :, t].astype(F32).reshape(rq, D_MODEL).astype(BF16)
        ar_ref[0, :, t] = _dot(wc_ref[...], z).reshape(shape).astype(BF16)
        ai_ref[0, :, t] = _dot(ws_ref[...], z).reshape(shape).astype(BF16)


def _fft_group(plan):
    nhi = DFT_N2 // plan.kron
    per_group = plan.r * plan.kron * D_MODEL * 4
    return max(1, min(nhi, (2 << 20) // per_group))


def fft_stage_a(z, plan, batch):
    q = plan.kron
    nhi = DFT_N2 // q
    hb = _fft_group(plan)
    zv = z.reshape(batch, plan.r, nhi, q, D_MODEL)
    out = jax.ShapeDtypeStruct((batch, plan.k1n, nhi, q, D_MODEL), BF16)
    ospec = pl.BlockSpec((1, plan.k1n, hb, q, D_MODEL), lambda b, h: (b, 0, h, 0, 0))
    wspec = pl.BlockSpec(plan.fwd_cos.shape, lambda b, h: (0, 0))
    ar, ai = pl.pallas_call(
        functools.partial(_fft_a_kernel, rq=plan.r * q),
        grid=(batch, nhi // hb),
        in_specs=[pl.BlockSpec((1, plan.r, hb, q, D_MODEL), lambda b, h: (b, 0, h, 0, 0)),
                  wspec, wspec],
        out_specs=[ospec, ospec],
        out_shape=[out, out],
        compiler_params=_params(("parallel", "parallel")),
        name="fft_stage_a",
    )(zv, plan.fwd_cos, plan.fwd_sin)
    shape = (batch, plan.k1n, DFT_N2, D_MODEL)
    return ar.reshape(shape), ai.reshape(shape)


def _twiddled_dft(fr, fi, tr, ti):
    return (fr * tr - fi * ti).astype(BF16), (fr * ti + fi * tr).astype(BF16)


def _fft_b_fwd_kernel(ar_ref, ai_ref, fr_ref, fi_ref, twr_ref, twi_ref, br_ref, bi_ref):
    n = DFT_N2
    gr, gi = _twiddled_dft(fr_ref[...], fi_ref[...], twr_ref[0], twi_ref[0])
    fwd = jnp.concatenate([jnp.concatenate([gr, -gi], axis=1),
                           jnp.concatenate([gi, gr], axis=1)], axis=0)
    for s in range(ar_ref.shape[0]):
        b = _dot(fwd, jnp.concatenate([ar_ref[s, 0], ai_ref[s, 0]], axis=0))
        br_ref[s, 0] = b[:n].astype(BF16)
        bi_ref[s, 0] = b[n:].astype(BF16)


def fft_stage_b_fwd(ar, ai, plan):
    batch = ar.shape[0]
    blk = pl.BlockSpec((batch, 1, DFT_N2, D_MODEL), lambda k: (0, k, 0, 0))
    cst = pl.BlockSpec((DFT_N2, DFT_N2), lambda k: (0, 0))
    tws = pl.BlockSpec((1, 1, DFT_N2), lambda k: (k, 0, 0))
    out = jax.ShapeDtypeStruct(ar.shape, BF16)
    return pl.pallas_call(
        _fft_b_fwd_kernel,
        grid=(plan.k1n,),
        in_specs=[blk, blk, cst, cst, tws, tws],
        out_specs=[blk, blk],
        out_shape=[out, out],
        compiler_params=_params(("parallel",)),
        name="fft_stage_b_fwd",
    )(ar, ai, plan.f_cos, plan.f_sin, plan.tw_cos[:, None, :], plan.tw_sin[:, None, :])


def _fft_b_conv_kernel(ar_ref, ai_ref, hr_ref, hi_ref, hb0_ref, fr_ref, fi_ref,
                       twr_ref, twi_ref, tcr_ref, tci_ref, cr_ref, ci_ref,
                       fwd_ref, inv_ref, kr_ref, ki_ref):
    n = DFT_N2

    @pl.when(pl.program_id(1) == 0)
    def _():
        fr = fr_ref[...]
        fi = fi_ref[...]
        gr, gi = _twiddled_dft(fr, fi, twr_ref[0], twi_ref[0])
        fwd_ref[:n, :n] = gr
        fwd_ref[:n, n:] = -gi
        fwd_ref[n:, :n] = gi
        fwd_ref[n:, n:] = gr
        gtr, gti = _twiddled_dft(fr, fi, tcr_ref[0], tci_ref[0])
        inv_ref[:n, :n] = gtr
        inv_ref[:n, n:] = gti
        inv_ref[n:, :n] = -gti
        inv_ref[n:, n:] = gtr
        kr_ref[...] = hr_ref[0, 0].astype(F32) + hr_ref[1, 0].astype(F32) - hb0_ref[...]
        ki_ref[...] = hi_ref[0, 0].astype(F32) - hi_ref[1, 0].astype(F32)

    a = jnp.concatenate([ar_ref[0, 0], ai_ref[0, 0]], axis=0)
    b = _dot(fwd_ref[...], a)
    br, bi = b[:n], b[n:]
    kr = kr_ref[...]
    ki = ki_ref[...]
    p = jnp.concatenate([(br * kr - bi * ki).astype(BF16), (br * ki + bi * kr).astype(BF16)], axis=0)
    c = _dot(inv_ref[...], p)
    cr_ref[0, 0] = c[:n].astype(BF16)
    ci_ref[0, 0] = c[n:].astype(BF16)


def fft_stage_b_conv(ar, ai, hr, hi, hb0, plan):
    batch = ar.shape[0]
    blk = pl.BlockSpec((1, 1, DFT_N2, D_MODEL), lambda k, b: (b, k, 0, 0))
    hblk = pl.BlockSpec((2, 1, DFT_N2, D_MODEL), lambda k, b: (0, k, 0, 0))
    cst = pl.BlockSpec((DFT_N2, DFT_N2), lambda k, b: (0, 0))
    tws = pl.BlockSpec((1, 1, DFT_N2), lambda k, b: (k, 0, 0))
    twc = pl.BlockSpec((1, DFT_N2, 1), lambda k, b: (k, 0, 0))
    out = jax.ShapeDtypeStruct(ar.shape, BF16)
    return pl.pallas_call(
        _fft_b_conv_kernel,
        grid=(plan.k1n, batch),
        in_specs=[blk, blk, hblk, hblk, pl.BlockSpec((1, D_MODEL), lambda k, b: (0, 0)),
                  cst, cst, tws, tws, twc, twc],
        out_specs=[blk, blk],
        out_shape=[out, out],
        scratch_shapes=[pltpu.VMEM((2 * DFT_N2, 2 * DFT_N2), BF16)] * 2
        + [pltpu.VMEM((DFT_N2, D_MODEL), F32)] * 2,
        compiler_params=_params(("parallel", "arbitrary")),
        name="fft_stage_b_conv",
    )(ar, ai, hr, hi, hb0, plan.f_cos, plan.f_sin,
      plan.tw_cos[:, None, :], plan.tw_sin[:, None, :],
      plan.tw_cos[:, :, None], plan.tw_sin[:, :, None])


def _fft_a_inv_kernel(cr_ref, ci_ref, vc_ref, vs_ref, z_ref, x0_ref, skip_ref, y_ref, *, kq):
    shape = z_ref.shape[1:2] + z_ref.shape[3:]
    for t in range(z_ref.shape[2]):
        cr = cr_ref[0, :, t].astype(F32).reshape(kq, D_MODEL).astype(BF16)
        ci = ci_ref[0, :, t].astype(F32).reshape(kq, D_MODEL).astype(BF16)
        conv = _dot(vc_ref[...], cr) + _dot(vs_ref[...], ci)
        y = conv.reshape(shape) + z_ref[0, :, t].astype(F32) * skip_ref[...]
        y_ref[0, :, t] = (y * x0_ref[0, :, t].astype(F32)).astype(BF16)


def fft_stage_a_inv(cr, ci, z, x0, skip, plan, batch):
    q = plan.kron
    nhi = DFT_N2 // q
    hb = _fft_group(plan)
    cshape = (batch, plan.k1n, nhi, q, D_MODEL)
    tshape = (batch, plan.r, nhi, q, D_MODEL)
    cspec = pl.BlockSpec((1, plan.k1n, hb, q, D_MODEL), lambda b, h: (b, 0, h, 0, 0))
    tspec = pl.BlockSpec((1, plan.r, hb, q, D_MODEL), lambda b, h: (b, 0, h, 0, 0))
    wspec = pl.BlockSpec(plan.inv_cos.shape, lambda b, h: (0, 0))
    y = pl.pallas_call(
        functools.partial(_fft_a_inv_kernel, kq=plan.k1n * q),
        grid=(batch, nhi // hb),
        in_specs=[cspec, cspec, wspec, wspec, tspec, tspec,
                  pl.BlockSpec((1, D_MODEL), lambda b, h: (0, 0))],
        out_specs=tspec,
        out_shape=jax.ShapeDtypeStruct(tshape, BF16),
        compiler_params=_params(("parallel", "parallel")),
        name="fft_stage_a_inv",
    )(cr.reshape(cshape), ci.reshape(cshape), plan.inv_cos, plan.inv_sin,
      z.reshape(tshape), x0.reshape(tshape), skip[None, :])
    return y.reshape(z.shape)


def hyena_mixer(src, mod, norm_g, p, batch, seq_len, route):
    z, x0, *rest = hyena_in(src, mod, norm_g, p["w_in"], p["conv_w"], p["conv_b"], seq_len)
    x = rest[0] if rest else src
    plan = _FFTPlan(seq_len)
    filt = hyena_filter(seq_len, p["w1"], p["b1"], p["w2"], p["b2"], p["w3"], p["freq"])
    fr, fi = fft_stage_a(filt.reshape(2 * seq_len, D_MODEL), plan, 2)
    hr, hi = fft_stage_b_fwd(fr, fi, plan)
    ar, ai = fft_stage_a(z, plan, batch)
    cr, ci = fft_stage_b_conv(ar, ai, hr, hi, filt[1, 0:1, :], plan)
    y = fft_stage_a_inv(cr, ci, z, x0, p["skip"], plan, batch)
    return proj_residual(y, p["w_out"], x, mod, 2, seq_len, route)


def _pool_kernel(*refs, tm, seq_len, n_stream):
    stream, refs = refs[:n_stream], refs[n_stream:]
    mod_ref, g_ref, w_ref, s_ref = refs[:4]
    refs = refs[4:]
    route_in, o_ref, route_out = refs[:N_ROUTE_IN], refs[N_ROUTE_IN], refs[N_ROUTE_IN + 1:]
    first, last = _edge_flags(tm, seq_len)
    m = mod_ref[0]
    x, ext = _stream_rows(stream)
    h = _modulate(ext, g_ref[...], m[0:1], m[1:2])
    n = tm + 2 * HALO
    row = lax.broadcasted_iota(jnp.int32, (n, 1), 0)
    outside = jnp.logical_or(jnp.logical_and(first, row < HALO),
                             jnp.logical_and(last, row >= HALO + tm))
    h = jnp.where(outside, 0.0, h)
    pos = (pl.program_id(0) * tm) % seq_len + lax.broadcasted_iota(jnp.int32, (tm, 1), 0)
    ys = []
    for gi, win in enumerate(POOL_WINDOWS):
        cols = slice(gi * POOL_GROUP_DIM, (gi + 1) * POOL_GROUP_DIM)
        hg = h[:, cols]
        acc = hg
        span = 1
        while span < win:
            acc = acc + pltpu.roll(acc, span, axis=0)
            span *= 2
        lead = win // 2 - 1
        if lead:
            acc = pltpu.roll(acc, n - lead, axis=0)
        half = win // 2
        cnt = jnp.minimum(pos + half, seq_len) - jnp.maximum(pos - half, 0)
        pooled = acc[HALO:HALO + tm] / cnt.astype(F32) - hg[HALO:HALO + tm]
        ys.append(_dot(pooled.astype(BF16), w_ref[gi]))
    y = jnp.concatenate(ys, axis=1) * s_ref[...]
    x1 = x + m[2:3] * y
    o_ref[...] = x1
    _route_tile(x1, m, *route_in, *route_out)


def pool_mixer(src, mod, norm_g, w_group, scale, seq_len, route):
    n_rows = (src.x if isinstance(src, _Pending) else src).shape[0]
    tm = ROW_TILE
    s_ops, s_specs = _stream_io(src, tm, n_rows, seq_len)
    r_ops, r_in, r_out, r_shape = _router_io(route, tm, n_rows)
    return pl.pallas_call(
        functools.partial(_pool_kernel, tm=tm, seq_len=seq_len, n_stream=len(s_ops)),
        grid=(n_rows // tm,),
        in_specs=s_specs + [_mod_spec(tm, seq_len), _const_spec((1, D_MODEL)),
                            _const_spec(w_group.shape), _const_spec((1, D_MODEL))] + r_in,
        out_specs=[_row_spec(tm)] + r_out,
        out_shape=[jax.ShapeDtypeStruct((n_rows, D_MODEL), F32)] + r_shape,
        compiler_params=_params(("arbitrary",)),
        name="pool_mixer",
    )(*s_ops, mod, norm_g[None, :], w_group.astype(BF16), scale[None, :], *r_ops)


def _sconv_kernel(*refs, tm, seq_len, n_stream):
    stream, refs = refs[:n_stream], refs[n_stream:]
    mod_ref, g_ref, w_ref, cw_ref, cb_ref, wo_ref = refs[:6]
    refs = refs[6:]
    route_in, o_ref = refs[:N_ROUTE_IN], refs[N_ROUTE_IN]
    route_out, y_ref = refs[N_ROUTE_IN + 1:N_ROUTE_IN + 1 + N_ROUTE_OUT], refs[-1]
    first, last = _edge_flags(tm, seq_len)
    m = mod_ref[0]
    x, ext = _stream_rows(stream)
    h = _modulate(ext, g_ref[...], m[0:1], m[1:2]).astype(BF16)
    tn = 256
    for j in range(D_MODEL // tn):
        cols = slice(j * tn, (j + 1) * tn)
        bg = _dot(h, w_ref[:, cols])[HALO:HALO + tm]
        cg = _dot(h, w_ref[:, D_MODEL + j * tn:D_MODEL + (j + 1) * tn])
        hp = _dot(h, w_ref[:, 2 * D_MODEL + j * tn:2 * D_MODEL + (j + 1) * tn])
        y_ref[:, cols] = (bg * _conv3(cg * hp, cw_ref, cb_ref, cols, tm, first, last)).astype(BF16)
    x1 = x + m[2:3] * _dot(y_ref[...], wo_ref[...])
    o_ref[...] = x1
    _route_tile(x1, m, *route_in, *route_out)


def sconv_mixer(src, mod, norm_g, w_in, conv_w, conv_b, w_out, seq_len, route):
    n_rows = (src.x if isinstance(src, _Pending) else src).shape[0]
    tm = ROW_TILE
    s_ops, s_specs = _stream_io(src, tm, n_rows, seq_len)
    r_ops, r_in, r_out, r_shape = _router_io(route, tm, n_rows)
    return pl.pallas_call(
        functools.partial(_sconv_kernel, tm=tm, seq_len=seq_len, n_stream=len(s_ops)),
        grid=(n_rows // tm,),
        in_specs=s_specs + [_mod_spec(tm, seq_len), _const_spec((1, D_MODEL)),
                            _const_spec((D_MODEL, 3 * D_MODEL)), _const_spec((3, D_MODEL)),
                            _const_spec((1, D_MODEL)), _const_spec((D_MODEL, D_MODEL))] + r_in,
        out_specs=[_row_spec(tm)] + r_out,
        out_shape=[jax.ShapeDtypeStruct((n_rows, D_MODEL), F32)] + r_shape,
        scratch_shapes=[pltpu.VMEM((tm, D_MODEL), BF16)],
        compiler_params=_params(("arbitrary",)),
        name="sconv_mixer",
    )(*s_ops, mod, norm_g[None, :], w_in.astype(BF16), conv_w, conv_b[None, :],
      w_out.astype(BF16), *r_ops)


def _pack_bf16(x):
    w = x.shape[1] // 2
    bits = pltpu.bitcast(x.astype(BF16).astype(F32), jnp.uint32)
    return (bits[:, :w] >> 16) | bits[:, w:]


def _unpack_bf16(p):
    lo = pltpu.bitcast(p << 16, F32)
    hi = pltpu.bitcast(p & jnp.uint32(0xFFFF0000), F32)
    return jnp.concatenate([lo, hi], axis=1)


META_E1, META_E2, META_W1, META_W2, META_R1, META_R2 = range(6)
META_ROWS = 8


def _route_tile(x, m, g_ref, wh_ref, b_ref, tri_ref, h_ref, meta_ref, meta_t_ref, cnt_ref):
    @pl.when(pl.program_id(0) == 0)
    def _():
        cnt_ref[...] = jnp.zeros_like(cnt_ref)

    ms = jnp.mean(x * x, axis=-1, keepdims=True)
    h = x * lax.rsqrt(ms + EPS) * (g_ref[...] * (1.0 + m[4:5])) + m[3:4]
    hi = h.astype(BF16)
    hi32 = hi.astype(F32)
    lo = (h - hi32).astype(BF16)
    half = D_MODEL // 2
    bits = pltpu.bitcast(hi32, jnp.uint32)
    h_ref[...] = (bits[:, :half] >> 16) | bits[:, half:]
    part = _dot(hi, wh_ref[...])
    lg = part[:, :LANES] + (part[:, LANES:] + _dot(lo, wh_ref[:, :LANES])) + b_ref[...]
    lgt = lg.T
    tm = lgt.shape[1]
    neg = -jnp.inf
    sub = lax.broadcasted_iota(jnp.int32, (HALO, tm), 0).astype(F32)

    def first_argmax(vals):
        top = jnp.max(vals, axis=0, keepdims=True)
        idx = jnp.min(jnp.where(vals == top, sub, float(HALO)), axis=0, keepdims=True)
        return top, idx

    gl = jnp.where(sub < MOE_GROUPS, lgt[MOE_N_EXPERTS:MOE_N_EXPERTS + HALO], neg)
    gmax, grp = first_argmax(gl)
    g_w = 1.0 / jnp.sum(jnp.exp(gl - gmax), axis=0, keepdims=True)
    el = lgt[:MOE_EXPERTS_PER_GROUP]
    for g in range(1, MOE_GROUPS):
        el = jnp.where(grp == float(g),
                       lgt[g * MOE_EXPERTS_PER_GROUP:(g + 1) * MOE_EXPERTS_PER_GROUP], el)
    v1, i1 = first_argmax(el)
    v2, i2 = first_argmax(jnp.where(sub == i1, neg, el))
    ex = jnp.exp(v2 - v1)
    w1 = 1.0 / (1.0 + ex)
    w2 = ex * w1
    e1 = grp * MOE_EXPERTS_PER_GROUP + i1
    e2 = grp * MOE_EXPERTS_PER_GROUP + i2
    expert = lax.broadcasted_iota(jnp.int32, (MOE_N_EXPERTS, tm), 0).astype(F32)
    onehot = jnp.where(jnp.logical_or(expert == e1, expert == e2), 1.0, 0.0)
    before = _dot(onehot.astype(BF16), tri_ref[...]) + cnt_ref[...]
    cnt_ref[...] += jnp.sum(onehot, axis=1, keepdims=True)
    r1 = jnp.sum(jnp.where(expert == e1, before, 0.0), axis=0, keepdims=True)
    r2 = jnp.sum(jnp.where(expert == e2, before, 0.0), axis=0, keepdims=True)
    fields = ((META_E1, e1), (META_E2, e2), (META_W1, w1 * g_w), (META_W2, w2 * g_w),
              (META_R1, r1), (META_R2, r2))
    field = lax.broadcasted_iota(jnp.int32, (LANES, tm), 0)
    meta_t = jnp.zeros((LANES, tm), F32)
    for row, val in fields:
        meta_t = jnp.where(field == row, val, meta_t)
    meta_t_ref[...] = meta_t[:META_ROWS]
    meta_ref[...] = meta_t.T


N_ROUTE_IN = 4
N_ROUTE_OUT = 4


def _router_io(route, tm, n_rows):
    norm_g, w_group, b_group, w_router, b_router = route
    w = _pad_to(jnp.concatenate([w_router, w_group], axis=1), D_MODEL, LANES)
    wh = w.astype(BF16)
    wl = (w - wh.astype(F32)).astype(BF16)
    whl = jnp.concatenate([wh, wl], axis=1)
    b = _pad_to(jnp.concatenate([b_router, b_group])[None, :], 1, LANES)
    tri = jnp.asarray(np.triu(np.ones((tm, tm), np.float32), 1), BF16)
    operands = (norm_g[None, :], whl, b, tri)
    in_specs = [_const_spec((1, D_MODEL)), _const_spec((D_MODEL, 2 * LANES)),
                _const_spec((1, LANES)), _const_spec((tm, tm))]
    out_specs = [_row_spec(tm, D_MODEL // 2), _row_spec(tm, LANES),
                 pl.BlockSpec((META_ROWS, tm), lambda i: (0, i)),
                 _const_spec((MOE_N_EXPERTS, 1))]
    out_shape = [jax.ShapeDtypeStruct((n_rows, D_MODEL // 2), jnp.uint32),
                 jax.ShapeDtypeStruct((n_rows, LANES), F32),
                 jax.ShapeDtypeStruct((META_ROWS, n_rows), F32),
                 jax.ShapeDtypeStruct((MOE_N_EXPERTS, 1), F32)]
    return operands, in_specs, out_specs, out_shape


def _slot_kernel(offs_ref, meta_t_ref, pos_ref):
    meta = meta_t_ref[...]
    start = jnp.zeros_like(meta)
    for e in range(MOE_N_EXPERTS):
        start = jnp.where(meta == float(e), offs_ref[e], start)
    shift = META_ROWS - (META_R1 - META_E1)
    pos_ref[...] = (start + pltpu.roll(meta, shift, axis=0)).astype(jnp.int32)


def moe_slots(meta_t, offsets):
    n_rows = meta_t.shape[1]
    tn = min(n_rows, 8192)
    blk = pl.BlockSpec((META_ROWS, tn), lambda i, offs: (0, i))
    return pl.pallas_call(
        _slot_kernel,
        grid_spec=pltpu.PrefetchScalarGridSpec(
            num_scalar_prefetch=1, grid=(n_rows // tn,), in_specs=[blk], out_specs=blk),
        out_shape=jax.ShapeDtypeStruct((META_ROWS, n_rows), jnp.int32),
        compiler_params=_params(("parallel",)),
        name="moe_slots",
    )(offsets, meta_t)


def _expert_kernel(te_ref, nu_ref, nv_ref, xs_ref, wg_ref, wu_ref, wd_ref, o_ref,
                   wgu_ref, wdb_ref):
    j = pl.program_id(0)

    @pl.when(jnp.logical_or(j == 0, te_ref[j] != te_ref[jnp.maximum(j - 1, 0)]))
    def _():
        wgu_ref[:, :MOE_D_FF] = wg_ref[0, 0].astype(BF16)
        wgu_ref[:, MOE_D_FF:] = wu_ref[0, 0].astype(BF16)
        wdb_ref[...] = wd_ref[0, 0].astype(BF16)

    @pl.when(j < nu_ref[0])
    def _():
        x = _unpack_bf16(xs_ref[...])
        row = lax.broadcasted_iota(jnp.int32, (x.shape[0], 1), 0)
        x = jnp.where(row < nv_ref[j], x, 0.0).astype(BF16)
        au = _dot(x, wgu_ref[...])
        hh = (_silu(au[:, :MOE_D_FF]) * au[:, MOE_D_FF:]).astype(BF16)
        o_ref[...] = _pack_bf16(_dot(hh, wdb_ref[...]))


def moe_experts(xs, tile_expert, n_used, n_valid, w_gate, w_up, w_down, layer, tm):
    n_slots, half = xs.shape
    wspec = lambda shape: pl.BlockSpec((1, 1) + shape, lambda j, te, nu, nv: (layer, te[j], 0, 0))
    row = pl.BlockSpec((tm, half), lambda j, te, nu, nv: (j, 0))
    row_in = pl.BlockSpec((tm, half), lambda j, te, nu, nv: (jnp.minimum(j, nu[0] - 1), 0))
    grid_spec = pltpu.PrefetchScalarGridSpec(
        num_scalar_prefetch=3,
        grid=(n_slots // tm,),
        in_specs=[row_in, wspec((D_MODEL, MOE_D_FF)), wspec((D_MODEL, MOE_D_FF)),
                  wspec((MOE_D_FF, D_MODEL))],
        out_specs=row,
        scratch_shapes=[pltpu.VMEM((D_MODEL, 2 * MOE_D_FF), BF16),
                        pltpu.VMEM((MOE_D_FF, D_MODEL), BF16)],
    )
    return pl.pallas_call(
        _expert_kernel,
        grid_spec=grid_spec,
        out_shape=jax.ShapeDtypeStruct((n_slots, half), jnp.uint32),
        compiler_params=_params(("arbitrary",)),
        name="moe_experts",
    )(tile_expert, n_used, n_valid, xs, w_gate, w_up, w_down)


def _combine_kernel(x_ref, ya_ref, yb_ref, meta_ref, mod_ref, o_ref):
    o_ref[...] = _moe_combined(x_ref, ya_ref, yb_ref, meta_ref, mod_ref[0][5:6])


def moe_combine(pending, seq_len):
    x, ya, yb, meta, mod = pending
    n_rows = x.shape[0]
    tm = STREAM_TILE
    half = D_MODEL // 2
    return pl.pallas_call(
        _combine_kernel,
        grid=(n_rows // tm,),
        in_specs=[_row_spec(tm), _row_spec(tm, half), _row_spec(tm, half), _row_spec(tm, LANES),
                  _mod_spec(tm, seq_len)],
        out_specs=_row_spec(tm),
        out_shape=jax.ShapeDtypeStruct((n_rows, D_MODEL), F32),
        compiler_params=_params(("parallel",)),
        name="moe_combine",
    )(x, ya, yb, meta, mod)


MOE_TILE = 512
MOE_TILE_LARGE = 1024
MOE_LARGE_ROWS = 32768


def hier_moe(routed, mod, w_gate, w_up, w_down, layer, seq_len):
    x, hp, meta, meta_t, counts = routed
    n_rows = x.shape[0]
    tm = MOE_TILE_LARGE if n_rows >= MOE_LARGE_ROWS else MOE_TILE
    cnt = counts[:, 0].astype(jnp.int32)
    padded = (cnt + tm - 1) // tm * tm
    ends = jnp.cumsum(padded)
    starts = ends - padded
    n_slots = 2 * n_rows + MOE_N_EXPERTS * tm
    tile_start = jnp.arange(n_slots // tm, dtype=jnp.int32) * tm
    tile_expert = jnp.minimum(jnp.sum(tile_start[:, None] >= ends[None, :], axis=1),
                              MOE_N_EXPERTS - 1).astype(jnp.int32)
    n_used = (ends[-1:] // tm).astype(jnp.int32)
    n_valid = jnp.clip(starts[tile_expert] + cnt[tile_expert] - tile_start, 0, tm).astype(jnp.int32)
    pos = moe_slots(meta_t, starts.astype(F32))
    idx0 = pos[META_E1].reshape(n_rows // SC_BLOCK, SC_BLOCK)
    idx1 = pos[META_E2].reshape(n_rows // SC_BLOCK, SC_BLOCK)
    xs = sc_scatter_rows(hp, idx0, idx1, n_slots)
    ys = moe_experts(xs, tile_expert, n_used, n_valid, w_gate, w_up, w_down, layer, tm)
    ya, yb = sc_gather_rows(ys, idx0, idx1)
    return _Pending(x, ya, yb, meta, mod)


SC_CORES = 2
SC_SUBCORES = 16
SC_WORKERS = SC_CORES * SC_SUBCORES
SC_BLOCK = 128


def _sc_mesh():
    return plsc.VectorSubcoreMesh(core_axis_name="c", subcore_axis_name="s")


def _sc_worker():
    return lax.axis_index("s") * SC_CORES + lax.axis_index("c")


def sc_scatter_rows(rows, idx0, idx1, n_slots):
    n_rows, width = rows.shape
    per_worker = n_rows // SC_BLOCK // SC_WORKERS

    @functools.partial(
        pl.kernel, mesh=_sc_mesh(),
        out_type=jax.ShapeDtypeStruct((n_slots, width), rows.dtype),
        scratch_types=[pltpu.VMEM((SC_BLOCK,), jnp.int32), pltpu.VMEM((SC_BLOCK,), jnp.int32),
                       pltpu.VMEM((SC_BLOCK, width), rows.dtype)],
        name="sc_scatter_rows",
    )
    def scatter(rows_hbm, i0_hbm, i1_hbm, out_hbm, i0_v, i1_v, rows_v):
        first = _sc_worker() * per_worker

        @pl.loop(0, per_worker)
        def _(j):
            blk = first + j
            pltpu.sync_copy(i0_hbm.at[blk], i0_v)
            pltpu.sync_copy(i1_hbm.at[blk], i1_v)
            pltpu.sync_copy(rows_hbm.at[pl.ds(blk * SC_BLOCK, SC_BLOCK)], rows_v)
            pltpu.sync_copy(rows_v, out_hbm.at[i0_v])
            pltpu.sync_copy(rows_v, out_hbm.at[i1_v])

    return scatter(rows, idx0, idx1)


def sc_gather_rows(src, idx0, idx1):
    width = src.shape[1]
    n_rows = idx0.shape[0] * SC_BLOCK
    per_worker = n_rows // SC_BLOCK // SC_WORKERS
    out = jax.ShapeDtypeStruct((n_rows, width), src.dtype)

    @functools.partial(
        pl.kernel, mesh=_sc_mesh(), out_type=(out, out),
        scratch_types=[pltpu.VMEM((SC_BLOCK,), jnp.int32), pltpu.VMEM((SC_BLOCK, width), src.dtype)],
        name="sc_gather_rows",
    )
    def gather(src_hbm, i0_hbm, i1_hbm, a_hbm, b_hbm, idx_v, rows_v):
        first = _sc_worker() * per_worker

        @pl.loop(0, per_worker)
        def _(j):
            blk = first + j
            dst = pl.ds(blk * SC_BLOCK, SC_BLOCK)
            pltpu.sync_copy(i0_hbm.at[blk], idx_v)
            pltpu.sync_copy(src_hbm.at[idx_v], rows_v)
            pltpu.sync_copy(rows_v, a_hbm.at[dst])
            pltpu.sync_copy(i1_hbm.at[blk], idx_v)
            pltpu.sync_copy(src_hbm.at[idx_v], rows_v)
            pltpu.sync_copy(rows_v, b_hbm.at[dst])

    return gather(src, idx0, idx1)


def _trunk(x3, mods, p):
    batch, seq_len, _ = x3.shape
    x = x3.reshape(batch * seq_len, D_MODEL)
    for i in range(DEPTH):
        mod = mods[i]
        g1 = p["norm1_g"][i]
        route = (p["norm2_g"][i], p["moe_w_group"][i], p["moe_b_group"][i], p["moe_w_router"][i],
                 p["moe_b_router"][i])
        kind = i % 4
        if kind == 0:
            if isinstance(x, _Pending):
                x = moe_combine(x, seq_len)
            q, k, v = attn_qkv(x, mod, g1, p["attn_wqkv"][0], p["attn_q_norm"][0],
                               p["attn_k_norm"][0], seq_len)
            o = attn_flash(q, k, v, batch, seq_len)
            routed = proj_residual(o, p["attn_wo"][0], x, mod, 2, seq_len, route)
        elif kind == 1:
            hp = {"w_in": p["hy_w_in"][0], "conv_w": p["hy_conv_w"][0], "conv_b": p["hy_conv_b"][0],
                  "w1": p["hy_ffn_w1"][0], "b1": p["hy_ffn_b1"][0], "w2": p["hy_ffn_w2"][0],
                  "b2": p["hy_ffn_b2"][0], "w3": p["hy_ffn_w3"][0], "freq": p["hy_freq"][0],
                  "skip": p["hy_skip"][0], "w_out": p["hy_w_out"][0]}
            routed = hyena_mixer(x, mod, g1, hp, batch, seq_len, route)
        elif kind == 2:
            routed = pool_mixer(x, mod, g1, p["pool_w"][0], p["pool_scale"][0], seq_len, route)
        else:
            routed = sconv_mixer(x, mod, g1, p["sc_w_in"][0], p["sc_conv_w"][0],
                                 p["sc_conv_b"][0], p["sc_w_out"][0], seq_len, route)
        x = hier_moe(routed, mod, p["moe_w_gate"], p["moe_w_up"], p["moe_w_down"], i, seq_len)
    return moe_combine(x, seq_len).reshape(batch, seq_len, D_MODEL)


def kernel(x_prompt, x_sample, c_prompt, c_sample, norm1_g, norm2_g, ada_w, ada_b, attn_wqkv, attn_q_norm, attn_k_norm, attn_wo, hy_w_in, hy_conv_w, hy_conv_b, hy_ffn_w1, hy_ffn_b1, hy_ffn_w2, hy_ffn_b2, hy_ffn_w3, hy_freq, hy_skip, hy_w_out, pool_w, pool_scale, sc_w_in, sc_conv_w, sc_conv_b, sc_w_out, moe_w_group, moe_b_group, moe_w_router, moe_b_router, moe_w_gate, moe_w_up, moe_w_down):
    p = dict(norm1_g=norm1_g, norm2_g=norm2_g, attn_wqkv=attn_wqkv, attn_q_norm=attn_q_norm,
             attn_k_norm=attn_k_norm, attn_wo=attn_wo, hy_w_in=hy_w_in, hy_conv_w=hy_conv_w,
             hy_conv_b=hy_conv_b, hy_ffn_w1=hy_ffn_w1, hy_ffn_b1=hy_ffn_b1, hy_ffn_w2=hy_ffn_w2,
             hy_ffn_b2=hy_ffn_b2, hy_ffn_w3=hy_ffn_w3, hy_freq=hy_freq, hy_skip=hy_skip,
             hy_w_out=hy_w_out, pool_w=pool_w, pool_scale=pool_scale, sc_w_in=sc_w_in,
             sc_conv_w=sc_conv_w, sc_conv_b=sc_conv_b, sc_w_out=sc_w_out, moe_w_group=moe_w_group,
             moe_b_group=moe_b_group, moe_w_router=moe_w_router, moe_b_router=moe_b_router,
             moe_w_gate=moe_w_gate, moe_w_up=moe_w_up, moe_w_down=moe_w_down)
    nb = c_prompt.shape[0]
    ns = c_sample.shape[0]
    rows = -(-(nb + ns) // HALO) * HALO
    c_all = jnp.pad(jnp.concatenate([c_prompt, c_sample], axis=0), ((0, rows - nb - ns), (0, 0)))
    mod = ada_mod(c_all, ada_w, ada_b).reshape(DEPTH, rows, 6, D_MODEL)
    mods_prompt = [mod[i, :nb] for i in range(DEPTH)]
    mods_sample = [mod[i, nb:nb + ns] for i in range(DEPTH)]
    return _trunk(x_prompt, mods_prompt, p), _trunk(x_sample, mods_sample, p)
```

```python
import functools
import math
from typing import NamedTuple

import jax
import jax.numpy as jnp
import numpy as np
from jax import lax
from jax.experimental import pallas as pl
from jax.experimental.pallas import tpu as pltpu
from jax.experimental.pallas import tpu_sc as plsc

F32 = jnp.float32
BF16 = jnp.bfloat16
F8 = jnp.float8_e4m3fn

D_MODEL = 1024
DEPTH = 4
EPS = 1e-6
GRID_W = 64
HEAD_DIM = 64
N_HEADS = 16
N_KV_HEADS = 4
Q_PER_KV = 4
ROPE_THETA = 10000.0
ROPE_FREQS = 16
HY_EMB_DIM = 33
HY_BANDS = 16
HY_FILTER_WIDTH = 64
HY_FAST_DECAY = 0.3
HY_SLOW_DECAY = 1.5
HY_TARGET = 1e-2
POOL_WINDOWS = (2, 4, 8, 16)
POOL_GROUP_DIM = 256
MOE_GROUPS = 4
MOE_EXPERTS_PER_GROUP = 8
MOE_N_EXPERTS = 32
MOE_D_FF = 256

LANES = 128
HALO = 8
DFT_N2 = 256
VMEM_LIMIT = 56 * 1024 * 1024

ROW_TILE = 512
STREAM_TILE = 1024


def _params(sem):
    return pltpu.CompilerParams(dimension_semantics=sem, vmem_limit_bytes=VMEM_LIMIT)


def _dot(a, b):
    return jnp.dot(a, b, preferred_element_type=F32)


def _split(a):
    hi = a.astype(BF16)
    lo = (a - hi.astype(F32)).astype(BF16)
    return hi, lo


def _dot3(a, b):
    ah, al = _split(a)
    bh, bl = _split(b)
    return _dot(ah, bh) + (_dot(ah, bl) + _dot(al, bh))


def _modulate(x, g, shift, scale):
    ms = jnp.mean(x * x, axis=-1, keepdims=True)
    return x * lax.rsqrt(ms + EPS) * g * (1.0 + scale) + shift


def _silu(x):
    return x * (1.0 / (1.0 + jnp.exp(-x)))


def _ada_kernel(c_ref, w_ref, b_ref, o_ref):
    c = c_ref[...]
    o_ref[0] = _dot3(_silu(c), w_ref[0]) + b_ref[0]


def ada_mod(c_all, ada_w, ada_b):
    rows = c_all.shape[0]
    n = ada_w.shape[2]
    tn = 1536
    return pl.pallas_call(
        _ada_kernel,
        grid=(DEPTH, n // tn),
        in_specs=[
            pl.BlockSpec((rows, D_MODEL), lambda l, j: (0, 0)),
            pl.BlockSpec((1, D_MODEL, tn), lambda l, j: (l, 0, j)),
            pl.BlockSpec((1, 1, tn), lambda l, j: (l, 0, j)),
        ],
        out_specs=pl.BlockSpec((1, rows, tn), lambda l, j: (l, 0, j)),
        out_shape=jax.ShapeDtypeStruct((DEPTH, rows, n), F32),
        compiler_params=_params(("parallel", "parallel")),
        name="ada_mod",
    )(c_all, ada_w, ada_b.reshape(DEPTH, 1, n))


def _row_spec(tm, width=D_MODEL):
    return pl.BlockSpec((tm, width), lambda i: (i, 0))


def _mod_spec(tm, seq_len):
    return pl.BlockSpec((1, 6, D_MODEL), lambda i: ((i * tm) // seq_len, 0, 0))


def _const_spec(shape):
    nd = len(shape)
    return pl.BlockSpec(shape, lambda i: (0,) * nd)


def _halo_specs(tm, n_rows, width=D_MODEL):
    per = tm // HALO
    last = n_rows // HALO - 1
    prev = pl.BlockSpec((HALO, width), lambda i: (jnp.maximum(i * per - 1, 0), 0))
    nxt = pl.BlockSpec((HALO, width), lambda i: (jnp.minimum((i + 1) * per, last), 0))
    return prev, nxt


class _Pending(NamedTuple):
    x: jax.Array
    ya: jax.Array
    yb: jax.Array
    meta: jax.Array
    mod: jax.Array


N_PLAIN_REFS = 3
N_PENDING_REFS = 13


def _stream_io(src, tm, n_rows, seq_len):
    if not isinstance(src, _Pending):
        prev, nxt = _halo_specs(tm, n_rows)
        return [src] * 3, [prev, _row_spec(tm), nxt]
    operands, specs = [], []
    for arr in (src.x, src.ya, src.yb, src.meta):
        width = arr.shape[1]
        prev, nxt = _halo_specs(tm, n_rows, width)
        operands += [arr] * 3
        specs += [prev, _row_spec(tm, width), nxt]
    return operands + [src.mod], specs + [_mod_spec(tm, seq_len)]


def _moe_combined(x_ref, ya_ref, yb_ref, meta_ref, gate):
    meta = meta_ref[...]
    y = (meta[:, META_W1:META_W1 + 1] * _unpack_bf16(ya_ref[...])
         + meta[:, META_W2:META_W2 + 1] * _unpack_bf16(yb_ref[...]))
    return x_ref[...] + gate * y


def _stream_rows(refs):
    if len(refs) == N_PLAIN_REFS:
        prev_ref, x_ref, next_ref = refs
        x = x_ref[...]
        return x, jnp.concatenate([prev_ref[...], x, next_ref[...]], axis=0)
    gate = refs[-1][0][5:6]
    parts = [_moe_combined(refs[k], refs[3 + k], refs[6 + k], refs[9 + k], gate) for k in range(3)]
    return parts[1], jnp.concatenate(parts, axis=0)


def _edge_flags(tm, seq_len):
    i = pl.program_id(0)
    per_seq = seq_len // tm
    pos = i % per_seq
    return pos == 0, pos == per_seq - 1


def _shift_rows(u, tm):
    n = u.shape[0]
    up = pltpu.roll(u, 1, axis=0)[HALO:HALO + tm]
    dn = pltpu.roll(u, n - 1, axis=0)[HALO:HALO + tm]
    return up, u[HALO:HALO + tm], dn


def _conv3(u, w_ref, b_ref, cols, tm, first, last, stage_ref=None):
    if stage_ref is None:
        up, mid, dn = _shift_rows(u, tm)
    else:
        stage_ref[...] = u
        up = stage_ref[pl.ds(HALO - 1, tm), :]
        mid = stage_ref[pl.ds(HALO, tm), :]
        dn = stage_ref[pl.ds(HALO + 1, tm), :]
    row = lax.broadcasted_iota(jnp.int32, (tm, 1), 0)
    up = jnp.where(jnp.logical_and(first, row == 0), 0.0, up)
    dn = jnp.where(jnp.logical_and(last, row == tm - 1), 0.0, dn)
    w = w_ref[:, cols]
    return up * w[0:1] + mid * w[1:2] + dn * w[2:3] + b_ref[:, cols]


def _norm_rope(t, gain, headmean, cos, sin_signed):
    width = t.shape[1]
    ms = _dot((t * t).astype(BF16), headmean[:width, :width])
    y = t * lax.rsqrt(ms + EPS) * gain
    lane = lax.broadcasted_iota(jnp.int32, y.shape, 1)
    first = (lane % 32) < ROPE_FREQS
    partner = jnp.where(first, pltpu.roll(y, width - ROPE_FREQS, axis=1),
                        pltpu.roll(y, ROPE_FREQS, axis=1))
    reps = width // LANES
    return y * jnp.tile(cos, (1, reps)) + partner * jnp.tile(sin_signed, (1, reps))


def _qkv_kernel(x_ref, mod_ref, g_ref, w_ref, qg_ref, kg_ref, hm_ref, cos_ref, sin_ref,
                q_ref, k_ref, v_ref):
    m = mod_ref[0]
    h = _modulate(x_ref[...], g_ref[...], m[0:1], m[1:2]).astype(BF16)
    qkv = _dot(h, w_ref[...])
    nq = N_HEADS * HEAD_DIM
    nk = N_KV_HEADS * HEAD_DIM
    cos = cos_ref[...]
    sin = sin_ref[...]
    hm = hm_ref[...]
    q = _norm_rope(qkv[:, :nq], qg_ref[...], hm, cos, sin)
    k = _norm_rope(qkv[:, nq:nq + nk], kg_ref[...], hm, cos, sin)
    v = qkv[:, nq + nk:]
    q_ref[...] = q.astype(BF16)
    ones = jnp.ones((v.shape[0], HEAD_DIM), F32)
    for g in range(N_KV_HEADS):
        sl = slice(g * HEAD_DIM, (g + 1) * HEAD_DIM)
        k_ref[g] = k[:, sl].astype(F8)
        v_ref[g] = jnp.concatenate([v[:, sl], ones], axis=1).astype(BF16)


def _rope_tables(seq_len):
    rows = seq_len // GRID_W
    r = jnp.broadcast_to(jnp.arange(rows)[:, None], (rows, GRID_W)).reshape(-1)
    c = jnp.broadcast_to(jnp.arange(GRID_W)[None, :], (rows, GRID_W)).reshape(-1)
    inv_freq = ROPE_THETA ** (-jnp.arange(ROPE_FREQS, dtype=F32) / ROPE_FREQS)
    pos = jnp.stack([r, c], axis=-1).astype(F32)
    ang = pos[:, :, None] * inv_freq[None, None, :]
    cos = jnp.cos(ang)
    sin = jnp.sin(ang)
    cos64 = jnp.concatenate([cos, cos], axis=-1).reshape(seq_len, HEAD_DIM)
    sin64 = jnp.concatenate([-sin, sin], axis=-1).reshape(seq_len, HEAD_DIM)
    return jnp.tile(cos64, (1, 2)), jnp.tile(sin64, (1, 2))


def attn_qkv(x, mod, norm_g, wqkv, q_norm, k_norm, seq_len):
    n_rows = x.shape[0]
    tm = STREAM_TILE
    nq = N_HEADS * HEAD_DIM
    nk = N_KV_HEADS * HEAD_DIM
    cos, sin = _rope_tables(seq_len)
    qg = jnp.tile(q_norm, N_HEADS)[None, :] * (HEAD_DIM ** -0.5 * math.log2(math.e))
    kg = jnp.tile(k_norm, N_KV_HEADS)[None, :]
    head = np.arange(nq) // HEAD_DIM
    headmean = jnp.asarray((head[:, None] == head[None, :]).astype(np.float32) / HEAD_DIM, BF16)
    per_seq = seq_len // tm
    tab_spec = pl.BlockSpec((tm, LANES), lambda i: (i % per_seq, 0))
    return pl.pallas_call(
        _qkv_kernel,
        grid=(n_rows // tm,),
        in_specs=[
            _row_spec(tm), _mod_spec(tm, seq_len), _const_spec((1, D_MODEL)),
            _const_spec((D_MODEL, nq + 2 * nk)), _const_spec((1, nq)), _const_spec((1, nk)),
            _const_spec((nq, nq)), tab_spec, tab_spec,
        ],
        out_specs=[
            _row_spec(tm, nq),
            pl.BlockSpec((N_KV_HEADS, tm, HEAD_DIM), lambda i: (0, i, 0)),
            pl.BlockSpec((N_KV_HEADS, tm, 2 * HEAD_DIM), lambda i: (0, i, 0)),
        ],
        out_shape=[
            jax.ShapeDtypeStruct((n_rows, nq), BF16),
            jax.ShapeDtypeStruct((N_KV_HEADS, n_rows, HEAD_DIM), F8),
            jax.ShapeDtypeStruct((N_KV_HEADS, n_rows, 2 * HEAD_DIM), BF16),
        ],
        compiler_params=_params(("parallel",)),
        name="attn_qkv",
    )(x, mod, norm_g[None, :], wqkv.astype(BF16), qg, kg, headmean, cos, sin)


def _flash_kernel(q_ref, k_ref, v_ref, o_ref, *, tq, tk, n_chunks):
    q = q_ref[...]
    qs = jnp.concatenate([q[:, j * HEAD_DIM:(j + 1) * HEAD_DIM] for j in range(Q_PER_KV)], axis=0)
    qs = qs.astype(F8)
    rows = Q_PER_KV * tq

    def body(c, carry):
        m, acc = carry
        start = pl.multiple_of(c * tk, tk)
        kc = k_ref[0, pl.ds(start, tk), :]
        vc = v_ref[0, pl.ds(start, tk), :]
        s = lax.dot_general(qs, kc, (((1,), (1,)), ((), ())), preferred_element_type=F32)
        m_new = jnp.maximum(m, jnp.max(s, axis=-1, keepdims=True))
        alpha = jnp.exp2(m - m_new)
        p = jnp.exp2(s - m_new)
        acc = acc * alpha + _dot(p.astype(BF16), vc)
        return m_new, acc

    m0 = jnp.full((rows, 1), -jnp.inf, F32)
    acc0 = jnp.zeros((rows, 2 * HEAD_DIM), F32)
    _, acc = lax.fori_loop(0, n_chunks, body, (m0, acc0))
    o = acc[:, :HEAD_DIM] / acc[:, HEAD_DIM:HEAD_DIM + 1]
    o_ref[...] = jnp.concatenate([o[j * tq:(j + 1) * tq] for j in range(Q_PER_KV)],
                                 axis=1).astype(BF16)


def attn_flash(q, k, v, batch, seq_len):
    n_rows = q.shape[0]
    tq = 512
    tk = min(seq_len, 2048)
    per_seq = seq_len // tq
    width = Q_PER_KV * HEAD_DIM
    kern = functools.partial(_flash_kernel, tq=tq, tk=tk, n_chunks=seq_len // tk)
    return pl.pallas_call(
        kern,
        grid=(batch, N_KV_HEADS, per_seq),
        in_specs=[
            pl.BlockSpec((tq, width), lambda b, g, i: (b * per_seq + i, g)),
            pl.BlockSpec((1, seq_len, HEAD_DIM), lambda b, g, i: (g, b, 0)),
            pl.BlockSpec((1, seq_len, 2 * HEAD_DIM), lambda b, g, i: (g, b, 0)),
        ],
        out_specs=pl.BlockSpec((tq, width), lambda b, g, i: (b * per_seq + i, g)),
        out_shape=jax.ShapeDtypeStruct((n_rows, N_HEADS * HEAD_DIM), BF16),
        compiler_params=_params(("parallel", "parallel", "parallel")),
        name="attn_flash",
    )(q, k, v)


def _proj_res_kernel(y_ref, w_ref, x_ref, mod_ref, *refs, gate_row):
    route_in, o_ref, route_out = refs[:N_ROUTE_IN], refs[N_ROUTE_IN], refs[N_ROUTE_IN + 1:]
    m = mod_ref[0]
    x1 = x_ref[...] + m[gate_row:gate_row + 1] * _dot(y_ref[...], w_ref[...])
    o_ref[...] = x1
    _route_tile(x1, m, *route_in, *route_out)


def proj_residual(y, w, x, mod, gate_row, seq_len, route):
    n_rows = x.shape[0]
    tm = STREAM_TILE
    r_ops, r_in, r_out, r_shape = _router_io(route, tm, n_rows)
    return pl.pallas_call(
        functools.partial(_proj_res_kernel, gate_row=gate_row),
        grid=(n_rows // tm,),
        in_specs=[_row_spec(tm, y.shape[1]), _const_spec(w.shape), _row_spec(tm),
                  _mod_spec(tm, seq_len)] + r_in,
        out_specs=[_row_spec(tm)] + r_out,
        out_shape=[jax.ShapeDtypeStruct((n_rows, D_MODEL), F32)] + r_shape,
        compiler_params=_params(("arbitrary",)),
        name="proj_residual",
    )(y, w.astype(BF16), x, mod, *r_ops)


CONV_COLS = 256


def _hy_in_kernel(*refs, tm, seq_len, n_stream):
    stream, refs = refs[:n_stream], refs[n_stream:]
    mod_ref, g_ref, w_ref, cw_ref, cb_ref, z_ref, x0_ref = refs[:7]
    stage_refs = refs[-3:]
    first, last = _edge_flags(tm, seq_len)
    m = mod_ref[0]
    x, ext = _stream_rows(stream)
    if n_stream == N_PENDING_REFS:
        refs[7][...] = x
    h = _modulate(ext, g_ref[...], m[0:1], m[1:2]).astype(BF16)
    tn = CONV_COLS
    for j in range(D_MODEL // tn):
        part = []
        for s in range(3):
            cols = slice(s * D_MODEL + j * tn, s * D_MODEL + (j + 1) * tn)
            part.append(_conv3(_dot(h, w_ref[:, cols]), cw_ref, cb_ref, cols, tm, first, last,
                               stage_refs[s]))
        out_cols = slice(j * tn, (j + 1) * tn)
        x0_ref[:, out_cols] = part[0].astype(BF16)
        z_ref[:, out_cols] = (part[2] * part[1]).astype(BF16)


def hyena_in(src, mod, norm_g, w_in, conv_w, conv_b, seq_len):
    pending = isinstance(src, _Pending)
    n_rows = (src.x if pending else src).shape[0]
    tm = ROW_TILE
    s_ops, s_specs = _stream_io(src, tm, n_rows, seq_len)
    x_out = [jax.ShapeDtypeStruct((n_rows, D_MODEL), F32)] if pending else []
    return pl.pallas_call(
        functools.partial(_hy_in_kernel, tm=tm, seq_len=seq_len, n_stream=len(s_ops)),
        grid=(n_rows // tm,),
        in_specs=s_specs + [_mod_spec(tm, seq_len), _const_spec((1, D_MODEL)),
                            _const_spec((D_MODEL, 3 * D_MODEL)), _const_spec((3, 3 * D_MODEL)),
                            _const_spec((1, 3 * D_MODEL))],
        out_specs=[_row_spec(tm)] * (2 + len(x_out)),
        out_shape=[jax.ShapeDtypeStruct((n_rows, D_MODEL), BF16),
                   jax.ShapeDtypeStruct((n_rows, D_MODEL), BF16)] + x_out,
        scratch_shapes=[pltpu.VMEM((tm + 2 * HALO, CONV_COLS), F32)] * 3,
        compiler_params=_params(("parallel",)),
        name="hyena_in",
    )(*s_ops, mod, norm_g[None, :], w_in.astype(BF16), conv_w, conv_b[None, :])


def _hy_filter_kernel(feat_ref, w1_ref, b1_ref, w2_ref, b2_ref, w3_ref, fr_ref, dl_ref, o_ref):
    feat = feat_ref[...]
    fr = fr_ref[...]
    a = jnp.sin(fr * (_dot3(feat, w1_ref[...]) + b1_ref[...]))
    a = jnp.sin(fr * (_dot3(a, w2_ref[...]) + b2_ref[...]))
    hf = _dot3(a, w3_ref[...])
    decay = jnp.exp(-feat[:, 0:1] * dl_ref[...])
    o_ref[0] = hf[:, :D_MODEL] * decay
    o_ref[1] = hf[:, D_MODEL:] * decay


def _pad_to(a, rows, cols):
    return jnp.pad(a.astype(F32), ((0, rows - a.shape[0]), (0, cols - a.shape[1])))


def hyena_filter(seq_len, w1, b1, w2, b2, w3, freq):
    t = jnp.linspace(0.0, 1.0, seq_len, dtype=F32)[:, None]
    w = 2.0 * math.pi * jnp.arange(seq_len, dtype=F32)[:, None] / seq_len
    f = jnp.linspace(1e-4, HY_BANDS - 1, HY_BANDS, dtype=F32)[None, :]
    feat = _pad_to(jnp.concatenate([t, jnp.cos(f * w), -jnp.sin(f * w)], axis=-1), seq_len, LANES)
    max_decay = math.log(HY_TARGET) / HY_FAST_DECAY
    min_decay = math.log(HY_TARGET) / HY_SLOW_DECAY
    absdelta = jnp.abs(jnp.linspace(min_decay, max_decay, D_MODEL, dtype=F32))[None, :]
    tl = 512
    return pl.pallas_call(
        _hy_filter_kernel,
        grid=(seq_len // tl,),
        in_specs=[_row_spec(tl, LANES), _const_spec((LANES, LANES)), _const_spec((1, LANES)),
                  _const_spec((LANES, LANES)), _const_spec((1, LANES)),
                  _const_spec((LANES, 2 * D_MODEL)), _const_spec((1, LANES)),
                  _const_spec((1, D_MODEL))],
        out_specs=pl.BlockSpec((2, tl, D_MODEL), lambda i: (0, i, 0)),
        out_shape=jax.ShapeDtypeStruct((2, seq_len, D_MODEL), F32),
        compiler_params=_params(("parallel",)),
        name="hyena_filter",
    )(feat, _pad_to(w1, LANES, LANES), _pad_to(b1[None, :], 1, LANES), _pad_to(w2, LANES, LANES),
      _pad_to(b2[None, :], 1, LANES), _pad_to(w3, LANES, 2 * D_MODEL),
      _pad_to(freq[None, :], 1, LANES), absdelta)


class _FFTPlan:
    def __init__(self, seq_len):
        self.n = 2 * seq_len
        self.n1 = self.n // DFT_N2
        self.r = self.n1 // 2
        self.k1n = self.n1 // 2 + 1
        self.kron = max(HALO, LANES // self.r)
        ang = 2.0 * np.pi * np.outer(np.arange(self.k1n), np.arange(self.r)) / self.n1
        eye = np.eye(self.kron)
        self.fwd_cos = jnp.asarray(np.kron(np.cos(ang), eye), BF16)
        self.fwd_sin = jnp.asarray(np.kron(-np.sin(ang), eye), BF16)
        wgt = np.full((self.k1n,), 2.0)
        wgt[0] = wgt[-1] = 1.0
        scale = (wgt / self.n)[None, :]
        self.inv_cos = jnp.asarray(np.kron(np.cos(ang).T * scale, eye), BF16)
        self.inv_sin = jnp.asarray(np.kron(-np.sin(ang).T * scale, eye), BF16)
        a2 = 2.0 * np.pi * np.outer(np.arange(DFT_N2), np.arange(DFT_N2)) / DFT_N2
        self.f_cos = jnp.asarray(np.cos(a2), F32)
        self.f_sin = jnp.asarray(-np.sin(a2), F32)
        tw = 2.0 * np.pi * np.outer(np.arange(self.k1n), np.arange(DFT_N2)) / self.n
        self.tw_cos = jnp.asarray(np.cos(tw), F32)
        self.tw_sin = jnp.asarray(-np.sin(tw), F32)


def _fft_a_kernel(z_ref, wc_ref, ws_ref, ar_ref, ai_ref, *, rq):
    shape = ar_ref.shape[1:2] + ar_ref.shape[3:]
    for t in range(z_ref.shape[2]):
        z = z_ref[0, :, t].astype(F32).reshape(rq, D_MODEL).astype(BF16)
        ar_ref[0, :, t] = _dot(wc_ref[...], z).reshape(shape).astype(BF16)
        ai_ref[0, :, t] = _dot(ws_ref[...], z).reshape(shape).astype(BF16)


def _fft_group(plan):
    nhi = DFT_N2 // plan.kron
    per_group = plan.r * plan.kron * D_MODEL * 4
    return max(1, min(nhi, (2 << 20) // per_group))


def fft_stage_a(z, plan, batch):
    q = plan.kron
    nhi = DFT_N2 // q
    hb = _fft_group(plan)
    zv = z.reshape(batch, plan.r, nhi, q, D_MODEL)
    out = jax.ShapeDtypeStruct((batch, plan.k1n, nhi, q, D_MODEL), BF16)
    ospec = pl.BlockSpec((1, plan.k1n, hb, q, D_MODEL), lambda b, h: (b, 0, h, 0, 0))
    wspec = pl.BlockSpec(plan.fwd_cos.shape, lambda b, h: (0, 0))
    ar, ai = pl.pallas_call(
        functools.partial(_fft_a_kernel, rq=plan.r * q),
        grid=(batch, nhi // hb),
        in_specs=[pl.BlockSpec((1, plan.r, hb, q, D_MODEL), lambda b, h: (b, 0, h, 0, 0)),
                  wspec, wspec],
        out_specs=[ospec, ospec],
        out_shape=[out, out],
        compiler_params=_params(("parallel", "parallel")),
        name="fft_stage_a",
    )(zv, plan.fwd_cos, plan.fwd_sin)
    shape = (batch, plan.k1n, DFT_N2, D_MODEL)
    return ar.reshape(shape), ai.reshape(shape)


def _twiddled_dft(fr, fi, tr, ti):
    return (fr * tr - fi * ti).astype(BF16), (fr * ti + fi * tr).astype(BF16)


def _fft_b_fwd_kernel(ar_ref, ai_ref, fr_ref, fi_ref, twr_ref, twi_ref, br_ref, bi_ref):
    n = DFT_N2
    gr, gi = _twiddled_dft(fr_ref[...], fi_ref[...], twr_ref[0], twi_ref[0])
    fwd = jnp.concatenate([jnp.concatenate([gr, -gi], axis=1),
                           jnp.concatenate([gi, gr], axis=1)], axis=0)
    for s in range(ar_ref.shape[0]):
        b = _dot(fwd, jnp.concatenate([ar_ref[s, 0], ai_ref[s, 0]], axis=0))
        br_ref[s, 0] = b[:n].astype(BF16)
        bi_ref[s, 0] = b[n:].astype(BF16)


def fft_stage_b_fwd(ar, ai, plan):
    batch = ar.shape[0]
    blk = pl.BlockSpec((batch, 1, DFT_N2, D_MODEL), lambda k: (0, k, 0, 0))
    cst = pl.BlockSpec((DFT_N2, DFT_N2), lambda k: (0, 0))
    tws = pl.BlockSpec((1, 1, DFT_N2), lambda k: (k, 0, 0))
    out = jax.ShapeDtypeStruct(ar.shape, BF16)
    return pl.pallas_call(
        _fft_b_fwd_kernel,
        grid=(plan.k1n,),
        in_specs=[blk, blk, cst, cst, tws, tws],
        out_specs=[blk, blk],
        out_shape=[out, out],
        compiler_params=_params(("parallel",)),
        name="fft_stage_b_fwd",
    )(ar, ai, plan.f_cos, plan.f_sin, plan.tw_cos[:, None, :], plan.tw_sin[:, None, :])


def _fft_b_conv_kernel(ar_ref, ai_ref, hr_ref, hi_ref, hb0_ref, fr_ref, fi_ref,
                       twr_ref, twi_ref, tcr_ref, tci_ref, cr_ref, ci_ref,
                       fwd_ref, inv_ref, kr_ref, ki_ref):
    n = DFT_N2

    @pl.when(pl.program_id(1) == 0)
    def _():
        fr = fr_ref[...]
        fi = fi_ref[...]
        gr, gi = _twiddled_dft(fr, fi, twr_ref[0], twi_ref[0])
        fwd_ref[:n, :n] = gr
        fwd_ref[:n, n:] = -gi
        fwd_ref[n:, :n] = gi
        fwd_ref[n:, n:] = gr
        gtr, gti = _twiddled_dft(fr, fi, tcr_ref[0], tci_ref[0])
        inv_ref[:n, :n] = gtr
        inv_ref[:n, n:] = gti
        inv_ref[n:, :n] = -gti
        inv_ref[n:, n:] = gtr
        kr_ref[...] = hr_ref[0, 0].astype(F32) + hr_ref[1, 0].astype(F32) - hb0_ref[...]
        ki_ref[...] = hi_ref[0, 0].astype(F32) - hi_ref[1, 0].astype(F32)

    a = jnp.concatenate([ar_ref[0, 0], ai_ref[0, 0]], axis=0)
    b = _dot(fwd_ref[...], a)
    br, bi = b[:n], b[n:]
    kr = kr_ref[...]
    ki = ki_ref[...]
    p = jnp.concatenate([(br * kr - bi * ki).astype(BF16), (br * ki + bi * kr).astype(BF16)], axis=0)
    c = _dot(inv_ref[...], p)
    cr_ref[0, 0] = c[:n].astype(BF16)
    ci_ref[0, 0] = c[n:].astype(BF16)


def fft_stage_b_conv(ar, ai, hr, hi, hb0, plan):
    batch = ar.shape[0]
    blk = pl.BlockSpec((1, 1, DFT_N2, D_MODEL), lambda k, b: (b, k, 0, 0))
    hblk = pl.BlockSpec((2, 1, DFT_N2, D_MODEL), lambda k, b: (0, k, 0, 0))
    cst = pl.BlockSpec((DFT_N2, DFT_N2), lambda k, b: (0, 0))
    tws = pl.BlockSpec((1, 1, DFT_N2), lambda k, b: (k, 0, 0))
    twc = pl.BlockSpec((1, DFT_N2, 1), lambda k, b: (k, 0, 0))
    out = jax.ShapeDtypeStruct(ar.shape, BF16)
    return pl.pallas_call(
        _fft_b_conv_kernel,
        grid=(plan.k1n, batch),
        in_specs=[blk, blk, hblk, hblk, pl.BlockSpec((1, D_MODEL), lambda k, b: (0, 0)),
                  cst, cst, tws, tws, twc, twc],
        out_specs=[blk, blk],
        out_shape=[out, out],
        scratch_shapes=[pltpu.VMEM((2 * DFT_N2, 2 * DFT_N2), BF16)] * 2
        + [pltpu.VMEM((DFT_N2, D_MODEL), F32)] * 2,
        compiler_params=_params(("parallel", "arbitrary")),
        name="fft_stage_b_conv",
    )(ar, ai, hr, hi, hb0, plan.f_cos, plan.f_sin,
      plan.tw_cos[:, None, :], plan.tw_sin[:, None, :],
      plan.tw_cos[:, :, None], plan.tw_sin[:, :, None])


def _fft_a_inv_kernel(cr_ref, ci_ref, vc_ref, vs_ref, z_ref, x0_ref, skip_ref, y_ref, *, kq):
    shape = z_ref.shape[1:2] + z_ref.shape[3:]
    for t in range(z_ref.shape[2]):
        cr = cr_ref[0, :, t].astype(F32).reshape(kq, D_MODEL).astype(BF16)
        ci = ci_ref[0, :, t].astype(F32).reshape(kq, D_MODEL).astype(BF16)
        conv = _dot(vc_ref[...], cr) + _dot(vs_ref[...], ci)
        y = conv.reshape(shape) + z_ref[0, :, t].astype(F32) * skip_ref[...]
        y_ref[0, :, t] = (y * x0_ref[0, :, t].astype(F32)).astype(BF16)


def fft_stage_a_inv(cr, ci, z, x0, skip, plan, batch):
    q = plan.kron
    nhi = DFT_N2 // q
    hb = _fft_group(plan)
    cshape = (batch, plan.k1n, nhi, q, D_MODEL)
    tshape = (batch, plan.r, nhi, q, D_MODEL)
    cspec = pl.BlockSpec((1, plan.k1n, hb, q, D_MODEL), lambda b, h: (b, 0, h, 0, 0))
    tspec = pl.BlockSpec((1, plan.r, hb, q, D_MODEL), lambda b, h: (b, 0, h, 0, 0))
    wspec = pl.BlockSpec(plan.inv_cos.shape, lambda b, h: (0, 0))
    y = pl.pallas_call(
        functools.partial(_fft_a_inv_kernel, kq=plan.k1n * q),
        grid=(batch, nhi // hb),
        in_specs=[cspec, cspec, wspec, wspec, tspec, tspec,
                  pl.BlockSpec((1, D_MODEL), lambda b, h: (0, 0))],
        out_specs=tspec,
        out_shape=jax.ShapeDtypeStruct(tshape, BF16),
        compiler_params=_params(("parallel", "parallel")),
        name="fft_stage_a_inv",
    )(cr.reshape(cshape), ci.reshape(cshape), plan.inv_cos, plan.inv_sin,
      z.reshape(tshape), x0.reshape(tshape), skip[None, :])
    return y.reshape(z.shape)


def hyena_mixer(src, mod, norm_g, p, batch, seq_len, route):
    z, x0, *rest = hyena_in(src, mod, norm_g, p["w_in"], p["conv_w"], p["conv_b"], seq_len)
    x = rest[0] if rest else src
    plan = _FFTPlan(seq_len)
    filt = hyena_filter(seq_len, p["w1"], p["b1"], p["w2"], p["b2"], p["w3"], p["freq"])
    fr, fi = fft_stage_a(filt.reshape(2 * seq_len, D_MODEL), plan, 2)
    hr, hi = fft_stage_b_fwd(fr, fi, plan)
    ar, ai = fft_stage_a(z, plan, batch)
    cr, ci = fft_stage_b_conv(ar, ai, hr, hi, filt[1, 0:1, :], plan)
    y = fft_stage_a_inv(cr, ci, z, x0, p["skip"], plan, batch)
    return proj_residual(y, p["w_out"], x, mod, 2, seq_len, route)


def _pool_kernel(*refs, tm, seq_len, n_stream):
    stream, refs = refs[:n_stream], refs[n_stream:]
    mod_ref, g_ref, w_ref, s_ref = refs[:4]
    refs = refs[4:]
    route_in, o_ref, route_out = refs[:N_ROUTE_IN], refs[N_ROUTE_IN], refs[N_ROUTE_IN + 1:]
    first, last = _edge_flags(tm, seq_len)
    m = mod_ref[0]
    x, ext = _stream_rows(stream)
    h = _modulate(ext, g_ref[...], m[0:1], m[1:2])
    n = tm + 2 * HALO
    row = lax.broadcasted_iota(jnp.int32, (n, 1), 0)
    outside = jnp.logical_or(jnp.logical_and(first, row < HALO),
                             jnp.logical_and(last, row >= HALO + tm))
    h = jnp.where(outside, 0.0, h)
    edge = lax.broadcasted_iota(jnp.int32, (HALO, 1), 0)
    pos_top = jnp.where(first, edge, HALO)
    pos_bot = jnp.where(last, seq_len - HALO + edge, HALO)

    def inv_count(pos, half):
        cnt = jnp.minimum(pos + half, seq_len) - jnp.maximum(pos - half, 0)
        return 1.0 / cnt.astype(F32)

    ys = []
    for gi, win in enumerate(POOL_WINDOWS):
        cols = slice(gi * POOL_GROUP_DIM, (gi + 1) * POOL_GROUP_DIM)
        hg = h[:, cols]
        acc = hg
        span = 1
        while span < win:
            acc = acc + pltpu.roll(acc, span, axis=0)
            span *= 2
        lead = win // 2 - 1
        if lead:
            acc = pltpu.roll(acc, n - lead, axis=0)
        half = win // 2
        window = acc[HALO:HALO + tm]
        mean = jnp.concatenate([window[:HALO] * inv_count(pos_top, half),
                                window[HALO:tm - HALO] * (1.0 / win),
                                window[tm - HALO:] * inv_count(pos_bot, half)], axis=0)
        pooled = mean - hg[HALO:HALO + tm]
        ys.append(_dot(pooled.astype(BF16), w_ref[gi]))
    y = jnp.concatenate(ys, axis=1) * s_ref[...]
    x1 = x + m[2:3] * y
    o_ref[...] = x1
    _route_tile(x1, m, *route_in, *route_out)


def pool_mixer(src, mod, norm_g, w_group, scale, seq_len, route):
    n_rows = (src.x if isinstance(src, _Pending) else src).shape[0]
    tm = ROW_TILE
    s_ops, s_specs = _stream_io(src, tm, n_rows, seq_len)
    r_ops, r_in, r_out, r_shape = _router_io(route, tm, n_rows)
    return pl.pallas_call(
        functools.partial(_pool_kernel, tm=tm, seq_len=seq_len, n_stream=len(s_ops)),
        grid=(n_rows // tm,),
        in_specs=s_specs + [_mod_spec(tm, seq_len), _const_spec((1, D_MODEL)),
                            _const_spec(w_group.shape), _const_spec((1, D_MODEL))] + r_in,
        out_specs=[_row_spec(tm)] + r_out,
        out_shape=[jax.ShapeDtypeStruct((n_rows, D_MODEL), F32)] + r_shape,
        compiler_params=_params(("arbitrary",)),
        name="pool_mixer",
    )(*s_ops, mod, norm_g[None, :], w_group.astype(BF16), scale[None, :], *r_ops)


def _sconv_kernel(*refs, tm, seq_len, n_stream):
    stream, refs = refs[:n_stream], refs[n_stream:]
    mod_ref, g_ref, w_ref, cw_ref, cb_ref, wo_ref = refs[:6]
    refs = refs[6:]
    route_in, o_ref = refs[:N_ROUTE_IN], refs[N_ROUTE_IN]
    route_out, y_ref = refs[N_ROUTE_IN + 1:N_ROUTE_IN + 1 + N_ROUTE_OUT], refs[-1]
    first, last = _edge_flags(tm, seq_len)
    m = mod_ref[0]
    x, ext = _stream_rows(stream)
    h = _modulate(ext, g_ref[...], m[0:1], m[1:2]).astype(BF16)
    tn = 256
    for j in range(D_MODEL // tn):
        cols = slice(j * tn, (j + 1) * tn)
        bg = _dot(h, w_ref[:, cols])[HALO:HALO + tm]
        cg = _dot(h, w_ref[:, D_MODEL + j * tn:D_MODEL + (j + 1) * tn])
        hp = _dot(h, w_ref[:, 2 * D_MODEL + j * tn:2 * D_MODEL + (j + 1) * tn])
        y_ref[:, cols] = (bg * _conv3(cg * hp, cw_ref, cb_ref, cols, tm, first, last)).astype(BF16)
    x1 = x + m[2:3] * _dot(y_ref[...], wo_ref[...])
    o_ref[...] = x1
    _route_tile(x1, m, *route_in, *route_out)


def sconv_mixer(src, mod, norm_g, w_in, conv_w, conv_b, w_out, seq_len, route):
    n_rows = (src.x if isinstance(src, _Pending) else src).shape[0]
    tm = ROW_TILE
    s_ops, s_specs = _stream_io(src, tm, n_rows, seq_len)
    r_ops, r_in, r_out, r_shape = _router_io(route, tm, n_rows)
    return pl.pallas_call(
        functools.partial(_sconv_kernel, tm=tm, seq_len=seq_len, n_stream=len(s_ops)),
        grid=(n_rows // tm,),
        in_specs=s_specs + [_mod_spec(tm, seq_len), _const_spec((1, D_MODEL)),
                            _const_spec((D_MODEL, 3 * D_MODEL)), _const_spec((3, D_MODEL)),
                            _const_spec((1, D_MODEL)), _const_spec((D_MODEL, D_MODEL))] + r_in,
        out_specs=[_row_spec(tm)] + r_out,
        out_shape=[jax.ShapeDtypeStruct((n_rows, D_MODEL), F32)] + r_shape,
        scratch_shapes=[pltpu.VMEM((tm, D_MODEL), BF16)],
        compiler_params=_params(("arbitrary",)),
        name="sconv_mixer",
    )(*s_ops, mod, norm_g[None, :], w_in.astype(BF16), conv_w, conv_b[None, :],
      w_out.astype(BF16), *r_ops)


def _pack_bf16(x):
    w = x.shape[1] // 2
    bits = pltpu.bitcast(x.astype(BF16).astype(F32), jnp.uint32)
    return (bits[:, :w] >> 16) | bits[:, w:]


def _unpack_bf16(p):
    lo = pltpu.bitcast(p << 16, F32)
    hi = pltpu.bitcast(p & jnp.uint32(0xFFFF0000), F32)
    return jnp.concatenate([lo, hi], axis=1)


META_E1, META_E2, META_W1, META_W2, META_R1, META_R2 = range(6)
META_ROWS = 8


def _route_tile(x, m, g_ref, wh_ref, b_ref, tri_ref, h_ref, meta_ref, meta_t_ref, cnt_ref):
    @pl.when(pl.program_id(0) == 0)
    def _():
        cnt_ref[...] = jnp.zeros_like(cnt_ref)

    ms = jnp.mean(x * x, axis=-1, keepdims=True)
    h = x * lax.rsqrt(ms + EPS) * (g_ref[...] * (1.0 + m[4:5])) + m[3:4]
    hi = h.astype(BF16)
    hi32 = hi.astype(F32)
    lo = (h - hi32).astype(BF16)
    half = D_MODEL // 2
    bits = pltpu.bitcast(hi32, jnp.uint32)
    h_ref[...] = (bits[:, :half] >> 16) | bits[:, half:]
    part = _dot(hi, wh_ref[...])
    lg = part[:, :LANES] + (part[:, LANES:] + _dot(lo, wh_ref[:, :LANES])) + b_ref[...]
    lgt = lg.T
    tm = lgt.shape[1]
    neg = -jnp.inf
    sub = lax.broadcasted_iota(jnp.int32, (HALO, tm), 0).astype(F32)

    def first_argmax(vals):
        top = jnp.max(vals, axis=0, keepdims=True)
        idx = jnp.min(jnp.where(vals == top, sub, float(HALO)), axis=0, keepdims=True)
        return top, idx

    gl = jnp.where(sub < MOE_GROUPS, lgt[MOE_N_EXPERTS:MOE_N_EXPERTS + HALO], neg)
    gmax, grp = first_argmax(gl)
    g_w = 1.0 / jnp.sum(jnp.exp(gl - gmax), axis=0, keepdims=True)
    el = lgt[:MOE_EXPERTS_PER_GROUP]
    for g in range(1, MOE_GROUPS):
        el = jnp.where(grp == float(g),
                       lgt[g * MOE_EXPERTS_PER_GROUP:(g + 1) * MOE_EXPERTS_PER_GROUP], el)
    v1, i1 = first_argmax(el)
    v2, i2 = first_argmax(jnp.where(sub == i1, neg, el))
    ex = jnp.exp(v2 - v1)
    w1 = 1.0 / (1.0 + ex)
    w2 = ex * w1
    e1 = grp * MOE_EXPERTS_PER_GROUP + i1
    e2 = grp * MOE_EXPERTS_PER_GROUP + i2
    expert = lax.broadcasted_iota(jnp.int32, (MOE_N_EXPERTS, tm), 0).astype(F32)
    onehot = jnp.where(jnp.logical_or(expert == e1, expert == e2), 1.0, 0.0)
    before = _dot(onehot.astype(BF16), tri_ref[...]) + cnt_ref[...]
    cnt_ref[...] += jnp.sum(onehot, axis=1, keepdims=True)
    r1 = jnp.sum(jnp.where(expert == e1, before, 0.0), axis=0, keepdims=True)
    r2 = jnp.sum(jnp.where(expert == e2, before, 0.0), axis=0, keepdims=True)
    fields = ((META_E1, e1), (META_E2, e2), (META_W1, w1 * g_w), (META_W2, w2 * g_w),
              (META_R1, r1), (META_R2, r2))
    field = lax.broadcasted_iota(jnp.int32, (LANES, tm), 0)
    meta_t = jnp.zeros((LANES, tm), F32)
    for row, val in fields:
        meta_t = jnp.where(field == row, val, meta_t)
    meta_t_ref[...] = meta_t[:META_ROWS]
    meta_ref[...] = meta_t.T


N_ROUTE_IN = 4
N_ROUTE_OUT = 4


def _router_io(route, tm, n_rows):
    norm_g, w_group, b_group, w_router, b_router = route
    w = _pad_to(jnp.concatenate([w_router, w_group], axis=1), D_MODEL, LANES)
    wh = w.astype(BF16)
    wl = (w - wh.astype(F32)).astype(BF16)
    whl = jnp.concatenate([wh, wl], axis=1)
    b = _pad_to(jnp.concatenate([b_router, b_group])[None, :], 1, LANES)
    tri = jnp.asarray(np.triu(np.ones((tm, tm), np.float32), 1), BF16)
    operands = (norm_g[None, :], whl, b, tri)
    in_specs = [_const_spec((1, D_MODEL)), _const_spec((D_MODEL, 2 * LANES)),
                _const_spec((1, LANES)), _const_spec((tm, tm))]
    out_specs = [_row_spec(tm, D_MODEL // 2), _row_spec(tm, LANES),
                 pl.BlockSpec((META_ROWS, tm), lambda i: (0, i)),
                 _const_spec((MOE_N_EXPERTS, 1))]
    out_shape = [jax.ShapeDtypeStruct((n_rows, D_MODEL // 2), jnp.uint32),
                 jax.ShapeDtypeStruct((n_rows, LANES), F32),
                 jax.ShapeDtypeStruct((META_ROWS, n_rows), F32),
                 jax.ShapeDtypeStruct((MOE_N_EXPERTS, 1), F32)]
    return operands, in_specs, out_specs, out_shape


def _slot_kernel(offs_ref, meta_t_ref, pos_ref):
    meta = meta_t_ref[...]
    start = jnp.zeros_like(meta)
    for e in range(MOE_N_EXPERTS):
        start = jnp.where(meta == float(e), offs_ref[e], start)
    shift = META_ROWS - (META_R1 - META_E1)
    pos_ref[...] = (start + pltpu.roll(meta, shift, axis=0)).astype(jnp.int32)


def moe_slots(meta_t, offsets):
    n_rows = meta_t.shape[1]
    tn = min(n_rows, 8192)
    blk = pl.BlockSpec((META_ROWS, tn), lambda i, offs: (0, i))
    return pl.pallas_call(
        _slot_kernel,
        grid_spec=pltpu.PrefetchScalarGridSpec(
            num_scalar_prefetch=1, grid=(n_rows // tn,), in_specs=[blk], out_specs=blk),
        out_shape=jax.ShapeDtypeStruct((META_ROWS, n_rows), jnp.int32),
        compiler_params=_params(("parallel",)),
        name="moe_slots",
    )(offsets, meta_t)


def _expert_kernel(te_ref, nu_ref, nv_ref, xs_ref, wg_ref, wu_ref, wd_ref, o_ref,
                   wgu_ref, wdb_ref):
    j = pl.program_id(0)

    @pl.when(jnp.logical_or(j == 0, te_ref[j] != te_ref[jnp.maximum(j - 1, 0)]))
    def _():
        wgu_ref[:, :MOE_D_FF] = wg_ref[0, 0].astype(BF16)
        wgu_ref[:, MOE_D_FF:] = wu_ref[0, 0].astype(BF16)
        wdb_ref[...] = wd_ref[0, 0].astype(BF16)

    @pl.when(j < nu_ref[0])
    def _():
        x = _unpack_bf16(xs_ref[...])
        row = lax.broadcasted_iota(jnp.int32, (x.shape[0], 1), 0)
        x = jnp.where(row < nv_ref[j], x, 0.0).astype(BF16)
        au = _dot(x, wgu_ref[...])
        hh = (_silu(au[:, :MOE_D_FF]) * au[:, MOE_D_FF:]).astype(BF16)
        o_ref[...] = _pack_bf16(_dot(hh, wdb_ref[...]))


def moe_experts(xs, tile_expert, n_used, n_valid, w_gate, w_up, w_down, layer, tm):
    n_slots, half = xs.shape
    wspec = lambda shape: pl.BlockSpec((1, 1) + shape, lambda j, te, nu, nv: (layer, te[j], 0, 0))
    row = pl.BlockSpec((tm, half), lambda j, te, nu, nv: (j, 0))
    row_in = pl.BlockSpec((tm, half), lambda j, te, nu, nv: (jnp.minimum(j, nu[0] - 1), 0))
    grid_spec = pltpu.PrefetchScalarGridSpec(
        num_scalar_prefetch=3,
        grid=(n_slots // tm,),
        in_specs=[row_in, wspec((D_MODEL, MOE_D_FF)), wspec((D_MODEL, MOE_D_FF)),
                  wspec((MOE_D_FF, D_MODEL))],
        out_specs=row,
        scratch_shapes=[pltpu.VMEM((D_MODEL, 2 * MOE_D_FF), BF16),
                        pltpu.VMEM((MOE_D_FF, D_MODEL), BF16)],
    )
    return pl.pallas_call(
        _expert_kernel,
        grid_spec=grid_spec,
        out_shape=jax.ShapeDtypeStruct((n_slots, half), jnp.uint32),
        compiler_params=_params(("arbitrary",)),
        name="moe_experts",
    )(tile_expert, n_used, n_valid, xs, w_gate, w_up, w_down)


def _combine_kernel(x_ref, ya_ref, yb_ref, meta_ref, mod_ref, o_ref):
    o_ref[...] = _moe_combined(x_ref, ya_ref, yb_ref, meta_ref, mod_ref[0][5:6])


def moe_combine(pending, seq_len):
    x, ya, yb, meta, mod = pending
    n_rows = x.shape[0]
    tm = STREAM_TILE
    half = D_MODEL // 2
    return pl.pallas_call(
        _combine_kernel,
        grid=(n_rows // tm,),
        in_specs=[_row_spec(tm), _row_spec(tm, half), _row_spec(tm, half), _row_spec(tm, LANES),
                  _mod_spec(tm, seq_len)],
        out_specs=_row_spec(tm),
        out_shape=jax.ShapeDtypeStruct((n_rows, D_MODEL), F32),
        compiler_params=_params(("parallel",)),
        name="moe_combine",
    )(x, ya, yb, meta, mod)


MOE_TILE = 512
MOE_TILE_LARGE = 1024
MOE_LARGE_ROWS = 32768


def hier_moe(routed, mod, w_gate, w_up, w_down, layer, seq_len):
    x, hp, meta, meta_t, counts = routed
    n_rows = x.shape[0]
    tm = MOE_TILE_LARGE if n_rows >= MOE_LARGE_ROWS else MOE_TILE
    cnt = counts[:, 0].astype(jnp.int32)
    padded = (cnt + tm - 1) // tm * tm
    ends = jnp.cumsum(padded)
    starts = ends - padded
    n_slots = 2 * n_rows + MOE_N_EXPERTS * tm
    tile_start = jnp.arange(n_slots // tm, dtype=jnp.int32) * tm
    tile_expert = jnp.minimum(jnp.sum(tile_start[:, None] >= ends[None, :], axis=1),
                              MOE_N_EXPERTS - 1).astype(jnp.int32)
    n_used = (ends[-1:] // tm).astype(jnp.int32)
    n_valid = jnp.clip(starts[tile_expert] + cnt[tile_expert] - tile_start, 0, tm).astype(jnp.int32)
    pos = moe_slots(meta_t, starts.astype(F32))
    idx0 = pos[META_E1].reshape(n_rows // SC_BLOCK, SC_BLOCK)
    idx1 = pos[META_E2].reshape(n_rows // SC_BLOCK, SC_BLOCK)
    xs = sc_scatter_rows(hp, idx0, idx1, n_slots)
    ys = moe_experts(xs, tile_expert, n_used, n_valid, w_gate, w_up, w_down, layer, tm)
    ya, yb = sc_gather_rows(ys, idx0, idx1)
    return _Pending(x, ya, yb, meta, mod)


SC_CORES = 2
SC_SUBCORES = 16
SC_WORKERS = SC_CORES * SC_SUBCORES
SC_BLOCK = 128


def _sc_mesh():
    return plsc.VectorSubcoreMesh(core_axis_name="c", subcore_axis_name="s")


def _sc_worker():
    return lax.axis_index("s") * SC_CORES + lax.axis_index("c")


def sc_scatter_rows(rows, idx0, idx1, n_slots):
    n_rows, width = rows.shape
    per_worker = n_rows // SC_BLOCK // SC_WORKERS

    @functools.partial(
        pl.kernel, mesh=_sc_mesh(),
        out_type=jax.ShapeDtypeStruct((n_slots, width), rows.dtype),
        scratch_types=[pltpu.VMEM((SC_BLOCK,), jnp.int32), pltpu.VMEM((SC_BLOCK,), jnp.int32),
                       pltpu.VMEM((SC_BLOCK, width), rows.dtype)],
        name="sc_scatter_rows",
    )
    def scatter(rows_hbm, i0_hbm, i1_hbm, out_hbm, i0_v, i1_v, rows_v):
        first = _sc_worker() * per_worker

        @pl.loop(0, per_worker)
        def _(j):
            blk = first + j
            pltpu.sync_copy(i0_hbm.at[blk], i0_v)
            pltpu.sync_copy(i1_hbm.at[blk], i1_v)
            pltpu.sync_copy(rows_hbm.at[pl.ds(blk * SC_BLOCK, SC_BLOCK)], rows_v)
            pltpu.sync_copy(rows_v, out_hbm.at[i0_v])
            pltpu.sync_copy(rows_v, out_hbm.at[i1_v])

    return scatter(rows, idx0, idx1)


def sc_gather_rows(src, idx0, idx1):
    width = src.shape[1]
    n_rows = idx0.shape[0] * SC_BLOCK
    per_worker = n_rows // SC_BLOCK // SC_WORKERS
    out = jax.ShapeDtypeStruct((n_rows, width), src.dtype)

    @functools.partial(
        pl.kernel, mesh=_sc_mesh(), out_type=(out, out),
        scratch_types=[pltpu.VMEM((SC_BLOCK,), jnp.int32), pltpu.VMEM((SC_BLOCK, width), src.dtype)],
        name="sc_gather_rows",
    )
    def gather(src_hbm, i0_hbm, i1_hbm, a_hbm, b_hbm, idx_v, rows_v):
        first = _sc_worker() * per_worker

        @pl.loop(0, per_worker)
        def _(j):
            blk = first + j
            dst = pl.ds(blk * SC_BLOCK, SC_BLOCK)
            pltpu.sync_copy(i0_hbm.at[blk], idx_v)
            pltpu.sync_copy(src_hbm.at[idx_v], rows_v)
            pltpu.sync_copy(rows_v, a_hbm.at[dst])
            pltpu.sync_copy(i1_hbm.at[blk], idx_v)
            pltpu.sync_copy(src_hbm.at[idx_v], rows_v)
            pltpu.sync_copy(rows_v, b_hbm.at[dst])

    return gather(src, idx0, idx1)


def _trunk(x3, mods, p):
    batch, seq_len, _ = x3.shape
    x = x3.reshape(batch * seq_len, D_MODEL)
    for i in range(DEPTH):
        mod = mods[i]
        g1 = p["norm1_g"][i]
        route = (p["norm2_g"][i], p["moe_w_group"][i], p["moe_b_group"][i], p["moe_w_router"][i],
                 p["moe_b_router"][i])
        kind = i % 4
        if kind == 0:
            if isinstance(x, _Pending):
                x = moe_combine(x, seq_len)
            q, k, v = attn_qkv(x, mod, g1, p["attn_wqkv"][0], p["attn_q_norm"][0],
                               p["attn_k_norm"][0], seq_len)
            o = attn_flash(q, k, v, batch, seq_len)
            routed = proj_residual(o, p["attn_wo"][0], x, mod, 2, seq_len, route)
        elif kind == 1:
            hp = {"w_in": p["hy_w_in"][0], "conv_w": p["hy_conv_w"][0], "conv_b": p["hy_conv_b"][0],
                  "w1": p["hy_ffn_w1"][0], "b1": p["hy_ffn_b1"][0], "w2": p["hy_ffn_w2"][0],
                  "b2": p["hy_ffn_b2"][0], "w3": p["hy_ffn_w3"][0], "freq": p["hy_freq"][0],
                  "skip": p["hy_skip"][0], "w_out": p["hy_w_out"][0]}
            routed = hyena_mixer(x, mod, g1, hp, batch, seq_len, route)
        elif kind == 2:
            routed = pool_mixer(x, mod, g1, p["pool_w"][0], p["pool_scale"][0], seq_len, route)
        else:
            routed = sconv_mixer(x, mod, g1, p["sc_w_in"][0], p["sc_conv_w"][0],
                                 p["sc_conv_b"][0], p["sc_w_out"][0], seq_len, route)
        x = hier_moe(routed, mod, p["moe_w_gate"], p["moe_w_up"], p["moe_w_down"], i, seq_len)
    return moe_combine(x, seq_len).reshape(batch, seq_len, D_MODEL)


def kernel(x_prompt, x_sample, c_prompt, c_sample, norm1_g, norm2_g, ada_w, ada_b, attn_wqkv, attn_q_norm, attn_k_norm, attn_wo, hy_w_in, hy_conv_w, hy_conv_b, hy_ffn_w1, hy_ffn_b1, hy_ffn_w2, hy_ffn_b2, hy_ffn_w3, hy_freq, hy_skip, hy_w_out, pool_w, pool_scale, sc_w_in, sc_conv_w, sc_conv_b, sc_w_out, moe_w_group, moe_b_group, moe_w_router, moe_b_router, moe_w_gate, moe_w_up, moe_w_down):
    p = dict(norm1_g=norm1_g, norm2_g=norm2_g, attn_wqkv=attn_wqkv, attn_q_norm=attn_q_norm,
             attn_k_norm=attn_k_norm, attn_wo=attn_wo, hy_w_in=hy_w_in, hy_conv_w=hy_conv_w,
             hy_conv_b=hy_conv_b, hy_ffn_w1=hy_ffn_w1, hy_ffn_b1=hy_ffn_b1, hy_ffn_w2=hy_ffn_w2,
             hy_ffn_b2=hy_ffn_b2, hy_ffn_w3=hy_ffn_w3, hy_freq=hy_freq, hy_skip=hy_skip,
             hy_w_out=hy_w_out, pool_w=pool_w, pool_scale=pool_scale, sc_w_in=sc_w_in,
             sc_conv_w=sc_conv_w, sc_conv_b=sc_conv_b, sc_w_out=sc_w_out, moe_w_group=moe_w_group,
             moe_b_group=moe_b_group, moe_w_router=moe_w_router, moe_b_router=moe_b_router,
             moe_w_gate=moe_w_gate, moe_w_up=moe_w_up, moe_w_down=moe_w_down)
    nb = c_prompt.shape[0]
    ns = c_sample.shape[0]
    rows = -(-(nb + ns) // HALO) * HALO
    c_all = jnp.pad(jnp.concatenate([c_prompt, c_sample], axis=0), ((0, rows - nb - ns), (0, 0)))
    mod = ada_mod(c_all, ada_w, ada_b).reshape(DEPTH, rows, 6, D_MODEL)
    mods_prompt = [mod[i, :nb] for i in range(DEPTH)]
    mods_sample = [mod[i, nb:nb + ns] for i in range(DEPTH)]
    return _trunk(x_prompt, mods_prompt, p), _trunk(x_sample, mods_sample, p)
```
